```python
import math
import jax, jax.numpy as jnp
from jax import lax
import numpy as np

D_MODEL = 1024
BATCH = 32
SEQ = 256
DEPTH = 1
DEC_BATCH = 4
DEC_SEQ = 2048
PAST_LEN = 512

GRID_W = 64
D_S5 = 3 * D_MODEL // 4
S5_GROUP = 16
S5_GROUPS = D_S5 // S5_GROUP
S5_STATE = 64
N_DIR = 2
DT_MIN = 1e-3
DT_MAX = 1e-1
D_FNET = D_MODEL - D_S5
FNET_GROUPS = 4
FNET_GROUP = D_FNET // FNET_GROUPS
N_BRANCH = 2
D_IN = D_S5 + D_FNET + N_BRANCH * D_MODEL
N_EXPERTS = 64
TOP_K = 6
N_EXPERT_GROUPS = 8
TOPK_GROUPS = 4
D_EXPERT = D_MODEL // 4
D_SHARED = D_MODEL // 4
ROUTED_SCALE = 2.5
N_MOD = 6
EPS = 1e-6

kernel_name = "s5_fnet_moe_prefix_diffusion_step"


def rms_norm(x, g):
    xf = x.astype(jnp.float32)
    y = xf * lax.rsqrt(jnp.mean(xf * xf, axis=-1, keepdims=True) + EPS) * g.astype(jnp.float32)
    return y.astype(x.dtype)


def grid_pos_embed(n_tokens, dtype):
    rows = n_tokens // GRID_W
    r, col = jnp.meshgrid(jnp.arange(rows, dtype=jnp.float32),
                          jnp.arange(GRID_W, dtype=jnp.float32), indexing="ij")
    quarter = D_MODEL // 4
    omega = 1.0 / (10000.0 ** (jnp.arange(quarter, dtype=jnp.float32) / quarter))

    def emb(pos):
        a = pos.reshape(-1)[:, None] * omega
        return jnp.concatenate([jnp.sin(a), jnp.cos(a)], axis=-1)

    return jnp.concatenate([emb(r), emb(col)], axis=-1).astype(dtype)


def adaln(cond, w_ada, b_ada):
    m = jax.nn.silu(cond) @ w_ada + b_ada
    m = m.reshape(cond.shape[:-1] + (N_MOD, D_MODEL))
    return [jnp.expand_dims(m[..., i, :], -2) for i in range(N_MOD)]


def _cmul(ar, ai, br, bi):
    return ar * br - ai * bi, ar * bi + ai * br


def _ssm_combine(e1, e2):
    a1r, a1i, b1r, b1i = e1
    a2r, a2i, b2r, b2i = e2
    ar, ai = _cmul(a1r, a1i, a2r, a2i)
    tr, ti = _cmul(a2r, a2i, b1r, b1i)
    return ar, ai, tr + b2r, ti + b2i


def s5_discretize(lam_re, lam_im, log_dt, b_re, b_im):
    lr = jnp.minimum(lam_re.astype(jnp.float32), -1e-4)
    li = lam_im.astype(jnp.float32)
    dt = jnp.exp(log_dt.astype(jnp.float32))[..., None]
    mag = jnp.exp(lr * dt)
    ar = mag * jnp.cos(li * dt)
    ai = mag * jnp.sin(li * dt)
    den = lr * lr + li * li
    nr = ar - 1.0
    qr = (nr * lr + ai * li) / den
    qi = (ai * lr - nr * li) / den
    br = b_re.astype(jnp.float32)
    bi = b_im.astype(jnp.float32)
    bbr = qr[..., None] * br - qi[..., None] * bi
    bbi = qr[..., None] * bi + qi[..., None] * br
    return ar, ai, bbr, bbi


def s5_scan(abar_r, abar_i, bu_r, bu_i, reverse):
    L = bu_r.shape[1]
    a_r = jnp.broadcast_to(abar_r, (1, L) + abar_r.shape)
    a_i = jnp.broadcast_to(abar_i, (1, L) + abar_i.shape)
    return lax.associative_scan(_ssm_combine, (a_r, a_i, bu_r, bu_i), reverse=reverse, axis=1)


def s5_branch(u, h0_r, h0_i, lam_re, lam_im, log_dt, b_re, b_im, c_re, c_im, d_skip, w_glu):
    Bn, L, _ = u.shape
    uf = u.astype(jnp.float32)
    ug = uf.reshape(Bn, L, S5_GROUPS, S5_GROUP)
    ar, ai, bbr, bbi = s5_discretize(lam_re, lam_im, log_dt, b_re, b_im)
    y = d_skip.astype(jnp.float32) * uf
    fin_r, fin_i = [], []
    for d in range(N_DIR):
        bu_r = jnp.einsum("gph,blgh->blgp", bbr[d], ug)
        bu_i = jnp.einsum("gph,blgh->blgp", bbi[d], ug)
        acr, aci, h_r, h_i = s5_scan(ar[d], ai[d], bu_r, bu_i, reverse=(d == 1))
        if h0_r is None:
            last = -1 if d == 0 else 0
            fin_r.append(h_r[:, last])
            fin_i.append(h_i[:, last])
        else:
            tr, ti = _cmul(acr, aci, h0_r[:, d, None].astype(jnp.float32), h0_i[:, d, None].astype(jnp.float32))
            h_r = h_r + tr
            h_i = h_i + ti
        read = (jnp.einsum("ghp,blgp->blgh", c_re[d].astype(jnp.float32), h_r)
                - jnp.einsum("ghp,blgp->blgh", c_im[d].astype(jnp.float32), h_i))
        y = y + read.reshape(Bn, L, D_S5)
    g = jax.nn.gelu(y)
    out = (g * jax.nn.sigmoid(g @ w_glu.astype(jnp.float32))).astype(u.dtype)
    if h0_r is None:
        return out, jnp.stack(fin_r, axis=1), jnp.stack(fin_i, axis=1)
    return out, None, None


def fourier_branch(z):
    Bn, L, _ = z.shape
    zf = z.astype(jnp.float32).reshape(Bn, L, FNET_GROUPS, FNET_GROUP)
    f = jnp.real(jnp.fft.fft2(zf, axes=(1, 3), norm="ortho"))
    return f.reshape(Bn, L, D_FNET).astype(z.dtype)


def moe(h, w_router, router_bias, w1, w3, w2, ws1, ws3, ws2):
    x = h.reshape(-1, D_MODEL)
    T = x.shape[0]
    scores = jax.nn.sigmoid((x @ w_router).astype(jnp.float32))
    biased = scores + router_bias.astype(jnp.float32)
    grp = biased.reshape(T, N_EXPERT_GROUPS, N_EXPERTS // N_EXPERT_GROUPS)
    grp_score = jnp.sum(lax.top_k(grp, 2)[0], axis=-1)
    _, grp_idx = lax.top_k(grp_score, TOPK_GROUPS)
    grp_mask = jnp.sum(jax.nn.one_hot(grp_idx, N_EXPERT_GROUPS, dtype=jnp.float32), axis=1)
    exp_mask = jnp.repeat(grp_mask, N_EXPERTS // N_EXPERT_GROUPS, axis=1)
    masked = jnp.where(exp_mask > 0, biased, -jnp.inf)
    _, idx = lax.top_k(masked, TOP_K)
    w = jnp.take_along_axis(scores, idx, axis=-1)
    w = w / jnp.sum(w, axis=-1, keepdims=True) * ROUTED_SCALE
    gates = jnp.sum(jax.nn.one_hot(idx, N_EXPERTS, dtype=jnp.float32) * w[..., None], axis=1)
    a = jnp.einsum("td,edf->tef", x, w1)
    b = jnp.einsum("td,edf->tef", x, w3)
    act = jax.nn.silu(a) * b * gates[:, :, None].astype(x.dtype)
    routed = jnp.einsum("tef,efd->td", act, w2)
    shared = (jax.nn.silu(x @ ws1) * (x @ ws3)) @ ws2
    return (routed + shared).reshape(h.shape).astype(h.dtype)


def trunk_layer(x, cond, h0_r, h0_i, p):
    sh1, sc1, g1, sh2, sc2, g2 = adaln(cond, p["w_ada"], p["b_ada"])
    h = rms_norm(x, p["norm1_g"]) * (1.0 + sc1) + sh1
    proj = h @ p["w_in"]
    u_s5 = proj[..., :D_S5]
    u_f = proj[..., D_S5:D_S5 + D_FNET]
    gl = jax.nn.sigmoid(proj[..., D_S5 + D_FNET:].astype(jnp.float32))
    gl = gl.reshape(gl.shape[:-1] + (N_BRANCH, D_MODEL))
    y_s5, st_r, st_i = s5_branch(u_s5, h0_r, h0_i, p["lam_re"], p["lam_im"], p["log_dt"],
                                 p["b_re"], p["b_im"], p["c_re"], p["c_im"], p["d_skip"], p["w_glu"])
    y_f = fourier_branch(u_f)
    merged = gl[..., 0, :] * (y_s5 @ p["w_proj_s5"]) + gl[..., 1, :] * (y_f @ p["w_proj_f"])
    x = x + g1 * (merged.astype(x.dtype) @ p["w_out"])
    h = rms_norm(x, p["norm2_g"]) * (1.0 + sc2) + sh2
    x = x + g2 * moe(h, p["w_router"], p["router_bias"], p["w1"], p["w3"], p["w2"],
                     p["ws1"], p["ws3"], p["ws2"])
    return x, st_r, st_i


def setup_inputs(seed: int = 0) -> dict:
    key = jax.random.key(seed)
    ks = jax.random.split(key, 32)
    f32 = jnp.float32
    nrm = lambda k, s, sc: jax.random.normal(k, s, f32) * sc
    n_idx = jnp.arange(S5_STATE, dtype=f32)
    lam_shape = (DEPTH, N_DIR, S5_GROUPS, S5_STATE)
    return {
        "x_prompt": nrm(ks[0], (BATCH, SEQ, D_MODEL), 1.0),
        "x_sample": nrm(ks[1], (DEC_BATCH, DEC_SEQ, D_MODEL), 1.0),
        "state_s5_re": nrm(ks[2], (DEC_BATCH, DEPTH, N_DIR, S5_GROUPS, S5_STATE), 0.5),
        "state_s5_im": nrm(ks[3], (DEC_BATCH, DEPTH, N_DIR, S5_GROUPS, S5_STATE), 0.5),
        "c": nrm(ks[4], (DEC_BATCH, D_MODEL), 1.0),
        "c_ctx": nrm(ks[5], (D_MODEL,), 1.0),
        "w_ada": nrm(ks[6], (DEPTH, D_MODEL, N_MOD * D_MODEL), 0.5 * D_MODEL ** -0.5),
        "b_ada": nrm(ks[7], (DEPTH, N_MOD * D_MODEL), 0.02),
        "norm1_g": 1.0 + nrm(ks[8], (DEPTH, D_MODEL), 0.02),
        "norm2_g": 1.0 + nrm(ks[9], (DEPTH, D_MODEL), 0.02),
        "w_in": nrm(ks[10], (DEPTH, D_MODEL, D_IN), D_MODEL ** -0.5),
        "lam_re": -0.5 + nrm(ks[11], lam_shape, 0.01),
        "lam_im": math.pi * n_idx + nrm(ks[12], lam_shape, 0.01),
        "log_dt": jax.random.uniform(ks[13], (DEPTH, N_DIR, S5_GROUPS), f32,
                                     math.log(DT_MIN), math.log(DT_MAX)),
        "b_re": nrm(ks[14], (DEPTH, N_DIR, S5_GROUPS, S5_STATE, S5_GROUP), (2 * S5_GROUP) ** -0.5),
        "b_im": nrm(ks[15], (DEPTH, N_DIR, S5_GROUPS, S5_STATE, S5_GROUP), (2 * S5_GROUP) ** -0.5),
        "c_re": nrm(ks[16], (DEPTH, N_DIR, S5_GROUPS, S5_GROUP, S5_STATE), S5_STATE ** -0.5),
        "c_im": nrm(ks[17], (DEPTH, N_DIR, S5_GROUPS, S5_GROUP, S5_STATE), S5_STATE ** -0.5),
        "d_skip": nrm(ks[18], (DEPTH, D_S5), 1.0),
        "w_glu": nrm(ks[19], (DEPTH, D_S5, D_S5), D_S5 ** -0.5),
        "w_proj_s5": nrm(ks[20], (DEPTH, D_S5, D_MODEL), D_S5 ** -0.5),
        "w_proj_f": nrm(ks[21], (DEPTH, D_FNET, D_MODEL), D_FNET ** -0.5),
        "w_out": nrm(ks[22], (DEPTH, D_MODEL, D_MODEL), D_MODEL ** -0.5),
        "w_router": nrm(ks[23], (DEPTH, D_MODEL, N_EXPERTS), D_MODEL ** -0.5),
        "router_bias": nrm(ks[24], (DEPTH, N_EXPERTS), 0.01),
        "w1": nrm(ks[25], (DEPTH, N_EXPERTS, D_MODEL, D_EXPERT), D_MODEL ** -0.5),
        "w3": nrm(ks[26], (DEPTH, N_EXPERTS, D_MODEL, D_EXPERT), D_MODEL ** -0.5),
        "w2": nrm(ks[27], (DEPTH, N_EXPERTS, D_EXPERT, D_MODEL), D_EXPERT ** -0.5),
        "ws1": nrm(ks[28], (DEPTH, D_MODEL, D_SHARED), D_MODEL ** -0.5),
        "ws3": nrm(ks[29], (DEPTH, D_MODEL, D_SHARED), D_MODEL ** -0.5),
        "ws2": nrm(ks[30], (DEPTH, D_SHARED, D_MODEL), D_SHARED ** -0.5),
        "final_norm_g": 1.0 + nrm(ks[31], (D_MODEL,), 0.02),
    }


def reference(x_prompt, x_sample, state_s5_re, state_s5_im, c, c_ctx, w_ada, b_ada, norm1_g, norm2_g,
              w_in, lam_re, lam_im, log_dt, b_re, b_im, c_re, c_im, d_skip, w_glu, w_proj_s5, w_proj_f,
              w_out, w_router, router_bias, w1, w3, w2, ws1, ws3, ws2, final_norm_g):
    ctx = x_prompt
    lat = x_sample + grid_pos_embed(x_sample.shape[1], x_sample.dtype)[None]
    new_re, new_im = [], []
    for l in range(DEPTH):
        p = dict(w_ada=w_ada[l], b_ada=b_ada[l], norm1_g=norm1_g[l], norm2_g=norm2_g[l], w_in=w_in[l],
                 lam_re=lam_re[l], lam_im=lam_im[l], log_dt=log_dt[l], b_re=b_re[l], b_im=b_im[l],
                 c_re=c_re[l], c_im=c_im[l], d_skip=d_skip[l], w_glu=w_glu[l], w_proj_s5=w_proj_s5[l],
                 w_proj_f=w_proj_f[l], w_out=w_out[l], w_router=w_router[l], router_bias=router_bias[l],
                 w1=w1[l], w3=w3[l], w2=w2[l], ws1=ws1[l], ws3=ws3[l], ws2=ws2[l])
        ctx, st_r, st_i = trunk_layer(ctx, c_ctx, None, None, p)
        lat, _, _ = trunk_layer(lat, c, state_s5_re[:, l], state_s5_im[:, l], p)
        new_re.append(st_r)
        new_im.append(st_i)
    y_prompt = rms_norm(ctx, final_norm_g)
    y_sample = rms_norm(lat, final_norm_g)
    new_state_re = jnp.stack(new_re, axis=1).astype(x_prompt.dtype)
    new_state_im = jnp.stack(new_im, axis=1).astype(x_prompt.dtype)
    return (y_prompt, y_sample, new_state_re, new_state_im)
```

```python
import functools
import math

import jax
import jax.numpy as jnp
from jax import lax
from jax.experimental import pallas as pl
from jax.experimental.pallas import tpu as pltpu

D_MODEL = 1024
GRID_W = 64
D_S5 = 768
S5_GROUP = 16
S5_GROUPS = 48
S5_STATE = 64
D_FNET = 256
FNET_GROUP = 64
N_EXPERTS = 64
TOP_K = 6
N_EXPERT_GROUPS = 8
EXPERTS_PER_GROUP = N_EXPERTS // N_EXPERT_GROUPS
TOPK_GROUPS = 4
D_EXPERT = 256
ROUTED_SCALE = 2.5
N_MOD = 6
EPS = 1e-6

S5_CHUNK = 16
S5_ROW = S5_CHUNK * S5_GROUP
MOD_ROWS = 8
VMEM_LIMIT = 56 * 1024 * 1024

BF16 = jnp.bfloat16
F32 = jnp.float32


def _params(sem, vmem=VMEM_LIMIT):
    return pltpu.CompilerParams(dimension_semantics=sem, vmem_limit_bytes=vmem)


def _rms(x, g):
    return x * lax.rsqrt(jnp.mean(x * x, axis=-1, keepdims=True) + EPS) * g


def _adaln_kernel(c_ref, w_ref, b_ref, o_ref):
    c = c_ref[...]
    o_ref[...] = jnp.dot(c * jax.nn.sigmoid(c), w_ref[...], precision=lax.Precision.HIGHEST,
                         preferred_element_type=F32) + b_ref[...]


def _adaln(cond, w_ada, b_ada):
    n_out = N_MOD * D_MODEL
    return pl.pallas_call(
        _adaln_kernel,
        out_shape=jax.ShapeDtypeStruct((MOD_ROWS, n_out), F32),
        grid=(N_MOD,),
        in_specs=[pl.BlockSpec((MOD_ROWS, D_MODEL), lambda i: (0, 0)),
                  pl.BlockSpec((D_MODEL, D_MODEL), lambda i: (0, i)),
                  pl.BlockSpec((1, D_MODEL), lambda i: (0, i))],
        out_specs=pl.BlockSpec((MOD_ROWS, D_MODEL), lambda i: (0, i)),
        compiler_params=_params(("arbitrary",)),
        name="adaln",
    )(cond, w_ada, b_ada.reshape(1, n_out))


def _inproj_kernel(has_pos, *refs):
    if has_pos:
        x_ref, pos_ref, mod_ref, g_ref, w_ref, us_ref, uf_ref, gt_ref = refs
        x = x_ref[...] + pos_ref[...]
    else:
        x_ref, mod_ref, g_ref, w_ref, us_ref, uf_ref, gt_ref = refs
        x = x_ref[...]
    m = mod_ref[0]
    h = _rms(x, g_ref[...]) * (1.0 + m[1:2, :]) + m[0:1, :]
    p = jnp.dot(h.astype(BF16), w_ref[...], preferred_element_type=F32)
    us_ref[...] = p[:, :D_S5].astype(BF16)
    uf_ref[...] = p[:, D_S5:D_MODEL].astype(BF16)
    gt_ref[...] = jax.nn.sigmoid(p[:, D_MODEL:]).astype(BF16)


def _inproj(x, pos, mod, mod_row, norm_g, w_in_bf, tm):
    n = x.shape[0]
    has_pos = pos is not None
    row = lambda i: (i, 0)
    in_specs = [pl.BlockSpec((tm, D_MODEL), row)]
    args = [x]
    if has_pos:
        nper = pos.shape[0] // tm
        in_specs.append(pl.BlockSpec((tm, D_MODEL), lambda i: (i % nper, 0)))
        args.append(pos)
    in_specs += [pl.BlockSpec((1, N_MOD, D_MODEL), lambda i: (mod_row(i, tm), 0, 0)),
                 pl.BlockSpec((1, D_MODEL), lambda i: (0, 0)),
                 pl.BlockSpec(w_in_bf.shape, lambda i: (0, 0))]
    args += [mod, norm_g, w_in_bf]
    return pl.pallas_call(
        functools.partial(_inproj_kernel, has_pos),
        out_shape=(jax.ShapeDtypeStruct((n, D_S5), BF16),
                   jax.ShapeDtypeStruct((n, D_FNET), BF16),
                   jax.ShapeDtypeStruct((n, 2 * D_MODEL), BF16)),
        grid=(n // tm,),
        in_specs=in_specs,
        out_specs=(pl.BlockSpec((tm, D_S5), row), pl.BlockSpec((tm, D_FNET), row),
                   pl.BlockSpec((tm, 2 * D_MODEL), row)),
        compiler_params=_params(("arbitrary",)),
        name="inproj",
    )(*args)


def _s5_operators(lam_re, lam_im, log_dt, b_re, b_im, c_re, c_im, d_skip):
    hi = lax.Precision.HIGHEST
    lr = jnp.minimum(lam_re.astype(F32), -1e-4)
    li = lam_im.astype(F32)
    dt = jnp.exp(log_dt.astype(F32))[..., None]
    mag = jnp.exp(lr * dt)
    ar = mag * jnp.cos(li * dt)
    ai = mag * jnp.sin(li * dt)
    den = lr * lr + li * li
    nr = ar - 1.0
    qr = (nr * lr + ai * li) / den
    qi = (ai * lr - nr * li) / den
    br = b_re.astype(F32)
    bi = b_im.astype(F32)
    bbr = qr[..., None] * br - qi[..., None] * bi
    bbi = qr[..., None] * bi + qi[..., None] * br
    n = jnp.arange(S5_CHUNK + 1, dtype=F32)[:, None, None, None]
    pmag = jnp.exp(n * (lr * dt))
    pr = pmag * jnp.cos(n * (li * dt))
    pi = pmag * jnp.sin(n * (li * dt))
    cr = c_re.astype(F32)
    ci = c_im.astype(F32)
    car = cr[None] * pr[..., None, :] - ci[None] * pi[..., None, :]
    cai = cr[None] * pi[..., None, :] + ci[None] * pr[..., None, :]
    kk = (jnp.einsum("ndghp,dgpi->ndghi", car[:S5_CHUNK], bbr, precision=hi)
          - jnp.einsum("ndghp,dgpi->ndghi", cai[:S5_CHUNK], bbi, precision=hi))
    t = jnp.arange(S5_CHUNK)
    diff = t[None, :] - t[:, None]
    kf = kk[:, 0]
    kb = kk[:, 1]
    mf = jnp.where((diff >= 0)[:, :, None, None, None], kf[jnp.clip(diff, 0, S5_CHUNK - 1)], 0.0)
    mb = jnp.where((diff <= 0)[:, :, None, None, None], kb[jnp.clip(-diff, 0, S5_CHUNK - 1)], 0.0)
    m = (mf + mb).transpose(2, 0, 4, 1, 3).reshape(S5_GROUPS, S5_ROW, S5_ROW)
    dvec = jnp.tile(d_skip.astype(F32).reshape(S5_GROUPS, S5_GROUP), (1, S5_CHUNK))
    m = m + jnp.eye(S5_ROW, dtype=F32)[None] * dvec[:, None, :]

    def w_in_part(pw_r, pw_i, d):
        re = pw_r[..., None] * bbr[d][None] - pw_i[..., None] * bbi[d][None]
        im = pw_r[..., None] * bbi[d][None] + pw_i[..., None] * bbr[d][None]
        f = lambda a: a.transpose(1, 0, 3, 2).reshape(S5_GROUPS, S5_ROW, S5_STATE)
        return f(re), f(im)

    wf_re, wf_im = w_in_part(pr[S5_CHUNK - 1::-1, 0], pi[S5_CHUNK - 1::-1, 0], 0)
    wb_re, wb_im = w_in_part(pr[:S5_CHUNK, 1], pi[:S5_CHUNK, 1], 1)
    w_in = jnp.concatenate([wf_re, wb_re, wf_im, wb_im], axis=-1)

    def w_out_part(ca):
        return ca.transpose(1, 3, 0, 2).reshape(S5_GROUPS, S5_STATE, S5_ROW)

    w_out = jnp.concatenate([w_out_part(car[1:, 0]), w_out_part(car[S5_CHUNK:0:-1, 1]),
                             -w_out_part(cai[1:, 0]), -w_out_part(cai[S5_CHUNK:0:-1, 1])], axis=1)
    coef = jnp.stack([jnp.concatenate([pr[S5_CHUNK, 0], pr[S5_CHUNK, 1]], axis=-1),
                      jnp.concatenate([pi[S5_CHUNK, 0], pi[S5_CHUNK, 1]], axis=-1)], axis=1)
    return m.astype(BF16), w_in.astype(BF16), w_out.astype(BF16), coef


def _s5_kernel(nb, nk, u_ref, m_ref, wi_ref, wo_ref, coef_ref, h0_ref, y_ref, fin_ref, s_ref, e_ref):
    half = 2 * S5_STATE
    u = u_ref[0]
    s_ref[...] = jnp.dot(u, wi_ref[0], preferred_element_type=F32)
    c_r = coef_ref[0, 0:1, :]
    c_i = coef_ref[0, 1:2, :]
    e_r = h0_ref[0, :, :half]
    e_i = h0_ref[0, :, half:]
    is_fwd = lax.broadcasted_iota(jnp.int32, (nb, half), 1) < S5_STATE
    for j in range(nk):
        rf = slice(j * nb, (j + 1) * nb)
        rb = slice((nk - 1 - j) * nb, (nk - j) * nb)
        e_ref[rf, 0:S5_STATE] = e_r[:, :S5_STATE]
        e_ref[rb, S5_STATE:half] = e_r[:, S5_STATE:]
        e_ref[rf, half:half + S5_STATE] = e_i[:, :S5_STATE]
        e_ref[rb, half + S5_STATE:] = e_i[:, S5_STATE:]
        s_r = jnp.where(is_fwd, s_ref[rf, :half], s_ref[rb, :half])
        s_i = jnp.where(is_fwd, s_ref[rf, half:], s_ref[rb, half:])
        e_r, e_i = c_r * e_r - c_i * e_i + s_r, c_r * e_i + c_i * e_r + s_i
    fin_ref[0, :, :half] = e_r
    fin_ref[0, :, half:] = e_i
    y = jnp.dot(u, m_ref[0], preferred_element_type=F32)
    y = y + jnp.dot(e_ref[...].astype(BF16), wo_ref[0], preferred_element_type=F32)
    y_ref[0] = y.astype(BF16)


def _s5(u_rows, ops, h0, nb, nk):
    m, w_in, w_out, coef = ops
    rows = nb * nk
    g3 = lambda g: (g, 0, 0)
    return pl.pallas_call(
        functools.partial(_s5_kernel, nb, nk),
        out_shape=(jax.ShapeDtypeStruct((S5_GROUPS, rows, S5_ROW), BF16),
                   jax.ShapeDtypeStruct((S5_GROUPS, nb, 4 * S5_STATE), F32)),
        grid=(S5_GROUPS,),
        in_specs=[pl.BlockSpec((1, rows, S5_ROW), g3),
                  pl.BlockSpec((1, S5_ROW, S5_ROW), g3),
                  pl.BlockSpec((1, S5_ROW, 4 * S5_STATE), g3),
                  pl.BlockSpec((1, 4 * S5_STATE, S5_ROW), g3),
                  pl.BlockSpec((1, 2, 2 * S5_STATE), g3),
                  pl.BlockSpec((1, nb, 4 * S5_STATE), g3)],
        out_specs=(pl.BlockSpec((1, rows, S5_ROW), g3),
                   pl.BlockSpec((1, nb, 4 * S5_STATE), g3)),
        scratch_shapes=[pltpu.VMEM((rows, 4 * S5_STATE), F32),
                        pltpu.VMEM((rows, 4 * S5_STATE), F32)],
        compiler_params=_params(("arbitrary",)),
        name="s5",
    )(u_rows, m, w_in, w_out, coef, h0)


def _dft_tables(seq):
    j = jnp.arange(seq, dtype=jnp.int32)
    ang = (2.0 * math.pi / seq) * ((j[:, None] * j[None, :]) % seq).astype(F32)
    cs = jnp.concatenate([jnp.cos(ang), -jnp.sin(ang)], axis=1).astype(BF16)
    c = jnp.arange(D_FNET, dtype=jnp.int32)
    same = (c[:, None] // FNET_GROUP) == (c[None, :] // FNET_GROUP)
    angc = (2.0 * math.pi / FNET_GROUP) * (((c[:, None] % FNET_GROUP) * (c[None, :] % FNET_GROUP))
                                           % FNET_GROUP).astype(F32)
    scale = 1.0 / math.sqrt(seq * FNET_GROUP)
    bdc = jnp.where(same, jnp.cos(angc) * scale, 0.0).astype(BF16)
    bds = jnp.where(same, jnp.sin(angc) * scale, 0.0).astype(BF16)
    return cs, bdc, bds


def _fourier_kernel(seq, z_ref, cs_ref, bdc_ref, bds_ref, o_ref, zz_ref):
    @pl.when(pl.program_id(1) == 0)
    def _():
        z = z_ref[...]
        zz_ref[0:seq, :] = jnp.dot(z, bdc_ref[...], preferred_element_type=F32).astype(BF16)
        zz_ref[seq:, :] = jnp.dot(z, bds_ref[...], preferred_element_type=F32).astype(BF16)

    o_ref[...] = jnp.dot(cs_ref[...], zz_ref[...], preferred_element_type=F32).astype(BF16)


def _fourier(z, nb, seq, tl):
    cs, bdc, bds = _dft_tables(seq)
    nt = seq // tl
    return pl.pallas_call(
        functools.partial(_fourier_kernel, seq),
        out_shape=jax.ShapeDtypeStruct(z.shape, BF16),
        grid=(nb, nt),
        in_specs=[pl.BlockSpec((seq, D_FNET), lambda b, i: (b, 0)),
                  pl.BlockSpec((tl, 2 * seq), lambda b, i: (i, 0)),
                  pl.BlockSpec((D_FNET, D_FNET), lambda b, i: (0, 0)),
                  pl.BlockSpec((D_FNET, D_FNET), lambda b, i: (0, 0))],
        out_specs=pl.BlockSpec((tl, D_FNET), lambda b, i: (b * nt + i, 0)),
        scratch_shapes=[pltpu.VMEM((2 * seq, D_FNET), BF16)],
        compiler_params=_params(("arbitrary", "arbitrary")),
        name="fourier",
    )(z, cs, bdc, bds)


def _first_argmax_mask(v, iota, size):
    m = jnp.max(v, axis=0, keepdims=True)
    first = jnp.min(jnp.where(v == m, iota, size), axis=0, keepdims=True)
    return iota == first


def _route(logits_t, bias_col):
    tm = logits_t.shape[1]
    neg = -jnp.inf
    s = jax.nn.sigmoid(logits_t)
    biased = s + bias_col
    io8 = lax.broadcasted_iota(jnp.int32, (EXPERTS_PER_GROUP, tm), 0)
    gs_rows = []
    for g in range(N_EXPERT_GROUPS):
        blk = biased[g * EXPERTS_PER_GROUP:(g + 1) * EXPERTS_PER_GROUP, :]
        m1 = jnp.max(blk, axis=0, keepdims=True)
        rest = jnp.where(_first_argmax_mask(blk, io8, EXPERTS_PER_GROUP), neg, blk)
        gs_rows.append(m1 + jnp.max(rest, axis=0, keepdims=True))
    gs = jnp.concatenate(gs_rows, axis=0)
    iog = lax.broadcasted_iota(jnp.int32, (N_EXPERT_GROUPS, tm), 0)
    gsel = jnp.zeros((N_EXPERT_GROUPS, tm), F32)
    for _ in range(TOPK_GROUPS):
        sel = _first_argmax_mask(gs, iog, N_EXPERT_GROUPS)
        gsel = jnp.where(sel, 1.0, gsel)
        gs = jnp.where(sel, neg, gs)
    emask = jnp.concatenate(
        [jnp.broadcast_to(gsel[g:g + 1, :], (EXPERTS_PER_GROUP, tm)) for g in range(N_EXPERT_GROUPS)], axis=0)
    v = jnp.where(emask > 0.0, biased, neg)
    ioe = lax.broadcasted_iota(jnp.int32, (N_EXPERTS, tm), 0)
    chosen = jnp.zeros((N_EXPERTS, tm), F32)
    for _ in range(TOP_K):
        sel = _first_argmax_mask(v, ioe, N_EXPERTS)
        chosen = jnp.where(sel, 1.0, chosen)
        v = jnp.where(sel, neg, v)
    w = jnp.where(chosen > 0.0, s, 0.0)
    return w / jnp.sum(w, axis=0, keepdims=True) * ROUTED_SCALE


def _merge_kernel(has_pos, *refs):
    if has_pos:
        (ys_ref, yf_ref, gt_ref, x_ref, pos_ref, mod_ref, n2_ref, wglu_ref, wps_ref, wpf_ref, wout_ref,
         wrt_ref, rb_ref, ws1_ref, ws3_ref, ws2_ref, xs_ref, h2_ref, gates_ref) = refs
        x = x_ref[...] + pos_ref[...]
    else:
        (ys_ref, yf_ref, gt_ref, x_ref, mod_ref, n2_ref, wglu_ref, wps_ref, wpf_ref, wout_ref,
         wrt_ref, rb_ref, ws1_ref, ws3_ref, ws2_ref, xs_ref, h2_ref, gates_ref) = refs
        x = x_ref[...]
    m = mod_ref[0]
    g = jax.nn.gelu(ys_ref[...].astype(F32))
    a = g * jax.nn.sigmoid(jnp.dot(g.astype(BF16), wglu_ref[...], preferred_element_type=F32))
    pa = jnp.dot(a.astype(BF16), wps_ref[...], preferred_element_type=F32)
    pb = jnp.dot(yf_ref[...], wpf_ref[...], preferred_element_type=F32)
    gt = gt_ref[...].astype(F32)
    merged = gt[:, :D_MODEL] * pa + gt[:, D_MODEL:] * pb
    x1 = x + m[2:3, :] * jnp.dot(merged.astype(BF16), wout_ref[...], preferred_element_type=F32)
    h2 = _rms(x1, n2_ref[...]) * (1.0 + m[4:5, :]) + m[3:4, :]
    hb = h2.astype(BF16)
    h2_ref[...] = hb
    logits_t = lax.dot_general(wrt_ref[...], h2, (((1,), (1,)), ((), ())),
                               precision=lax.Precision.HIGHEST, preferred_element_type=F32)
    gates_ref[...] = _route(logits_t, rb_ref[...])
    s1 = jnp.dot(hb, ws1_ref[...], preferred_element_type=F32)
    s3 = jnp.dot(hb, ws3_ref[...], preferred_element_type=F32)
    shared = jnp.dot((s1 * jax.nn.sigmoid(s1) * s3).astype(BF16), ws2_ref[...], preferred_element_type=F32)
    xs_ref[...] = x1 + m[5:6, :] * shared


def _merge(ys, yf, gt, x, pos, mod, mod_row, n2, weights, tm):
    n = x.shape[0]
    has_pos = pos is not None
    row = lambda i: (i, 0)
    const = lambda a: pl.BlockSpec(a.shape, lambda i: (0,) * a.ndim)
    in_specs = [pl.BlockSpec((tm, D_S5), row), pl.BlockSpec((tm, D_FNET), row),
                pl.BlockSpec((tm, 2 * D_MODEL), row), pl.BlockSpec((tm, D_MODEL), row)]
    args = [ys, yf, gt, x]
    if has_pos:
        nper = pos.shape[0] // tm
        in_specs.append(pl.BlockSpec((tm, D_MODEL), lambda i: (i % nper, 0)))
        args.append(pos)
    in_specs += [pl.BlockSpec((1, N_MOD, D_MODEL), lambda i: (mod_row(i, tm), 0, 0)), const(n2)]
    args += [mod, n2]
    in_specs += [const(w) for w in weights]
    args += list(weights)
    return pl.pallas_call(
        functools.partial(_merge_kernel, has_pos),
        out_shape=(jax.ShapeDtypeStruct((n, D_MODEL), F32),
                   jax.ShapeDtypeStruct((n, D_MODEL), BF16),
                   jax.ShapeDtypeStruct((N_EXPERTS, n), F32)),
        grid=(n // tm,),
        in_specs=in_specs,
        out_specs=(pl.BlockSpec((tm, D_MODEL), row), pl.BlockSpec((tm, D_MODEL), row),
                   pl.BlockSpec((N_EXPERTS, tm), lambda i: (0, i))),
        compiler_params=_params(("arbitrary",)),
        name="merge",
    )(*args)


def _moe_kernel(h_ref, g_ref, w1_ref, w3_ref, w2_ref, xs_ref, mod_ref, fg_ref, o_ref, acc_ref):
    e = pl.program_id(1)

    @pl.when(e == 0)
    def _():
        acc_ref[...] = jnp.zeros_like(acc_ref)

    hb = h_ref[...]
    a = jnp.dot(hb, w1_ref[0].astype(BF16), preferred_element_type=F32)
    b = jnp.dot(hb, w3_ref[0].astype(BF16), preferred_element_type=F32)
    gm = g_ref[...]
    lane = lax.broadcasted_iota(jnp.int32, gm.shape, 1)
    gcol = jnp.sum(jnp.where(lane == e, gm, 0.0), axis=1, keepdims=True)
    act = a * jax.nn.sigmoid(a) * b * gcol
    acc_ref[...] += jnp.dot(act.astype(BF16), w2_ref[0].astype(BF16), preferred_element_type=F32)

    @pl.when(e == N_EXPERTS - 1)
    def _():
        x2 = xs_ref[...] + mod_ref[0][5:6, :] * acc_ref[...]
        o_ref[...] = _rms(x2, fg_ref[...])


def _moe(h2, gates, w1, w3, w2, xs, mod, mod_row, fg, tm):
    n = h2.shape[0]
    row = lambda i, e: (i, 0)
    return pl.pallas_call(
        _moe_kernel,
        out_shape=jax.ShapeDtypeStruct((n, D_MODEL), F32),
        grid=(n // tm, N_EXPERTS),
        in_specs=[pl.BlockSpec((tm, D_MODEL), row),
                  pl.BlockSpec((tm, N_EXPERTS), row),
                  pl.BlockSpec((1, D_MODEL, D_EXPERT), lambda i, e: (e, 0, 0)),
                  pl.BlockSpec((1, D_MODEL, D_EXPERT), lambda i, e: (e, 0, 0)),
                  pl.BlockSpec((1, D_EXPERT, D_MODEL), lambda i, e: (e, 0, 0)),
                  pl.BlockSpec((tm, D_MODEL), row),
                  pl.BlockSpec((1, N_MOD, D_MODEL), lambda i, e: (mod_row(i, tm), 0, 0)),
                  pl.BlockSpec((1, D_MODEL), lambda i, e: (0, 0))],
        out_specs=pl.BlockSpec((tm, D_MODEL), row),
        scratch_shapes=[pltpu.VMEM((tm, D_MODEL), F32)],
        compiler_params=_params(("arbitrary", "arbitrary")),
        name="moe",
    )(h2, gates, w1, w3, w2, xs, mod, fg)


def _grid_pos_embed(n_tokens):
    rows = n_tokens // GRID_W
    r, col = jnp.meshgrid(jnp.arange(rows, dtype=F32), jnp.arange(GRID_W, dtype=F32), indexing="ij")
    quarter = D_MODEL // 4
    omega = 1.0 / (10000.0 ** (jnp.arange(quarter, dtype=F32) / quarter))

    def emb(p):
        a = p.reshape(-1)[:, None] * omega
        return jnp.concatenate([jnp.sin(a), jnp.cos(a)], axis=-1)

    return jnp.concatenate([emb(r), emb(col)], axis=-1)


def _stream(x3, pos, mod, first_row, h0, s5_ops, p):
    nb, seq, _ = x3.shape
    n = nb * seq
    nk = seq // S5_CHUNK
    x = x3.reshape(n, D_MODEL)
    per_seq_mod = first_row > 0

    def mod_row(i, tm):
        return first_row + (i * tm) // seq if per_seq_mod else 0

    us, uf, gt = _inproj(x, pos, mod, mod_row, p["norm1_g"], p["w_in"], 512)
    u_rows = (us.reshape(nb, nk, S5_CHUNK, S5_GROUPS, S5_GROUP).transpose(3, 1, 0, 2, 4)
              .reshape(S5_GROUPS, nk * nb, S5_ROW))
    y_rows, fin = _s5(u_rows, s5_ops, h0, nb, nk)
    ys = (y_rows.reshape(S5_GROUPS, nk, nb, S5_CHUNK, S5_GROUP).transpose(2, 1, 3, 0, 4)
          .reshape(n, D_S5))
    yf = _fourier(uf, nb, seq, min(seq, 512))
    xs, h2, gates_t = _merge(ys, yf, gt, x, pos, mod, mod_row, p["norm2_g"], p["merge_w"], 256)
    out = _moe(h2, gates_t.T, p["w1"], p["w3"], p["w2"], xs, mod, mod_row, p["final_g"], 1024)
    return out.reshape(nb, seq, D_MODEL), fin


def kernel(x_prompt, x_sample, state_s5_re, state_s5_im, c, c_ctx, w_ada, b_ada, norm1_g, norm2_g, w_in,
           lam_re, lam_im, log_dt, b_re, b_im, c_re, c_im, d_skip, w_glu, w_proj_s5, w_proj_f, w_out,
           w_router, router_bias, w1, w3, w2, ws1, ws3, ws2, final_norm_g):
    nb_ctx = x_prompt.shape[0]
    nb_lat, seq_lat, _ = x_sample.shape
    half = 2 * S5_STATE

    cond = jnp.concatenate([c_ctx[None], c, jnp.zeros((MOD_ROWS - 1 - nb_lat, D_MODEL), F32)], axis=0)
    mod = _adaln(cond, w_ada[0], b_ada[0]).reshape(MOD_ROWS, N_MOD, D_MODEL)

    s5_ops = _s5_operators(lam_re[0], lam_im[0], log_dt[0], b_re[0], b_im[0], c_re[0], c_im[0], d_skip[0])
    p = dict(
        norm1_g=norm1_g[0][None], norm2_g=norm2_g[0][None], final_g=final_norm_g[None],
        w_in=w_in[0].astype(BF16), w1=w1[0], w3=w3[0], w2=w2[0],
        merge_w=(w_glu[0].astype(BF16), w_proj_s5[0].astype(BF16), w_proj_f[0].astype(BF16),
                 w_out[0].astype(BF16), w_router[0].T, router_bias[0][:, None],
                 ws1[0].astype(BF16), ws3[0].astype(BF16), ws2[0].astype(BF16)))

    def pack_state(sr, si):
        f = lambda a: a.astype(F32).transpose(2, 0, 1, 3).reshape(S5_GROUPS, a.shape[0], half)
        return jnp.concatenate([f(sr), f(si)], axis=-1)

    def unpack_state(fin, lo):
        nb = fin.shape[1]
        return fin[..., lo:lo + half].reshape(S5_GROUPS, nb, 2, S5_STATE).transpose(1, 2, 0, 3)[:, None]

    h0_ctx = jnp.zeros((S5_GROUPS, nb_ctx, 2 * half), F32)
    y_prompt, fin = _stream(x_prompt, None, mod, 0, h0_ctx, s5_ops, p)
    h0_lat = pack_state(state_s5_re[:, 0], state_s5_im[:, 0])
    y_sample, _ = _stream(x_sample, _grid_pos_embed(seq_lat), mod, 1, h0_lat, s5_ops, p)
    return (y_prompt, y_sample, unpack_state(fin, 0).astype(x_prompt.dtype),
            unpack_state(fin, half).astype(x_prompt.dtype))
```

```python
import functools
import math

import jax
import jax.numpy as jnp
from jax import lax
from jax.experimental import pallas as pl
from jax.experimental.pallas import tpu as pltpu

D_MODEL = 1024
GRID_W = 64
D_S5 = 768
S5_GROUP = 16
S5_GROUPS = 48
S5_STATE = 64
D_FNET = 256
FNET_GROUP = 64
N_EXPERTS = 64
TOP_K = 6
N_EXPERT_GROUPS = 8
EXPERTS_PER_GROUP = N_EXPERTS // N_EXPERT_GROUPS
TOPK_GROUPS = 4
D_EXPERT = 256
ROUTED_SCALE = 2.5
N_MOD = 6
EPS = 1e-6

S5_CHUNK = 16
S5_ROW = S5_CHUNK * S5_GROUP
MOD_ROWS = 8
ROUTE_ROWS = 8
VMEM_LIMIT = 56 * 1024 * 1024

BF16 = jnp.bfloat16
F32 = jnp.float32


def _params(sem, vmem=VMEM_LIMIT):
    return pltpu.CompilerParams(dimension_semantics=sem, vmem_limit_bytes=vmem)


def _rms(x, g):
    return x * lax.rsqrt(jnp.mean(x * x, axis=-1, keepdims=True) + EPS) * g


def _adaln_kernel(c_ref, w_ref, b_ref, o_ref):
    c = c_ref[...]
    o_ref[...] = jnp.dot(c * jax.nn.sigmoid(c), w_ref[...], precision=lax.Precision.HIGHEST,
                         preferred_element_type=F32) + b_ref[...]


def _adaln(cond, w_ada, b_ada):
    n_out = N_MOD * D_MODEL
    return pl.pallas_call(
        _adaln_kernel,
        out_shape=jax.ShapeDtypeStruct((MOD_ROWS, n_out), F32),
        grid=(N_MOD,),
        in_specs=[pl.BlockSpec((MOD_ROWS, D_MODEL), lambda i: (0, 0)),
                  pl.BlockSpec((D_MODEL, D_MODEL), lambda i: (0, i)),
                  pl.BlockSpec((1, D_MODEL), lambda i: (0, i))],
        out_specs=pl.BlockSpec((MOD_ROWS, D_MODEL), lambda i: (0, i)),
        compiler_params=_params(("arbitrary",)),
        name="adaln",
    )(cond, w_ada, b_ada.reshape(1, n_out))


def _inproj_kernel(has_pos, *refs):
    if has_pos:
        x_ref, pos_ref, mod_ref, g_ref, w_ref, us_ref, uf_ref, gt_ref = refs
        x = x_ref[...] + pos_ref[...]
    else:
        x_ref, mod_ref, g_ref, w_ref, us_ref, uf_ref, gt_ref = refs
        x = x_ref[...]
    m = mod_ref[0]
    h = _rms(x, g_ref[...]) * (1.0 + m[1:2, :]) + m[0:1, :]
    p = jnp.dot(h.astype(BF16), w_ref[...], preferred_element_type=F32)
    us_ref[...] = p[:, :D_S5].astype(BF16)
    uf_ref[...] = p[:, D_S5:D_MODEL].astype(BF16)
    gt_ref[...] = jax.nn.sigmoid(p[:, D_MODEL:]).astype(BF16)


def _inproj(x, pos, mod, mod_row, norm_g, w_in_bf, tm):
    n = x.shape[0]
    has_pos = pos is not None
    row = lambda i: (i, 0)
    in_specs = [pl.BlockSpec((tm, D_MODEL), row)]
    args = [x]
    if has_pos:
        nper = pos.shape[0] // tm
        in_specs.append(pl.BlockSpec((tm, D_MODEL), lambda i: (i % nper, 0)))
        args.append(pos)
    in_specs += [pl.BlockSpec((1, N_MOD, D_MODEL), lambda i: (mod_row(i, tm), 0, 0)),
                 pl.BlockSpec((1, D_MODEL), lambda i: (0, 0)),
                 pl.BlockSpec(w_in_bf.shape, lambda i: (0, 0))]
    args += [mod, norm_g, w_in_bf]
    return pl.pallas_call(
        functools.partial(_inproj_kernel, has_pos),
        out_shape=(jax.ShapeDtypeStruct((n, D_S5), BF16),
                   jax.ShapeDtypeStruct((n, D_FNET), BF16),
                   jax.ShapeDtypeStruct((n, 2 * D_MODEL), BF16)),
        grid=(n // tm,),
        in_specs=in_specs,
        out_specs=(pl.BlockSpec((tm, D_S5), row), pl.BlockSpec((tm, D_FNET), row),
                   pl.BlockSpec((tm, 2 * D_MODEL), row)),
        compiler_params=_params(("arbitrary",)),
        name="inproj",
    )(*args)


def _s5_operators(lam_re, lam_im, log_dt, b_re, b_im, c_re, c_im, d_skip):
    hi = lax.Precision.HIGHEST
    lr = jnp.minimum(lam_re.astype(F32), -1e-4)
    li = lam_im.astype(F32)
    dt = jnp.exp(log_dt.astype(F32))[..., None]
    mag = jnp.exp(lr * dt)
    ar = mag * jnp.cos(li * dt)
    ai = mag * jnp.sin(li * dt)
    den = lr * lr + li * li
    nr = ar - 1.0
    qr = (nr * lr + ai * li) / den
    qi = (ai * lr - nr * li) / den
    br = b_re.astype(F32)
    bi = b_im.astype(F32)
    bbr = qr[..., None] * br - qi[..., None] * bi
    bbi = qr[..., None] * bi + qi[..., None] * br
    n = jnp.arange(S5_CHUNK + 1, dtype=F32)[:, None, None, None]
    pmag = jnp.exp(n * (lr * dt))
    pr = pmag * jnp.cos(n * (li * dt))
    pi = pmag * jnp.sin(n * (li * dt))
    cr = c_re.astype(F32)
    ci = c_im.astype(F32)
    car = cr[None] * pr[..., None, :] - ci[None] * pi[..., None, :]
    cai = cr[None] * pi[..., None, :] + ci[None] * pr[..., None, :]
    kk = (jnp.einsum("ndghp,dgpi->ndghi", car[:S5_CHUNK], bbr, precision=hi)
          - jnp.einsum("ndghp,dgpi->ndghi", cai[:S5_CHUNK], bbi, precision=hi))
    t = jnp.arange(S5_CHUNK)
    diff = t[None, :] - t[:, None]
    kf = kk[:, 0]
    kb = kk[:, 1]
    mf = jnp.where((diff >= 0)[:, :, None, None, None], kf[jnp.clip(diff, 0, S5_CHUNK - 1)], 0.0)
    mb = jnp.where((diff <= 0)[:, :, None, None, None], kb[jnp.clip(-diff, 0, S5_CHUNK - 1)], 0.0)
    m = (mf + mb).transpose(2, 0, 4, 1, 3).reshape(S5_GROUPS, S5_ROW, S5_ROW)
    dvec = jnp.tile(d_skip.astype(F32).reshape(S5_GROUPS, S5_GROUP), (1, S5_CHUNK))
    m = m + jnp.eye(S5_ROW, dtype=F32)[None] * dvec[:, None, :]

    def w_in_part(pw_r, pw_i, d):
        re = pw_r[..., None] * bbr[d][None] - pw_i[..., None] * bbi[d][None]
        im = pw_r[..., None] * bbi[d][None] + pw_i[..., None] * bbr[d][None]
        f = lambda a: a.transpose(1, 0, 3, 2).reshape(S5_GROUPS, S5_ROW, S5_STATE)
        return f(re), f(im)

    wf_re, wf_im = w_in_part(pr[S5_CHUNK - 1::-1, 0], pi[S5_CHUNK - 1::-1, 0], 0)
    wb_re, wb_im = w_in_part(pr[:S5_CHUNK, 1], pi[:S5_CHUNK, 1], 1)
    w_in = jnp.concatenate([wf_re, wb_re, wf_im, wb_im], axis=-1)

    def w_out_part(ca):
        return ca.transpose(1, 3, 0, 2).reshape(S5_GROUPS, S5_STATE, S5_ROW)

    w_out = jnp.concatenate([w_out_part(car[1:, 0]), w_out_part(car[S5_CHUNK:0:-1, 1]),
                             -w_out_part(cai[1:, 0]), -w_out_part(cai[S5_CHUNK:0:-1, 1])], axis=1)
    coef = jnp.stack([jnp.concatenate([pr[S5_CHUNK, 0], pr[S5_CHUNK, 1]], axis=-1),
                      jnp.concatenate([pi[S5_CHUNK, 0], pi[S5_CHUNK, 1]], axis=-1)], axis=1)
    return m.astype(BF16), w_in.astype(BF16), w_out.astype(BF16), coef


def _s5_kernel(nb, nk, u_ref, m_ref, wi_ref, wo_ref, coef_ref, h0_ref, y_ref, fin_ref, s_ref, e_ref):
    half = 2 * S5_STATE
    u = u_ref[0]
    s_ref[...] = jnp.dot(u, wi_ref[0], preferred_element_type=F32)
    c_r = coef_ref[0, 0:1, :]
    c_i = coef_ref[0, 1:2, :]
    e_r = h0_ref[0, :, :half]
    e_i = h0_ref[0, :, half:]
    is_fwd = lax.broadcasted_iota(jnp.int32, (nb, half), 1) < S5_STATE
    for j in range(nk):
        rf = slice(j * nb, (j + 1) * nb)
        rb = slice((nk - 1 - j) * nb, (nk - j) * nb)
        e_ref[rf, 0:S5_STATE] = e_r[:, :S5_STATE]
        e_ref[rb, S5_STATE:half] = e_r[:, S5_STATE:]
        e_ref[rf, half:half + S5_STATE] = e_i[:, :S5_STATE]
        e_ref[rb, half + S5_STATE:] = e_i[:, S5_STATE:]
        s_r = jnp.where(is_fwd, s_ref[rf, :half], s_ref[rb, :half])
        s_i = jnp.where(is_fwd, s_ref[rf, half:], s_ref[rb, half:])
        e_r, e_i = c_r * e_r - c_i * e_i + s_r, c_r * e_i + c_i * e_r + s_i
    fin_ref[0, :, :half] = e_r
    fin_ref[0, :, half:] = e_i
    y = jnp.dot(u, m_ref[0], preferred_element_type=F32)
    y = y + jnp.dot(e_ref[...].astype(BF16), wo_ref[0], preferred_element_type=F32)
    y_ref[0] = y.astype(BF16)


def _s5(u_rows, ops, h0, nb, nk):
    m, w_in, w_out, coef = ops
    rows = nb * nk
    g3 = lambda g: (g, 0, 0)
    return pl.pallas_call(
        functools.partial(_s5_kernel, nb, nk),
        out_shape=(jax.ShapeDtypeStruct((S5_GROUPS, rows, S5_ROW), BF16),
                   jax.ShapeDtypeStruct((S5_GROUPS, nb, 4 * S5_STATE), F32)),
        grid=(S5_GROUPS,),
        in_specs=[pl.BlockSpec((1, rows, S5_ROW), g3),
                  pl.BlockSpec((1, S5_ROW, S5_ROW), g3),
                  pl.BlockSpec((1, S5_ROW, 4 * S5_STATE), g3),
                  pl.BlockSpec((1, 4 * S5_STATE, S5_ROW), g3),
                  pl.BlockSpec((1, 2, 2 * S5_STATE), g3),
                  pl.BlockSpec((1, nb, 4 * S5_STATE), g3)],
        out_specs=(pl.BlockSpec((1, rows, S5_ROW), g3),
                   pl.BlockSpec((1, nb, 4 * S5_STATE), g3)),
        scratch_shapes=[pltpu.VMEM((rows, 4 * S5_STATE), F32),
                        pltpu.VMEM((rows, 4 * S5_STATE), F32)],
        compiler_params=_params(("arbitrary",)),
        name="s5",
    )(u_rows, m, w_in, w_out, coef, h0)


def _dft_tables(seq):
    j = jnp.arange(seq, dtype=jnp.int32)
    ang = (2.0 * math.pi / seq) * ((j[:, None] * j[None, :]) % seq).astype(F32)
    cs = jnp.concatenate([jnp.cos(ang), -jnp.sin(ang)], axis=1).astype(BF16)
    c = jnp.arange(D_FNET, dtype=jnp.int32)
    same = (c[:, None] // FNET_GROUP) == (c[None, :] // FNET_GROUP)
    angc = (2.0 * math.pi / FNET_GROUP) * (((c[:, None] % FNET_GROUP) * (c[None, :] % FNET_GROUP))
                                           % FNET_GROUP).astype(F32)
    scale = 1.0 / math.sqrt(seq * FNET_GROUP)
    bdc = jnp.where(same, jnp.cos(angc) * scale, 0.0).astype(BF16)
    bds = jnp.where(same, jnp.sin(angc) * scale, 0.0).astype(BF16)
    return cs, bdc, bds


def _fourier_kernel(seq, z_ref, cs_ref, bdc_ref, bds_ref, o_ref, zz_ref):
    @pl.when(pl.program_id(1) == 0)
    def _():
        z = z_ref[...]
        zz_ref[0:seq, :] = jnp.dot(z, bdc_ref[...], preferred_element_type=F32).astype(BF16)
        zz_ref[seq:, :] = jnp.dot(z, bds_ref[...], preferred_element_type=F32).astype(BF16)

    o_ref[...] = jnp.dot(cs_ref[...], zz_ref[...], preferred_element_type=F32).astype(BF16)


def _fourier(z, nb, seq, tl):
    cs, bdc, bds = _dft_tables(seq)
    nt = seq // tl
    return pl.pallas_call(
        functools.partial(_fourier_kernel, seq),
        out_shape=jax.ShapeDtypeStruct(z.shape, BF16),
        grid=(nb, nt),
        in_specs=[pl.BlockSpec((seq, D_FNET), lambda b, i: (b, 0)),
                  pl.BlockSpec((tl, 2 * seq), lambda b, i: (i, 0)),
                  pl.BlockSpec((D_FNET, D_FNET), lambda b, i: (0, 0)),
                  pl.BlockSpec((D_FNET, D_FNET), lambda b, i: (0, 0))],
        out_specs=pl.BlockSpec((tl, D_FNET), lambda b, i: (b * nt + i, 0)),
        scratch_shapes=[pltpu.VMEM((2 * seq, D_FNET), BF16)],
        compiler_params=_params(("arbitrary", "arbitrary")),
        name="fourier",
    )(z, cs, bdc, bds)


def _first_argmax_mask(v, iota, size):
    m = jnp.max(v, axis=0, keepdims=True)
    first = jnp.min(jnp.where(v == m, iota, size), axis=0, keepdims=True)
    return iota == first


def _route(logits_t, bias_col):
    tm = logits_t.shape[1]
    neg = -jnp.inf
    s = jax.nn.sigmoid(logits_t)
    biased = s + bias_col
    io8 = lax.broadcasted_iota(jnp.int32, (EXPERTS_PER_GROUP, tm), 0)
    gs_rows = []
    for g in range(N_EXPERT_GROUPS):
        blk = biased[g * EXPERTS_PER_GROUP:(g + 1) * EXPERTS_PER_GROUP, :]
        m1 = jnp.max(blk, axis=0, keepdims=True)
        rest = jnp.where(_first_argmax_mask(blk, io8, EXPERTS_PER_GROUP), neg, blk)
        gs_rows.append(m1 + jnp.max(rest, axis=0, keepdims=True))
    gs = jnp.concatenate(gs_rows, axis=0)
    iog = lax.broadcasted_iota(jnp.int32, (N_EXPERT_GROUPS, tm), 0)
    gsel = jnp.zeros((N_EXPERT_GROUPS, tm), F32)
    for _ in range(TOPK_GROUPS):
        sel = _first_argmax_mask(gs, iog, N_EXPERT_GROUPS)
        gsel = jnp.where(sel, 1.0, gsel)
        gs = jnp.where(sel, neg, gs)
    emask = jnp.concatenate(
        [jnp.broadcast_to(gsel[g:g + 1, :], (EXPERTS_PER_GROUP, tm)) for g in range(N_EXPERT_GROUPS)], axis=0)
    v = jnp.where(emask > 0.0, biased, neg)
    ioe = lax.broadcasted_iota(jnp.int32, (N_EXPERTS, tm), 0)
    idx_rows, s_rows = [], []
    for _ in range(TOP_K):
        sel = _first_argmax_mask(v, ioe, N_EXPERTS)
        idx_rows.append(jnp.sum(jnp.where(sel, ioe, 0), axis=0, keepdims=True))
        s_rows.append(jnp.sum(jnp.where(sel, s, 0.0), axis=0, keepdims=True))
        v = jnp.where(sel, neg, v)
    denom = s_rows[0]
    for r in s_rows[1:]:
        denom = denom + r
    pad = ROUTE_ROWS - TOP_K
    idx = jnp.concatenate(idx_rows + [jnp.zeros((pad, tm), jnp.int32)], axis=0)
    w = jnp.concatenate([r / denom * ROUTED_SCALE for r in s_rows] + [jnp.zeros((pad, tm), F32)], axis=0)
    return idx, w


def _merge_kernel(has_pos, *refs):
    if has_pos:
        (ys_ref, yf_ref, gt_ref, x_ref, pos_ref, mod_ref, n2_ref, wglu_ref, wps_ref, wpf_ref, wout_ref,
         wrt_ref, rb_ref, ws1_ref, ws3_ref, ws2_ref, xs_ref, h2_ref, ridx_ref, rw_ref) = refs
        x = x_ref[...] + pos_ref[...]
    else:
        (ys_ref, yf_ref, gt_ref, x_ref, mod_ref, n2_ref, wglu_ref, wps_ref, wpf_ref, wout_ref,
         wrt_ref, rb_ref, ws1_ref, ws3_ref, ws2_ref, xs_ref, h2_ref, ridx_ref, rw_ref) = refs
        x = x_ref[...]
    m = mod_ref[0]
    g = jax.nn.gelu(ys_ref[...].astype(F32))
    a = g * jax.nn.sigmoid(jnp.dot(g.astype(BF16), wglu_ref[...], preferred_element_type=F32))
    pa = jnp.dot(a.astype(BF16), wps_ref[...], preferred_element_type=F32)
    pb = jnp.dot(yf_ref[...], wpf_ref[...], preferred_element_type=F32)
    gt = gt_ref[...].astype(F32)
    merged = gt[:, :D_MODEL] * pa + gt[:, D_MODEL:] * pb
    x1 = x + m[2:3, :] * jnp.dot(merged.astype(BF16), wout_ref[...], preferred_element_type=F32)
    h2 = _rms(x1, n2_ref[...]) * (1.0 + m[4:5, :]) + m[3:4, :]
    hb = h2.astype(BF16)
    h2_ref[...] = h2.reshape(h2_ref.shape)
    logits_t = lax.dot_general(wrt_ref[...], h2, (((1,), (1,)), ((), ())),
                               precision=lax.Precision.HIGHEST, preferred_element_type=F32)
    ridx_ref[...], rw_ref[...] = _route(logits_t, rb_ref[...])
    s1 = jnp.dot(hb, ws1_ref[...], preferred_element_type=F32)
    s3 = jnp.dot(hb, ws3_ref[...], preferred_element_type=F32)
    shared = jnp.dot((s1 * jax.nn.sigmoid(s1) * s3).astype(BF16), ws2_ref[...], preferred_element_type=F32)
    xs_ref[...] = x1 + m[5:6, :] * shared


def _merge(ys, yf, gt, x, pos, mod, mod_row, n2, weights, tm):
    n = x.shape[0]
    has_pos = pos is not None
    row = lambda i: (i, 0)
    const = lambda a: pl.BlockSpec(a.shape, lambda i: (0,) * a.ndim)
    in_specs = [pl.BlockSpec((tm, D_S5), row), pl.BlockSpec((tm, D_FNET), row),
                pl.BlockSpec((tm, 2 * D_MODEL), row), pl.BlockSpec((tm, D_MODEL), row)]
    args = [ys, yf, gt, x]
    if has_pos:
        nper = pos.shape[0] // tm
        in_specs.append(pl.BlockSpec((tm, D_MODEL), lambda i: (i % nper, 0)))
        args.append(pos)
    in_specs += [pl.BlockSpec((1, N_MOD, D_MODEL), lambda i: (mod_row(i, tm), 0, 0)), const(n2)]
    args += [mod, n2]
    in_specs += [const(w) for w in weights]
    args += list(weights)
    return pl.pallas_call(
        functools.partial(_merge_kernel, has_pos),
        out_shape=(jax.ShapeDtypeStruct((n, D_MODEL), F32),
                   jax.ShapeDtypeStruct((n, 1, D_MODEL), F32),
                   jax.ShapeDtypeStruct((ROUTE_ROWS, n), jnp.int32),
                   jax.ShapeDtypeStruct((ROUTE_ROWS, n), F32)),
        grid=(n // tm,),
        in_specs=in_specs,
        out_specs=(pl.BlockSpec((tm, D_MODEL), row), pl.BlockSpec((tm, 1, D_MODEL), lambda i: (i, 0, 0)),
                   pl.BlockSpec((ROUTE_ROWS, tm), lambda i: (0, i)),
                   pl.BlockSpec((ROUTE_ROWS, tm), lambda i: (0, i))),
        compiler_params=_params(("arbitrary",)),
        name="merge",
    )(*args)


MOE_SUB = 4096
MOE_TM = 128
MOE_TMAX = MOE_SUB * TOP_K // MOE_TM + N_EXPERTS
MOE_DUMMY = 256
MOE_RMW = 8


def _moe_plan(ridx, rw):
    n = ridx.shape[1]
    nsub = n // MOE_SUB
    t = jnp.arange(n, dtype=jnp.int32)
    key = ((t // MOE_SUB) * N_EXPERTS)[None] + ridx[:TOP_K]
    key = (key * MOE_SUB + (t % MOE_SUB)[None]).reshape(-1)
    skey, sw = lax.sort((key, rw[:TOP_K].reshape(-1)), num_keys=1)
    hits = ridx[:TOP_K].reshape(TOP_K, nsub, 1, MOE_SUB) == jnp.arange(N_EXPERTS, dtype=jnp.int32)[None, None, :, None]
    cnt = jnp.sum(hits.astype(jnp.int32), axis=(0, 3))
    poff = (jnp.cumsum(cnt.reshape(-1)) - cnt.reshape(-1)).reshape(nsub, N_EXPERTS)
    ntile = (cnt + MOE_TM - 1) // MOE_TM
    tcum = jnp.cumsum(ntile, axis=1)
    toff = tcum - ntile
    ntot = tcum[:, -1]
    j = jnp.arange(MOE_TMAX, dtype=jnp.int32)
    tile_id = jnp.minimum(j[None], ntot[:, None] - 1)
    te = jnp.sum(tile_id[:, :, None] >= tcum[:, None, :], axis=-1).astype(jnp.int32)
    valid_tile = j[None] < ntot[:, None]
    pick = lambda a: jnp.take_along_axis(a, te, axis=1)
    r = ((j[None] - pick(toff)) * MOE_TM)[:, :, None] + jnp.arange(MOE_TM, dtype=jnp.int32)[None, None]
    valid = valid_tile[:, :, None] & (r < pick(cnt)[:, :, None])
    pair = jnp.where(valid, pick(poff)[:, :, None] + r, 0)
    tok = skey[pair] % MOE_SUB
    tok_g = jnp.where(valid, tok, 0).reshape(-1)
    tok_s = jnp.where(valid, tok, MOE_SUB).reshape(-1)
    gate = jnp.where(valid, sw[pair], 0.0).reshape(nsub * MOE_TMAX, 1, MOE_TM)
    return te.reshape(-1), ntot.astype(jnp.int32), tok_g, tok_s, gate


def _moe_kernel(te_ref, nt_ref, tg_ref, ts_ref, g_ref, src_ref, w1_ref, w3_ref, w2_ref, y_ref,
                xt_ref, x2_ref, ot_ref, w1b_ref, w3b_ref, w2b_ref):
    sub = pl.program_id(0)
    t = pl.program_id(1)
    tile = sub * MOE_TMAX + t

    @pl.when(t == 0)
    def _():
        y_ref[...] = jnp.zeros_like(y_ref)

    @pl.when((t == 0) | (te_ref[tile] != te_ref[jnp.maximum(tile, 1) - 1]))
    def _():
        w1b_ref[...] = w1_ref[0].astype(BF16)
        w3b_ref[...] = w3_ref[0].astype(BF16)
        w2b_ref[...] = w2_ref[0].astype(BF16)

    @pl.when(t < nt_ref[sub])
    def _():
        base = tile * MOE_TM
        for mi in range(MOE_TM):
            xt_ref[mi, 0] = src_ref[tg_ref[base + mi], 0]
        x2_ref[...] = xt_ref[...].reshape(MOE_TM, D_MODEL)
        x = x2_ref[...].astype(BF16)
        a = jnp.dot(x, w1b_ref[...], preferred_element_type=F32)
        b = jnp.dot(x, w3b_ref[...], preferred_element_type=F32)
        rows = lax.broadcasted_iota(jnp.int32, (MOE_TM, MOE_TM), 0)
        cols = lax.broadcasted_iota(jnp.int32, (MOE_TM, MOE_TM), 1)
        gcol = jnp.sum(jnp.where(rows == cols, jnp.broadcast_to(g_ref[0], (MOE_TM, MOE_TM)), 0.0),
                       axis=1, keepdims=True)
        act = (a * jax.nn.sigmoid(a) * b * gcol).astype(BF16)
        o = jnp.dot(act, w2b_ref[...], preferred_element_type=F32)
        ot_ref[...] = o.reshape(MOE_TM, 1, D_MODEL)
        for u in range(0, MOE_TM, MOE_RMW):
            new = []
            for i in range(MOE_RMW):
                tok = ts_ref[base + u + i]
                new.append((tok, y_ref[tok, 0] + ot_ref[u + i, 0]))
            for tok, v in new:
                y_ref[tok, 0] = v


def _moe(h2_rows, plan, w1, w3, w2):
    te, nt, tok_g, tok_s, gate = plan
    nsub = h2_rows.shape[0] // MOE_SUB
    wmap = lambda s, t, te, nt: (te[s * MOE_TMAX + t], 0, 0)
    sub3 = lambda s, t, te, nt: (s, 0, 0)
    grid_spec = pltpu.PrefetchScalarGridSpec(
        num_scalar_prefetch=2,
        grid=(nsub, MOE_TMAX),
        in_specs=[pl.BlockSpec(memory_space=pltpu.SMEM),
                  pl.BlockSpec(memory_space=pltpu.SMEM),
                  pl.BlockSpec((1, 1, MOE_TM), lambda s, t, te, nt: (s * MOE_TMAX + t, 0, 0)),
                  pl.BlockSpec((MOE_SUB, 1, D_MODEL), sub3, pipeline_mode=pl.Buffered(1)),
                  pl.BlockSpec((1, D_MODEL, D_EXPERT), wmap),
                  pl.BlockSpec((1, D_MODEL, D_EXPERT), wmap),
                  pl.BlockSpec((1, D_EXPERT, D_MODEL), wmap)],
        out_specs=pl.BlockSpec((MOE_SUB + MOE_DUMMY, 1, D_MODEL), sub3, pipeline_mode=pl.Buffered(1)),
        scratch_shapes=[pltpu.VMEM((MOE_TM, 1, D_MODEL), F32), pltpu.VMEM((MOE_TM, D_MODEL), F32),
                        pltpu.VMEM((MOE_TM, 1, D_MODEL), F32),
                        pltpu.VMEM((D_MODEL, D_EXPERT), BF16), pltpu.VMEM((D_MODEL, D_EXPERT), BF16),
                        pltpu.VMEM((D_EXPERT, D_MODEL), BF16)])
    return pl.pallas_call(
        _moe_kernel,
        grid_spec=grid_spec,
        out_shape=jax.ShapeDtypeStruct((nsub * (MOE_SUB + MOE_DUMMY), 1, D_MODEL), F32),
        compiler_params=_params(("arbitrary", "arbitrary")),
        name="moe",
    )(te, nt, tok_g, tok_s, gate, h2_rows, w1, w3, w2)


def _final_kernel(xs_ref, y_ref, mod_ref, fg_ref, o_ref, y2_ref):
    y2_ref[...] = y_ref[...].reshape(y2_ref.shape)
    x2 = xs_ref[...] + mod_ref[0][5:6, :] * y2_ref[...]
    o_ref[...] = _rms(x2, fg_ref[...])


def _final(xs, y_rows, mod, mod_row, fg, tm):
    n = xs.shape[0]
    per_sub = MOE_SUB // tm
    stride = (MOE_SUB + MOE_DUMMY) // tm
    row = lambda i: (i, 0)
    return pl.pallas_call(
        _final_kernel,
        out_shape=jax.ShapeDtypeStruct((n, D_MODEL), F32),
        grid=(n // tm,),
        in_specs=[pl.BlockSpec((tm, D_MODEL), row),
                  pl.BlockSpec((tm, 1, D_MODEL), lambda i: ((i // per_sub) * stride + i % per_sub, 0, 0)),
                  pl.BlockSpec((1, N_MOD, D_MODEL), lambda i: (mod_row(i, tm), 0, 0)),
                  pl.BlockSpec((1, D_MODEL), lambda i: (0, 0))],
        out_specs=pl.BlockSpec((tm, D_MODEL), row),
        scratch_shapes=[pltpu.VMEM((tm, D_MODEL), F32)],
        compiler_params=_params(("arbitrary",)),
        name="final",
    )(xs, y_rows, mod, fg)


def _grid_pos_embed(n_tokens):
    rows = n_tokens // GRID_W
    r, col = jnp.meshgrid(jnp.arange(rows, dtype=F32), jnp.arange(GRID_W, dtype=F32), indexing="ij")
    quarter = D_MODEL // 4
    omega = 1.0 / (10000.0 ** (jnp.arange(quarter, dtype=F32) / quarter))

    def emb(p):
        a = p.reshape(-1)[:, None] * omega
        return jnp.concatenate([jnp.sin(a), jnp.cos(a)], axis=-1)

    return jnp.concatenate([emb(r), emb(col)], axis=-1)


def _stream(x3, pos, mod, first_row, h0, s5_ops, p):
    nb, seq, _ = x3.shape
    n = nb * seq
    nk = seq // S5_CHUNK
    x = x3.reshape(n, D_MODEL)
    per_seq_mod = first_row > 0

    def mod_row(i, tm):
        return first_row + (i * tm) // seq if per_seq_mod else 0

    us, uf, gt = _inproj(x, pos, mod, mod_row, p["norm1_g"], p["w_in"], 512)
    u_rows = (us.reshape(nb, nk, S5_CHUNK, S5_GROUPS, S5_GROUP).transpose(3, 1, 0, 2, 4)
              .reshape(S5_GROUPS, nk * nb, S5_ROW))
    y_rows, fin = _s5(u_rows, s5_ops, h0, nb, nk)
    ys = (y_rows.reshape(S5_GROUPS, nk, nb, S5_CHUNK, S5_GROUP).transpose(2, 1, 3, 0, 4)
          .reshape(n, D_S5))
    yf = _fourier(uf, nb, seq, min(seq, 512))
    xs, h2_rows, ridx, rw = _merge(ys, yf, gt, x, pos, mod, mod_row, p["norm2_g"], p["merge_w"], 256)
    y_rows = _moe(h2_rows, _moe_plan(ridx, rw), p["w1"], p["w3"], p["w2"])
    out = _final(xs, y_rows, mod, mod_row, p["final_g"], 256)
    return out.reshape(nb, seq, D_MODEL), fin


def kernel(x_prompt, x_sample, state_s5_re, state_s5_im, c, c_ctx, w_ada, b_ada, norm1_g, norm2_g, w_in,
           lam_re, lam_im, log_dt, b_re, b_im, c_re, c_im, d_skip, w_glu, w_proj_s5, w_proj_f, w_out,
           w_router, router_bias, w1, w3, w2, ws1, ws3, ws2, final_norm_g):
    nb_ctx = x_prompt.shape[0]
    nb_lat, seq_lat, _ = x_sample.shape
    half = 2 * S5_STATE

    cond = jnp.concatenate([c_ctx[None], c, jnp.zeros((MOD_ROWS - 1 - nb_lat, D_MODEL), F32)], axis=0)
    mod = _adaln(cond, w_ada[0], b_ada[0]).reshape(MOD_ROWS, N_MOD, D_MODEL)

    s5_ops = _s5_operators(lam_re[0], lam_im[0], log_dt[0], b_re[0], b_im[0], c_re[0], c_im[0], d_skip[0])
    p = dict(
        norm1_g=norm1_g[0][None], norm2_g=norm2_g[0][None], final_g=final_norm_g[None],
        w_in=w_in[0].astype(BF16), w1=w1[0], w3=w3[0], w2=w2[0],
        merge_w=(w_glu[0].astype(BF16), w_proj_s5[0].astype(BF16), w_proj_f[0].astype(BF16),
                 w_out[0].astype(BF16), w_router[0].T, router_bias[0][:, None],
                 ws1[0].astype(BF16), ws3[0].astype(BF16), ws2[0].astype(BF16)))

    def pack_state(sr, si):
        f = lambda a: a.astype(F32).transpose(2, 0, 1, 3).reshape(S5_GROUPS, a.shape[0], half)
        return jnp.concatenate([f(sr), f(si)], axis=-1)

    def unpack_state(fin, lo):
        nb = fin.shape[1]
        return fin[..., lo:lo + half].reshape(S5_GROUPS, nb, 2, S5_STATE).transpose(1, 2, 0, 3)[:, None]

    h0_ctx = jnp.zeros((S5_GROUPS, nb_ctx, 2 * half), F32)
    y_prompt, fin = _stream(x_prompt, None, mod, 0, h0_ctx, s5_ops, p)
    h0_lat = pack_state(state_s5_re[:, 0], state_s5_im[:, 0])
    y_sample, _ = _stream(x_sample, _grid_pos_embed(seq_lat), mod, 1, h0_lat, s5_ops, p)
    return (y_prompt, y_sample, unpack_state(fin, 0).astype(x_prompt.dtype),
            unpack_state(fin, half).astype(x_prompt.dtype))
```

```python
import functools
import math

import jax
import jax.numpy as jnp
from jax import lax
from jax.experimental import pallas as pl
from jax.experimental.pallas import tpu as pltpu

D_MODEL = 1024
GRID_W = 64
D_S5 = 768
S5_GROUP = 16
S5_GROUPS = 48
S5_STATE = 64
D_FNET = 256
FNET_GROUP = 64
N_EXPERTS = 64
TOP_K = 6
N_EXPERT_GROUPS = 8
EXPERTS_PER_GROUP = N_EXPERTS // N_EXPERT_GROUPS
TOPK_GROUPS = 4
D_EXPERT = 256
ROUTED_SCALE = 2.5
N_MOD = 6
EPS = 1e-6

S5_CHUNK = 16
S5_ROW = S5_CHUNK * S5_GROUP
MOD_ROWS = 8
ROUTE_ROWS = 8
LANES = 128
ROW_SUB = D_MODEL // LANES
VMEM_LIMIT = 56 * 1024 * 1024

BF16 = jnp.bfloat16
F32 = jnp.float32


def _params(sem, vmem=VMEM_LIMIT):
    return pltpu.CompilerParams(dimension_semantics=sem, vmem_limit_bytes=vmem)


def _rms(x, g):
    return x * lax.rsqrt(jnp.mean(x * x, axis=-1, keepdims=True) + EPS) * g


def _adaln_kernel(c_ref, w_ref, b_ref, o_ref):
    c = c_ref[...]
    o_ref[...] = jnp.dot(c * jax.nn.sigmoid(c), w_ref[...], precision=lax.Precision.HIGHEST,
                         preferred_element_type=F32) + b_ref[...]


def _adaln(cond, w_ada, b_ada):
    n_out = N_MOD * D_MODEL
    return pl.pallas_call(
        _adaln_kernel,
        out_shape=jax.ShapeDtypeStruct((MOD_ROWS, n_out), F32),
        grid=(N_MOD,),
        in_specs=[pl.BlockSpec((MOD_ROWS, D_MODEL), lambda i: (0, 0)),
                  pl.BlockSpec((D_MODEL, D_MODEL), lambda i: (0, i)),
                  pl.BlockSpec((1, D_MODEL), lambda i: (0, i))],
        out_specs=pl.BlockSpec((MOD_ROWS, D_MODEL), lambda i: (0, i)),
        compiler_params=_params(("arbitrary",)),
        name="adaln",
    )(cond, w_ada, b_ada.reshape(1, n_out))


def _inproj_kernel(has_pos, *refs):
    if has_pos:
        x_ref, pos_ref, mod_ref, g_ref, w_ref, us_ref, uf_ref, gt_ref = refs
        x = x_ref[...] + pos_ref[...]
    else:
        x_ref, mod_ref, g_ref, w_ref, us_ref, uf_ref, gt_ref = refs
        x = x_ref[...]
    m = mod_ref[0]
    h = _rms(x, g_ref[...]) * (1.0 + m[1:2, :]) + m[0:1, :]
    p = jnp.dot(h.astype(BF16), w_ref[...], preferred_element_type=F32)
    us_ref[...] = p[:, :D_S5].astype(BF16)
    uf_ref[...] = p[:, D_S5:D_MODEL].astype(BF16)
    gt_ref[...] = jax.nn.sigmoid(p[:, D_MODEL:]).astype(BF16)


def _inproj(x, pos, mod, mod_row, norm_g, w_in_bf, tm):
    n = x.shape[0]
    has_pos = pos is not None
    row = lambda i: (i, 0)
    in_specs = [pl.BlockSpec((tm, D_MODEL), row)]
    args = [x]
    if has_pos:
        nper = pos.shape[0] // tm
        in_specs.append(pl.BlockSpec((tm, D_MODEL), lambda i: (i % nper, 0)))
        args.append(pos)
    in_specs += [pl.BlockSpec((1, N_MOD, D_MODEL), lambda i: (mod_row(i, tm), 0, 0)),
                 pl.BlockSpec((1, D_MODEL), lambda i: (0, 0)),
                 pl.BlockSpec(w_in_bf.shape, lambda i: (0, 0))]
    args += [mod, norm_g, w_in_bf]
    return pl.pallas_call(
        functools.partial(_inproj_kernel, has_pos),
        out_shape=(jax.ShapeDtypeStruct((n, D_S5), BF16),
                   jax.ShapeDtypeStruct((n, D_FNET), BF16),
                   jax.ShapeDtypeStruct((n, 2 * D_MODEL), BF16)),
        grid=(n // tm,),
        in_specs=in_specs,
        out_specs=(pl.BlockSpec((tm, D_S5), row), pl.BlockSpec((tm, D_FNET), row),
                   pl.BlockSpec((tm, 2 * D_MODEL), row)),
        compiler_params=_params(("arbitrary",)),
        name="inproj",
    )(*args)


def _s5_operators(lam_re, lam_im, log_dt, b_re, b_im, c_re, c_im, d_skip):
    hi = lax.Precision.HIGHEST
    lr = jnp.minimum(lam_re.astype(F32), -1e-4)
    li = lam_im.astype(F32)
    dt = jnp.exp(log_dt.astype(F32))[..., None]
    mag = jnp.exp(lr * dt)
    ar = mag * jnp.cos(li * dt)
    ai = mag * jnp.sin(li * dt)
    den = lr * lr + li * li
    nr = ar - 1.0
    qr = (nr * lr + ai * li) / den
    qi = (ai * lr - nr * li) / den
    br = b_re.astype(F32)
    bi = b_im.astype(F32)
    bbr = qr[..., None] * br - qi[..., None] * bi
    bbi = qr[..., None] * bi + qi[..., None] * br
    n = jnp.arange(S5_CHUNK + 1, dtype=F32)[:, None, None, None]
    pmag = jnp.exp(n * (lr * dt))
    pr = pmag * jnp.cos(n * (li * dt))
    pi = pmag * jnp.sin(n * (li * dt))
    cr = c_re.astype(F32)
    ci = c_im.astype(F32)
    car = cr[None] * pr[..., None, :] - ci[None] * pi[..., None, :]
    cai = cr[None] * pi[..., None, :] + ci[None] * pr[..., None, :]
    kk = (jnp.einsum("ndghp,dgpi->ndghi", car[:S5_CHUNK], bbr, precision=hi)
          - jnp.einsum("ndghp,dgpi->ndghi", cai[:S5_CHUNK], bbi, precision=hi))
    t = jnp.arange(S5_CHUNK)
    diff = t[None, :] - t[:, None]
    kf = kk[:, 0]
    kb = kk[:, 1]
    mf = jnp.where((diff >= 0)[:, :, None, None, None], kf[jnp.clip(diff, 0, S5_CHUNK - 1)], 0.0)
    mb = jnp.where((diff <= 0)[:, :, None, None, None], kb[jnp.clip(-diff, 0, S5_CHUNK - 1)], 0.0)
    m = (mf + mb).transpose(2, 0, 4, 1, 3).reshape(S5_GROUPS, S5_ROW, S5_ROW)
    dvec = jnp.tile(d_skip.astype(F32).reshape(S5_GROUPS, S5_GROUP), (1, S5_CHUNK))
    m = m + jnp.eye(S5_ROW, dtype=F32)[None] * dvec[:, None, :]

    def w_in_part(pw_r, pw_i, d):
        re = pw_r[..., None] * bbr[d][None] - pw_i[..., None] * bbi[d][None]
        im = pw_r[..., None] * bbi[d][None] + pw_i[..., None] * bbr[d][None]
        f = lambda a: a.transpose(1, 0, 3, 2).reshape(S5_GROUPS, S5_ROW, S5_STATE)
        return f(re), f(im)

    wf_re, wf_im = w_in_part(pr[S5_CHUNK - 1::-1, 0], pi[S5_CHUNK - 1::-1, 0], 0)
    wb_re, wb_im = w_in_part(pr[:S5_CHUNK, 1], pi[:S5_CHUNK, 1], 1)
    w_in = jnp.concatenate([wf_re, wb_re, wf_im, wb_im], axis=-1)

    def w_out_part(ca):
        return ca.transpose(1, 3, 0, 2).reshape(S5_GROUPS, S5_STATE, S5_ROW)

    w_out = jnp.concatenate([w_out_part(car[1:, 0]), w_out_part(car[S5_CHUNK:0:-1, 1]),
                             -w_out_part(cai[1:, 0]), -w_out_part(cai[S5_CHUNK:0:-1, 1])], axis=1)
    coef = jnp.stack([jnp.concatenate([pr[S5_CHUNK, 0], pr[S5_CHUNK, 1]], axis=-1),
                      jnp.concatenate([pi[S5_CHUNK, 0], pi[S5_CHUNK, 1]], axis=-1)], axis=1)
    return m.astype(BF16), w_in.astype(BF16), w_out.astype(BF16), coef


def _s5_kernel(nb, nk, u_ref, m_ref, wi_ref, wo_ref, coef_ref, h0_ref, y_ref, fin_ref, s_ref, e_ref):
    half = 2 * S5_STATE
    u = u_ref[0]
    s_ref[...] = jnp.dot(u, wi_ref[0], preferred_element_type=F32)
    c_r = coef_ref[0, 0:1, :]
    c_i = coef_ref[0, 1:2, :]
    e_r = h0_ref[0, :, :half]
    e_i = h0_ref[0, :, half:]
    is_fwd = lax.broadcasted_iota(jnp.int32, (nb, half), 1) < S5_STATE
    for j in range(nk):
        rf = slice(j * nb, (j + 1) * nb)
        rb = slice((nk - 1 - j) * nb, (nk - j) * nb)
        e_ref[rf, 0:S5_STATE] = e_r[:, :S5_STATE]
        e_ref[rb, S5_STATE:half] = e_r[:, S5_STATE:]
        e_ref[rf, half:half + S5_STATE] = e_i[:, :S5_STATE]
        e_ref[rb, half + S5_STATE:] = e_i[:, S5_STATE:]
        s_r = jnp.where(is_fwd, s_ref[rf, :half], s_ref[rb, :half])
        s_i = jnp.where(is_fwd, s_ref[rf, half:], s_ref[rb, half:])
        e_r, e_i = c_r * e_r - c_i * e_i + s_r, c_r * e_i + c_i * e_r + s_i
    fin_ref[0, :, :half] = e_r
    fin_ref[0, :, half:] = e_i
    y = jnp.dot(u, m_ref[0], preferred_element_type=F32)
    y = y + jnp.dot(e_ref[...].astype(BF16), wo_ref[0], preferred_element_type=F32)
    y_ref[0] = y.astype(BF16)


def _s5(u_rows, ops, h0, nb, nk):
    m, w_in, w_out, coef = ops
    rows = nb * nk
    g3 = lambda g: (g, 0, 0)
    return pl.pallas_call(
        functools.partial(_s5_kernel, nb, nk),
        out_shape=(jax.ShapeDtypeStruct((S5_GROUPS, rows, S5_ROW), BF16),
                   jax.ShapeDtypeStruct((S5_GROUPS, nb, 4 * S5_STATE), F32)),
        grid=(S5_GROUPS,),
        in_specs=[pl.BlockSpec((1, rows, S5_ROW), g3),
                  pl.BlockSpec((1, S5_ROW, S5_ROW), g3),
                  pl.BlockSpec((1, S5_ROW, 4 * S5_STATE), g3),
                  pl.BlockSpec((1, 4 * S5_STATE, S5_ROW), g3),
                  pl.BlockSpec((1, 2, 2 * S5_STATE), g3),
                  pl.BlockSpec((1, nb, 4 * S5_STATE), g3)],
        out_specs=(pl.BlockSpec((1, rows, S5_ROW), g3),
                   pl.BlockSpec((1, nb, 4 * S5_STATE), g3)),
        scratch_shapes=[pltpu.VMEM((rows, 4 * S5_STATE), F32),
                        pltpu.VMEM((rows, 4 * S5_STATE), F32)],
        compiler_params=_params(("arbitrary",)),
        name="s5",
    )(u_rows, m, w_in, w_out, coef, h0)


def _dft_tables(seq):
    j = jnp.arange(seq, dtype=jnp.int32)
    ang = (2.0 * math.pi / seq) * ((j[:, None] * j[None, :]) % seq).astype(F32)
    cs = jnp.concatenate([jnp.cos(ang), -jnp.sin(ang)], axis=1).astype(BF16)
    c = jnp.arange(D_FNET, dtype=jnp.int32)
    same = (c[:, None] // FNET_GROUP) == (c[None, :] // FNET_GROUP)
    angc = (2.0 * math.pi / FNET_GROUP) * (((c[:, None] % FNET_GROUP) * (c[None, :] % FNET_GROUP))
                                           % FNET_GROUP).astype(F32)
    scale = 1.0 / math.sqrt(seq * FNET_GROUP)
    bdc = jnp.where(same, jnp.cos(angc) * scale, 0.0).astype(BF16)
    bds = jnp.where(same, jnp.sin(angc) * scale, 0.0).astype(BF16)
    return cs, bdc, bds


def _fourier_kernel(seq, z_ref, cs_ref, bdc_ref, bds_ref, o_ref, zz_ref):
    @pl.when(pl.program_id(1) == 0)
    def _():
        z = z_ref[...]
        zz_ref[0:seq, :] = jnp.dot(z, bdc_ref[...], preferred_element_type=F32).astype(BF16)
        zz_ref[seq:, :] = jnp.dot(z, bds_ref[...], preferred_element_type=F32).astype(BF16)

    o_ref[...] = jnp.dot(cs_ref[...], zz_ref[...], preferred_element_type=F32).astype(BF16)


def _fourier(z, nb, seq, tl):
    cs, bdc, bds = _dft_tables(seq)
    nt = seq // tl
    return pl.pallas_call(
        functools.partial(_fourier_kernel, seq),
        out_shape=jax.ShapeDtypeStruct(z.shape, BF16),
        grid=(nb, nt),
        in_specs=[pl.BlockSpec((seq, D_FNET), lambda b, i: (b, 0)),
                  pl.BlockSpec((tl, 2 * seq), lambda b, i: (i, 0)),
                  pl.BlockSpec((D_FNET, D_FNET), lambda b, i: (0, 0)),
                  pl.BlockSpec((D_FNET, D_FNET), lambda b, i: (0, 0))],
        out_specs=pl.BlockSpec((tl, D_FNET), lambda b, i: (b * nt + i, 0)),
        scratch_shapes=[pltpu.VMEM((2 * seq, D_FNET), BF16)],
        compiler_params=_params(("arbitrary", "arbitrary")),
        name="fourier",
    )(z, cs, bdc, bds)


def _first_argmax_mask(v, iota, size):
    m = jnp.max(v, axis=0, keepdims=True)
    first = jnp.min(jnp.where(v == m, iota, size), axis=0, keepdims=True)
    return iota == first


def _route(logits_t, bias_col):
    tm = logits_t.shape[1]
    neg = -jnp.inf
    s = jax.nn.sigmoid(logits_t)
    biased = s + bias_col
    io8 = lax.broadcasted_iota(jnp.int32, (EXPERTS_PER_GROUP, tm), 0)
    gs_rows = []
    for g in range(N_EXPERT_GROUPS):
        blk = biased[g * EXPERTS_PER_GROUP:(g + 1) * EXPERTS_PER_GROUP, :]
        m1 = jnp.max(blk, axis=0, keepdims=True)
        rest = jnp.where(_first_argmax_mask(blk, io8, EXPERTS_PER_GROUP), neg, blk)
        gs_rows.append(m1 + jnp.max(rest, axis=0, keepdims=True))
    gs = jnp.concatenate(gs_rows, axis=0)
    iog = lax.broadcasted_iota(jnp.int32, (N_EXPERT_GROUPS, tm), 0)
    gsel = jnp.zeros((N_EXPERT_GROUPS, tm), F32)
    for _ in range(TOPK_GROUPS):
        sel = _first_argmax_mask(gs, iog, N_EXPERT_GROUPS)
        gsel = jnp.where(sel, 1.0, gsel)
        gs = jnp.where(sel, neg, gs)
    emask = jnp.concatenate(
        [jnp.broadcast_to(gsel[g:g + 1, :], (EXPERTS_PER_GROUP, tm)) for g in range(N_EXPERT_GROUPS)], axis=0)
    v = jnp.where(emask > 0.0, biased, neg)
    ioe = lax.broadcasted_iota(jnp.int32, (N_EXPERTS, tm), 0)
    idx_rows, s_rows = [], []
    for _ in range(TOP_K):
        sel = _first_argmax_mask(v, ioe, N_EXPERTS)
        idx_rows.append(jnp.sum(jnp.where(sel, ioe, 0), axis=0, keepdims=True))
        s_rows.append(jnp.sum(jnp.where(sel, s, 0.0), axis=0, keepdims=True))
        v = jnp.where(sel, neg, v)
    denom = s_rows[0]
    for r in s_rows[1:]:
        denom = denom + r
    pad = ROUTE_ROWS - TOP_K
    idx = jnp.concatenate(idx_rows + [jnp.zeros((pad, tm), jnp.int32)], axis=0)
    w = jnp.concatenate([r / denom * ROUTED_SCALE for r in s_rows] + [jnp.zeros((pad, tm), F32)], axis=0)
    return idx, w


def _merge_kernel(has_pos, *refs):
    if has_pos:
        (ys_ref, yf_ref, gt_ref, x_ref, pos_ref, mod_ref, n2_ref, wglu_ref, wps_ref, wpf_ref, wout_ref,
         wrt_ref, rb_ref, ws1_ref, ws3_ref, ws2_ref, xs_ref, h2_ref, ridx_ref, rw_ref) = refs
        x = x_ref[...] + pos_ref[...]
    else:
        (ys_ref, yf_ref, gt_ref, x_ref, mod_ref, n2_ref, wglu_ref, wps_ref, wpf_ref, wout_ref,
         wrt_ref, rb_ref, ws1_ref, ws3_ref, ws2_ref, xs_ref, h2_ref, ridx_ref, rw_ref) = refs
        x = x_ref[...]
    m = mod_ref[0]
    g = jax.nn.gelu(ys_ref[...].astype(F32))
    a = g * jax.nn.sigmoid(jnp.dot(g.astype(BF16), wglu_ref[...], preferred_element_type=F32))
    pa = jnp.dot(a.astype(BF16), wps_ref[...], preferred_element_type=F32)
    pb = jnp.dot(yf_ref[...], wpf_ref[...], preferred_element_type=F32)
    gt = gt_ref[...].astype(F32)
    merged = gt[:, :D_MODEL] * pa + gt[:, D_MODEL:] * pb
    x1 = x + m[2:3, :] * jnp.dot(merged.astype(BF16), wout_ref[...], preferred_element_type=F32)
    h2 = _rms(x1, n2_ref[...]) * (1.0 + m[4:5, :]) + m[3:4, :]
    hb = h2.astype(BF16)
    for j in range(ROW_SUB):
        h2_ref[pl.ds(j, h2.shape[0], stride=ROW_SUB), :] = h2[:, j * LANES:(j + 1) * LANES]
    logits_t = lax.dot_general(wrt_ref[...], h2, (((1,), (1,)), ((), ())),
                               precision=lax.Precision.HIGHEST, preferred_element_type=F32)
    ridx_ref[...], rw_ref[...] = _route(logits_t, rb_ref[...])
    s1 = jnp.dot(hb, ws1_ref[...], preferred_element_type=F32)
    s3 = jnp.dot(hb, ws3_ref[...], preferred_element_type=F32)
    shared = jnp.dot((s1 * jax.nn.sigmoid(s1) * s3).astype(BF16), ws2_ref[...], preferred_element_type=F32)
    xs_ref[...] = x1 + m[5:6, :] * shared


def _merge(ys, yf, gt, x, pos, mod, mod_row, n2, weights, tm):
    n = x.shape[0]
    has_pos = pos is not None
    row = lambda i: (i, 0)
    const = lambda a: pl.BlockSpec(a.shape, lambda i: (0,) * a.ndim)
    in_specs = [pl.BlockSpec((tm, D_S5), row), pl.BlockSpec((tm, D_FNET), row),
                pl.BlockSpec((tm, 2 * D_MODEL), row), pl.BlockSpec((tm, D_MODEL), row)]
    args = [ys, yf, gt, x]
    if has_pos:
        nper = pos.shape[0] // tm
        in_specs.append(pl.BlockSpec((tm, D_MODEL), lambda i: (i % nper, 0)))
        args.append(pos)
    in_specs += [pl.BlockSpec((1, N_MOD, D_MODEL), lambda i: (mod_row(i, tm), 0, 0)), const(n2)]
    args += [mod, n2]
    in_specs += [const(w) for w in weights]
    args += list(weights)
    return pl.pallas_call(
        functools.partial(_merge_kernel, has_pos),
        out_shape=(jax.ShapeDtypeStruct((n, D_MODEL), F32),
                   jax.ShapeDtypeStruct((n * ROW_SUB, LANES), F32),
                   jax.ShapeDtypeStruct((ROUTE_ROWS, n), jnp.int32),
                   jax.ShapeDtypeStruct((ROUTE_ROWS, n), F32)),
        grid=(n // tm,),
        in_specs=in_specs,
        out_specs=(pl.BlockSpec((tm, D_MODEL), row), pl.BlockSpec((tm * ROW_SUB, LANES), row),
                   pl.BlockSpec((ROUTE_ROWS, tm), lambda i: (0, i)),
                   pl.BlockSpec((ROUTE_ROWS, tm), lambda i: (0, i))),
        compiler_params=_params(("arbitrary",)),
        name="merge",
    )(*args)


MOE_SUB = 4096
MOE_TM = 128
MOE_TMAX = MOE_SUB * TOP_K // MOE_TM + N_EXPERTS
MOE_TS = MOE_TMAX + 2
MOE_DUMMY = 256
MOE_RMW = 16
ROW_TILE = MOE_TM * ROW_SUB


def _moe_plan(ridx, rw):
    n = ridx.shape[1]
    nsub = n // MOE_SUB
    npair = n * TOP_K
    t = jnp.arange(n, dtype=jnp.int32)
    key = ((t // MOE_SUB) * N_EXPERTS)[None] + ridx[:TOP_K]
    key = (key * MOE_SUB + (t % MOE_SUB)[None]).reshape(-1)
    skey, sw = lax.sort((key, rw[:TOP_K].reshape(-1)), num_keys=1)
    stok = jnp.concatenate([(skey % MOE_SUB) * ROW_SUB, jnp.zeros((MOE_TM,), jnp.int32)])
    sw_rows = jnp.concatenate([sw, jnp.zeros((LANES,), F32)]).reshape(npair // LANES + 1, 1, LANES)
    hits = ridx[:TOP_K].reshape(TOP_K, nsub, 1, MOE_SUB) == jnp.arange(N_EXPERTS, dtype=jnp.int32)[None, None, :, None]
    cnt = jnp.sum(hits.astype(jnp.int32), axis=(0, 3))
    poff = (jnp.cumsum(cnt.reshape(-1)) - cnt.reshape(-1)).reshape(nsub, N_EXPERTS)
    ntile = (cnt + MOE_TM - 1) // MOE_TM
    tcum = jnp.cumsum(ntile, axis=1)
    toff = tcum - ntile
    tstart = jnp.concatenate([toff, tcum[:, -1:]], axis=1).reshape(-1).astype(jnp.int32)
    j = jnp.arange(MOE_TS, dtype=jnp.int32) - 1
    valid = (j[None] >= 0) & (j[None] < tcum[:, -1:])
    te = jnp.minimum(jnp.sum(j[None, :, None] >= tcum[:, None, :], axis=-1), N_EXPERTS - 1)
    pick = lambda a: jnp.take_along_axis(a, te, axis=1)
    first = (j[None] - pick(toff)) * MOE_TM
    p0 = jnp.where(valid, pick(poff) + first, 0).reshape(-1).astype(jnp.int32)
    nv = jnp.where(valid, jnp.minimum(pick(cnt) - first, MOE_TM), 0).reshape(-1).astype(jnp.int32)
    return tstart, p0, nv, stok, sw_rows


def _moe_kernel(ts_ref, p0_ref, nv_ref, tok_ref, sw_ref, src_ref, w1_ref, w3_ref, w2_ref, y_ref,
                xt_ref, xb_ref, ot_ref, w1b_ref, w3b_ref, w2b_ref):
    sub = pl.program_id(0)
    e = pl.program_id(1)
    base = sub * MOE_TS + 1

    def gather(p0):
        for mi in range(MOE_TM):
            tok = pl.multiple_of(tok_ref[p0 + mi], ROW_SUB)
            xt_ref[mi * ROW_SUB:(mi + 1) * ROW_SUB, :] = src_ref[pl.ds(tok, ROW_SUB), :]
        for j in range(ROW_SUB):
            xb_ref[:, j * LANES:(j + 1) * LANES] = xt_ref[pl.ds(j, MOE_TM, stride=ROW_SUB), :].astype(BF16)

    def scatter(p0, nv, masked):
        for u in range(0, MOE_TM, MOE_RMW):
            new = []
            for i in range(MOE_RMW):
                tok = tok_ref[p0 + u + i]
                if masked:
                    tok = jnp.where(u + i < nv, tok, MOE_SUB * ROW_SUB)
                tok = pl.multiple_of(tok, ROW_SUB)
                new.append((tok, y_ref[pl.ds(tok, ROW_SUB), :]
                            + ot_ref[(u + i) * ROW_SUB:(u + i + 1) * ROW_SUB, :]))
            for tok, v in new:
                y_ref[pl.ds(tok, ROW_SUB), :] = v

    @pl.when(e == 0)
    def _():
        y_ref[...] = jnp.zeros_like(y_ref)
        ot_ref[...] = jnp.zeros_like(ot_ref)
        gather(p0_ref[base])

    w1b_ref[...] = w1_ref[0].astype(BF16)
    w3b_ref[...] = w3_ref[0].astype(BF16)
    w2b_ref[...] = w2_ref[0].astype(BF16)

    def step(cur, masked):
        scatter(p0_ref[cur - 1], nv_ref[cur - 1], masked)
        p0 = p0_ref[cur]
        nv = nv_ref[cur]
        x = xb_ref[...]
        a = jnp.dot(x, w1b_ref[...], preferred_element_type=F32)
        b = jnp.dot(x, w3b_ref[...], preferred_element_type=F32)
        r0 = p0 // LANES
        c = p0 % LANES
        lane = lax.broadcasted_iota(jnp.int32, (1, LANES), 1)
        ga = pltpu.roll(sw_ref[r0], LANES - c, axis=1)
        gb = pltpu.roll(sw_ref[r0 + 1], LANES - c, axis=1)
        g = jnp.where(lane < nv, jnp.where(lane < LANES - c, ga, gb), 0.0)
        rows = lax.broadcasted_iota(jnp.int32, (MOE_TM, MOE_TM), 0)
        cols = lax.broadcasted_iota(jnp.int32, (MOE_TM, MOE_TM), 1)
        gcol = jnp.sum(jnp.where(rows == cols, jnp.broadcast_to(g, (MOE_TM, MOE_TM)), 0.0),
                       axis=1, keepdims=True)
        act = (a * jax.nn.sigmoid(a) * b * gcol).astype(BF16)
        o = jnp.dot(act, w2b_ref[...], preferred_element_type=F32)
        for j in range(ROW_SUB):
            ot_ref[pl.ds(j, MOE_TM, stride=ROW_SUB), :] = o[:, j * LANES:(j + 1) * LANES]
        gather(p0_ref[cur + 1])

    def body(i, carry):
        cur = base + i
        prev_full = nv_ref[cur - 1] == MOE_TM

        @pl.when(prev_full)
        def _():
            step(cur, False)

        @pl.when(jnp.logical_not(prev_full))
        def _():
            step(cur, True)

        return carry

    first = ts_ref[sub * (N_EXPERTS + 1) + e]
    last = ts_ref[sub * (N_EXPERTS + 1) + e + 1]
    lax.fori_loop(first, last, body, 0)

    @pl.when(e == N_EXPERTS - 1)
    def _():
        scatter(p0_ref[base + last - 1], nv_ref[base + last - 1], True)


def _moe(h2_rows, plan, w1, w3, w2):
    tstart, p0, nv, stok, sw_rows = plan
    nsub = h2_rows.shape[0] // (MOE_SUB * ROW_SUB)
    wmap = lambda s, e, ts, p0, nv: (e, 0, 0)
    sub2 = lambda s, e, ts, p0, nv: (s, 0)
    grid_spec = pltpu.PrefetchScalarGridSpec(
        num_scalar_prefetch=3,
        grid=(nsub, N_EXPERTS),
        in_specs=[pl.BlockSpec(memory_space=pltpu.SMEM),
                  pl.BlockSpec(sw_rows.shape, lambda s, e, ts, p0, nv: (0, 0, 0)),
                  pl.BlockSpec((MOE_SUB * ROW_SUB, LANES), sub2, pipeline_mode=pl.Buffered(1)),
                  pl.BlockSpec((1, D_MODEL, D_EXPERT), wmap),
                  pl.BlockSpec((1, D_MODEL, D_EXPERT), wmap),
                  pl.BlockSpec((1, D_EXPERT, D_MODEL), wmap)],
        out_specs=pl.BlockSpec(((MOE_SUB + MOE_DUMMY) * ROW_SUB, LANES), sub2, pipeline_mode=pl.Buffered(1)),
        scratch_shapes=[pltpu.VMEM((ROW_TILE, LANES), F32), pltpu.VMEM((MOE_TM, D_MODEL), BF16),
                        pltpu.VMEM((ROW_TILE, LANES), F32),
                        pltpu.VMEM((D_MODEL, D_EXPERT), BF16), pltpu.VMEM((D_MODEL, D_EXPERT), BF16),
                        pltpu.VMEM((D_EXPERT, D_MODEL), BF16)])
    return pl.pallas_call(
        _moe_kernel,
        grid_spec=grid_spec,
        out_shape=jax.ShapeDtypeStruct((nsub * (MOE_SUB + MOE_DUMMY) * ROW_SUB, LANES), F32),
        compiler_params=_params(("arbitrary", "arbitrary")),
        name="moe",
    )(tstart, p0, nv, stok, sw_rows, h2_rows, w1, w3, w2)


def _final_kernel(xs_ref, y_ref, mod_ref, fg_ref, o_ref):
    tm = xs_ref.shape[0]
    y = jnp.concatenate([y_ref[pl.ds(j, tm, stride=ROW_SUB), :] for j in range(ROW_SUB)], axis=1)
    x2 = xs_ref[...] + mod_ref[0][5:6, :] * y
    o_ref[...] = _rms(x2, fg_ref[...])


def _final(xs, y_rows, mod, mod_row, fg, tm):
    n = xs.shape[0]
    per_sub = MOE_SUB // tm
    stride = (MOE_SUB + MOE_DUMMY) // tm
    row = lambda i: (i, 0)
    return pl.pallas_call(
        _final_kernel,
        out_shape=jax.ShapeDtypeStruct((n, D_MODEL), F32),
        grid=(n // tm,),
        in_specs=[pl.BlockSpec((tm, D_MODEL), row),
                  pl.BlockSpec((tm * ROW_SUB, LANES), lambda i: ((i // per_sub) * stride + i % per_sub, 0)),
                  pl.BlockSpec((1, N_MOD, D_MODEL), lambda i: (mod_row(i, tm), 0, 0)),
                  pl.BlockSpec((1, D_MODEL), lambda i: (0, 0))],
        out_specs=pl.BlockSpec((tm, D_MODEL), row),
        compiler_params=_params(("arbitrary",)),
        name="final",
    )(xs, y_rows, mod, fg)


def _grid_pos_embed(n_tokens):
    rows = n_tokens // GRID_W
    r, col = jnp.meshgrid(jnp.arange(rows, dtype=F32), jnp.arange(GRID_W, dtype=F32), indexing="ij")
    quarter = D_MODEL // 4
    omega = 1.0 / (10000.0 ** (jnp.arange(quarter, dtype=F32) / quarter))

    def emb(p):
        a = p.reshape(-1)[:, None] * omega
        return jnp.concatenate([jnp.sin(a), jnp.cos(a)], axis=-1)

    return jnp.concatenate([emb(r), emb(col)], axis=-1)


def _stream(x3, pos, mod, first_row, h0, s5_ops, p):
    nb, seq, _ = x3.shape
    n = nb * seq
    nk = seq // S5_CHUNK
    x = x3.reshape(n, D_MODEL)
    per_seq_mod = first_row > 0

    def mod_row(i, tm):
        return first_row + (i * tm) // seq if per_seq_mod else 0

    us, uf, gt = _inproj(x, pos, mod, mod_row, p["norm1_g"], p["w_in"], 512)
    u_rows = (us.reshape(nb, nk, S5_CHUNK, S5_GROUPS, S5_GROUP).transpose(3, 1, 0, 2, 4)
              .reshape(S5_GROUPS, nk * nb, S5_ROW))
    y_rows, fin = _s5(u_rows, s5_ops, h0, nb, nk)
    ys = (y_rows.reshape(S5_GROUPS, nk, nb, S5_CHUNK, S5_GROUP).transpose(2, 1, 3, 0, 4)
          .reshape(n, D_S5))
    yf = _fourier(uf, nb, seq, min(seq, 512))
    xs, h2_rows, ridx, rw = _merge(ys, yf, gt, x, pos, mod, mod_row, p["norm2_g"], p["merge_w"], 256)
    y_rows = _moe(h2_rows, _moe_plan(ridx, rw), p["w1"], p["w3"], p["w2"])
    out = _final(xs, y_rows, mod, mod_row, p["final_g"], 256)
    return out.reshape(nb, seq, D_MODEL), fin


def kernel(x_prompt, x_sample, state_s5_re, state_s5_im, c, c_ctx, w_ada, b_ada, norm1_g, norm2_g, w_in,
           lam_re, lam_im, log_dt, b_re, b_im, c_re, c_im, d_skip, w_glu, w_proj_s5, w_proj_f, w_out,
           w_router, router_bias, w1, w3, w2, ws1, ws3, ws2, final_norm_g):
    nb_ctx = x_prompt.shape[0]
    nb_lat, seq_lat, _ = x_sample.shape
    half = 2 * S5_STATE

    cond = jnp.concatenate([c_ctx[None], c, jnp.zeros((MOD_ROWS - 1 - nb_lat, D_MODEL), F32)], axis=0)
    mod = _adaln(cond, w_ada[0], b_ada[0]).reshape(MOD_ROWS, N_MOD, D_MODEL)

    s5_ops = _s5_operators(lam_re[0], lam_im[0], log_dt[0], b_re[0], b_im[0], c_re[0], c_im[0], d_skip[0])
    p = dict(
        norm1_g=norm1_g[0][None], norm2_g=norm2_g[0][None], final_g=final_norm_g[None],
        w_in=w_in[0].astype(BF16), w1=w1[0], w3=w3[0], w2=w2[0],
        merge_w=(w_glu[0].astype(BF16), w_proj_s5[0].astype(BF16), w_proj_f[0].astype(BF16),
                 w_out[0].astype(BF16), w_router[0].T, router_bias[0][:, None],
                 ws1[0].astype(BF16), ws3[0].astype(BF16), ws2[0].astype(BF16)))

    def pack_state(sr, si):
        f = lambda a: a.astype(F32).transpose(2, 0, 1, 3).reshape(S5_GROUPS, a.shape[0], half)
        return jnp.concatenate([f(sr), f(si)], axis=-1)

    def unpack_state(fin, lo):
        nb = fin.shape[1]
        return fin[..., lo:lo + half].reshape(S5_GROUPS, nb, 2, S5_STATE).transpose(1, 2, 0, 3)[:, None]

    h0_ctx = jnp.zeros((S5_GROUPS, nb_ctx, 2 * half), F32)
    y_prompt, fin = _stream(x_prompt, None, mod, 0, h0_ctx, s5_ops, p)
    h0_lat = pack_state(state_s5_re[:, 0], state_s5_im[:, 0])
    y_sample, _ = _stream(x_sample, _grid_pos_embed(seq_lat), mod, 1, h0_lat, s5_ops, p)
    return (y_prompt, y_sample, unpack_state(fin, 0).astype(x_prompt.dtype),
            unpack_state(fin, half).astype(x_prompt.dtype))
```

```python
import functools
import math

import jax
import jax.numpy as jnp
from jax import lax
from jax.experimental import pallas as pl
from jax.experimental.pallas import tpu as pltpu

D_MODEL = 1024
GRID_W = 64
D_S5 = 768
S5_GROUP = 16
S5_GROUPS = 48
S5_STATE = 64
D_FNET = 256
FNET_GROUP = 64
N_EXPERTS = 64
TOP_K = 6
N_EXPERT_GROUPS = 8
EXPERTS_PER_GROUP = N_EXPERTS // N_EXPERT_GROUPS
TOPK_GROUPS = 4
D_EXPERT = 256
ROUTED_SCALE = 2.5
N_MOD = 6
EPS = 1e-6

S5_CHUNK = 16
S5_ROW = S5_CHUNK * S5_GROUP
MOD_ROWS = 8
ROUTE_ROWS = 8
LANES = 128
ROW_SUB = D_MODEL // LANES
VMEM_LIMIT = 56 * 1024 * 1024

BF16 = jnp.bfloat16
F32 = jnp.float32


def _params(sem, vmem=VMEM_LIMIT):
    return pltpu.CompilerParams(dimension_semantics=sem, vmem_limit_bytes=vmem)


def _rms(x, g):
    return x * lax.rsqrt(jnp.mean(x * x, axis=-1, keepdims=True) + EPS) * g


def _adaln_kernel(c_ref, w_ref, b_ref, o_ref):
    c = c_ref[...]
    o_ref[...] = jnp.dot(c * jax.nn.sigmoid(c), w_ref[...], precision=lax.Precision.HIGHEST,
                         preferred_element_type=F32) + b_ref[...]


def _adaln(cond, w_ada, b_ada):
    n_out = N_MOD * D_MODEL
    return pl.pallas_call(
        _adaln_kernel,
        out_shape=jax.ShapeDtypeStruct((MOD_ROWS, n_out), F32),
        grid=(N_MOD,),
        in_specs=[pl.BlockSpec((MOD_ROWS, D_MODEL), lambda i: (0, 0)),
                  pl.BlockSpec((D_MODEL, D_MODEL), lambda i: (0, i)),
                  pl.BlockSpec((1, D_MODEL), lambda i: (0, i))],
        out_specs=pl.BlockSpec((MOD_ROWS, D_MODEL), lambda i: (0, i)),
        compiler_params=_params(("arbitrary",)),
        name="adaln",
    )(cond, w_ada, b_ada.reshape(1, n_out))


def _inproj_kernel(has_pos, *refs):
    if has_pos:
        x_ref, pos_ref, mod_ref, g_ref, w_ref, us_ref, uf_ref, gt_ref = refs
        x = x_ref[...] + pos_ref[...]
    else:
        x_ref, mod_ref, g_ref, w_ref, us_ref, uf_ref, gt_ref = refs
        x = x_ref[...]
    m = mod_ref[0]
    h = _rms(x, g_ref[...]) * (1.0 + m[1:2, :]) + m[0:1, :]
    p = jnp.dot(h.astype(BF16), w_ref[...], preferred_element_type=F32)
    us_ref[...] = p[:, :D_S5]
    uf_ref[...] = p[:, D_S5:D_MODEL].astype(BF16)
    gt_ref[...] = jax.nn.sigmoid(p[:, D_MODEL:]).astype(BF16)


def _inproj(x, pos, mod, mod_row, norm_g, w_in_bf, tm):
    n = x.shape[0]
    has_pos = pos is not None
    row = lambda i: (i, 0)
    in_specs = [pl.BlockSpec((tm, D_MODEL), row)]
    args = [x]
    if has_pos:
        nper = pos.shape[0] // tm
        in_specs.append(pl.BlockSpec((tm, D_MODEL), lambda i: (i % nper, 0)))
        args.append(pos)
    in_specs += [pl.BlockSpec((1, N_MOD, D_MODEL), lambda i: (mod_row(i, tm), 0, 0)),
                 pl.BlockSpec((1, D_MODEL), lambda i: (0, 0)),
                 pl.BlockSpec(w_in_bf.shape, lambda i: (0, 0))]
    args += [mod, norm_g, w_in_bf]
    return pl.pallas_call(
        functools.partial(_inproj_kernel, has_pos),
        out_shape=(jax.ShapeDtypeStruct((n, D_S5), F32),
                   jax.ShapeDtypeStruct((n, D_FNET), BF16),
                   jax.ShapeDtypeStruct((n, 2 * D_MODEL), BF16)),
        grid=(n // tm,),
        in_specs=in_specs,
        out_specs=(pl.BlockSpec((tm, D_S5), row), pl.BlockSpec((tm, D_FNET), row),
                   pl.BlockSpec((tm, 2 * D_MODEL), row)),
        compiler_params=_params(("arbitrary",)),
        name="inproj",
    )(*args)


def _s5_operators(lam_re, lam_im, log_dt, b_re, b_im, c_re, c_im, d_skip):
    hi = lax.Precision.HIGHEST
    lr = jnp.minimum(lam_re.astype(F32), -1e-4)
    li = lam_im.astype(F32)
    dt = jnp.exp(log_dt.astype(F32))[..., None]
    mag = jnp.exp(lr * dt)
    ar = mag * jnp.cos(li * dt)
    ai = mag * jnp.sin(li * dt)
    den = lr * lr + li * li
    nr = ar - 1.0
    qr = (nr * lr + ai * li) / den
    qi = (ai * lr - nr * li) / den
    br = b_re.astype(F32)
    bi = b_im.astype(F32)
    bbr = qr[..., None] * br - qi[..., None] * bi
    bbi = qr[..., None] * bi + qi[..., None] * br
    n = jnp.arange(S5_CHUNK + 1, dtype=F32)[:, None, None, None]
    pmag = jnp.exp(n * (lr * dt))
    pr = pmag * jnp.cos(n * (li * dt))
    pi = pmag * jnp.sin(n * (li * dt))
    cr = c_re.astype(F32)
    ci = c_im.astype(F32)
    car = cr[None] * pr[..., None, :] - ci[None] * pi[..., None, :]
    cai = cr[None] * pi[..., None, :] + ci[None] * pr[..., None, :]
    kk = (jnp.einsum("ndghp,dgpi->ndghi", car[:S5_CHUNK], bbr, precision=hi)
          - jnp.einsum("ndghp,dgpi->ndghi", cai[:S5_CHUNK], bbi, precision=hi))
    t = jnp.arange(S5_CHUNK)
    diff = t[None, :] - t[:, None]
    kf = kk[:, 0]
    kb = kk[:, 1]
    mf = jnp.where((diff >= 0)[:, :, None, None, None], kf[jnp.clip(diff, 0, S5_CHUNK - 1)], 0.0)
    mb = jnp.where((diff <= 0)[:, :, None, None, None], kb[jnp.clip(-diff, 0, S5_CHUNK - 1)], 0.0)
    m = (mf + mb).transpose(2, 0, 4, 1, 3).reshape(S5_GROUPS, S5_ROW, S5_ROW)
    dvec = jnp.tile(d_skip.astype(F32).reshape(S5_GROUPS, S5_GROUP), (1, S5_CHUNK))
    m = m + jnp.eye(S5_ROW, dtype=F32)[None] * dvec[:, None, :]

    def w_in_part(pw_r, pw_i, d):
        re = pw_r[..., None] * bbr[d][None] - pw_i[..., None] * bbi[d][None]
        im = pw_r[..., None] * bbi[d][None] + pw_i[..., None] * bbr[d][None]
        f = lambda a: a.transpose(1, 0, 3, 2).reshape(S5_GROUPS, S5_ROW, S5_STATE)
        return f(re), f(im)

    wf_re, wf_im = w_in_part(pr[S5_CHUNK - 1::-1, 0], pi[S5_CHUNK - 1::-1, 0], 0)
    wb_re, wb_im = w_in_part(pr[:S5_CHUNK, 1], pi[:S5_CHUNK, 1], 1)
    w_in = jnp.concatenate([wf_re, wb_re, wf_im, wb_im], axis=-1)

    def w_out_part(ca):
        return ca.transpose(1, 3, 0, 2).reshape(S5_GROUPS, S5_STATE, S5_ROW)

    w_out = jnp.concatenate([w_out_part(car[1:, 0]), w_out_part(car[S5_CHUNK:0:-1, 1]),
                             -w_out_part(cai[1:, 0]), -w_out_part(cai[S5_CHUNK:0:-1, 1])], axis=1)
    coef = jnp.stack([jnp.concatenate([pr[S5_CHUNK, 0], pr[S5_CHUNK, 1]], axis=-1),
                      jnp.concatenate([pi[S5_CHUNK, 0], pi[S5_CHUNK, 1]], axis=-1)], axis=1)
    return m.astype(BF16), w_in.astype(BF16), w_out.astype(BF16), coef


S5_BLOCK_GROUPS = LANES // S5_GROUP


def _s5_perm():
    a = jnp.arange(S5_CHUNK * LANES, dtype=jnp.int32)
    dst = ((a % LANES) // S5_GROUP) * S5_ROW + (a // LANES) * S5_GROUP + a % S5_GROUP
    return (dst[:, None] == a[None, :]).astype(BF16)


def _s5_kernel(nb, nk, u_ref, perm_ref, m_ref, wi_ref, wo_ref, coef_ref, h0_ref, y_ref, fin_ref,
               sr_ref, si_ref, efr_ref, ebr_ref, efi_ref, ebi_ref, uall_ref, yall_ref):
    gl = pl.program_id(1)
    rows = nb * nk
    half = 2 * S5_STATE

    @pl.when(gl == 0)
    def _():
        xcat = jnp.concatenate([u_ref[pl.ds(t, rows, stride=S5_CHUNK), :].astype(BF16)
                                for t in range(S5_CHUNK)], axis=1)
        uall = jnp.dot(xcat, perm_ref[...], preferred_element_type=F32).astype(BF16)
        for g in range(S5_BLOCK_GROUPS):
            uall_ref[g] = uall[:, g * S5_ROW:(g + 1) * S5_ROW]

    u = uall_ref[gl]
    s = jnp.dot(u, wi_ref[0], preferred_element_type=F32)
    sr_ref[...] = s[:, :half]
    si_ref[...] = s[:, half:]
    c_r = coef_ref[0, 0:1, :]
    c_i = coef_ref[0, 1:2, :]
    e_r = h0_ref[0, :, :half]
    e_i = h0_ref[0, :, half:]
    is_fwd = lax.broadcasted_iota(jnp.int32, (nb, half), 1) < S5_STATE
    for j in range(nk):
        rf = pl.ds(j, nb, stride=nk)
        rb = pl.ds(nk - 1 - j, nb, stride=nk)
        efr_ref[rf, :] = e_r
        ebr_ref[rb, :] = e_r
        efi_ref[rf, :] = e_i
        ebi_ref[rb, :] = e_i
        s_r = jnp.where(is_fwd, sr_ref[rf, :], sr_ref[rb, :])
        s_i = jnp.where(is_fwd, si_ref[rf, :], si_ref[rb, :])
        e_r, e_i = c_r * e_r - c_i * e_i + s_r, c_r * e_i + c_i * e_r + s_i
    fin_ref[0, :, :half] = e_r
    fin_ref[0, :, half:] = e_i
    y = jnp.dot(u, m_ref[0], preferred_element_type=F32)
    fwd_rows = lax.broadcasted_iota(jnp.int32, (rows, half), 1) < S5_STATE
    e = jnp.concatenate([jnp.where(fwd_rows, efr_ref[...], ebr_ref[...]),
                         jnp.where(fwd_rows, efi_ref[...], ebi_ref[...])], axis=1).astype(BF16)
    yall_ref[gl] = (y + jnp.dot(e, wo_ref[0], preferred_element_type=F32)).astype(BF16)

    @pl.when(gl == S5_BLOCK_GROUPS - 1)
    def _():
        ycat = jnp.concatenate([yall_ref[g] for g in range(S5_BLOCK_GROUPS)], axis=1)
        out = lax.dot_general(ycat, perm_ref[...], (((1,), (1,)), ((), ())), preferred_element_type=F32)
        for t in range(S5_CHUNK):
            y_ref[pl.ds(t, rows, stride=S5_CHUNK), :] = out[:, t * LANES:(t + 1) * LANES]


def _s5(u, ops, h0, nb, nk):
    m, w_in, w_out, coef = ops
    n = u.shape[0]
    rows = nb * nk
    perm = _s5_perm()
    g3 = lambda b, g: (b * S5_BLOCK_GROUPS + g, 0, 0)
    blk = lambda b, g: (0, b)
    return pl.pallas_call(
        functools.partial(_s5_kernel, nb, nk),
        out_shape=(jax.ShapeDtypeStruct((n, D_S5), F32),
                   jax.ShapeDtypeStruct((S5_GROUPS, nb, 4 * S5_STATE), F32)),
        grid=(S5_GROUPS // S5_BLOCK_GROUPS, S5_BLOCK_GROUPS),
        in_specs=[pl.BlockSpec((n, LANES), blk),
                  pl.BlockSpec(perm.shape, lambda b, g: (0, 0)),
                  pl.BlockSpec((1, S5_ROW, S5_ROW), g3),
                  pl.BlockSpec((1, S5_ROW, 4 * S5_STATE), g3),
                  pl.BlockSpec((1, 4 * S5_STATE, S5_ROW), g3),
                  pl.BlockSpec((1, 2, 2 * S5_STATE), g3),
                  pl.BlockSpec((1, nb, 4 * S5_STATE), g3)],
        out_specs=(pl.BlockSpec((n, LANES), blk),
                   pl.BlockSpec((1, nb, 4 * S5_STATE), g3)),
        scratch_shapes=[pltpu.VMEM((rows, 2 * S5_STATE), F32)] * 6 + [
            pltpu.VMEM((S5_BLOCK_GROUPS, rows, S5_ROW), BF16),
            pltpu.VMEM((S5_BLOCK_GROUPS, rows, S5_ROW), BF16)],
        compiler_params=_params(("arbitrary", "arbitrary")),
        name="s5",
    )(u, perm, m, w_in, w_out, coef, h0)


def _dft_tables(seq):
    j = jnp.arange(seq, dtype=jnp.int32)
    ang = (2.0 * math.pi / seq) * ((j[:, None] * j[None, :]) % seq).astype(F32)
    cs = jnp.concatenate([jnp.cos(ang), -jnp.sin(ang)], axis=1).astype(BF16)
    c = jnp.arange(D_FNET, dtype=jnp.int32)
    same = (c[:, None] // FNET_GROUP) == (c[None, :] // FNET_GROUP)
    angc = (2.0 * math.pi / FNET_GROUP) * (((c[:, None] % FNET_GROUP) * (c[None, :] % FNET_GROUP))
                                           % FNET_GROUP).astype(F32)
    scale = 1.0 / math.sqrt(seq * FNET_GROUP)
    bdc = jnp.where(same, jnp.cos(angc) * scale, 0.0).astype(BF16)
    bds = jnp.where(same, jnp.sin(angc) * scale, 0.0).astype(BF16)
    return cs, bdc, bds


def _fourier_kernel(seq, z_ref, cs_ref, bdc_ref, bds_ref, o_ref, zz_ref):
    @pl.when(pl.program_id(1) == 0)
    def _():
        z = z_ref[...]
        zz_ref[0:seq, :] = jnp.dot(z, bdc_ref[...], preferred_element_type=F32).astype(BF16)
        zz_ref[seq:, :] = jnp.dot(z, bds_ref[...], preferred_element_type=F32).astype(BF16)

    o_ref[...] = jnp.dot(cs_ref[...], zz_ref[...], preferred_element_type=F32).astype(BF16)


def _fourier(z, nb, seq, tl):
    cs, bdc, bds = _dft_tables(seq)
    nt = seq // tl
    return pl.pallas_call(
        functools.partial(_fourier_kernel, seq),
        out_shape=jax.ShapeDtypeStruct(z.shape, BF16),
        grid=(nb, nt),
        in_specs=[pl.BlockSpec((seq, D_FNET), lambda b, i: (b, 0)),
                  pl.BlockSpec((tl, 2 * seq), lambda b, i: (i, 0)),
                  pl.BlockSpec((D_FNET, D_FNET), lambda b, i: (0, 0)),
                  pl.BlockSpec((D_FNET, D_FNET), lambda b, i: (0, 0))],
        out_specs=pl.BlockSpec((tl, D_FNET), lambda b, i: (b * nt + i, 0)),
        scratch_shapes=[pltpu.VMEM((2 * seq, D_FNET), BF16)],
        compiler_params=_params(("arbitrary", "arbitrary")),
        name="fourier",
    )(z, cs, bdc, bds)


def _first_argmax_mask(v, iota, size):
    m = jnp.max(v, axis=0, keepdims=True)
    first = jnp.min(jnp.where(v == m, iota, size), axis=0, keepdims=True)
    return iota == first


def _route(logits_t, bias_col):
    tm = logits_t.shape[1]
    neg = -jnp.inf
    s = jax.nn.sigmoid(logits_t)
    biased = s + bias_col
    io8 = lax.broadcasted_iota(jnp.int32, (EXPERTS_PER_GROUP, tm), 0)
    gs_rows = []
    for g in range(N_EXPERT_GROUPS):
        blk = biased[g * EXPERTS_PER_GROUP:(g + 1) * EXPERTS_PER_GROUP, :]
        m1 = jnp.max(blk, axis=0, keepdims=True)
        rest = jnp.where(_first_argmax_mask(blk, io8, EXPERTS_PER_GROUP), neg, blk)
        gs_rows.append(m1 + jnp.max(rest, axis=0, keepdims=True))
    gs = jnp.concatenate(gs_rows, axis=0)
    iog = lax.broadcasted_iota(jnp.int32, (N_EXPERT_GROUPS, tm), 0)
    gsel = jnp.zeros((N_EXPERT_GROUPS, tm), F32)
    for _ in range(TOPK_GROUPS):
        sel = _first_argmax_mask(gs, iog, N_EXPERT_GROUPS)
        gsel = jnp.where(sel, 1.0, gsel)
        gs = jnp.where(sel, neg, gs)
    emask = jnp.concatenate(
        [jnp.broadcast_to(gsel[g:g + 1, :], (EXPERTS_PER_GROUP, tm)) for g in range(N_EXPERT_GROUPS)], axis=0)
    v = jnp.where(emask > 0.0, biased, neg)
    ioe = lax.broadcasted_iota(jnp.int32, (N_EXPERTS, tm), 0)
    idx_rows, s_rows = [], []
    for _ in range(TOP_K):
        sel = _first_argmax_mask(v, ioe, N_EXPERTS)
        idx_rows.append(jnp.sum(jnp.where(sel, ioe, 0), axis=0, keepdims=True))
        s_rows.append(jnp.sum(jnp.where(sel, s, 0.0), axis=0, keepdims=True))
        v = jnp.where(sel, neg, v)
    denom = s_rows[0]
    for r in s_rows[1:]:
        denom = denom + r
    pad = ROUTE_ROWS - TOP_K
    idx = jnp.concatenate(idx_rows + [jnp.zeros((pad, tm), jnp.int32)], axis=0)
    w = jnp.concatenate([r / denom * ROUTED_SCALE for r in s_rows] + [jnp.zeros((pad, tm), F32)], axis=0)
    return idx, w


def _merge_kernel(has_pos, *refs):
    if has_pos:
        (ys_ref, yf_ref, gt_ref, x_ref, pos_ref, mod_ref, n2_ref, wglu_ref, wps_ref, wpf_ref, wout_ref,
         wrt_ref, rb_ref, ws1_ref, ws3_ref, ws2_ref, xs_ref, h2_ref, ridx_ref, rw_ref) = refs
        x = x_ref[...] + pos_ref[...]
    else:
        (ys_ref, yf_ref, gt_ref, x_ref, mod_ref, n2_ref, wglu_ref, wps_ref, wpf_ref, wout_ref,
         wrt_ref, rb_ref, ws1_ref, ws3_ref, ws2_ref, xs_ref, h2_ref, ridx_ref, rw_ref) = refs
        x = x_ref[...]
    m = mod_ref[0]
    g = jax.nn.gelu(ys_ref[...].astype(F32))
    a = g * jax.nn.sigmoid(jnp.dot(g.astype(BF16), wglu_ref[...], preferred_element_type=F32))
    pa = jnp.dot(a.astype(BF16), wps_ref[...], preferred_element_type=F32)
    pb = jnp.dot(yf_ref[...], wpf_ref[...], preferred_element_type=F32)
    gt = gt_ref[...].astype(F32)
    merged = gt[:, :D_MODEL] * pa + gt[:, D_MODEL:] * pb
    x1 = x + m[2:3, :] * jnp.dot(merged.astype(BF16), wout_ref[...], preferred_element_type=F32)
    h2 = _rms(x1, n2_ref[...]) * (1.0 + m[4:5, :]) + m[3:4, :]
    hb = h2.astype(BF16)
    for j in range(ROW_SUB):
        h2_ref[pl.ds(j, h2.shape[0], stride=ROW_SUB), :] = h2[:, j * LANES:(j + 1) * LANES]
    logits_t = lax.dot_general(wrt_ref[...], h2, (((1,), (1,)), ((), ())),
                               precision=lax.Precision.HIGHEST, preferred_element_type=F32)
    ridx_ref[...], rw_ref[...] = _route(logits_t, rb_ref[...])
    s1 = jnp.dot(hb, ws1_ref[...], preferred_element_type=F32)
    s3 = jnp.dot(hb, ws3_ref[...], preferred_element_type=F32)
    shared = jnp.dot((s1 * jax.nn.sigmoid(s1) * s3).astype(BF16), ws2_ref[...], preferred_element_type=F32)
    xs_ref[...] = x1 + m[5:6, :] * shared


def _merge(ys, yf, gt, x, pos, mod, mod_row, n2, weights, tm):
    n = x.shape[0]
    has_pos = pos is not None
    row = lambda i: (i, 0)
    const = lambda a: pl.BlockSpec(a.shape, lambda i: (0,) * a.ndim)
    in_specs = [pl.BlockSpec((tm, D_S5), row), pl.BlockSpec((tm, D_FNET), row),
                pl.BlockSpec((tm, 2 * D_MODEL), row), pl.BlockSpec((tm, D_MODEL), row)]
    args = [ys, yf, gt, x]
    if has_pos:
        nper = pos.shape[0] // tm
        in_specs.append(pl.BlockSpec((tm, D_MODEL), lambda i: (i % nper, 0)))
        args.append(pos)
    in_specs += [pl.BlockSpec((1, N_MOD, D_MODEL), lambda i: (mod_row(i, tm), 0, 0)), const(n2)]
    args += [mod, n2]
    in_specs += [const(w) for w in weights]
    args += list(weights)
    return pl.pallas_call(
        functools.partial(_merge_kernel, has_pos),
        out_shape=(jax.ShapeDtypeStruct((n, D_MODEL), F32),
                   jax.ShapeDtypeStruct((n * ROW_SUB, LANES), F32),
                   jax.ShapeDtypeStruct((ROUTE_ROWS, n), jnp.int32),
                   jax.ShapeDtypeStruct((ROUTE_ROWS, n), F32)),
        grid=(n // tm,),
        in_specs=in_specs,
        out_specs=(pl.BlockSpec((tm, D_MODEL), row), pl.BlockSpec((tm * ROW_SUB, LANES), row),
                   pl.BlockSpec((ROUTE_ROWS, tm), lambda i: (0, i)),
                   pl.BlockSpec((ROUTE_ROWS, tm), lambda i: (0, i))),
        compiler_params=_params(("arbitrary",)),
        name="merge",
    )(*args)


MOE_SUB = 4096
MOE_TM = 128
MOE_TMAX = MOE_SUB * TOP_K // MOE_TM + N_EXPERTS
MOE_TS = MOE_TMAX + 2
MOE_DUMMY = 256
MOE_RMW = 16
ROW_TILE = MOE_TM * ROW_SUB


def _moe_plan(ridx, rw):
    n = ridx.shape[1]
    nsub = n // MOE_SUB
    npair = n * TOP_K
    t = jnp.arange(n, dtype=jnp.int32)
    key = ((t // MOE_SUB) * N_EXPERTS)[None] + ridx[:TOP_K]
    key = (key * MOE_SUB + (t % MOE_SUB)[None]).reshape(-1)
    skey, sw = lax.sort((key, rw[:TOP_K].reshape(-1)), num_keys=1)
    stok = jnp.concatenate([(skey % MOE_SUB) * ROW_SUB, jnp.zeros((MOE_TM,), jnp.int32)])
    sw_rows = jnp.concatenate([sw, jnp.zeros((LANES,), F32)]).reshape(npair // LANES + 1, 1, LANES)
    hits = ridx[:TOP_K].reshape(TOP_K, nsub, 1, MOE_SUB) == jnp.arange(N_EXPERTS, dtype=jnp.int32)[None, None, :, None]
    cnt = jnp.sum(hits.astype(jnp.int32), axis=(0, 3))
    poff = (jnp.cumsum(cnt.reshape(-1)) - cnt.reshape(-1)).reshape(nsub, N_EXPERTS)
    ntile = (cnt + MOE_TM - 1) // MOE_TM
    tcum = jnp.cumsum(ntile, axis=1)
    toff = tcum - ntile
    tstart = jnp.concatenate([toff, tcum[:, -1:]], axis=1).reshape(-1).astype(jnp.int32)
    j = jnp.arange(MOE_TS, dtype=jnp.int32) - 1
    valid = (j[None] >= 0) & (j[None] < tcum[:, -1:])
    te = jnp.minimum(jnp.sum(j[None, :, None] >= tcum[:, None, :], axis=-1), N_EXPERTS - 1)
    pick = lambda a: jnp.take_along_axis(a, te, axis=1)
    first = (j[None] - pick(toff)) * MOE_TM
    p0 = jnp.where(valid, pick(poff) + first, 0).reshape(-1).astype(jnp.int32)
    nv = jnp.where(valid, jnp.minimum(pick(cnt) - first, MOE_TM), 0).reshape(-1).astype(jnp.int32)
    return tstart, p0, nv, stok, sw_rows


def _moe_kernel(ts_ref, p0_ref, nv_ref, tok_ref, sw_ref, src_ref, w1_ref, w3_ref, w2_ref, y_ref,
                xt_ref, xb_ref, ot_ref, w1b_ref, w3b_ref, w2b_ref):
    sub = pl.program_id(0)
    e = pl.program_id(1)
    base = sub * MOE_TS + 1

    def gather(p0):
        for mi in range(MOE_TM):
            tok = pl.multiple_of(tok_ref[p0 + mi], ROW_SUB)
            xt_ref[mi * ROW_SUB:(mi + 1) * ROW_SUB, :] = src_ref[pl.ds(tok, ROW_SUB), :]
        for j in range(ROW_SUB):
            xb_ref[:, j * LANES:(j + 1) * LANES] = xt_ref[pl.ds(j, MOE_TM, stride=ROW_SUB), :].astype(BF16)

    def scatter(p0, nv, masked):
        for u in range(0, MOE_TM, MOE_RMW):
            new = []
            for i in range(MOE_RMW):
                tok = tok_ref[p0 + u + i]
                if masked:
                    tok = jnp.where(u + i < nv, tok, MOE_SUB * ROW_SUB)
                tok = pl.multiple_of(tok, ROW_SUB)
                new.append((tok, y_ref[pl.ds(tok, ROW_SUB), :]
                            + ot_ref[(u + i) * ROW_SUB:(u + i + 1) * ROW_SUB, :]))
            for tok, v in new:
                y_ref[pl.ds(tok, ROW_SUB), :] = v

    @pl.when(e == 0)
    def _():
        y_ref[...] = jnp.zeros_like(y_ref)
        ot_ref[...] = jnp.zeros_like(ot_ref)
        gather(p0_ref[base])

    w1b_ref[...] = w1_ref[0].astype(BF16)
    w3b_ref[...] = w3_ref[0].astype(BF16)
    w2b_ref[...] = w2_ref[0].astype(BF16)

    def step(cur, masked):
        scatter(p0_ref[cur - 1], nv_ref[cur - 1], masked)
        p0 = p0_ref[cur]
        nv = nv_ref[cur]
        x = xb_ref[...]
        a = jnp.dot(x, w1b_ref[...], preferred_element_type=F32)
        b = jnp.dot(x, w3b_ref[...], preferred_element_type=F32)
        r0 = p0 // LANES
        c = p0 % LANES
        lane = lax.broadcasted_iota(jnp.int32, (1, LANES), 1)
        ga = pltpu.roll(sw_ref[r0], LANES - c, axis=1)
        gb = pltpu.roll(sw_ref[r0 + 1], LANES - c, axis=1)
        g = jnp.where(lane < nv, jnp.where(lane < LANES - c, ga, gb), 0.0)
        rows = lax.broadcasted_iota(jnp.int32, (MOE_TM, MOE_TM), 0)
        cols = lax.broadcasted_iota(jnp.int32, (MOE_TM, MOE_TM), 1)
        gcol = jnp.sum(jnp.where(rows == cols, jnp.broadcast_to(g, (MOE_TM, MOE_TM)), 0.0),
                       axis=1, keepdims=True)
        act = (a * jax.nn.sigmoid(a) * b * gcol).astype(BF16)
        o = jnp.dot(act, w2b_ref[...], preferred_element_type=F32)
        for j in range(ROW_SUB):
            ot_ref[pl.ds(j, MOE_TM, stride=ROW_SUB), :] = o[:, j * LANES:(j + 1) * LANES]
        gather(p0_ref[cur + 1])

    def body(i, carry):
        cur = base + i
        prev_full = nv_ref[cur - 1] == MOE_TM

        @pl.when(prev_full)
        def _():
            step(cur, False)

        @pl.when(jnp.logical_not(prev_full))
        def _():
            step(cur, True)

        return carry

    first = ts_ref[sub * (N_EXPERTS + 1) + e]
    last = ts_ref[sub * (N_EXPERTS + 1) + e + 1]
    lax.fori_loop(first, last, body, 0)

    @pl.when(e == N_EXPERTS - 1)
    def _():
        scatter(p0_ref[base + last - 1], nv_ref[base + last - 1], True)


def _moe(h2_rows, plan, w1, w3, w2):
    tstart, p0, nv, stok, sw_rows = plan
    nsub = h2_rows.shape[0] // (MOE_SUB * ROW_SUB)
    wmap = lambda s, e, ts, p0, nv: (e, 0, 0)
    sub2 = lambda s, e, ts, p0, nv: (s, 0)
    grid_spec = pltpu.PrefetchScalarGridSpec(
        num_scalar_prefetch=3,
        grid=(nsub, N_EXPERTS),
        in_specs=[pl.BlockSpec(memory_space=pltpu.SMEM),
                  pl.BlockSpec(sw_rows.shape, lambda s, e, ts, p0, nv: (0, 0, 0)),
                  pl.BlockSpec((MOE_SUB * ROW_SUB, LANES), sub2, pipeline_mode=pl.Buffered(1)),
                  pl.BlockSpec((1, D_MODEL, D_EXPERT), wmap),
                  pl.BlockSpec((1, D_MODEL, D_EXPERT), wmap),
                  pl.BlockSpec((1, D_EXPERT, D_MODEL), wmap)],
        out_specs=pl.BlockSpec(((MOE_SUB + MOE_DUMMY) * ROW_SUB, LANES), sub2, pipeline_mode=pl.Buffered(1)),
        scratch_shapes=[pltpu.VMEM((ROW_TILE, LANES), F32), pltpu.VMEM((MOE_TM, D_MODEL), BF16),
                        pltpu.VMEM((ROW_TILE, LANES), F32),
                        pltpu.VMEM((D_MODEL, D_EXPERT), BF16), pltpu.VMEM((D_MODEL, D_EXPERT), BF16),
                        pltpu.VMEM((D_EXPERT, D_MODEL), BF16)])
    return pl.pallas_call(
        _moe_kernel,
        grid_spec=grid_spec,
        out_shape=jax.ShapeDtypeStruct((nsub * (MOE_SUB + MOE_DUMMY) * ROW_SUB, LANES), F32),
        compiler_params=_params(("arbitrary", "arbitrary")),
        name="moe",
    )(tstart, p0, nv, stok, sw_rows, h2_rows, w1, w3, w2)


def _final_kernel(xs_ref, y_ref, mod_ref, fg_ref, o_ref):
    tm = xs_ref.shape[0]
    y = jnp.concatenate([y_ref[pl.ds(j, tm, stride=ROW_SUB), :] for j in range(ROW_SUB)], axis=1)
    x2 = xs_ref[...] + mod_ref[0][5:6, :] * y
    o_ref[...] = _rms(x2, fg_ref[...])


def _final(xs, y_rows, mod, mod_row, fg, tm):
    n = xs.shape[0]
    per_sub = MOE_SUB // tm
    stride = (MOE_SUB + MOE_DUMMY) // tm
    row = lambda i: (i, 0)
    return pl.pallas_call(
        _final_kernel,
        out_shape=jax.ShapeDtypeStruct((n, D_MODEL), F32),
        grid=(n // tm,),
        in_specs=[pl.BlockSpec((tm, D_MODEL), row),
                  pl.BlockSpec((tm * ROW_SUB, LANES), lambda i: ((i // per_sub) * stride + i % per_sub, 0)),
                  pl.BlockSpec((1, N_MOD, D_MODEL), lambda i: (mod_row(i, tm), 0, 0)),
                  pl.BlockSpec((1, D_MODEL), lambda i: (0, 0))],
        out_specs=pl.BlockSpec((tm, D_MODEL), row),
        compiler_params=_params(("arbitrary",)),
        name="final",
    )(xs, y_rows, mod, fg)


def _grid_pos_embed(n_tokens):
    rows = n_tokens // GRID_W
    r, col = jnp.meshgrid(jnp.arange(rows, dtype=F32), jnp.arange(GRID_W, dtype=F32), indexing="ij")
    quarter = D_MODEL // 4
    omega = 1.0 / (10000.0 ** (jnp.arange(quarter, dtype=F32) / quarter))

    def emb(p):
        a = p.reshape(-1)[:, None] * omega
        return jnp.concatenate([jnp.sin(a), jnp.cos(a)], axis=-1)

    return jnp.concatenate([emb(r), emb(col)], axis=-1)


def _stream(x3, pos, mod, first_row, h0, s5_ops, p):
    nb, seq, _ = x3.shape
    n = nb * seq
    nk = seq // S5_CHUNK
    x = x3.reshape(n, D_MODEL)
    per_seq_mod = first_row > 0

    def mod_row(i, tm):
        return first_row + (i * tm) // seq if per_seq_mod else 0

    us, uf, gt = _inproj(x, pos, mod, mod_row, p["norm1_g"], p["w_in"], 512)
    ys, fin = _s5(us, s5_ops, h0, nb, nk)
    yf = _fourier(uf, nb, seq, min(seq, 512))
    xs, h2_rows, ridx, rw = _merge(ys, yf, gt, x, pos, mod, mod_row, p["norm2_g"], p["merge_w"], 256)
    y_rows = _moe(h2_rows, _moe_plan(ridx, rw), p["w1"], p["w3"], p["w2"])
    out = _final(xs, y_rows, mod, mod_row, p["final_g"], 256)
    return out.reshape(nb, seq, D_MODEL), fin


def kernel(x_prompt, x_sample, state_s5_re, state_s5_im, c, c_ctx, w_ada, b_ada, norm1_g, norm2_g, w_in,
           lam_re, lam_im, log_dt, b_re, b_im, c_re, c_im, d_skip, w_glu, w_proj_s5, w_proj_f, w_out,
           w_router, router_bias, w1, w3, w2, ws1, ws3, ws2, final_norm_g):
    nb_ctx = x_prompt.shape[0]
    nb_lat, seq_lat, _ = x_sample.shape
    half = 2 * S5_STATE

    cond = jnp.concatenate([c_ctx[None], c, jnp.zeros((MOD_ROWS - 1 - nb_lat, D_MODEL), F32)], axis=0)
    mod = _adaln(cond, w_ada[0], b_ada[0]).reshape(MOD_ROWS, N_MOD, D_MODEL)

    s5_ops = _s5_operators(lam_re[0], lam_im[0], log_dt[0], b_re[0], b_im[0], c_re[0], c_im[0], d_skip[0])
    p = dict(
        norm1_g=norm1_g[0][None], norm2_g=norm2_g[0][None], final_g=final_norm_g[None],
        w_in=w_in[0].astype(BF16), w1=w1[0], w3=w3[0], w2=w2[0],
        merge_w=(w_glu[0].astype(BF16), w_proj_s5[0].astype(BF16), w_proj_f[0].astype(BF16),
                 w_out[0].astype(BF16), w_router[0].T, router_bias[0][:, None],
                 ws1[0].astype(BF16), ws3[0].astype(BF16), ws2[0].astype(BF16)))

    def pack_state(sr, si):
        f = lambda a: a.astype(F32).transpose(2, 0, 1, 3).reshape(S5_GROUPS, a.shape[0], half)
        return jnp.concatenate([f(sr), f(si)], axis=-1)

    def unpack_state(fin, lo):
        nb = fin.shape[1]
        return fin[..., lo:lo + half].reshape(S5_GROUPS, nb, 2, S5_STATE).transpose(1, 2, 0, 3)[:, None]

    h0_ctx = jnp.zeros((S5_GROUPS, nb_ctx, 2 * half), F32)
    y_prompt, fin = _stream(x_prompt, None, mod, 0, h0_ctx, s5_ops, p)
    h0_lat = pack_state(state_s5_re[:, 0], state_s5_im[:, 0])
    y_sample, _ = _stream(x_sample, _grid_pos_embed(seq_lat), mod, 1, h0_lat, s5_ops, p)
    return (y_prompt, y_sample, unpack_state(fin, 0).astype(x_prompt.dtype),
            unpack_state(fin, half).astype(x_prompt.dtype))
```

```python
import functools
import math

import jax
import jax.numpy as jnp
from jax import lax
from jax.experimental import pallas as pl
from jax.experimental.pallas import tpu as pltpu

D_MODEL = 1024
GRID_W = 64
D_S5 = 768
S5_GROUP = 16
S5_GROUPS = 48
S5_STATE = 64
D_FNET = 256
FNET_GROUP = 64
N_EXPERTS = 64
TOP_K = 6
N_EXPERT_GROUPS = 8
EXPERTS_PER_GROUP = N_EXPERTS // N_EXPERT_GROUPS
TOPK_GROUPS = 4
D_EXPERT = 256
ROUTED_SCALE = 2.5
N_MOD = 6
EPS = 1e-6

S5_CHUNK = 16
S5_ROW = S5_CHUNK * S5_GROUP
MOD_ROWS = 8
ROUTE_ROWS = 8
LANES = 128
ROW_SUB = D_MODEL // LANES
VMEM_LIMIT = 56 * 1024 * 1024

BF16 = jnp.bfloat16
F32 = jnp.float32


def _params(sem, vmem=VMEM_LIMIT):
    return pltpu.CompilerParams(dimension_semantics=sem, vmem_limit_bytes=vmem)


def _rms(x, g):
    return x * lax.rsqrt(jnp.mean(x * x, axis=-1, keepdims=True) + EPS) * g


def _adaln_kernel(c_ref, w_ref, b_ref, o_ref):
    c = c_ref[...]
    o_ref[...] = jnp.dot(c * jax.nn.sigmoid(c), w_ref[...], precision=lax.Precision.HIGHEST,
                         preferred_element_type=F32) + b_ref[...]


def _adaln(cond, w_ada, b_ada):
    n_out = N_MOD * D_MODEL
    return pl.pallas_call(
        _adaln_kernel,
        out_shape=jax.ShapeDtypeStruct((MOD_ROWS, n_out), F32),
        grid=(N_MOD,),
        in_specs=[pl.BlockSpec((MOD_ROWS, D_MODEL), lambda i: (0, 0)),
                  pl.BlockSpec((D_MODEL, D_MODEL), lambda i: (0, i)),
                  pl.BlockSpec((1, D_MODEL), lambda i: (0, i))],
        out_specs=pl.BlockSpec((MOD_ROWS, D_MODEL), lambda i: (0, i)),
        compiler_params=_params(("arbitrary",)),
        name="adaln",
    )(cond, w_ada, b_ada.reshape(1, n_out))


def _inproj_kernel(has_pos, *refs):
    if has_pos:
        x_ref, pos_ref, mod_ref, g_ref, w_ref, us_ref, uf_ref, gt_ref = refs
        x = x_ref[...] + pos_ref[...]
    else:
        x_ref, mod_ref, g_ref, w_ref, us_ref, uf_ref, gt_ref = refs
        x = x_ref[...]
    m = mod_ref[0]
    h = _rms(x, g_ref[...]) * (1.0 + m[1:2, :]) + m[0:1, :]
    p = jnp.dot(h.astype(BF16), w_ref[...], preferred_element_type=F32)
    us_ref[...] = p[:, :D_S5]
    uf_ref[...] = p[:, D_S5:D_MODEL].astype(BF16)
    gt_ref[...] = jax.nn.sigmoid(p[:, D_MODEL:]).astype(BF16)


def _inproj(x, pos, mod, mod_row, norm_g, w_in_bf, tm):
    n = x.shape[0]
    has_pos = pos is not None
    row = lambda i: (i, 0)
    in_specs = [pl.BlockSpec((tm, D_MODEL), row)]
    args = [x]
    if has_pos:
        nper = pos.shape[0] // tm
        in_specs.append(pl.BlockSpec((tm, D_MODEL), lambda i: (i % nper, 0)))
        args.append(pos)
    in_specs += [pl.BlockSpec((1, N_MOD, D_MODEL), lambda i: (mod_row(i, tm), 0, 0)),
                 pl.BlockSpec((1, D_MODEL), lambda i: (0, 0)),
                 pl.BlockSpec(w_in_bf.shape, lambda i: (0, 0))]
    args += [mod, norm_g, w_in_bf]
    return pl.pallas_call(
        functools.partial(_inproj_kernel, has_pos),
        out_shape=(jax.ShapeDtypeStruct((n, D_S5), F32),
                   jax.ShapeDtypeStruct((n, D_FNET), BF16),
                   jax.ShapeDtypeStruct((n, 2 * D_MODEL), BF16)),
        grid=(n // tm,),
        in_specs=in_specs,
        out_specs=(pl.BlockSpec((tm, D_S5), row), pl.BlockSpec((tm, D_FNET), row),
                   pl.BlockSpec((tm, 2 * D_MODEL), row)),
        compiler_params=_params(("arbitrary",)),
        name="inproj",
    )(*args)


def _shift_lanes(x, k):
    if k == 0:
        return x
    z = jnp.zeros((x.shape[0], abs(k)), x.dtype)
    if k > 0:
        return jnp.concatenate([z, x[:, :x.shape[1] - k]], axis=1)
    return jnp.concatenate([x[:, -k:], z], axis=1)


def _s5ops_kernel(lam_re_ref, lam_im_ref, dt_ref, btr_ref, bti_ref, cr_ref, ci_ref, d_ref,
                  m_ref, wi_ref, wot_ref, coef_ref):
    hi = lax.Precision.HIGHEST
    lr = jnp.minimum(lam_re_ref[0], -1e-4)
    li = lam_im_ref[0]
    dt = jnp.exp(dt_ref[0])
    mag = jnp.exp(lr * dt)
    ar = mag * jnp.cos(li * dt)
    ai = mag * jnp.sin(li * dt)
    den = lr * lr + li * li
    nr = ar - 1.0
    qr = (nr * lr + ai * li) / den
    qi = (ai * lr - nr * li) / den
    pr, pi = [], []
    for n in range(S5_CHUNK + 1):
        pm = jnp.exp(float(n) * (lr * dt))
        pr.append(pm * jnp.cos(float(n) * (li * dt)))
        pi.append(pm * jnp.sin(float(n) * (li * dt)))
    bbr, bbi, car, cai = [], [], [], []
    for d in range(2):
        btr = btr_ref[0, d]
        bti = bti_ref[0, d]
        bbr.append(qr[d:d + 1] * btr - qi[d:d + 1] * bti)
        bbi.append(qr[d:d + 1] * bti + qi[d:d + 1] * btr)
        cr = cr_ref[0, d]
        ci = ci_ref[0, d]
        car.append([cr * pr[n][d:d + 1] - ci * pi[n][d:d + 1] for n in range(S5_CHUNK + 1)])
        cai.append([cr * pi[n][d:d + 1] + ci * pr[n][d:d + 1] for n in range(S5_CHUNK + 1)])

    def lag_kernels(d, order):
        a = jnp.concatenate([car[d][n] for n in order], axis=0)
        b = jnp.concatenate([cai[d][n] for n in order], axis=0)
        dn = (((1,), (1,)), ((), ()))
        return (lax.dot_general(bbr[d], a, dn, precision=hi, preferred_element_type=F32)
                - lax.dot_general(bbi[d], b, dn, precision=hi, preferred_element_type=F32))

    ktf = lag_kernels(0, range(S5_CHUNK))
    ktb = lag_kernels(1, range(S5_CHUNK - 1, -1, -1))
    row = lax.broadcasted_iota(jnp.int32, (S5_GROUP, S5_ROW), 0)
    lane = lax.broadcasted_iota(jnp.int32, (S5_GROUP, S5_ROW), 1)
    dcol = d_ref[0]
    for j in range(S5_CHUNK):
        rows = slice(j * S5_GROUP, (j + 1) * S5_GROUP)
        blk = _shift_lanes(ktf, S5_GROUP * j) + _shift_lanes(ktb, -S5_GROUP * (S5_CHUNK - 1 - j))
        blk = blk + jnp.where(lane == S5_GROUP * j + row, dcol, 0.0)
        m_ref[0, rows, :] = blk.astype(BF16)
        nf = S5_CHUNK - 1 - j
        wi = jnp.concatenate([pr[nf][0:1] * bbr[0] - pi[nf][0:1] * bbi[0],
                              pr[j][1:2] * bbr[1] - pi[j][1:2] * bbi[1],
                              pr[nf][0:1] * bbi[0] + pi[nf][0:1] * bbr[0],
                              pr[j][1:2] * bbi[1] + pi[j][1:2] * bbr[1]], axis=1)
        wi_ref[0, rows, :] = wi.astype(BF16)
        wot = jnp.concatenate([car[0][j + 1], car[1][S5_CHUNK - j],
                               -cai[0][j + 1], -cai[1][S5_CHUNK - j]], axis=1)
        wot_ref[0, rows, :] = wot.astype(BF16)
    coef_ref[0, 0:1, :] = jnp.concatenate([pr[S5_CHUNK][0:1], pr[S5_CHUNK][1:2]], axis=1)
    coef_ref[0, 1:2, :] = jnp.concatenate([pi[S5_CHUNK][0:1], pi[S5_CHUNK][1:2]], axis=1)


def _s5ops(lam_re, lam_im, log_dt, b_re, b_im, c_re, c_im, d_skip):
    g3 = lambda g: (g, 0, 0)
    g4 = lambda g: (g, 0, 0, 0)
    sw = lambda a: jnp.swapaxes(a.astype(F32), 0, 1)
    dt = jnp.broadcast_to(sw(log_dt)[..., None], (S5_GROUPS, 2, S5_STATE))
    args = (sw(lam_re), sw(lam_im), dt, sw(jnp.swapaxes(b_re, 2, 3)), sw(jnp.swapaxes(b_im, 2, 3)),
            sw(c_re), sw(c_im), d_skip.astype(F32).reshape(S5_GROUPS, S5_GROUP, 1))
    vec = pl.BlockSpec((1, 2, S5_STATE), g3)
    mat = pl.BlockSpec((1, 2, S5_GROUP, S5_STATE), g4)
    op = pl.BlockSpec((1, S5_ROW, S5_ROW), g3)
    return pl.pallas_call(
        _s5ops_kernel,
        out_shape=(jax.ShapeDtypeStruct((S5_GROUPS, S5_ROW, S5_ROW), BF16),) * 3
        + (jax.ShapeDtypeStruct((S5_GROUPS, 2, 2 * S5_STATE), F32),),
        grid=(S5_GROUPS,),
        in_specs=[vec, vec, vec, mat, mat, mat, mat, pl.BlockSpec((1, S5_GROUP, 1), g3)],
        out_specs=(op, op, op, pl.BlockSpec((1, 2, 2 * S5_STATE), g3)),
        compiler_params=_params(("arbitrary",)),
        name="s5ops",
    )(*args)


S5_BLOCK_GROUPS = LANES // S5_GROUP


def _s5_perm():
    a = jnp.arange(S5_CHUNK * LANES, dtype=jnp.int32)
    dst = ((a % LANES) // S5_GROUP) * S5_ROW + (a // LANES) * S5_GROUP + a % S5_GROUP
    return (dst[:, None] == a[None, :]).astype(BF16)


def _s5_kernel(nb, nk, u_ref, perm_ref, m_ref, wi_ref, wot_ref, coef_ref, h0_ref, y_ref, fin_ref,
               sr_ref, si_ref, efr_ref, ebr_ref, efi_ref, ebi_ref, uall_ref, yall_ref):
    gl = pl.program_id(1)
    rows = nb * nk
    half = 2 * S5_STATE

    @pl.when(gl == 0)
    def _():
        xcat = jnp.concatenate([u_ref[pl.ds(t, rows, stride=S5_CHUNK), :].astype(BF16)
                                for t in range(S5_CHUNK)], axis=1)
        uall = jnp.dot(xcat, perm_ref[...], preferred_element_type=F32).astype(BF16)
        for g in range(S5_BLOCK_GROUPS):
            uall_ref[g] = uall[:, g * S5_ROW:(g + 1) * S5_ROW]

    u = uall_ref[gl]
    s = jnp.dot(u, wi_ref[0], preferred_element_type=F32)
    sr_ref[...] = s[:, :half]
    si_ref[...] = s[:, half:]
    c_r = coef_ref[0, 0:1, :]
    c_i = coef_ref[0, 1:2, :]
    e_r = h0_ref[0, :, :half]
    e_i = h0_ref[0, :, half:]
    is_fwd = lax.broadcasted_iota(jnp.int32, (nb, half), 1) < S5_STATE
    for j in range(nk):
        rf = pl.ds(j, nb, stride=nk)
        rb = pl.ds(nk - 1 - j, nb, stride=nk)
        efr_ref[rf, :] = e_r
        ebr_ref[rb, :] = e_r
        efi_ref[rf, :] = e_i
        ebi_ref[rb, :] = e_i
        s_r = jnp.where(is_fwd, sr_ref[rf, :], sr_ref[rb, :])
        s_i = jnp.where(is_fwd, si_ref[rf, :], si_ref[rb, :])
        e_r, e_i = c_r * e_r - c_i * e_i + s_r, c_r * e_i + c_i * e_r + s_i
    fin_ref[0, :, :half] = e_r
    fin_ref[0, :, half:] = e_i
    y = jnp.dot(u, m_ref[0], preferred_element_type=F32)
    fwd_rows = lax.broadcasted_iota(jnp.int32, (rows, half), 1) < S5_STATE
    e = jnp.concatenate([jnp.where(fwd_rows, efr_ref[...], ebr_ref[...]),
                         jnp.where(fwd_rows, efi_ref[...], ebi_ref[...])], axis=1).astype(BF16)
    y = y + lax.dot_general(e, wot_ref[0], (((1,), (1,)), ((), ())), preferred_element_type=F32)
    yall_ref[gl] = y.astype(BF16)

    @pl.when(gl == S5_BLOCK_GROUPS - 1)
    def _():
        ycat = jnp.concatenate([yall_ref[g] for g in range(S5_BLOCK_GROUPS)], axis=1)
        out = lax.dot_general(ycat, perm_ref[...], (((1,), (1,)), ((), ())), preferred_element_type=F32)
        for t in range(S5_CHUNK):
            y_ref[pl.ds(t, rows, stride=S5_CHUNK), :] = out[:, t * LANES:(t + 1) * LANES]


def _s5(u, ops, h0, nb, nk):
    m, w_in, w_out, coef = ops
    n = u.shape[0]
    rows = nb * nk
    perm = _s5_perm()
    g3 = lambda b, g: (b * S5_BLOCK_GROUPS + g, 0, 0)
    blk = lambda b, g: (0, b)
    return pl.pallas_call(
        functools.partial(_s5_kernel, nb, nk),
        out_shape=(jax.ShapeDtypeStruct((n, D_S5), F32),
                   jax.ShapeDtypeStruct((S5_GROUPS, nb, 4 * S5_STATE), F32)),
        grid=(S5_GROUPS // S5_BLOCK_GROUPS, S5_BLOCK_GROUPS),
        in_specs=[pl.BlockSpec((n, LANES), blk),
                  pl.BlockSpec(perm.shape, lambda b, g: (0, 0)),
                  pl.BlockSpec((1, S5_ROW, S5_ROW), g3),
                  pl.BlockSpec((1, S5_ROW, 4 * S5_STATE), g3),
                  pl.BlockSpec((1, 4 * S5_STATE, S5_ROW), g3),
                  pl.BlockSpec((1, 2, 2 * S5_STATE), g3),
                  pl.BlockSpec((1, nb, 4 * S5_STATE), g3)],
        out_specs=(pl.BlockSpec((n, LANES), blk),
                   pl.BlockSpec((1, nb, 4 * S5_STATE), g3)),
        scratch_shapes=[pltpu.VMEM((rows, 2 * S5_STATE), F32)] * 6 + [
            pltpu.VMEM((S5_BLOCK_GROUPS, rows, S5_ROW), BF16),
            pltpu.VMEM((S5_BLOCK_GROUPS, rows, S5_ROW), BF16)],
        compiler_params=_params(("arbitrary", "arbitrary")),
        name="s5",
    )(u, perm, m, w_in, w_out, coef, h0)


def _dft_tables(seq):
    j = jnp.arange(seq, dtype=jnp.int32)
    ang = (2.0 * math.pi / seq) * ((j[:, None] * j[None, :]) % seq).astype(F32)
    cs = jnp.concatenate([jnp.cos(ang), -jnp.sin(ang)], axis=1).astype(BF16)
    c = jnp.arange(D_FNET, dtype=jnp.int32)
    same = (c[:, None] // FNET_GROUP) == (c[None, :] // FNET_GROUP)
    angc = (2.0 * math.pi / FNET_GROUP) * (((c[:, None] % FNET_GROUP) * (c[None, :] % FNET_GROUP))
                                           % FNET_GROUP).astype(F32)
    scale = 1.0 / math.sqrt(seq * FNET_GROUP)
    bdc = jnp.where(same, jnp.cos(angc) * scale, 0.0).astype(BF16)
    bds = jnp.where(same, jnp.sin(angc) * scale, 0.0).astype(BF16)
    return cs, bdc, bds


def _fourier_kernel(seq, z_ref, cs_ref, bdc_ref, bds_ref, o_ref, zz_ref):
    @pl.when(pl.program_id(1) == 0)
    def _():
        z = z_ref[...]
        zz_ref[0:seq, :] = jnp.dot(z, bdc_ref[...], preferred_element_type=F32).astype(BF16)
        zz_ref[seq:, :] = jnp.dot(z, bds_ref[...], preferred_element_type=F32).astype(BF16)

    o_ref[...] = jnp.dot(cs_ref[...], zz_ref[...], preferred_element_type=F32).astype(BF16)


def _fourier(z, nb, seq, tl):
    cs, bdc, bds = _dft_tables(seq)
    nt = seq // tl
    return pl.pallas_call(
        functools.partial(_fourier_kernel, seq),
        out_shape=jax.ShapeDtypeStruct(z.shape, BF16),
        grid=(nb, nt),
        in_specs=[pl.BlockSpec((seq, D_FNET), lambda b, i: (b, 0)),
                  pl.BlockSpec((tl, 2 * seq), lambda b, i: (i, 0)),
                  pl.BlockSpec((D_FNET, D_FNET), lambda b, i: (0, 0)),
                  pl.BlockSpec((D_FNET, D_FNET), lambda b, i: (0, 0))],
        out_specs=pl.BlockSpec((tl, D_FNET), lambda b, i: (b * nt + i, 0)),
        scratch_shapes=[pltpu.VMEM((2 * seq, D_FNET), BF16)],
        compiler_params=_params(("arbitrary", "arbitrary")),
        name="fourier",
    )(z, cs, bdc, bds)


def _first_argmax_mask(v, iota, size):
    m = jnp.max(v, axis=0, keepdims=True)
    first = jnp.min(jnp.where(v == m, iota, size), axis=0, keepdims=True)
    return iota == first


def _route(logits_t, bias_col):
    tm = logits_t.shape[1]
    neg = -jnp.inf
    s = jax.nn.sigmoid(logits_t)
    biased = s + bias_col
    io8 = lax.broadcasted_iota(jnp.int32, (EXPERTS_PER_GROUP, tm), 0)
    gs_rows = []
    for g in range(N_EXPERT_GROUPS):
        blk = biased[g * EXPERTS_PER_GROUP:(g + 1) * EXPERTS_PER_GROUP, :]
        m1 = jnp.max(blk, axis=0, keepdims=True)
        rest = jnp.where(_first_argmax_mask(blk, io8, EXPERTS_PER_GROUP), neg, blk)
        gs_rows.append(m1 + jnp.max(rest, axis=0, keepdims=True))
    gs = jnp.concatenate(gs_rows, axis=0)
    iog = lax.broadcasted_iota(jnp.int32, (N_EXPERT_GROUPS, tm), 0)
    gsel = jnp.zeros((N_EXPERT_GROUPS, tm), F32)
    for _ in range(TOPK_GROUPS):
        sel = _first_argmax_mask(gs, iog, N_EXPERT_GROUPS)
        gsel = jnp.where(sel, 1.0, gsel)
        gs = jnp.where(sel, neg, gs)
    emask = jnp.concatenate(
        [jnp.broadcast_to(gsel[g:g + 1, :], (EXPERTS_PER_GROUP, tm)) for g in range(N_EXPERT_GROUPS)], axis=0)
    v = jnp.where(emask > 0.0, biased, neg)
    ioe = lax.broadcasted_iota(jnp.int32, (N_EXPERTS, tm), 0)
    idx_rows, s_rows = [], []
    for _ in range(TOP_K):
        sel = _first_argmax_mask(v, ioe, N_EXPERTS)
        idx_rows.append(jnp.sum(jnp.where(sel, ioe, 0), axis=0, keepdims=True))
        s_rows.append(jnp.sum(jnp.where(sel, s, 0.0), axis=0, keepdims=True))
        v = jnp.where(sel, neg, v)
    denom = s_rows[0]
    for r in s_rows[1:]:
        denom = denom + r
    pad = ROUTE_ROWS - TOP_K
    idx = jnp.concatenate(idx_rows + [jnp.zeros((pad, tm), jnp.int32)], axis=0)
    w = jnp.concatenate([r / denom * ROUTED_SCALE for r in s_rows] + [jnp.zeros((pad, tm), F32)], axis=0)
    return idx, w


def _merge_kernel(has_pos, *refs):
    if has_pos:
        (ys_ref, yf_ref, gt_ref, x_ref, pos_ref, mod_ref, n2_ref, wglu_ref, wps_ref, wpf_ref, wout_ref,
         wrt_ref, rb_ref, ws1_ref, ws3_ref, ws2_ref, xs_ref, h2_ref, ridx_ref, rw_ref) = refs
        x = x_ref[...] + pos_ref[...]
    else:
        (ys_ref, yf_ref, gt_ref, x_ref, mod_ref, n2_ref, wglu_ref, wps_ref, wpf_ref, wout_ref,
         wrt_ref, rb_ref, ws1_ref, ws3_ref, ws2_ref, xs_ref, h2_ref, ridx_ref, rw_ref) = refs
        x = x_ref[...]
    m = mod_ref[0]
    g = jax.nn.gelu(ys_ref[...].astype(F32))
    a = g * jax.nn.sigmoid(jnp.dot(g.astype(BF16), wglu_ref[...], preferred_element_type=F32))
    pa = jnp.dot(a.astype(BF16), wps_ref[...], preferred_element_type=F32)
    pb = jnp.dot(yf_ref[...], wpf_ref[...], preferred_element_type=F32)
    gt = gt_ref[...].astype(F32)
    merged = gt[:, :D_MODEL] * pa + gt[:, D_MODEL:] * pb
    x1 = x + m[2:3, :] * jnp.dot(merged.astype(BF16), wout_ref[...], preferred_element_type=F32)
    h2 = _rms(x1, n2_ref[...]) * (1.0 + m[4:5, :]) + m[3:4, :]
    hb = h2.astype(BF16)
    for j in range(ROW_SUB):
        h2_ref[pl.ds(j, h2.shape[0], stride=ROW_SUB), :] = h2[:, j * LANES:(j + 1) * LANES]
    logits_t = lax.dot_general(wrt_ref[...], h2, (((1,), (1,)), ((), ())),
                               precision=lax.Precision.HIGHEST, preferred_element_type=F32)
    ridx_ref[...], rw_ref[...] = _route(logits_t, rb_ref[...])
    s1 = jnp.dot(hb, ws1_ref[...], preferred_element_type=F32)
    s3 = jnp.dot(hb, ws3_ref[...], preferred_element_type=F32)
    shared = jnp.dot((s1 * jax.nn.sigmoid(s1) * s3).astype(BF16), ws2_ref[...], preferred_element_type=F32)
    xs_ref[...] = x1 + m[5:6, :] * shared


def _merge(ys, yf, gt, x, pos, mod, mod_row, n2, weights, tm):
    n = x.shape[0]
    has_pos = pos is not None
    row = lambda i: (i, 0)
    const = lambda a: pl.BlockSpec(a.shape, lambda i: (0,) * a.ndim)
    in_specs = [pl.BlockSpec((tm, D_S5), row), pl.BlockSpec((tm, D_FNET), row),
                pl.BlockSpec((tm, 2 * D_MODEL), row), pl.BlockSpec((tm, D_MODEL), row)]
    args = [ys, yf, gt, x]
    if has_pos:
        nper = pos.shape[0] // tm
        in_specs.append(pl.BlockSpec((tm, D_MODEL), lambda i: (i % nper, 0)))
        args.append(pos)
    in_specs += [pl.BlockSpec((1, N_MOD, D_MODEL), lambda i: (mod_row(i, tm), 0, 0)), const(n2)]
    args += [mod, n2]
    in_specs += [const(w) for w in weights]
    args += list(weights)
    return pl.pallas_call(
        functools.partial(_merge_kernel, has_pos),
        out_shape=(jax.ShapeDtypeStruct((n, D_MODEL), F32),
                   jax.ShapeDtypeStruct((n * ROW_SUB, LANES), F32),
                   jax.ShapeDtypeStruct((ROUTE_ROWS, n), jnp.int32),
                   jax.ShapeDtypeStruct((ROUTE_ROWS, n), F32)),
        grid=(n // tm,),
        in_specs=in_specs,
        out_specs=(pl.BlockSpec((tm, D_MODEL), row), pl.BlockSpec((tm * ROW_SUB, LANES), row),
                   pl.BlockSpec((ROUTE_ROWS, tm), lambda i: (0, i)),
                   pl.BlockSpec((ROUTE_ROWS, tm), lambda i: (0, i))),
        compiler_params=_params(("arbitrary",)),
        name="merge",
    )(*args)


MOE_SUB = 4096
MOE_TM = 128
MOE_TMAX = MOE_SUB * TOP_K // MOE_TM + N_EXPERTS
MOE_TS = MOE_TMAX + 2
MOE_DUMMY = 256
MOE_RMW = 16
ROW_TILE = MOE_TM * ROW_SUB


def _moe_plan(ridx, rw):
    n = ridx.shape[1]
    nsub = n // MOE_SUB
    npair = n * TOP_K
    t = jnp.arange(n, dtype=jnp.int32)
    key = ((t // MOE_SUB) * N_EXPERTS)[None] + ridx[:TOP_K]
    key = (key * MOE_SUB + (t % MOE_SUB)[None]).reshape(-1)
    skey, sw = lax.sort((key, rw[:TOP_K].reshape(-1)), num_keys=1)
    stok = jnp.concatenate([(skey % MOE_SUB) * ROW_SUB, jnp.zeros((MOE_TM,), jnp.int32)])
    sw_rows = jnp.concatenate([sw, jnp.zeros((LANES,), F32)]).reshape(npair // LANES + 1, 1, LANES)
    hits = ridx[:TOP_K].reshape(TOP_K, nsub, 1, MOE_SUB) == jnp.arange(N_EXPERTS, dtype=jnp.int32)[None, None, :, None]
    cnt = jnp.sum(hits.astype(jnp.int32), axis=(0, 3))
    poff = (jnp.cumsum(cnt.reshape(-1)) - cnt.reshape(-1)).reshape(nsub, N_EXPERTS)
    ntile = (cnt + MOE_TM - 1) // MOE_TM
    tcum = jnp.cumsum(ntile, axis=1)
    toff = tcum - ntile
    tstart = jnp.concatenate([toff, tcum[:, -1:]], axis=1).reshape(-1).astype(jnp.int32)
    j = jnp.arange(MOE_TS, dtype=jnp.int32) - 1
    valid = (j[None] >= 0) & (j[None] < tcum[:, -1:])
    te = jnp.minimum(jnp.sum(j[None, :, None] >= tcum[:, None, :], axis=-1), N_EXPERTS - 1)
    pick = lambda a: jnp.take_along_axis(a, te, axis=1)
    first = (j[None] - pick(toff)) * MOE_TM
    p0 = jnp.where(valid, pick(poff) + first, 0).reshape(-1).astype(jnp.int32)
    nv = jnp.where(valid, jnp.minimum(pick(cnt) - first, MOE_TM), 0).reshape(-1).astype(jnp.int32)
    return tstart, p0, nv, stok, sw_rows


def _moe_kernel(ts_ref, p0_ref, nv_ref, tok_ref, sw_ref, src_ref, w1_ref, w3_ref, w2_ref, y_ref,
                xt_ref, xb_ref, ot_ref, w1b_ref, w3b_ref, w2b_ref):
    sub = pl.program_id(0)
    e = pl.program_id(1)
    base = sub * MOE_TS + 1

    def gather(p0):
        for mi in range(MOE_TM):
            tok = pl.multiple_of(tok_ref[p0 + mi], ROW_SUB)
            xt_ref[mi * ROW_SUB:(mi + 1) * ROW_SUB, :] = src_ref[pl.ds(tok, ROW_SUB), :]
        for j in range(ROW_SUB):
            xb_ref[:, j * LANES:(j + 1) * LANES] = xt_ref[pl.ds(j, MOE_TM, stride=ROW_SUB), :].astype(BF16)

    def scatter(p0, nv, masked):
        for u in range(0, MOE_TM, MOE_RMW):
            new = []
            for i in range(MOE_RMW):
                tok = tok_ref[p0 + u + i]
                if masked:
                    tok = jnp.where(u + i < nv, tok, MOE_SUB * ROW_SUB)
                tok = pl.multiple_of(tok, ROW_SUB)
                new.append((tok, y_ref[pl.ds(tok, ROW_SUB), :]
                            + ot_ref[(u + i) * ROW_SUB:(u + i + 1) * ROW_SUB, :]))
            for tok, v in new:
                y_ref[pl.ds(tok, ROW_SUB), :] = v

    @pl.when(e == 0)
    def _():
        y_ref[...] = jnp.zeros_like(y_ref)
        ot_ref[...] = jnp.zeros_like(ot_ref)
        gather(p0_ref[base])

    w1b_ref[...] = w1_ref[0].astype(BF16)
    w3b_ref[...] = w3_ref[0].astype(BF16)
    w2b_ref[...] = w2_ref[0].astype(BF16)

    def step(cur, masked):
        scatter(p0_ref[cur - 1], nv_ref[cur - 1], masked)
        p0 = p0_ref[cur]
        nv = nv_ref[cur]
        x = xb_ref[...]
        a = jnp.dot(x, w1b_ref[...], preferred_element_type=F32)
        b = jnp.dot(x, w3b_ref[...], preferred_element_type=F32)
        r0 = p0 // LANES
        c = p0 % LANES
        lane = lax.broadcasted_iota(jnp.int32, (1, LANES), 1)
        ga = pltpu.roll(sw_ref[r0], LANES - c, axis=1)
        gb = pltpu.roll(sw_ref[r0 + 1], LANES - c, axis=1)
        g = jnp.where(lane < nv, jnp.where(lane < LANES - c, ga, gb), 0.0)
        rows = lax.broadcasted_iota(jnp.int32, (MOE_TM, MOE_TM), 0)
        cols = lax.broadcasted_iota(jnp.int32, (MOE_TM, MOE_TM), 1)
        gcol = jnp.sum(jnp.where(rows == cols, jnp.broadcast_to(g, (MOE_TM, MOE_TM)), 0.0),
                       axis=1, keepdims=True)
        act = (a * jax.nn.sigmoid(a) * b * gcol).astype(BF16)
        o = jnp.dot(act, w2b_ref[...], preferred_element_type=F32)
        for j in range(ROW_SUB):
            ot_ref[pl.ds(j, MOE_TM, stride=ROW_SUB), :] = o[:, j * LANES:(j + 1) * LANES]
        gather(p0_ref[cur + 1])

    def body(i, carry):
        cur = base + i
        prev_full = nv_ref[cur - 1] == MOE_TM

        @pl.when(prev_full)
        def _():
            step(cur, False)

        @pl.when(jnp.logical_not(prev_full))
        def _():
            step(cur, True)

        return carry

    first = ts_ref[sub * (N_EXPERTS + 1) + e]
    last = ts_ref[sub * (N_EXPERTS + 1) + e + 1]
    lax.fori_loop(first, last, body, 0)

    @pl.when(e == N_EXPERTS - 1)
    def _():
        scatter(p0_ref[base + last - 1], nv_ref[base + last - 1], True)


def _moe(h2_rows, plan, w1, w3, w2):
    tstart, p0, nv, stok, sw_rows = plan
    nsub = h2_rows.shape[0] // (MOE_SUB * ROW_SUB)
    wmap = lambda s, e, ts, p0, nv: (e, 0, 0)
    sub2 = lambda s, e, ts, p0, nv: (s, 0)
    grid_spec = pltpu.PrefetchScalarGridSpec(
        num_scalar_prefetch=3,
        grid=(nsub, N_EXPERTS),
        in_specs=[pl.BlockSpec(memory_space=pltpu.SMEM),
                  pl.BlockSpec(sw_rows.shape, lambda s, e, ts, p0, nv: (0, 0, 0)),
                  pl.BlockSpec((MOE_SUB * ROW_SUB, LANES), sub2, pipeline_mode=pl.Buffered(1)),
                  pl.BlockSpec((1, D_MODEL, D_EXPERT), wmap),
                  pl.BlockSpec((1, D_MODEL, D_EXPERT), wmap),
                  pl.BlockSpec((1, D_EXPERT, D_MODEL), wmap)],
        out_specs=pl.BlockSpec(((MOE_SUB + MOE_DUMMY) * ROW_SUB, LANES), sub2, pipeline_mode=pl.Buffered(1)),
        scratch_shapes=[pltpu.VMEM((ROW_TILE, LANES), F32), pltpu.VMEM((MOE_TM, D_MODEL), BF16),
                        pltpu.VMEM((ROW_TILE, LANES), F32),
                        pltpu.VMEM((D_MODEL, D_EXPERT), BF16), pltpu.VMEM((D_MODEL, D_EXPERT), BF16),
                        pltpu.VMEM((D_EXPERT, D_MODEL), BF16)])
    return pl.pallas_call(
        _moe_kernel,
        grid_spec=grid_spec,
        out_shape=jax.ShapeDtypeStruct((nsub * (MOE_SUB + MOE_DUMMY) * ROW_SUB, LANES), F32),
        compiler_params=_params(("arbitrary", "arbitrary")),
        name="moe",
    )(tstart, p0, nv, stok, sw_rows, h2_rows, w1, w3, w2)


def _final_kernel(xs_ref, y_ref, mod_ref, fg_ref, o_ref):
    tm = xs_ref.shape[0]
    y = jnp.concatenate([y_ref[pl.ds(j, tm, stride=ROW_SUB), :] for j in range(ROW_SUB)], axis=1)
    x2 = xs_ref[...] + mod_ref[0][5:6, :] * y
    o_ref[...] = _rms(x2, fg_ref[...])


def _final(xs, y_rows, mod, mod_row, fg, tm):
    n = xs.shape[0]
    per_sub = MOE_SUB // tm
    stride = (MOE_SUB + MOE_DUMMY) // tm
    row = lambda i: (i, 0)
    return pl.pallas_call(
        _final_kernel,
        out_shape=jax.ShapeDtypeStruct((n, D_MODEL), F32),
        grid=(n // tm,),
        in_specs=[pl.BlockSpec((tm, D_MODEL), row),
                  pl.BlockSpec((tm * ROW_SUB, LANES), lambda i: ((i // per_sub) * stride + i % per_sub, 0)),
                  pl.BlockSpec((1, N_MOD, D_MODEL), lambda i: (mod_row(i, tm), 0, 0)),
                  pl.BlockSpec((1, D_MODEL), lambda i: (0, 0))],
        out_specs=pl.BlockSpec((tm, D_MODEL), row),
        compiler_params=_params(("arbitrary",)),
        name="final",
    )(xs, y_rows, mod, fg)


def _grid_pos_embed(n_tokens):
    rows = n_tokens // GRID_W
    r, col = jnp.meshgrid(jnp.arange(rows, dtype=F32), jnp.arange(GRID_W, dtype=F32), indexing="ij")
    quarter = D_MODEL // 4
    omega = 1.0 / (10000.0 ** (jnp.arange(quarter, dtype=F32) / quarter))

    def emb(p):
        a = p.reshape(-1)[:, None] * omega
        return jnp.concatenate([jnp.sin(a), jnp.cos(a)], axis=-1)

    return jnp.concatenate([emb(r), emb(col)], axis=-1)


def _stream(x3, pos, mod, first_row, h0, s5_ops, p):
    nb, seq, _ = x3.shape
    n = nb * seq
    nk = seq // S5_CHUNK
    x = x3.reshape(n, D_MODEL)
    per_seq_mod = first_row > 0

    def mod_row(i, tm):
        return first_row + (i * tm) // seq if per_seq_mod else 0

    us, uf, gt = _inproj(x, pos, mod, mod_row, p["norm1_g"], p["w_in"], 512)
    ys, fin = _s5(us, s5_ops, h0, nb, nk)
    yf = _fourier(uf, nb, seq, min(seq, 512))
    xs, h2_rows, ridx, rw = _merge(ys, yf, gt, x, pos, mod, mod_row, p["norm2_g"], p["merge_w"], 256)
    y_rows = _moe(h2_rows, _moe_plan(ridx, rw), p["w1"], p["w3"], p["w2"])
    out = _final(xs, y_rows, mod, mod_row, p["final_g"], 256)
    return out.reshape(nb, seq, D_MODEL), fin


def kernel(x_prompt, x_sample, state_s5_re, state_s5_im, c, c_ctx, w_ada, b_ada, norm1_g, norm2_g, w_in,
           lam_re, lam_im, log_dt, b_re, b_im, c_re, c_im, d_skip, w_glu, w_proj_s5, w_proj_f, w_out,
           w_router, router_bias, w1, w3, w2, ws1, ws3, ws2, final_norm_g):
    nb_ctx = x_prompt.shape[0]
    nb_lat, seq_lat, _ = x_sample.shape
    half = 2 * S5_STATE

    cond = jnp.concatenate([c_ctx[None], c, jnp.zeros((MOD_ROWS - 1 - nb_lat, D_MODEL), F32)], axis=0)
    mod = _adaln(cond, w_ada[0], b_ada[0]).reshape(MOD_ROWS, N_MOD, D_MODEL)

    s5_ops = _s5ops(lam_re[0], lam_im[0], log_dt[0], b_re[0], b_im[0], c_re[0], c_im[0], d_skip[0])
    p = dict(
        norm1_g=norm1_g[0][None], norm2_g=norm2_g[0][None], final_g=final_norm_g[None],
        w_in=w_in[0].astype(BF16), w1=w1[0], w3=w3[0], w2=w2[0],
        merge_w=(w_glu[0].astype(BF16), w_proj_s5[0].astype(BF16), w_proj_f[0].astype(BF16),
                 w_out[0].astype(BF16), w_router[0].T, router_bias[0][:, None],
                 ws1[0].astype(BF16), ws3[0].astype(BF16), ws2[0].astype(BF16)))

    def pack_state(sr, si):
        f = lambda a: a.astype(F32).transpose(2, 0, 1, 3).reshape(S5_GROUPS, a.shape[0], half)
        return jnp.concatenate([f(sr), f(si)], axis=-1)

    def unpack_state(fin, lo):
        nb = fin.shape[1]
        return fin[..., lo:lo + half].reshape(S5_GROUPS, nb, 2, S5_STATE).transpose(1, 2, 0, 3)[:, None]

    h0_ctx = jnp.zeros((S5_GROUPS, nb_ctx, 2 * half), F32)
    y_prompt, fin = _stream(x_prompt, None, mod, 0, h0_ctx, s5_ops, p)
    h0_lat = pack_state(state_s5_re[:, 0], state_s5_im[:, 0])
    y_sample, _ = _stream(x_sample, _grid_pos_embed(seq_lat), mod, 1, h0_lat, s5_ops, p)
    return (y_prompt, y_sample, unpack_state(fin, 0).astype(x_prompt.dtype),
            unpack_state(fin, half).astype(x_prompt.dtype))
```

```python
import functools
import math

import jax
import jax.numpy as jnp
from jax import lax
from jax.experimental import pallas as pl
from jax.experimental.pallas import tpu as pltpu

D_MODEL = 1024
GRID_W = 64
D_S5 = 768
S5_GROUP = 16
S5_GROUPS = 48
S5_STATE = 64
D_FNET = 256
FNET_GROUP = 64
N_EXPERTS = 64
TOP_K = 6
N_EXPERT_GROUPS = 8
EXPERTS_PER_GROUP = N_EXPERTS // N_EXPERT_GROUPS
TOPK_GROUPS = 4
D_EXPERT = 256
ROUTED_SCALE = 2.5
N_MOD = 6
EPS = 1e-6

S5_CHUNK = 16
S5_ROW = S5_CHUNK * S5_GROUP
MOD_ROWS = 8
ROUTE_ROWS = 8
LANES = 128
ROW_SUB = D_MODEL // LANES
VMEM_LIMIT = 56 * 1024 * 1024

BF16 = jnp.bfloat16
F32 = jnp.float32


def _params(sem, vmem=VMEM_LIMIT):
    return pltpu.CompilerParams(dimension_semantics=sem, vmem_limit_bytes=vmem)


def _rms(x, g):
    return x * lax.rsqrt(jnp.mean(x * x, axis=-1, keepdims=True) + EPS) * g


def _adaln_kernel(c_ref, w_ref, b_ref, o_ref):
    c = c_ref[...]
    o_ref[...] = jnp.dot(c * jax.nn.sigmoid(c), w_ref[...], precision=lax.Precision.HIGHEST,
                         preferred_element_type=F32) + b_ref[...]


def _adaln(cond, w_ada, b_ada):
    n_out = N_MOD * D_MODEL
    return pl.pallas_call(
        _adaln_kernel,
        out_shape=jax.ShapeDtypeStruct((MOD_ROWS, n_out), F32),
        grid=(N_MOD,),
        in_specs=[pl.BlockSpec((MOD_ROWS, D_MODEL), lambda i: (0, 0)),
                  pl.BlockSpec((D_MODEL, D_MODEL), lambda i: (0, i)),
                  pl.BlockSpec((1, D_MODEL), lambda i: (0, i))],
        out_specs=pl.BlockSpec((MOD_ROWS, D_MODEL), lambda i: (0, i)),
        compiler_params=_params(("arbitrary",)),
        name="adaln",
    )(cond, w_ada, b_ada.reshape(1, n_out))


def _inproj_kernel(has_pos, *refs):
    if has_pos:
        x_ref, pos_ref, mod_ref, g_ref, w_ref, us_ref, uf_ref, gt_ref = refs
        x = x_ref[...] + pos_ref[...]
    else:
        x_ref, mod_ref, g_ref, w_ref, us_ref, uf_ref, gt_ref = refs
        x = x_ref[...]
    m = mod_ref[0]
    h = _rms(x, g_ref[...]) * (1.0 + m[1:2, :]) + m[0:1, :]
    p = jnp.dot(h.astype(BF16), w_ref[...], preferred_element_type=F32)
    us_ref[...] = p[:, :D_S5]
    uf_ref[...] = p[:, D_S5:D_MODEL].astype(BF16)
    gt_ref[...] = jax.nn.sigmoid(p[:, D_MODEL:]).astype(BF16)


def _inproj(x, pos, mod, mod_row, norm_g, w_in_bf, tm):
    n = x.shape[0]
    has_pos = pos is not None
    row = lambda i: (i, 0)
    in_specs = [pl.BlockSpec((tm, D_MODEL), row)]
    args = [x]
    if has_pos:
        nper = pos.shape[0] // tm
        in_specs.append(pl.BlockSpec((tm, D_MODEL), lambda i: (i % nper, 0)))
        args.append(pos)
    in_specs += [pl.BlockSpec((1, N_MOD, D_MODEL), lambda i: (mod_row(i, tm), 0, 0)),
                 pl.BlockSpec((1, D_MODEL), lambda i: (0, 0)),
                 pl.BlockSpec(w_in_bf.shape, lambda i: (0, 0))]
    args += [mod, norm_g, w_in_bf]
    return pl.pallas_call(
        functools.partial(_inproj_kernel, has_pos),
        out_shape=(jax.ShapeDtypeStruct((n, D_S5), F32),
                   jax.ShapeDtypeStruct((n, D_FNET), BF16),
                   jax.ShapeDtypeStruct((n, 2 * D_MODEL), BF16)),
        grid=(n // tm,),
        in_specs=in_specs,
        out_specs=(pl.BlockSpec((tm, D_S5), row), pl.BlockSpec((tm, D_FNET), row),
                   pl.BlockSpec((tm, 2 * D_MODEL), row)),
        compiler_params=_params(("arbitrary",)),
        name="inproj",
    )(*args)


def _shift_lanes(x, k):
    if k == 0:
        return x
    z = jnp.zeros((x.shape[0], abs(k)), x.dtype)
    if k > 0:
        return jnp.concatenate([z, x[:, :x.shape[1] - k]], axis=1)
    return jnp.concatenate([x[:, -k:], z], axis=1)


def _s5ops_kernel(lam_re_ref, lam_im_ref, dt_ref, btr_ref, bti_ref, cr_ref, ci_ref, d_ref,
                  m_ref, wi_ref, wot_ref, coef_ref):
    hi = lax.Precision.HIGHEST
    lr = jnp.minimum(lam_re_ref[0], -1e-4)
    li = lam_im_ref[0]
    dt = jnp.exp(dt_ref[0])
    mag = jnp.exp(lr * dt)
    ar = mag * jnp.cos(li * dt)
    ai = mag * jnp.sin(li * dt)
    den = lr * lr + li * li
    nr = ar - 1.0
    qr = (nr * lr + ai * li) / den
    qi = (ai * lr - nr * li) / den
    pr, pi = [], []
    for n in range(S5_CHUNK + 1):
        pm = jnp.exp(float(n) * (lr * dt))
        pr.append(pm * jnp.cos(float(n) * (li * dt)))
        pi.append(pm * jnp.sin(float(n) * (li * dt)))
    bbr, bbi, car, cai = [], [], [], []
    for d in range(2):
        btr = btr_ref[0, d]
        bti = bti_ref[0, d]
        bbr.append(qr[d:d + 1] * btr - qi[d:d + 1] * bti)
        bbi.append(qr[d:d + 1] * bti + qi[d:d + 1] * btr)
        cr = cr_ref[0, d]
        ci = ci_ref[0, d]
        car.append([cr * pr[n][d:d + 1] - ci * pi[n][d:d + 1] for n in range(S5_CHUNK + 1)])
        cai.append([cr * pi[n][d:d + 1] + ci * pr[n][d:d + 1] for n in range(S5_CHUNK + 1)])

    def lag_kernels(d, order):
        a = jnp.concatenate([car[d][n] for n in order], axis=0)
        b = jnp.concatenate([cai[d][n] for n in order], axis=0)
        dn = (((1,), (1,)), ((), ()))
        return (lax.dot_general(bbr[d], a, dn, precision=hi, preferred_element_type=F32)
                - lax.dot_general(bbi[d], b, dn, precision=hi, preferred_element_type=F32))

    ktf = lag_kernels(0, range(S5_CHUNK))
    ktb = lag_kernels(1, range(S5_CHUNK - 1, -1, -1))
    row = lax.broadcasted_iota(jnp.int32, (S5_GROUP, S5_ROW), 0)
    lane = lax.broadcasted_iota(jnp.int32, (S5_GROUP, S5_ROW), 1)
    dcol = d_ref[0]
    for j in range(S5_CHUNK):
        rows = slice(j * S5_GROUP, (j + 1) * S5_GROUP)
        blk = _shift_lanes(ktf, S5_GROUP * j) + _shift_lanes(ktb, -S5_GROUP * (S5_CHUNK - 1 - j))
        blk = blk + jnp.where(lane == S5_GROUP * j + row, dcol, 0.0)
        m_ref[0, rows, :] = blk.astype(BF16)
        nf = S5_CHUNK - 1 - j
        wi = jnp.concatenate([pr[nf][0:1] * bbr[0] - pi[nf][0:1] * bbi[0],
                              pr[j][1:2] * bbr[1] - pi[j][1:2] * bbi[1],
                              pr[nf][0:1] * bbi[0] + pi[nf][0:1] * bbr[0],
                              pr[j][1:2] * bbi[1] + pi[j][1:2] * bbr[1]], axis=1)
        wi_ref[0, rows, :] = wi.astype(BF16)
        wot = jnp.concatenate([car[0][j + 1], car[1][S5_CHUNK - j],
                               -cai[0][j + 1], -cai[1][S5_CHUNK - j]], axis=1)
        wot_ref[0, rows, :] = wot.astype(BF16)
    coef_ref[0, 0:1, :] = jnp.concatenate([pr[S5_CHUNK][0:1], pr[S5_CHUNK][1:2]], axis=1)
    coef_ref[0, 1:2, :] = jnp.concatenate([pi[S5_CHUNK][0:1], pi[S5_CHUNK][1:2]], axis=1)


def _s5ops(lam_re, lam_im, log_dt, b_re, b_im, c_re, c_im, d_skip):
    g3 = lambda g: (g, 0, 0)
    g4 = lambda g: (g, 0, 0, 0)
    sw = lambda a: jnp.swapaxes(a.astype(F32), 0, 1)
    dt = jnp.broadcast_to(sw(log_dt)[..., None], (S5_GROUPS, 2, S5_STATE))
    args = (sw(lam_re), sw(lam_im), dt, sw(jnp.swapaxes(b_re, 2, 3)), sw(jnp.swapaxes(b_im, 2, 3)),
            sw(c_re), sw(c_im), d_skip.astype(F32).reshape(S5_GROUPS, S5_GROUP, 1))
    vec = pl.BlockSpec((1, 2, S5_STATE), g3)
    mat = pl.BlockSpec((1, 2, S5_GROUP, S5_STATE), g4)
    op = pl.BlockSpec((1, S5_ROW, S5_ROW), g3)
    return pl.pallas_call(
        _s5ops_kernel,
        out_shape=(jax.ShapeDtypeStruct((S5_GROUPS, S5_ROW, S5_ROW), BF16),) * 3
        + (jax.ShapeDtypeStruct((S5_GROUPS, 2, 2 * S5_STATE), F32),),
        grid=(S5_GROUPS,),
        in_specs=[vec, vec, vec, mat, mat, mat, mat, pl.BlockSpec((1, S5_GROUP, 1), g3)],
        out_specs=(op, op, op, pl.BlockSpec((1, 2, 2 * S5_STATE), g3)),
        compiler_params=_params(("arbitrary",)),
        name="s5ops",
    )(*args)


S5_BLOCK_GROUPS = LANES // S5_GROUP


def _s5_perm():
    a = jnp.arange(S5_CHUNK * LANES, dtype=jnp.int32)
    dst = ((a % LANES) // S5_GROUP) * S5_ROW + (a // LANES) * S5_GROUP + a % S5_GROUP
    return (dst[:, None] == a[None, :]).astype(BF16)


def _s5_kernel(nb, nk, u_ref, perm_ref, m_ref, wi_ref, wot_ref, coef_ref, h0_ref, y_ref, fin_ref,
               sr_ref, si_ref, efr_ref, ebr_ref, efi_ref, ebi_ref, uall_ref, yall_ref):
    gl = pl.program_id(1)
    rows = nb * nk
    half = 2 * S5_STATE

    @pl.when(gl == 0)
    def _():
        xcat = jnp.concatenate([u_ref[pl.ds(t, rows, stride=S5_CHUNK), :].astype(BF16)
                                for t in range(S5_CHUNK)], axis=1)
        uall = jnp.dot(xcat, perm_ref[...], preferred_element_type=F32).astype(BF16)
        for g in range(S5_BLOCK_GROUPS):
            uall_ref[g] = uall[:, g * S5_ROW:(g + 1) * S5_ROW]

    u = uall_ref[gl]
    s = jnp.dot(u, wi_ref[0], preferred_element_type=F32)
    sr_ref[...] = s[:, :half]
    si_ref[...] = s[:, half:]
    c_r = coef_ref[0, 0:1, :]
    c_i = coef_ref[0, 1:2, :]
    e_r = h0_ref[0, :, :half]
    e_i = h0_ref[0, :, half:]
    is_fwd = lax.broadcasted_iota(jnp.int32, (nb, half), 1) < S5_STATE
    for j in range(nk):
        rf = pl.ds(j, nb, stride=nk)
        rb = pl.ds(nk - 1 - j, nb, stride=nk)
        efr_ref[rf, :] = e_r
        ebr_ref[rb, :] = e_r
        efi_ref[rf, :] = e_i
        ebi_ref[rb, :] = e_i
        s_r = jnp.where(is_fwd, sr_ref[rf, :], sr_ref[rb, :])
        s_i = jnp.where(is_fwd, si_ref[rf, :], si_ref[rb, :])
        e_r, e_i = c_r * e_r - c_i * e_i + s_r, c_r * e_i + c_i * e_r + s_i
    fin_ref[0, :, :half] = e_r
    fin_ref[0, :, half:] = e_i
    y = jnp.dot(u, m_ref[0], preferred_element_type=F32)
    fwd_rows = lax.broadcasted_iota(jnp.int32, (rows, half), 1) < S5_STATE
    e = jnp.concatenate([jnp.where(fwd_rows, efr_ref[...], ebr_ref[...]),
                         jnp.where(fwd_rows, efi_ref[...], ebi_ref[...])], axis=1).astype(BF16)
    y = y + lax.dot_general(e, wot_ref[0], (((1,), (1,)), ((), ())), preferred_element_type=F32)
    yall_ref[gl] = y.astype(BF16)

    @pl.when(gl == S5_BLOCK_GROUPS - 1)
    def _():
        ycat = jnp.concatenate([yall_ref[g] for g in range(S5_BLOCK_GROUPS)], axis=1)
        out = lax.dot_general(ycat, perm_ref[...], (((1,), (1,)), ((), ())), preferred_element_type=F32)
        for t in range(S5_CHUNK):
            y_ref[pl.ds(t, rows, stride=S5_CHUNK), :] = out[:, t * LANES:(t + 1) * LANES]


def _s5(u, ops, h0, nb, nk):
    m, w_in, w_out, coef = ops
    n = u.shape[0]
    rows = nb * nk
    perm = _s5_perm()
    g3 = lambda b, g: (b * S5_BLOCK_GROUPS + g, 0, 0)
    blk = lambda b, g: (0, b)
    return pl.pallas_call(
        functools.partial(_s5_kernel, nb, nk),
        out_shape=(jax.ShapeDtypeStruct((n, D_S5), F32),
                   jax.ShapeDtypeStruct((S5_GROUPS, nb, 4 * S5_STATE), F32)),
        grid=(S5_GROUPS // S5_BLOCK_GROUPS, S5_BLOCK_GROUPS),
        in_specs=[pl.BlockSpec((n, LANES), blk),
                  pl.BlockSpec(perm.shape, lambda b, g: (0, 0)),
                  pl.BlockSpec((1, S5_ROW, S5_ROW), g3),
                  pl.BlockSpec((1, S5_ROW, 4 * S5_STATE), g3),
                  pl.BlockSpec((1, 4 * S5_STATE, S5_ROW), g3),
                  pl.BlockSpec((1, 2, 2 * S5_STATE), g3),
                  pl.BlockSpec((1, nb, 4 * S5_STATE), g3)],
        out_specs=(pl.BlockSpec((n, LANES), blk),
                   pl.BlockSpec((1, nb, 4 * S5_STATE), g3)),
        scratch_shapes=[pltpu.VMEM((rows, 2 * S5_STATE), F32)] * 6 + [
            pltpu.VMEM((S5_BLOCK_GROUPS, rows, S5_ROW), BF16),
            pltpu.VMEM((S5_BLOCK_GROUPS, rows, S5_ROW), BF16)],
        compiler_params=_params(("arbitrary", "arbitrary")),
        name="s5",
    )(u, perm, m, w_in, w_out, coef, h0)


def _dft_tables(seq):
    k = jnp.arange(seq, dtype=jnp.int32)
    na = seq // FNET_GROUP
    ang_a = (2.0 * math.pi / na) * ((jnp.arange(na, dtype=jnp.int32)[:, None] * k[None, :]) % na).astype(F32)
    ang_b = (2.0 * math.pi / seq) * ((jnp.arange(FNET_GROUP, dtype=jnp.int32)[:, None] * k[None, :]) % seq).astype(F32)
    ca, sa = jnp.cos(ang_a)[:, None, :], jnp.sin(ang_a)[:, None, :]
    cb, sb = jnp.cos(ang_b)[None, :, :], jnp.sin(ang_b)[None, :, :]
    cos_jk = (ca * cb - sa * sb).reshape(seq, seq)
    sin_jk = (sa * cb + ca * sb).reshape(seq, seq)
    cs = jnp.concatenate([cos_jk, -sin_jk], axis=1).astype(BF16)
    c = jnp.arange(D_FNET, dtype=jnp.int32)
    same = (c[:, None] // FNET_GROUP) == (c[None, :] // FNET_GROUP)
    angc = (2.0 * math.pi / FNET_GROUP) * (((c[:, None] % FNET_GROUP) * (c[None, :] % FNET_GROUP))
                                           % FNET_GROUP).astype(F32)
    scale = 1.0 / math.sqrt(seq * FNET_GROUP)
    bdc = jnp.where(same, jnp.cos(angc) * scale, 0.0).astype(BF16)
    bds = jnp.where(same, jnp.sin(angc) * scale, 0.0).astype(BF16)
    return cs, bdc, bds


def _fourier_kernel(seq, z_ref, cs_ref, bdc_ref, bds_ref, o_ref, zz_ref):
    @pl.when(pl.program_id(1) == 0)
    def _():
        z = z_ref[...]
        zz_ref[0:seq, :] = jnp.dot(z, bdc_ref[...], preferred_element_type=F32).astype(BF16)
        zz_ref[seq:, :] = jnp.dot(z, bds_ref[...], preferred_element_type=F32).astype(BF16)

    o_ref[...] = jnp.dot(cs_ref[...], zz_ref[...], preferred_element_type=F32).astype(BF16)


def _fourier(z, nb, seq, tl):
    cs, bdc, bds = _dft_tables(seq)
    nt = seq // tl
    return pl.pallas_call(
        functools.partial(_fourier_kernel, seq),
        out_shape=jax.ShapeDtypeStruct(z.shape, BF16),
        grid=(nb, nt),
        in_specs=[pl.BlockSpec((seq, D_FNET), lambda b, i: (b, 0)),
                  pl.BlockSpec((tl, 2 * seq), lambda b, i: (i, 0)),
                  pl.BlockSpec((D_FNET, D_FNET), lambda b, i: (0, 0)),
                  pl.BlockSpec((D_FNET, D_FNET), lambda b, i: (0, 0))],
        out_specs=pl.BlockSpec((tl, D_FNET), lambda b, i: (b * nt + i, 0)),
        scratch_shapes=[pltpu.VMEM((2 * seq, D_FNET), BF16)],
        compiler_params=_params(("arbitrary", "arbitrary")),
        name="fourier",
    )(z, cs, bdc, bds)


def _first_argmax_mask(v, iota, size):
    m = jnp.max(v, axis=0, keepdims=True)
    first = jnp.min(jnp.where(v == m, iota, size), axis=0, keepdims=True)
    return iota == first


def _route(logits_t, bias_col):
    tm = logits_t.shape[1]
    neg = -jnp.inf
    s = jax.nn.sigmoid(logits_t)
    biased = s + bias_col
    io8 = lax.broadcasted_iota(jnp.int32, (EXPERTS_PER_GROUP, tm), 0)
    gs_rows = []
    for g in range(N_EXPERT_GROUPS):
        blk = biased[g * EXPERTS_PER_GROUP:(g + 1) * EXPERTS_PER_GROUP, :]
        m1 = jnp.max(blk, axis=0, keepdims=True)
        rest = jnp.where(_first_argmax_mask(blk, io8, EXPERTS_PER_GROUP), neg, blk)
        gs_rows.append(m1 + jnp.max(rest, axis=0, keepdims=True))
    gs = jnp.concatenate(gs_rows, axis=0)
    iog = lax.broadcasted_iota(jnp.int32, (N_EXPERT_GROUPS, tm), 0)
    gsel = jnp.zeros((N_EXPERT_GROUPS, tm), F32)
    for _ in range(TOPK_GROUPS):
        sel = _first_argmax_mask(gs, iog, N_EXPERT_GROUPS)
        gsel = jnp.where(sel, 1.0, gsel)
        gs = jnp.where(sel, neg, gs)
    emask = jnp.concatenate(
        [jnp.broadcast_to(gsel[g:g + 1, :], (EXPERTS_PER_GROUP, tm)) for g in range(N_EXPERT_GROUPS)], axis=0)
    v = jnp.where(emask > 0.0, biased, neg)
    ioe = lax.broadcasted_iota(jnp.int32, (N_EXPERTS, tm), 0)
    idx_rows, s_rows = [], []
    for _ in range(TOP_K):
        sel = _first_argmax_mask(v, ioe, N_EXPERTS)
        idx_rows.append(jnp.sum(jnp.where(sel, ioe, 0), axis=0, keepdims=True))
        s_rows.append(jnp.sum(jnp.where(sel, s, 0.0), axis=0, keepdims=True))
        v = jnp.where(sel, neg, v)
    denom = s_rows[0]
    for r in s_rows[1:]:
        denom = denom + r
    pad = ROUTE_ROWS - TOP_K
    idx = jnp.concatenate(idx_rows + [jnp.zeros((pad, tm), jnp.int32)], axis=0)
    w = jnp.concatenate([r / denom * ROUTED_SCALE for r in s_rows] + [jnp.zeros((pad, tm), F32)], axis=0)
    return idx, w


def _merge_kernel(has_pos, *refs):
    if has_pos:
        (ys_ref, yf_ref, gt_ref, x_ref, pos_ref, mod_ref, n2_ref, wglu_ref, wps_ref, wpf_ref, wout_ref,
         wrt_ref, rb_ref, ws1_ref, ws3_ref, ws2_ref, xs_ref, h2_ref, ridx_ref, rw_ref) = refs
        x = x_ref[...] + pos_ref[...]
    else:
        (ys_ref, yf_ref, gt_ref, x_ref, mod_ref, n2_ref, wglu_ref, wps_ref, wpf_ref, wout_ref,
         wrt_ref, rb_ref, ws1_ref, ws3_ref, ws2_ref, xs_ref, h2_ref, ridx_ref, rw_ref) = refs
        x = x_ref[...]
    m = mod_ref[0]
    g = jax.nn.gelu(ys_ref[...].astype(F32))
    a = g * jax.nn.sigmoid(jnp.dot(g.astype(BF16), wglu_ref[...], preferred_element_type=F32))
    pa = jnp.dot(a.astype(BF16), wps_ref[...], preferred_element_type=F32)
    pb = jnp.dot(yf_ref[...], wpf_ref[...], preferred_element_type=F32)
    gt = gt_ref[...].astype(F32)
    merged = gt[:, :D_MODEL] * pa + gt[:, D_MODEL:] * pb
    x1 = x + m[2:3, :] * jnp.dot(merged.astype(BF16), wout_ref[...], preferred_element_type=F32)
    h2 = _rms(x1, n2_ref[...]) * (1.0 + m[4:5, :]) + m[3:4, :]
    hb = h2.astype(BF16)
    for j in range(ROW_SUB):
        h2_ref[pl.ds(j, h2.shape[0], stride=ROW_SUB), :] = h2[:, j * LANES:(j + 1) * LANES]
    wr = wrt_ref[...]
    wr_hi = wr.astype(BF16)
    wr_lo = (wr - wr_hi.astype(F32)).astype(BF16)
    h_lo = (h2 - hb.astype(F32)).astype(BF16)
    dn = (((1,), (1,)), ((), ()))
    logits_t = (lax.dot_general(wr_hi, hb, dn, preferred_element_type=F32)
                + lax.dot_general(wr_hi, h_lo, dn, preferred_element_type=F32)
                + lax.dot_general(wr_lo, hb, dn, preferred_element_type=F32))
    ridx_ref[...], rw_ref[...] = _route(logits_t, rb_ref[...])
    s1 = jnp.dot(hb, ws1_ref[...], preferred_element_type=F32)
    s3 = jnp.dot(hb, ws3_ref[...], preferred_element_type=F32)
    shared = jnp.dot((s1 * jax.nn.sigmoid(s1) * s3).astype(BF16), ws2_ref[...], preferred_element_type=F32)
    xs_ref[...] = x1 + m[5:6, :] * shared


def _merge(ys, yf, gt, x, pos, mod, mod_row, n2, weights, tm):
    n = x.shape[0]
    has_pos = pos is not None
    row = lambda i: (i, 0)
    const = lambda a: pl.BlockSpec(a.shape, lambda i: (0,) * a.ndim)
    in_specs = [pl.BlockSpec((tm, D_S5), row), pl.BlockSpec((tm, D_FNET), row),
                pl.BlockSpec((tm, 2 * D_MODEL), row), pl.BlockSpec((tm, D_MODEL), row)]
    args = [ys, yf, gt, x]
    if has_pos:
        nper = pos.shape[0] // tm
        in_specs.append(pl.BlockSpec((tm, D_MODEL), lambda i: (i % nper, 0)))
        args.append(pos)
    in_specs += [pl.BlockSpec((1, N_MOD, D_MODEL), lambda i: (mod_row(i, tm), 0, 0)), const(n2)]
    args += [mod, n2]
    in_specs += [const(w) for w in weights]
    args += list(weights)
    return pl.pallas_call(
        functools.partial(_merge_kernel, has_pos),
        out_shape=(jax.ShapeDtypeStruct((n, D_MODEL), F32),
                   jax.ShapeDtypeStruct((n * ROW_SUB, LANES), F32),
                   jax.ShapeDtypeStruct((ROUTE_ROWS, n), jnp.int32),
                   jax.ShapeDtypeStruct((ROUTE_ROWS, n), F32)),
        grid=(n // tm,),
        in_specs=in_specs,
        out_specs=(pl.BlockSpec((tm, D_MODEL), row), pl.BlockSpec((tm * ROW_SUB, LANES), row),
                   pl.BlockSpec((ROUTE_ROWS, tm), lambda i: (0, i)),
                   pl.BlockSpec((ROUTE_ROWS, tm), lambda i: (0, i))),
        compiler_params=_params(("arbitrary",)),
        name="merge",
    )(*args)


MOE_SUB = 4096
MOE_TM = 128
MOE_TMAX = MOE_SUB * TOP_K // MOE_TM + N_EXPERTS
MOE_TS = MOE_TMAX + 2
MOE_DUMMY = 256
MOE_RMW = 16
ROW_TILE = MOE_TM * ROW_SUB


def _moe_plan(ridx, rw):
    n = ridx.shape[1]
    nsub = n // MOE_SUB
    npair = n * TOP_K
    t = jnp.arange(n, dtype=jnp.int32)
    key = ((t // MOE_SUB) * N_EXPERTS)[None] + ridx[:TOP_K]
    key = (key * MOE_SUB + (t % MOE_SUB)[None]).reshape(-1)
    skey, sw = lax.sort((key, rw[:TOP_K].reshape(-1)), num_keys=1)
    stok = jnp.concatenate([(skey % MOE_SUB) * ROW_SUB, jnp.zeros((MOE_TM,), jnp.int32)])
    sw_rows = jnp.concatenate([sw, jnp.zeros((LANES,), F32)]).reshape(npair // LANES + 1, 1, LANES)
    hits = ridx[:TOP_K].reshape(TOP_K, nsub, 1, MOE_SUB) == jnp.arange(N_EXPERTS, dtype=jnp.int32)[None, None, :, None]
    cnt = jnp.sum(hits.astype(jnp.int32), axis=(0, 3))
    poff = (jnp.cumsum(cnt.reshape(-1)) - cnt.reshape(-1)).reshape(nsub, N_EXPERTS)
    ntile = (cnt + MOE_TM - 1) // MOE_TM
    tcum = jnp.cumsum(ntile, axis=1)
    toff = tcum - ntile
    tstart = jnp.concatenate([toff, tcum[:, -1:]], axis=1).reshape(-1).astype(jnp.int32)
    j = jnp.arange(MOE_TS, dtype=jnp.int32) - 1
    valid = (j[None] >= 0) & (j[None] < tcum[:, -1:])
    te = jnp.minimum(jnp.sum(j[None, :, None] >= tcum[:, None, :], axis=-1), N_EXPERTS - 1)
    pick = lambda a: jnp.take_along_axis(a, te, axis=1)
    first = (j[None] - pick(toff)) * MOE_TM
    p0 = jnp.where(valid, pick(poff) + first, 0).reshape(-1).astype(jnp.int32)
    nv = jnp.where(valid, jnp.minimum(pick(cnt) - first, MOE_TM), 0).reshape(-1).astype(jnp.int32)
    return tstart, p0, nv, stok, sw_rows


def _moe_kernel(ts_ref, p0_ref, nv_ref, tok_ref, sw_ref, src_ref, w1_ref, w3_ref, w2_ref, y_ref,
                xt_ref, xb_ref, ot_ref, w1b_ref, w3b_ref, w2b_ref):
    sub = pl.program_id(0)
    e = pl.program_id(1)
    base = sub * MOE_TS + 1

    def gather(p0):
        for mi in range(MOE_TM):
            tok = pl.multiple_of(tok_ref[p0 + mi], ROW_SUB)
            xt_ref[mi * ROW_SUB:(mi + 1) * ROW_SUB, :] = src_ref[pl.ds(tok, ROW_SUB), :]
        for j in range(ROW_SUB):
            xb_ref[:, j * LANES:(j + 1) * LANES] = xt_ref[pl.ds(j, MOE_TM, stride=ROW_SUB), :].astype(BF16)

    def scatter(p0, nv, masked):
        for u in range(0, MOE_TM, MOE_RMW):
            new = []
            for i in range(MOE_RMW):
                tok = tok_ref[p0 + u + i]
                if masked:
                    tok = jnp.where(u + i < nv, tok, MOE_SUB * ROW_SUB)
                tok = pl.multiple_of(tok, ROW_SUB)
                new.append((tok, y_ref[pl.ds(tok, ROW_SUB), :]
                            + ot_ref[(u + i) * ROW_SUB:(u + i + 1) * ROW_SUB, :]))
            for tok, v in new:
                y_ref[pl.ds(tok, ROW_SUB), :] = v

    @pl.when(e == 0)
    def _():
        y_ref[...] = jnp.zeros_like(y_ref)
        ot_ref[...] = jnp.zeros_like(ot_ref)
        gather(p0_ref[base])

    w1b_ref[...] = w1_ref[0].astype(BF16)
    w3b_ref[...] = w3_ref[0].astype(BF16)
    w2b_ref[...] = w2_ref[0].astype(BF16)

    def step(cur, masked):
        scatter(p0_ref[cur - 1], nv_ref[cur - 1], masked)
        p0 = p0_ref[cur]
        nv = nv_ref[cur]
        x = xb_ref[...]
        a = jnp.dot(x, w1b_ref[...], preferred_element_type=F32)
        b = jnp.dot(x, w3b_ref[...], preferred_element_type=F32)
        r0 = p0 // LANES
        c = p0 % LANES
        lane = lax.broadcasted_iota(jnp.int32, (1, LANES), 1)
        ga = pltpu.roll(sw_ref[r0], LANES - c, axis=1)
        gb = pltpu.roll(sw_ref[r0 + 1], LANES - c, axis=1)
        g = jnp.where(lane < nv, jnp.where(lane < LANES - c, ga, gb), 0.0)
        rows = lax.broadcasted_iota(jnp.int32, (MOE_TM, MOE_TM), 0)
        cols = lax.broadcasted_iota(jnp.int32, (MOE_TM, MOE_TM), 1)
        gcol = jnp.sum(jnp.where(rows == cols, jnp.broadcast_to(g, (MOE_TM, MOE_TM)), 0.0),
                       axis=1, keepdims=True)
        act = (a * jax.nn.sigmoid(a) * b * gcol).astype(BF16)
        o = jnp.dot(act, w2b_ref[...], preferred_element_type=F32)
        for j in range(ROW_SUB):
            ot_ref[pl.ds(j, MOE_TM, stride=ROW_SUB), :] = o[:, j * LANES:(j + 1) * LANES]
        gather(p0_ref[cur + 1])

    def body(i, carry):
        cur = base + i
        prev_full = nv_ref[cur - 1] == MOE_TM

        @pl.when(prev_full)
        def _():
            step(cur, False)

        @pl.when(jnp.logical_not(prev_full))
        def _():
            step(cur, True)

        return carry

    first = ts_ref[sub * (N_EXPERTS + 1) + e]
    last = ts_ref[sub * (N_EXPERTS + 1) + e + 1]
    lax.fori_loop(first, last, body, 0)

    @pl.when(e == N_EXPERTS - 1)
    def _():
        scatter(p0_ref[base + last - 1], nv_ref[base + last - 1], True)


def _moe(h2_rows, plan, w1, w3, w2):
    tstart, p0, nv, stok, sw_rows = plan
    nsub = h2_rows.shape[0] // (MOE_SUB * ROW_SUB)
    wmap = lambda s, e, ts, p0, nv: (e, 0, 0)
    sub2 = lambda s, e, ts, p0, nv: (s, 0)
    grid_spec = pltpu.PrefetchScalarGridSpec(
        num_scalar_prefetch=3,
        grid=(nsub, N_EXPERTS),
        in_specs=[pl.BlockSpec(memory_space=pltpu.SMEM),
                  pl.BlockSpec(sw_rows.shape, lambda s, e, ts, p0, nv: (0, 0, 0)),
                  pl.BlockSpec((MOE_SUB * ROW_SUB, LANES), sub2, pipeline_mode=pl.Buffered(1)),
                  pl.BlockSpec((1, D_MODEL, D_EXPERT), wmap),
                  pl.BlockSpec((1, D_MODEL, D_EXPERT), wmap),
                  pl.BlockSpec((1, D_EXPERT, D_MODEL), wmap)],
        out_specs=pl.BlockSpec(((MOE_SUB + MOE_DUMMY) * ROW_SUB, LANES), sub2, pipeline_mode=pl.Buffered(1)),
        scratch_shapes=[pltpu.VMEM((ROW_TILE, LANES), F32), pltpu.VMEM((MOE_TM, D_MODEL), BF16),
                        pltpu.VMEM((ROW_TILE, LANES), F32),
                        pltpu.VMEM((D_MODEL, D_EXPERT), BF16), pltpu.VMEM((D_MODEL, D_EXPERT), BF16),
                        pltpu.VMEM((D_EXPERT, D_MODEL), BF16)])
    return pl.pallas_call(
        _moe_kernel,
        grid_spec=grid_spec,
        out_shape=jax.ShapeDtypeStruct((nsub * (MOE_SUB + MOE_DUMMY) * ROW_SUB, LANES), F32),
        compiler_params=_params(("arbitrary", "arbitrary")),
        name="moe",
    )(tstart, p0, nv, stok, sw_rows, h2_rows, w1, w3, w2)


def _final_kernel(xs_ref, y_ref, mod_ref, fg_ref, o_ref):
    tm = xs_ref.shape[0]
    y = jnp.concatenate([y_ref[pl.ds(j, tm, stride=ROW_SUB), :] for j in range(ROW_SUB)], axis=1)
    x2 = xs_ref[...] + mod_ref[0][5:6, :] * y
    o_ref[...] = _rms(x2, fg_ref[...])


def _final(xs, y_rows, mod, mod_row, fg, tm):
    n = xs.shape[0]
    per_sub = MOE_SUB // tm
    stride = (MOE_SUB + MOE_DUMMY) // tm
    row = lambda i: (i, 0)
    return pl.pallas_call(
        _final_kernel,
        out_shape=jax.ShapeDtypeStruct((n, D_MODEL), F32),
        grid=(n // tm,),
        in_specs=[pl.BlockSpec((tm, D_MODEL), row),
                  pl.BlockSpec((tm * ROW_SUB, LANES), lambda i: ((i // per_sub) * stride + i % per_sub, 0)),
                  pl.BlockSpec((1, N_MOD, D_MODEL), lambda i: (mod_row(i, tm), 0, 0)),
                  pl.BlockSpec((1, D_MODEL), lambda i: (0, 0))],
        out_specs=pl.BlockSpec((tm, D_MODEL), row),
        compiler_params=_params(("arbitrary",)),
        name="final",
    )(xs, y_rows, mod, fg)


def _grid_pos_embed(n_tokens):
    rows = n_tokens // GRID_W
    r, col = jnp.meshgrid(jnp.arange(rows, dtype=F32), jnp.arange(GRID_W, dtype=F32), indexing="ij")
    quarter = D_MODEL // 4
    omega = 1.0 / (10000.0 ** (jnp.arange(quarter, dtype=F32) / quarter))

    def emb(p):
        a = p.reshape(-1)[:, None] * omega
        return jnp.concatenate([jnp.sin(a), jnp.cos(a)], axis=-1)

    return jnp.concatenate([emb(r), emb(col)], axis=-1)


def _stream(x3, pos, mod, first_row, h0, s5_ops, p):
    nb, seq, _ = x3.shape
    n = nb * seq
    nk = seq // S5_CHUNK
    x = x3.reshape(n, D_MODEL)
    per_seq_mod = first_row > 0

    def mod_row(i, tm):
        return first_row + (i * tm) // seq if per_seq_mod else 0

    us, uf, gt = _inproj(x, pos, mod, mod_row, p["norm1_g"], p["w_in"], 512)
    ys, fin = _s5(us, s5_ops, h0, nb, nk)
    yf = _fourier(uf, nb, seq, min(seq, 512))
    xs, h2_rows, ridx, rw = _merge(ys, yf, gt, x, pos, mod, mod_row, p["norm2_g"], p["merge_w"], 256)
    y_rows = _moe(h2_rows, _moe_plan(ridx, rw), p["w1"], p["w3"], p["w2"])
    out = _final(xs, y_rows, mod, mod_row, p["final_g"], 256)
    return out.reshape(nb, seq, D_MODEL), fin


def kernel(x_prompt, x_sample, state_s5_re, state_s5_im, c, c_ctx, w_ada, b_ada, norm1_g, norm2_g, w_in,
           lam_re, lam_im, log_dt, b_re, b_im, c_re, c_im, d_skip, w_glu, w_proj_s5, w_proj_f, w_out,
           w_router, router_bias, w1, w3, w2, ws1, ws3, ws2, final_norm_g):
    nb_ctx = x_prompt.shape[0]
    nb_lat, seq_lat, _ = x_sample.shape
    half = 2 * S5_STATE

    cond = jnp.concatenate([c_ctx[None], c, jnp.zeros((MOD_ROWS - 1 - nb_lat, D_MODEL), F32)], axis=0)
    mod = _adaln(cond, w_ada[0], b_ada[0]).reshape(MOD_ROWS, N_MOD, D_MODEL)

    s5_ops = _s5ops(lam_re[0], lam_im[0], log_dt[0], b_re[0], b_im[0], c_re[0], c_im[0], d_skip[0])
    p = dict(
        norm1_g=norm1_g[0][None], norm2_g=norm2_g[0][None], final_g=final_norm_g[None],
        w_in=w_in[0].astype(BF16), w1=w1[0], w3=w3[0], w2=w2[0],
        merge_w=(w_glu[0].astype(BF16), w_proj_s5[0].astype(BF16), w_proj_f[0].astype(BF16),
                 w_out[0].astype(BF16), w_router[0].T, router_bias[0][:, None],
                 ws1[0].astype(BF16), ws3[0].astype(BF16), ws2[0].astype(BF16)))

    def pack_state(sr, si):
        f = lambda a: a.astype(F32).transpose(2, 0, 1, 3).reshape(S5_GROUPS, a.shape[0], half)
        return jnp.concatenate([f(sr), f(si)], axis=-1)

    def unpack_state(fin, lo):
        nb = fin.shape[1]
        return fin[..., lo:lo + half].reshape(S5_GROUPS, nb, 2, S5_STATE).transpose(1, 2, 0, 3)[:, None]

    h0_ctx = jnp.zeros((S5_GROUPS, nb_ctx, 2 * half), F32)
    y_prompt, fin = _stream(x_prompt, None, mod, 0, h0_ctx, s5_ops, p)
    h0_lat = pack_state(state_s5_re[:, 0], state_s5_im[:, 0])
    y_sample, _ = _stream(x_sample, _grid_pos_embed(seq_lat), mod, 1, h0_lat, s5_ops, p)
    return (y_prompt, y_sample, unpack_state(fin, 0).astype(x_prompt.dtype),
            unpack_state(fin, half).astype(x_prompt.dtype))
```

```python
import functools
import math

import jax
import jax.numpy as jnp
from jax import lax
from jax.experimental import pallas as pl
from jax.experimental.pallas import tpu as pltpu

D_MODEL = 1024
GRID_W = 64
D_S5 = 768
S5_GROUP = 16
S5_GROUPS = 48
S5_STATE = 64
D_FNET = 256
FNET_GROUP = 64
N_EXPERTS = 64
TOP_K = 6
N_EXPERT_GROUPS = 8
EXPERTS_PER_GROUP = N_EXPERTS // N_EXPERT_GROUPS
TOPK_GROUPS = 4
D_EXPERT = 256
ROUTED_SCALE = 2.5
N_MOD = 6
EPS = 1e-6

S5_CHUNK = 16
S5_ROW = S5_CHUNK * S5_GROUP
MOD_ROWS = 8
ROUTE_ROWS = 8
LANES = 128
ROW_SUB = D_MODEL // LANES
VMEM_LIMIT = 56 * 1024 * 1024

BF16 = jnp.bfloat16
F32 = jnp.float32


def _params(sem, vmem=VMEM_LIMIT):
    return pltpu.CompilerParams(dimension_semantics=sem, vmem_limit_bytes=vmem)


def _rms(x, g):
    return x * lax.rsqrt(jnp.mean(x * x, axis=-1, keepdims=True) + EPS) * g


def _adaln_kernel(c_ref, w_ref, b_ref, o_ref):
    c = c_ref[...]
    o_ref[...] = jnp.dot(c * jax.nn.sigmoid(c), w_ref[...], precision=lax.Precision.HIGHEST,
                         preferred_element_type=F32) + b_ref[...]


def _adaln(cond, w_ada, b_ada):
    n_out = N_MOD * D_MODEL
    return pl.pallas_call(
        _adaln_kernel,
        out_shape=jax.ShapeDtypeStruct((MOD_ROWS, n_out), F32),
        grid=(N_MOD,),
        in_specs=[pl.BlockSpec((MOD_ROWS, D_MODEL), lambda i: (0, 0)),
                  pl.BlockSpec((D_MODEL, D_MODEL), lambda i: (0, i)),
                  pl.BlockSpec((1, D_MODEL), lambda i: (0, i))],
        out_specs=pl.BlockSpec((MOD_ROWS, D_MODEL), lambda i: (0, i)),
        compiler_params=_params(("arbitrary",)),
        name="adaln",
    )(cond, w_ada, b_ada.reshape(1, n_out))


def _inproj_kernel(has_pos, *refs):
    if has_pos:
        x_ref, pos_ref, mod_ref, g_ref, w_ref, us_ref, uf_ref, gt_ref = refs
        x = x_ref[...] + pos_ref[...]
    else:
        x_ref, mod_ref, g_ref, w_ref, us_ref, uf_ref, gt_ref = refs
        x = x_ref[...]
    m = mod_ref[0]
    h = _rms(x, g_ref[...]) * (1.0 + m[1:2, :]) + m[0:1, :]
    p = jnp.dot(h.astype(BF16), w_ref[...], preferred_element_type=F32)
    us_ref[...] = p[:, :D_S5]
    uf_ref[...] = p[:, D_S5:D_MODEL].astype(BF16)
    gt_ref[...] = jax.nn.sigmoid(p[:, D_MODEL:]).astype(BF16)


def _inproj(x, pos, mod, mod_row, norm_g, w_in_bf, tm):
    n = x.shape[0]
    has_pos = pos is not None
    row = lambda i: (i, 0)
    in_specs = [pl.BlockSpec((tm, D_MODEL), row)]
    args = [x]
    if has_pos:
        nper = pos.shape[0] // tm
        in_specs.append(pl.BlockSpec((tm, D_MODEL), lambda i: (i % nper, 0)))
        args.append(pos)
    in_specs += [pl.BlockSpec((1, N_MOD, D_MODEL), lambda i: (mod_row(i, tm), 0, 0)),
                 pl.BlockSpec((1, D_MODEL), lambda i: (0, 0)),
                 pl.BlockSpec(w_in_bf.shape, lambda i: (0, 0))]
    args += [mod, norm_g, w_in_bf]
    return pl.pallas_call(
        functools.partial(_inproj_kernel, has_pos),
        out_shape=(jax.ShapeDtypeStruct((n, D_S5), F32),
                   jax.ShapeDtypeStruct((n, D_FNET), BF16),
                   jax.ShapeDtypeStruct((n, 2 * D_MODEL), BF16)),
        grid=(n // tm,),
        in_specs=in_specs,
        out_specs=(pl.BlockSpec((tm, D_S5), row), pl.BlockSpec((tm, D_FNET), row),
                   pl.BlockSpec((tm, 2 * D_MODEL), row)),
        compiler_params=_params(("arbitrary",)),
        name="inproj",
    )(*args)


def _shift_lanes(x, k):
    if k == 0:
        return x
    z = jnp.zeros((x.shape[0], abs(k)), x.dtype)
    if k > 0:
        return jnp.concatenate([z, x[:, :x.shape[1] - k]], axis=1)
    return jnp.concatenate([x[:, -k:], z], axis=1)


def _s5ops_kernel(lam_re_ref, lam_im_ref, dt_ref, btr_ref, bti_ref, cr_ref, ci_ref, d_ref,
                  m_ref, wi_ref, wot_ref, coef_ref):
    hi = lax.Precision.HIGHEST
    lr = jnp.minimum(lam_re_ref[0], -1e-4)
    li = lam_im_ref[0]
    dt = jnp.exp(dt_ref[0])
    mag = jnp.exp(lr * dt)
    ar = mag * jnp.cos(li * dt)
    ai = mag * jnp.sin(li * dt)
    den = lr * lr + li * li
    nr = ar - 1.0
    qr = (nr * lr + ai * li) / den
    qi = (ai * lr - nr * li) / den
    pr, pi = [], []
    for n in range(S5_CHUNK + 1):
        pm = jnp.exp(float(n) * (lr * dt))
        pr.append(pm * jnp.cos(float(n) * (li * dt)))
        pi.append(pm * jnp.sin(float(n) * (li * dt)))
    bbr, bbi, car, cai = [], [], [], []
    for d in range(2):
        btr = btr_ref[0, d]
        bti = bti_ref[0, d]
        bbr.append(qr[d:d + 1] * btr - qi[d:d + 1] * bti)
        bbi.append(qr[d:d + 1] * bti + qi[d:d + 1] * btr)
        cr = cr_ref[0, d]
        ci = ci_ref[0, d]
        car.append([cr * pr[n][d:d + 1] - ci * pi[n][d:d + 1] for n in range(S5_CHUNK + 1)])
        cai.append([cr * pi[n][d:d + 1] + ci * pr[n][d:d + 1] for n in range(S5_CHUNK + 1)])

    def lag_kernels(d, order):
        a = jnp.concatenate([car[d][n] for n in order], axis=0)
        b = jnp.concatenate([cai[d][n] for n in order], axis=0)
        dn = (((1,), (1,)), ((), ()))
        return (lax.dot_general(bbr[d], a, dn, precision=hi, preferred_element_type=F32)
                - lax.dot_general(bbi[d], b, dn, precision=hi, preferred_element_type=F32))

    ktf = lag_kernels(0, range(S5_CHUNK))
    ktb = lag_kernels(1, range(S5_CHUNK - 1, -1, -1))
    row = lax.broadcasted_iota(jnp.int32, (S5_GROUP, S5_ROW), 0)
    lane = lax.broadcasted_iota(jnp.int32, (S5_GROUP, S5_ROW), 1)
    dcol = d_ref[0]
    for j in range(S5_CHUNK):
        rows = slice(j * S5_GROUP, (j + 1) * S5_GROUP)
        blk = _shift_lanes(ktf, S5_GROUP * j) + _shift_lanes(ktb, -S5_GROUP * (S5_CHUNK - 1 - j))
        blk = blk + jnp.where(lane == S5_GROUP * j + row, dcol, 0.0)
        m_ref[0, rows, :] = blk.astype(BF16)
        nf = S5_CHUNK - 1 - j
        wi = jnp.concatenate([pr[nf][0:1] * bbr[0] - pi[nf][0:1] * bbi[0],
                              pr[j][1:2] * bbr[1] - pi[j][1:2] * bbi[1],
                              pr[nf][0:1] * bbi[0] + pi[nf][0:1] * bbr[0],
                              pr[j][1:2] * bbi[1] + pi[j][1:2] * bbr[1]], axis=1)
        wi_ref[0, rows, :] = wi.astype(BF16)
        wot = jnp.concatenate([car[0][j + 1], car[1][S5_CHUNK - j],
                               -cai[0][j + 1], -cai[1][S5_CHUNK - j]], axis=1)
        wot_ref[0, rows, :] = wot.astype(BF16)
    coef_ref[0, 0:1, :] = jnp.concatenate([pr[S5_CHUNK][0:1], pr[S5_CHUNK][1:2]], axis=1)
    coef_ref[0, 1:2, :] = jnp.concatenate([pi[S5_CHUNK][0:1], pi[S5_CHUNK][1:2]], axis=1)


def _s5ops(lam_re, lam_im, log_dt, b_re, b_im, c_re, c_im, d_skip):
    g3 = lambda g: (g, 0, 0)
    g4 = lambda g: (g, 0, 0, 0)
    sw = lambda a: jnp.swapaxes(a.astype(F32), 0, 1)
    dt = jnp.broadcast_to(sw(log_dt)[..., None], (S5_GROUPS, 2, S5_STATE))
    args = (sw(lam_re), sw(lam_im), dt, sw(jnp.swapaxes(b_re, 2, 3)), sw(jnp.swapaxes(b_im, 2, 3)),
            sw(c_re), sw(c_im), d_skip.astype(F32).reshape(S5_GROUPS, S5_GROUP, 1))
    vec = pl.BlockSpec((1, 2, S5_STATE), g3)
    mat = pl.BlockSpec((1, 2, S5_GROUP, S5_STATE), g4)
    op = pl.BlockSpec((1, S5_ROW, S5_ROW), g3)
    return pl.pallas_call(
        _s5ops_kernel,
        out_shape=(jax.ShapeDtypeStruct((S5_GROUPS, S5_ROW, S5_ROW), BF16),) * 3
        + (jax.ShapeDtypeStruct((S5_GROUPS, 2, 2 * S5_STATE), F32),),
        grid=(S5_GROUPS,),
        in_specs=[vec, vec, vec, mat, mat, mat, mat, pl.BlockSpec((1, S5_GROUP, 1), g3)],
        out_specs=(op, op, op, pl.BlockSpec((1, 2, 2 * S5_STATE), g3)),
        compiler_params=_params(("arbitrary",)),
        name="s5ops",
    )(*args)


S5_BLOCK_GROUPS = LANES // S5_GROUP


def _s5_perm():
    a = jnp.arange(S5_CHUNK * LANES, dtype=jnp.int32)
    dst = ((a % LANES) // S5_GROUP) * S5_ROW + (a // LANES) * S5_GROUP + a % S5_GROUP
    return (dst[:, None] == a[None, :]).astype(BF16)


def _s5_kernel(nb, nk, u_ref, perm_ref, m_ref, wi_ref, wot_ref, coef_ref, h0_ref, y_ref, fin_ref,
               sr_ref, si_ref, efr_ref, ebr_ref, efi_ref, ebi_ref, uall_ref, yall_ref):
    gl = pl.program_id(1)
    rows = nb * nk
    half = 2 * S5_STATE

    @pl.when(gl == 0)
    def _():
        xcat = jnp.concatenate([u_ref[pl.ds(t, rows, stride=S5_CHUNK), :].astype(BF16)
                                for t in range(S5_CHUNK)], axis=1)
        uall = jnp.dot(xcat, perm_ref[...], preferred_element_type=F32).astype(BF16)
        for g in range(S5_BLOCK_GROUPS):
            uall_ref[g] = uall[:, g * S5_ROW:(g + 1) * S5_ROW]

    u = uall_ref[gl]
    s = jnp.dot(u, wi_ref[0], preferred_element_type=F32)
    sr_ref[...] = s[:, :half]
    si_ref[...] = s[:, half:]
    c_r = coef_ref[0, 0:1, :]
    c_i = coef_ref[0, 1:2, :]
    e_r = h0_ref[0, :, :half]
    e_i = h0_ref[0, :, half:]
    is_fwd = lax.broadcasted_iota(jnp.int32, (nb, half), 1) < S5_STATE
    for j in range(nk):
        rf = pl.ds(j, nb, stride=nk)
        rb = pl.ds(nk - 1 - j, nb, stride=nk)
        efr_ref[rf, :] = e_r
        ebr_ref[rb, :] = e_r
        efi_ref[rf, :] = e_i
        ebi_ref[rb, :] = e_i
        s_r = jnp.where(is_fwd, sr_ref[rf, :], sr_ref[rb, :])
        s_i = jnp.where(is_fwd, si_ref[rf, :], si_ref[rb, :])
        e_r, e_i = c_r * e_r - c_i * e_i + s_r, c_r * e_i + c_i * e_r + s_i
    fin_ref[0, :, :half] = e_r
    fin_ref[0, :, half:] = e_i
    y = jnp.dot(u, m_ref[0], preferred_element_type=F32)
    fwd_rows = lax.broadcasted_iota(jnp.int32, (rows, half), 1) < S5_STATE
    e = jnp.concatenate([jnp.where(fwd_rows, efr_ref[...], ebr_ref[...]),
                         jnp.where(fwd_rows, efi_ref[...], ebi_ref[...])], axis=1).astype(BF16)
    y = y + lax.dot_general(e, wot_ref[0], (((1,), (1,)), ((), ())), preferred_element_type=F32)
    yall_ref[gl] = y.astype(BF16)

    @pl.when(gl == S5_BLOCK_GROUPS - 1)
    def _():
        ycat = jnp.concatenate([yall_ref[g] for g in range(S5_BLOCK_GROUPS)], axis=1)
        out = lax.dot_general(ycat, perm_ref[...], (((1,), (1,)), ((), ())), preferred_element_type=F32)
        for t in range(S5_CHUNK):
            y_ref[pl.ds(t, rows, stride=S5_CHUNK), :] = out[:, t * LANES:(t + 1) * LANES]


def _s5(u, ops, h0, nb, nk):
    m, w_in, w_out, coef = ops
    n = u.shape[0]
    rows = nb * nk
    perm = _s5_perm()
    g3 = lambda b, g: (b * S5_BLOCK_GROUPS + g, 0, 0)
    blk = lambda b, g: (0, b)
    return pl.pallas_call(
        functools.partial(_s5_kernel, nb, nk),
        out_shape=(jax.ShapeDtypeStruct((n, D_S5), F32),
                   jax.ShapeDtypeStruct((S5_GROUPS, nb, 4 * S5_STATE), F32)),
        grid=(S5_GROUPS // S5_BLOCK_GROUPS, S5_BLOCK_GROUPS),
        in_specs=[pl.BlockSpec((n, LANES), blk),
                  pl.BlockSpec(perm.shape, lambda b, g: (0, 0)),
                  pl.BlockSpec((1, S5_ROW, S5_ROW), g3),
                  pl.BlockSpec((1, S5_ROW, 4 * S5_STATE), g3),
                  pl.BlockSpec((1, 4 * S5_STATE, S5_ROW), g3),
                  pl.BlockSpec((1, 2, 2 * S5_STATE), g3),
                  pl.BlockSpec((1, nb, 4 * S5_STATE), g3)],
        out_specs=(pl.BlockSpec((n, LANES), blk),
                   pl.BlockSpec((1, nb, 4 * S5_STATE), g3)),
        scratch_shapes=[pltpu.VMEM((rows, 2 * S5_STATE), F32)] * 6 + [
            pltpu.VMEM((S5_BLOCK_GROUPS, rows, S5_ROW), BF16),
            pltpu.VMEM((S5_BLOCK_GROUPS, rows, S5_ROW), BF16)],
        compiler_params=_params(("arbitrary", "arbitrary")),
        name="s5",
    )(u, perm, m, w_in, w_out, coef, h0)


def _dft_tables(seq):
    k = jnp.arange(seq, dtype=jnp.int32)
    na = seq // FNET_GROUP
    ang_a = (2.0 * math.pi / na) * ((jnp.arange(na, dtype=jnp.int32)[:, None] * k[None, :]) % na).astype(F32)
    ang_b = (2.0 * math.pi / seq) * ((jnp.arange(FNET_GROUP, dtype=jnp.int32)[:, None] * k[None, :]) % seq).astype(F32)
    ca, sa = jnp.cos(ang_a)[:, None, :], jnp.sin(ang_a)[:, None, :]
    cb, sb = jnp.cos(ang_b)[None, :, :], jnp.sin(ang_b)[None, :, :]
    cos_jk = (ca * cb - sa * sb).reshape(seq, seq)
    sin_jk = (sa * cb + ca * sb).reshape(seq, seq)
    cs = jnp.concatenate([cos_jk, -sin_jk], axis=1).astype(BF16)
    c = jnp.arange(D_FNET, dtype=jnp.int32)
    same = (c[:, None] // FNET_GROUP) == (c[None, :] // FNET_GROUP)
    angc = (2.0 * math.pi / FNET_GROUP) * (((c[:, None] % FNET_GROUP) * (c[None, :] % FNET_GROUP))
                                           % FNET_GROUP).astype(F32)
    scale = 1.0 / math.sqrt(seq * FNET_GROUP)
    bdc = jnp.where(same, jnp.cos(angc) * scale, 0.0).astype(BF16)
    bds = jnp.where(same, jnp.sin(angc) * scale, 0.0).astype(BF16)
    return cs, bdc, bds


def _fourier_kernel(seq, z_ref, cs_ref, bdc_ref, bds_ref, o_ref, zz_ref):
    @pl.when(pl.program_id(1) == 0)
    def _():
        z = z_ref[...]
        zz_ref[0:seq, :] = jnp.dot(z, bdc_ref[...], preferred_element_type=F32).astype(BF16)
        zz_ref[seq:, :] = jnp.dot(z, bds_ref[...], preferred_element_type=F32).astype(BF16)

    o_ref[...] = jnp.dot(cs_ref[...], zz_ref[...], preferred_element_type=F32).astype(BF16)


def _fourier(z, nb, seq, tl):
    cs, bdc, bds = _dft_tables(seq)
    nt = seq // tl
    return pl.pallas_call(
        functools.partial(_fourier_kernel, seq),
        out_shape=jax.ShapeDtypeStruct(z.shape, BF16),
        grid=(nb, nt),
        in_specs=[pl.BlockSpec((seq, D_FNET), lambda b, i: (b, 0)),
                  pl.BlockSpec((tl, 2 * seq), lambda b, i: (i, 0)),
                  pl.BlockSpec((D_FNET, D_FNET), lambda b, i: (0, 0)),
                  pl.BlockSpec((D_FNET, D_FNET), lambda b, i: (0, 0))],
        out_specs=pl.BlockSpec((tl, D_FNET), lambda b, i: (b * nt + i, 0)),
        scratch_shapes=[pltpu.VMEM((2 * seq, D_FNET), BF16)],
        compiler_params=_params(("arbitrary", "arbitrary")),
        name="fourier",
    )(z, cs, bdc, bds)


def _first_argmax_mask(v, iota, size):
    m = jnp.max(v, axis=0, keepdims=True)
    first = jnp.min(jnp.where(v == m, iota, size), axis=0, keepdims=True)
    return iota == first


def _route(logits_t, bias_col):
    tm = logits_t.shape[1]
    neg = -jnp.inf
    s = jax.nn.sigmoid(logits_t)
    biased = s + bias_col
    io8 = lax.broadcasted_iota(jnp.int32, (EXPERTS_PER_GROUP, tm), 0)
    gs_rows = []
    for g in range(N_EXPERT_GROUPS):
        blk = biased[g * EXPERTS_PER_GROUP:(g + 1) * EXPERTS_PER_GROUP, :]
        m1 = jnp.max(blk, axis=0, keepdims=True)
        rest = jnp.where(_first_argmax_mask(blk, io8, EXPERTS_PER_GROUP), neg, blk)
        gs_rows.append(m1 + jnp.max(rest, axis=0, keepdims=True))
    gs = jnp.concatenate(gs_rows, axis=0)
    iog = lax.broadcasted_iota(jnp.int32, (N_EXPERT_GROUPS, tm), 0)
    gsel = jnp.zeros((N_EXPERT_GROUPS, tm), F32)
    for _ in range(TOPK_GROUPS):
        sel = _first_argmax_mask(gs, iog, N_EXPERT_GROUPS)
        gsel = jnp.where(sel, 1.0, gsel)
        gs = jnp.where(sel, neg, gs)
    emask = jnp.concatenate(
        [jnp.broadcast_to(gsel[g:g + 1, :], (EXPERTS_PER_GROUP, tm)) for g in range(N_EXPERT_GROUPS)], axis=0)
    v = jnp.where(emask > 0.0, biased, neg)
    ioe = lax.broadcasted_iota(jnp.int32, (N_EXPERTS, tm), 0)
    idx_rows, s_rows = [], []
    for _ in range(TOP_K):
        sel = _first_argmax_mask(v, ioe, N_EXPERTS)
        idx_rows.append(jnp.sum(jnp.where(sel, ioe, 0), axis=0, keepdims=True))
        s_rows.append(jnp.sum(jnp.where(sel, s, 0.0), axis=0, keepdims=True))
        v = jnp.where(sel, neg, v)
    denom = s_rows[0]
    for r in s_rows[1:]:
        denom = denom + r
    pad = ROUTE_ROWS - TOP_K
    idx = jnp.concatenate(idx_rows + [jnp.zeros((pad, tm), jnp.int32)], axis=0)
    w = jnp.concatenate([r / denom * ROUTED_SCALE for r in s_rows] + [jnp.zeros((pad, tm), F32)], axis=0)
    return idx, w


def _merge_kernel(has_pos, *refs):
    if has_pos:
        (ys_ref, yf_ref, gt_ref, x_ref, pos_ref, mod_ref, n2_ref, wglu_ref, wps_ref, wpf_ref, wout_ref,
         wrt_ref, rb_ref, ws1_ref, ws3_ref, ws2_ref, xs_ref, h2_ref, ridx_ref, rw_ref) = refs
        x = x_ref[...] + pos_ref[...]
    else:
        (ys_ref, yf_ref, gt_ref, x_ref, mod_ref, n2_ref, wglu_ref, wps_ref, wpf_ref, wout_ref,
         wrt_ref, rb_ref, ws1_ref, ws3_ref, ws2_ref, xs_ref, h2_ref, ridx_ref, rw_ref) = refs
        x = x_ref[...]
    m = mod_ref[0]
    g = jax.nn.gelu(ys_ref[...].astype(F32))
    a = g * jax.nn.sigmoid(jnp.dot(g.astype(BF16), wglu_ref[...], preferred_element_type=F32))
    pa = jnp.dot(a.astype(BF16), wps_ref[...], preferred_element_type=F32)
    pb = jnp.dot(yf_ref[...], wpf_ref[...], preferred_element_type=F32)
    gt = gt_ref[...].astype(F32)
    merged = gt[:, :D_MODEL] * pa + gt[:, D_MODEL:] * pb
    x1 = x + m[2:3, :] * jnp.dot(merged.astype(BF16), wout_ref[...], preferred_element_type=F32)
    h2 = _rms(x1, n2_ref[...]) * (1.0 + m[4:5, :]) + m[3:4, :]
    hb = h2.astype(BF16)
    for j in range(ROW_SUB):
        h2_ref[pl.ds(j, h2.shape[0], stride=ROW_SUB), :] = h2[:, j * LANES:(j + 1) * LANES]
    wr = wrt_ref[...]
    wr_hi = wr.astype(BF16)
    wr_lo = (wr - wr_hi.astype(F32)).astype(BF16)
    h_lo = (h2 - hb.astype(F32)).astype(BF16)
    dn = (((1,), (1,)), ((), ()))
    logits_t = (lax.dot_general(wr_hi, hb, dn, preferred_element_type=F32)
                + lax.dot_general(wr_hi, h_lo, dn, preferred_element_type=F32)
                + lax.dot_general(wr_lo, hb, dn, preferred_element_type=F32))
    ridx_ref[...], rw_ref[...] = _route(logits_t, rb_ref[...])
    s1 = jnp.dot(hb, ws1_ref[...], preferred_element_type=F32)
    s3 = jnp.dot(hb, ws3_ref[...], preferred_element_type=F32)
    shared = jnp.dot((s1 * jax.nn.sigmoid(s1) * s3).astype(BF16), ws2_ref[...], preferred_element_type=F32)
    xs_ref[...] = x1 + m[5:6, :] * shared


def _merge(ys, yf, gt, x, pos, mod, mod_row, n2, weights, tm):
    n = x.shape[0]
    has_pos = pos is not None
    row = lambda i: (i, 0)
    const = lambda a: pl.BlockSpec(a.shape, lambda i: (0,) * a.ndim)
    in_specs = [pl.BlockSpec((tm, D_S5), row), pl.BlockSpec((tm, D_FNET), row),
                pl.BlockSpec((tm, 2 * D_MODEL), row), pl.BlockSpec((tm, D_MODEL), row)]
    args = [ys, yf, gt, x]
    if has_pos:
        nper = pos.shape[0] // tm
        in_specs.append(pl.BlockSpec((tm, D_MODEL), lambda i: (i % nper, 0)))
        args.append(pos)
    in_specs += [pl.BlockSpec((1, N_MOD, D_MODEL), lambda i: (mod_row(i, tm), 0, 0)), const(n2)]
    args += [mod, n2]
    in_specs += [const(w) for w in weights]
    args += list(weights)
    return pl.pallas_call(
        functools.partial(_merge_kernel, has_pos),
        out_shape=(jax.ShapeDtypeStruct((n, D_MODEL), F32),
                   jax.ShapeDtypeStruct((n * ROW_SUB, LANES), F32),
                   jax.ShapeDtypeStruct((ROUTE_ROWS, n), jnp.int32),
                   jax.ShapeDtypeStruct((ROUTE_ROWS, n), F32)),
        grid=(n // tm,),
        in_specs=in_specs,
        out_specs=(pl.BlockSpec((tm, D_MODEL), row), pl.BlockSpec((tm * ROW_SUB, LANES), row),
                   pl.BlockSpec((ROUTE_ROWS, tm), lambda i: (0, i)),
                   pl.BlockSpec((ROUTE_ROWS, tm), lambda i: (0, i))),
        compiler_params=_params(("arbitrary",)),
        name="merge",
    )(*args)


MOE_SUB = 4096
MOE_TM = 256
MOE_TMAX = MOE_SUB * TOP_K // MOE_TM + N_EXPERTS
MOE_TS = MOE_TMAX + 2
MOE_DUMMY = 256
MOE_RMW = 16
ROW_TILE = MOE_TM * ROW_SUB


def _moe_plan(ridx, rw):
    n = ridx.shape[1]
    nsub = n // MOE_SUB
    npair = n * TOP_K
    t = jnp.arange(n, dtype=jnp.int32)
    key = ((t // MOE_SUB) * N_EXPERTS)[None] + ridx[:TOP_K]
    key = (key * MOE_SUB + (t % MOE_SUB)[None]).reshape(-1)
    skey, sw = lax.sort((key, rw[:TOP_K].reshape(-1)), num_keys=1)
    stok = jnp.concatenate([(skey % MOE_SUB) * ROW_SUB, jnp.zeros((MOE_TM,), jnp.int32)])
    sw_rows = jnp.concatenate([sw, jnp.zeros((MOE_TM,), F32)]).reshape((npair + MOE_TM) // LANES, 1, LANES)
    hits = ridx[:TOP_K].reshape(TOP_K, nsub, 1, MOE_SUB) == jnp.arange(N_EXPERTS, dtype=jnp.int32)[None, None, :, None]
    cnt = jnp.sum(hits.astype(jnp.int32), axis=(0, 3))
    poff = (jnp.cumsum(cnt.reshape(-1)) - cnt.reshape(-1)).reshape(nsub, N_EXPERTS)
    ntile = (cnt + MOE_TM - 1) // MOE_TM
    tcum = jnp.cumsum(ntile, axis=1)
    toff = tcum - ntile
    tstart = jnp.concatenate([toff, tcum[:, -1:]], axis=1).reshape(-1).astype(jnp.int32)
    j = jnp.arange(MOE_TS, dtype=jnp.int32) - 1
    valid = (j[None] >= 0) & (j[None] < tcum[:, -1:])
    te = jnp.minimum(jnp.sum(j[None, :, None] >= tcum[:, None, :], axis=-1), N_EXPERTS - 1)
    pick = lambda a: jnp.take_along_axis(a, te, axis=1)
    first = (j[None] - pick(toff)) * MOE_TM
    p0 = jnp.where(valid, pick(poff) + first, 0).reshape(-1).astype(jnp.int32)
    nv = jnp.where(valid, jnp.minimum(pick(cnt) - first, MOE_TM), 0).reshape(-1).astype(jnp.int32)
    return tstart, p0, nv, stok, sw_rows


def _moe_kernel(ts_ref, p0_ref, nv_ref, tok_ref, sw_ref, src_ref, w1_ref, w3_ref, w2_ref, y_ref,
                xt_ref, xb_ref, ot_ref, w1b_ref, w3b_ref, w2b_ref):
    sub = pl.program_id(0)
    e = pl.program_id(1)
    base = sub * MOE_TS + 1

    def gather(p0):
        for mi in range(MOE_TM):
            tok = pl.multiple_of(tok_ref[p0 + mi], ROW_SUB)
            xt_ref[mi * ROW_SUB:(mi + 1) * ROW_SUB, :] = src_ref[pl.ds(tok, ROW_SUB), :]
        for j in range(ROW_SUB):
            xb_ref[:, j * LANES:(j + 1) * LANES] = xt_ref[pl.ds(j, MOE_TM, stride=ROW_SUB), :].astype(BF16)

    def scatter(p0, nv, masked):
        for u in range(0, MOE_TM, MOE_RMW):
            new = []
            for i in range(MOE_RMW):
                tok = tok_ref[p0 + u + i]
                if masked:
                    tok = jnp.where(u + i < nv, tok, MOE_SUB * ROW_SUB)
                tok = pl.multiple_of(tok, ROW_SUB)
                new.append((tok, y_ref[pl.ds(tok, ROW_SUB), :]
                            + ot_ref[(u + i) * ROW_SUB:(u + i + 1) * ROW_SUB, :]))
            for tok, v in new:
                y_ref[pl.ds(tok, ROW_SUB), :] = v

    @pl.when(e == 0)
    def _():
        y_ref[...] = jnp.zeros_like(y_ref)
        ot_ref[...] = jnp.zeros_like(ot_ref)
        gather(p0_ref[base])

    w1b_ref[...] = w1_ref[0].astype(BF16)
    w3b_ref[...] = w3_ref[0].astype(BF16)
    w2b_ref[...] = w2_ref[0].astype(BF16)

    def step(cur, masked):
        scatter(p0_ref[cur - 1], nv_ref[cur - 1], masked)
        p0 = p0_ref[cur]
        nv = nv_ref[cur]
        x = xb_ref[...]
        a = jnp.dot(x, w1b_ref[...], preferred_element_type=F32)
        b = jnp.dot(x, w3b_ref[...], preferred_element_type=F32)
        r0 = p0 // LANES
        c = p0 % LANES
        lane = lax.broadcasted_iota(jnp.int32, (1, LANES), 1)
        rows = lax.broadcasted_iota(jnp.int32, (LANES, LANES), 0)
        cols = lax.broadcasted_iota(jnp.int32, (LANES, LANES), 1)
        gparts = []
        for hh in range(MOE_TM // LANES):
            ga = pltpu.roll(sw_ref[r0 + hh], LANES - c, axis=1)
            gb = pltpu.roll(sw_ref[r0 + hh + 1], LANES - c, axis=1)
            g = jnp.where(lane + hh * LANES < nv, jnp.where(lane < LANES - c, ga, gb), 0.0)
            gparts.append(jnp.sum(jnp.where(rows == cols, jnp.broadcast_to(g, (LANES, LANES)), 0.0),
                                  axis=1, keepdims=True))
        gcol = jnp.concatenate(gparts, axis=0)
        act = (a * jax.nn.sigmoid(a) * b * gcol).astype(BF16)
        o = jnp.dot(act, w2b_ref[...], preferred_element_type=F32)
        for j in range(ROW_SUB):
            ot_ref[pl.ds(j, MOE_TM, stride=ROW_SUB), :] = o[:, j * LANES:(j + 1) * LANES]
        gather(p0_ref[cur + 1])

    def body(i, carry):
        cur = base + i
        prev_full = nv_ref[cur - 1] == MOE_TM

        @pl.when(prev_full)
        def _():
            step(cur, False)

        @pl.when(jnp.logical_not(prev_full))
        def _():
            step(cur, True)

        return carry

    first = ts_ref[sub * (N_EXPERTS + 1) + e]
    last = ts_ref[sub * (N_EXPERTS + 1) + e + 1]
    lax.fori_loop(first, last, body, 0)

    @pl.when(e == N_EXPERTS - 1)
    def _():
        scatter(p0_ref[base + last - 1], nv_ref[base + last - 1], True)


def _moe(h2_rows, plan, w1, w3, w2):
    tstart, p0, nv, stok, sw_rows = plan
    nsub = h2_rows.shape[0] // (MOE_SUB * ROW_SUB)
    wmap = lambda s, e, ts, p0, nv: (e, 0, 0)
    sub2 = lambda s, e, ts, p0, nv: (s, 0)
    grid_spec = pltpu.PrefetchScalarGridSpec(
        num_scalar_prefetch=3,
        grid=(nsub, N_EXPERTS),
        in_specs=[pl.BlockSpec(memory_space=pltpu.SMEM),
                  pl.BlockSpec(sw_rows.shape, lambda s, e, ts, p0, nv: (0, 0, 0)),
                  pl.BlockSpec((MOE_SUB * ROW_SUB, LANES), sub2, pipeline_mode=pl.Buffered(1)),
                  pl.BlockSpec((1, D_MODEL, D_EXPERT), wmap),
                  pl.BlockSpec((1, D_MODEL, D_EXPERT), wmap),
                  pl.BlockSpec((1, D_EXPERT, D_MODEL), wmap)],
        out_specs=pl.BlockSpec(((MOE_SUB + MOE_DUMMY) * ROW_SUB, LANES), sub2, pipeline_mode=pl.Buffered(1)),
        scratch_shapes=[pltpu.VMEM((ROW_TILE, LANES), F32), pltpu.VMEM((MOE_TM, D_MODEL), BF16),
                        pltpu.VMEM((ROW_TILE, LANES), F32),
                        pltpu.VMEM((D_MODEL, D_EXPERT), BF16), pltpu.VMEM((D_MODEL, D_EXPERT), BF16),
                        pltpu.VMEM((D_EXPERT, D_MODEL), BF16)])
    return pl.pallas_call(
        _moe_kernel,
        grid_spec=grid_spec,
        out_shape=jax.ShapeDtypeStruct((nsub * (MOE_SUB + MOE_DUMMY) * ROW_SUB, LANES), F32),
        compiler_params=_params(("arbitrary", "arbitrary")),
        name="moe",
    )(tstart, p0, nv, stok, sw_rows, h2_rows, w1, w3, w2)


def _final_kernel(xs_ref, y_ref, mod_ref, fg_ref, o_ref):
    tm = xs_ref.shape[0]
    y = jnp.concatenate([y_ref[pl.ds(j, tm, stride=ROW_SUB), :] for j in range(ROW_SUB)], axis=1)
    x2 = xs_ref[...] + mod_ref[0][5:6, :] * y
    o_ref[...] = _rms(x2, fg_ref[...])


def _final(xs, y_rows, mod, mod_row, fg, tm):
    n = xs.shape[0]
    per_sub = MOE_SUB // tm
    stride = (MOE_SUB + MOE_DUMMY) // tm
    row = lambda i: (i, 0)
    return pl.pallas_call(
        _final_kernel,
        out_shape=jax.ShapeDtypeStruct((n, D_MODEL), F32),
        grid=(n // tm,),
        in_specs=[pl.BlockSpec((tm, D_MODEL), row),
                  pl.BlockSpec((tm * ROW_SUB, LANES), lambda i: ((i // per_sub) * stride + i % per_sub, 0)),
                  pl.BlockSpec((1, N_MOD, D_MODEL), lambda i: (mod_row(i, tm), 0, 0)),
                  pl.BlockSpec((1, D_MODEL), lambda i: (0, 0))],
        out_specs=pl.BlockSpec((tm, D_MODEL), row),
        compiler_params=_params(("arbitrary",)),
        name="final",
    )(xs, y_rows, mod, fg)


def _grid_pos_embed(n_tokens):
    rows = n_tokens // GRID_W
    r, col = jnp.meshgrid(jnp.arange(rows, dtype=F32), jnp.arange(GRID_W, dtype=F32), indexing="ij")
    quarter = D_MODEL // 4
    omega = 1.0 / (10000.0 ** (jnp.arange(quarter, dtype=F32) / quarter))

    def emb(p):
        a = p.reshape(-1)[:, None] * omega
        return jnp.concatenate([jnp.sin(a), jnp.cos(a)], axis=-1)

    return jnp.concatenate([emb(r), emb(col)], axis=-1)


def _stream(x3, pos, mod, first_row, h0, s5_ops, p):
    nb, seq, _ = x3.shape
    n = nb * seq
    nk = seq // S5_CHUNK
    x = x3.reshape(n, D_MODEL)
    per_seq_mod = first_row > 0

    def mod_row(i, tm):
        return first_row + (i * tm) // seq if per_seq_mod else 0

    us, uf, gt = _inproj(x, pos, mod, mod_row, p["norm1_g"], p["w_in"], 512)
    ys, fin = _s5(us, s5_ops, h0, nb, nk)
    yf = _fourier(uf, nb, seq, min(seq, 512))
    xs, h2_rows, ridx, rw = _merge(ys, yf, gt, x, pos, mod, mod_row, p["norm2_g"], p["merge_w"], 256)
    y_rows = _moe(h2_rows, _moe_plan(ridx, rw), p["w1"], p["w3"], p["w2"])
    out = _final(xs, y_rows, mod, mod_row, p["final_g"], 256)
    return out.reshape(nb, seq, D_MODEL), fin


def kernel(x_prompt, x_sample, state_s5_re, state_s5_im, c, c_ctx, w_ada, b_ada, norm1_g, norm2_g, w_in,
           lam_re, lam_im, log_dt, b_re, b_im, c_re, c_im, d_skip, w_glu, w_proj_s5, w_proj_f, w_out,
           w_router, router_bias, w1, w3, w2, ws1, ws3, ws2, final_norm_g):
    nb_ctx = x_prompt.shape[0]
    nb_lat, seq_lat, _ = x_sample.shape
    half = 2 * S5_STATE

    cond = jnp.concatenate([c_ctx[None], c, jnp.zeros((MOD_ROWS - 1 - nb_lat, D_MODEL), F32)], axis=0)
    mod = _adaln(cond, w_ada[0], b_ada[0]).reshape(MOD_ROWS, N_MOD, D_MODEL)

    s5_ops = _s5ops(lam_re[0], lam_im[0], log_dt[0], b_re[0], b_im[0], c_re[0], c_im[0], d_skip[0])
    p = dict(
        norm1_g=norm1_g[0][None], norm2_g=norm2_g[0][None], final_g=final_norm_g[None],
        w_in=w_in[0].astype(BF16), w1=w1[0], w3=w3[0], w2=w2[0],
        merge_w=(w_glu[0].astype(BF16), w_proj_s5[0].astype(BF16), w_proj_f[0].astype(BF16),
                 w_out[0].astype(BF16), w_router[0].T, router_bias[0][:, None],
                 ws1[0].astype(BF16), ws3[0].astype(BF16), ws2[0].astype(BF16)))

    def pack_state(sr, si):
        f = lambda a: a.astype(F32).transpose(2, 0, 1, 3).reshape(S5_GROUPS, a.shape[0], half)
        return jnp.concatenate([f(sr), f(si)], axis=-1)

    def unpack_state(fin, lo):
        nb = fin.shape[1]
        return fin[..., lo:lo + half].reshape(S5_GROUPS, nb, 2, S5_STATE).transpose(1, 2, 0, 3)[:, None]

    h0_ctx = jnp.zeros((S5_GROUPS, nb_ctx, 2 * half), F32)
    y_prompt, fin = _stream(x_prompt, None, mod, 0, h0_ctx, s5_ops, p)
    h0_lat = pack_state(state_s5_re[:, 0], state_s5_im[:, 0])
    y_sample, _ = _stream(x_sample, _grid_pos_embed(seq_lat), mod, 1, h0_lat, s5_ops, p)
    return (y_prompt, y_sample, unpack_state(fin, 0).astype(x_prompt.dtype),
            unpack_state(fin, half).astype(x_prompt.dtype))
```

```python
import functools
import math

import jax
import jax.numpy as jnp
from jax import lax
from jax.experimental import pallas as pl
from jax.experimental.pallas import tpu as pltpu

D_MODEL = 1024
GRID_W = 64
D_S5 = 768
S5_GROUP = 16
S5_GROUPS = 48
S5_STATE = 64
D_FNET = 256
FNET_GROUP = 64
N_EXPERTS = 64
TOP_K = 6
N_EXPERT_GROUPS = 8
EXPERTS_PER_GROUP = N_EXPERTS // N_EXPERT_GROUPS
TOPK_GROUPS = 4
D_EXPERT = 256
ROUTED_SCALE = 2.5
N_MOD = 6
EPS = 1e-6

S5_CHUNK = 16
S5_ROW = S5_CHUNK * S5_GROUP
MOD_ROWS = 8
ROUTE_ROWS = 8
LANES = 128
ROW_SUB = D_MODEL // LANES
VMEM_LIMIT = 56 * 1024 * 1024

BF16 = jnp.bfloat16
F32 = jnp.float32


def _params(sem, vmem=VMEM_LIMIT):
    return pltpu.CompilerParams(dimension_semantics=sem, vmem_limit_bytes=vmem)


def _rms(x, g):
    return x * lax.rsqrt(jnp.mean(x * x, axis=-1, keepdims=True) + EPS) * g


def _adaln_kernel(c_ref, w_ref, b_ref, o_ref):
    c = c_ref[...]
    o_ref[...] = jnp.dot(c * jax.nn.sigmoid(c), w_ref[...], precision=lax.Precision.HIGHEST,
                         preferred_element_type=F32) + b_ref[...]


def _adaln(cond, w_ada, b_ada):
    n_out = N_MOD * D_MODEL
    return pl.pallas_call(
        _adaln_kernel,
        out_shape=jax.ShapeDtypeStruct((MOD_ROWS, n_out), F32),
        grid=(N_MOD,),
        in_specs=[pl.BlockSpec((MOD_ROWS, D_MODEL), lambda i: (0, 0)),
                  pl.BlockSpec((D_MODEL, D_MODEL), lambda i: (0, i)),
                  pl.BlockSpec((1, D_MODEL), lambda i: (0, i))],
        out_specs=pl.BlockSpec((MOD_ROWS, D_MODEL), lambda i: (0, i)),
        compiler_params=_params(("arbitrary",)),
        name="adaln",
    )(cond, w_ada, b_ada.reshape(1, n_out))


def _inproj_kernel(has_pos, *refs):
    if has_pos:
        x_ref, pos_ref, mod_ref, g_ref, w_ref, us_ref, uf_ref, gt_ref = refs
        x = x_ref[...] + pos_ref[...]
    else:
        x_ref, mod_ref, g_ref, w_ref, us_ref, uf_ref, gt_ref = refs
        x = x_ref[...]
    m = mod_ref[0]
    h = _rms(x, g_ref[...]) * (1.0 + m[1:2, :]) + m[0:1, :]
    p = jnp.dot(h.astype(BF16), w_ref[...], preferred_element_type=F32)
    us_ref[...] = p[:, :D_S5]
    uf_ref[...] = p[:, D_S5:D_MODEL].astype(BF16)
    gt_ref[...] = jax.nn.sigmoid(p[:, D_MODEL:]).astype(BF16)


def _inproj(x, pos, mod, mod_row, norm_g, w_in_bf, tm):
    n = x.shape[0]
    has_pos = pos is not None
    row = lambda i: (i, 0)
    in_specs = [pl.BlockSpec((tm, D_MODEL), row)]
    args = [x]
    if has_pos:
        nper = pos.shape[0] // tm
        in_specs.append(pl.BlockSpec((tm, D_MODEL), lambda i: (i % nper, 0)))
        args.append(pos)
    in_specs += [pl.BlockSpec((1, N_MOD, D_MODEL), lambda i: (mod_row(i, tm), 0, 0)),
                 pl.BlockSpec((1, D_MODEL), lambda i: (0, 0)),
                 pl.BlockSpec(w_in_bf.shape, lambda i: (0, 0))]
    args += [mod, norm_g, w_in_bf]
    return pl.pallas_call(
        functools.partial(_inproj_kernel, has_pos),
        out_shape=(jax.ShapeDtypeStruct((n, D_S5), F32),
                   jax.ShapeDtypeStruct((n, D_FNET), BF16),
                   jax.ShapeDtypeStruct((n, 2 * D_MODEL), BF16)),
        grid=(n // tm,),
        in_specs=in_specs,
        out_specs=(pl.BlockSpec((tm, D_S5), row), pl.BlockSpec((tm, D_FNET), row),
                   pl.BlockSpec((tm, 2 * D_MODEL), row)),
        compiler_params=_params(("arbitrary",)),
        name="inproj",
    )(*args)


def _shift_lanes(x, k):
    if k == 0:
        return x
    z = jnp.zeros((x.shape[0], abs(k)), x.dtype)
    if k > 0:
        return jnp.concatenate([z, x[:, :x.shape[1] - k]], axis=1)
    return jnp.concatenate([x[:, -k:], z], axis=1)


def _s5ops_kernel(lam_re_ref, lam_im_ref, dt_ref, btr_ref, bti_ref, cr_ref, ci_ref, d_ref,
                  m_ref, wi_ref, wot_ref, coef_ref):
    hi = lax.Precision.HIGHEST
    lr = jnp.minimum(lam_re_ref[0], -1e-4)
    li = lam_im_ref[0]
    dt = jnp.exp(dt_ref[0])
    mag = jnp.exp(lr * dt)
    ar = mag * jnp.cos(li * dt)
    ai = mag * jnp.sin(li * dt)
    den = lr * lr + li * li
    nr = ar - 1.0
    qr = (nr * lr + ai * li) / den
    qi = (ai * lr - nr * li) / den
    pr, pi = [], []
    for n in range(S5_CHUNK + 1):
        pm = jnp.exp(float(n) * (lr * dt))
        pr.append(pm * jnp.cos(float(n) * (li * dt)))
        pi.append(pm * jnp.sin(float(n) * (li * dt)))
    bbr, bbi, car, cai = [], [], [], []
    for d in range(2):
        btr = btr_ref[0, d]
        bti = bti_ref[0, d]
        bbr.append(qr[d:d + 1] * btr - qi[d:d + 1] * bti)
        bbi.append(qr[d:d + 1] * bti + qi[d:d + 1] * btr)
        cr = cr_ref[0, d]
        ci = ci_ref[0, d]
        car.append([cr * pr[n][d:d + 1] - ci * pi[n][d:d + 1] for n in range(S5_CHUNK + 1)])
        cai.append([cr * pi[n][d:d + 1] + ci * pr[n][d:d + 1] for n in range(S5_CHUNK + 1)])

    def lag_kernels(d, order):
        a = jnp.concatenate([car[d][n] for n in order], axis=0)
        b = jnp.concatenate([cai[d][n] for n in order], axis=0)
        dn = (((1,), (1,)), ((), ()))
        return (lax.dot_general(bbr[d], a, dn, precision=hi, preferred_element_type=F32)
                - lax.dot_general(bbi[d], b, dn, precision=hi, preferred_element_type=F32))

    ktf = lag_kernels(0, range(S5_CHUNK))
    ktb = lag_kernels(1, range(S5_CHUNK - 1, -1, -1))
    row = lax.broadcasted_iota(jnp.int32, (S5_GROUP, S5_ROW), 0)
    lane = lax.broadcasted_iota(jnp.int32, (S5_GROUP, S5_ROW), 1)
    dcol = d_ref[0]
    for j in range(S5_CHUNK):
        rows = slice(j * S5_GROUP, (j + 1) * S5_GROUP)
        blk = _shift_lanes(ktf, S5_GROUP * j) + _shift_lanes(ktb, -S5_GROUP * (S5_CHUNK - 1 - j))
        blk = blk + jnp.where(lane == S5_GROUP * j + row, dcol, 0.0)
        m_ref[0, rows, :] = blk.astype(BF16)
        nf = S5_CHUNK - 1 - j
        wi = jnp.concatenate([pr[nf][0:1] * bbr[0] - pi[nf][0:1] * bbi[0],
                              pr[j][1:2] * bbr[1] - pi[j][1:2] * bbi[1],
                              pr[nf][0:1] * bbi[0] + pi[nf][0:1] * bbr[0],
                              pr[j][1:2] * bbi[1] + pi[j][1:2] * bbr[1]], axis=1)
        wi_ref[0, rows, :] = wi.astype(BF16)
        wot = jnp.concatenate([car[0][j + 1], car[1][S5_CHUNK - j],
                               -cai[0][j + 1], -cai[1][S5_CHUNK - j]], axis=1)
        wot_ref[0, rows, :] = wot.astype(BF16)
    coef_ref[0, 0:1, :] = jnp.concatenate([pr[S5_CHUNK][0:1], pr[S5_CHUNK][1:2]], axis=1)
    coef_ref[0, 1:2, :] = jnp.concatenate([pi[S5_CHUNK][0:1], pi[S5_CHUNK][1:2]], axis=1)


def _s5ops(lam_re, lam_im, log_dt, b_re, b_im, c_re, c_im, d_skip):
    g3 = lambda g: (g, 0, 0)
    g4 = lambda g: (g, 0, 0, 0)
    sw = lambda a: jnp.swapaxes(a.astype(F32), 0, 1)
    dt = jnp.broadcast_to(sw(log_dt)[..., None], (S5_GROUPS, 2, S5_STATE))
    args = (sw(lam_re), sw(lam_im), dt, sw(jnp.swapaxes(b_re, 2, 3)), sw(jnp.swapaxes(b_im, 2, 3)),
            sw(c_re), sw(c_im), d_skip.astype(F32).reshape(S5_GROUPS, S5_GROUP, 1))
    vec = pl.BlockSpec((1, 2, S5_STATE), g3)
    mat = pl.BlockSpec((1, 2, S5_GROUP, S5_STATE), g4)
    op = pl.BlockSpec((1, S5_ROW, S5_ROW), g3)
    return pl.pallas_call(
        _s5ops_kernel,
        out_shape=(jax.ShapeDtypeStruct((S5_GROUPS, S5_ROW, S5_ROW), BF16),) * 3
        + (jax.ShapeDtypeStruct((S5_GROUPS, 2, 2 * S5_STATE), F32),),
        grid=(S5_GROUPS,),
        in_specs=[vec, vec, vec, mat, mat, mat, mat, pl.BlockSpec((1, S5_GROUP, 1), g3)],
        out_specs=(op, op, op, pl.BlockSpec((1, 2, 2 * S5_STATE), g3)),
        compiler_params=_params(("arbitrary",)),
        name="s5ops",
    )(*args)


S5_BLOCK_GROUPS = LANES // S5_GROUP


def _s5_perm():
    a = jnp.arange(S5_CHUNK * LANES, dtype=jnp.int32)
    dst = ((a % LANES) // S5_GROUP) * S5_ROW + (a // LANES) * S5_GROUP + a % S5_GROUP
    return (dst[:, None] == a[None, :]).astype(BF16)


def _s5_kernel(nb, nk, u_ref, perm_ref, m_ref, wi_ref, wot_ref, coef_ref, h0_ref, y_ref, fin_ref,
               sr_ref, si_ref, efr_ref, ebr_ref, efi_ref, ebi_ref, uall_ref, yall_ref):
    gl = pl.program_id(1)
    rows = nb * nk
    half = 2 * S5_STATE

    @pl.when(gl == 0)
    def _():
        xcat = jnp.concatenate([u_ref[pl.ds(t, rows, stride=S5_CHUNK), :].astype(BF16)
                                for t in range(S5_CHUNK)], axis=1)
        uall = jnp.dot(xcat, perm_ref[...], preferred_element_type=F32).astype(BF16)
        for g in range(S5_BLOCK_GROUPS):
            uall_ref[g] = uall[:, g * S5_ROW:(g + 1) * S5_ROW]

    u = uall_ref[gl]
    s = jnp.dot(u, wi_ref[0], preferred_element_type=F32)
    sr_ref[...] = s[:, :half]
    si_ref[...] = s[:, half:]
    c_r = coef_ref[0, 0:1, :]
    c_i = coef_ref[0, 1:2, :]
    e_r = h0_ref[0, :, :half]
    e_i = h0_ref[0, :, half:]
    is_fwd = lax.broadcasted_iota(jnp.int32, (nb, half), 1) < S5_STATE
    for j in range(nk):
        rf = pl.ds(j, nb, stride=nk)
        rb = pl.ds(nk - 1 - j, nb, stride=nk)
        efr_ref[rf, :] = e_r
        ebr_ref[rb, :] = e_r
        efi_ref[rf, :] = e_i
        ebi_ref[rb, :] = e_i
        s_r = jnp.where(is_fwd, sr_ref[rf, :], sr_ref[rb, :])
        s_i = jnp.where(is_fwd, si_ref[rf, :], si_ref[rb, :])
        e_r, e_i = c_r * e_r - c_i * e_i + s_r, c_r * e_i + c_i * e_r + s_i
    fin_ref[0, :, :half] = e_r
    fin_ref[0, :, half:] = e_i
    y = jnp.dot(u, m_ref[0], preferred_element_type=F32)
    fwd_rows = lax.broadcasted_iota(jnp.int32, (rows, half), 1) < S5_STATE
    e = jnp.concatenate([jnp.where(fwd_rows, efr_ref[...], ebr_ref[...]),
                         jnp.where(fwd_rows, efi_ref[...], ebi_ref[...])], axis=1).astype(BF16)
    y = y + lax.dot_general(e, wot_ref[0], (((1,), (1,)), ((), ())), preferred_element_type=F32)
    yall_ref[gl] = y.astype(BF16)

    @pl.when(gl == S5_BLOCK_GROUPS - 1)
    def _():
        ycat = jnp.concatenate([yall_ref[g] for g in range(S5_BLOCK_GROUPS)], axis=1)
        out = lax.dot_general(ycat, perm_ref[...], (((1,), (1,)), ((), ())), preferred_element_type=F32)
        for t in range(S5_CHUNK):
            y_ref[pl.ds(t, rows, stride=S5_CHUNK), :] = out[:, t * LANES:(t + 1) * LANES]


def _s5(u, ops, h0, nb, nk):
    m, w_in, w_out, coef = ops
    n = u.shape[0]
    rows = nb * nk
    perm = _s5_perm()
    g3 = lambda b, g: (b * S5_BLOCK_GROUPS + g, 0, 0)
    blk = lambda b, g: (0, b)
    return pl.pallas_call(
        functools.partial(_s5_kernel, nb, nk),
        out_shape=(jax.ShapeDtypeStruct((n, D_S5), F32),
                   jax.ShapeDtypeStruct((S5_GROUPS, nb, 4 * S5_STATE), F32)),
        grid=(S5_GROUPS // S5_BLOCK_GROUPS, S5_BLOCK_GROUPS),
        in_specs=[pl.BlockSpec((n, LANES), blk),
                  pl.BlockSpec(perm.shape, lambda b, g: (0, 0)),
                  pl.BlockSpec((1, S5_ROW, S5_ROW), g3),
                  pl.BlockSpec((1, S5_ROW, 4 * S5_STATE), g3),
                  pl.BlockSpec((1, 4 * S5_STATE, S5_ROW), g3),
                  pl.BlockSpec((1, 2, 2 * S5_STATE), g3),
                  pl.BlockSpec((1, nb, 4 * S5_STATE), g3)],
        out_specs=(pl.BlockSpec((n, LANES), blk),
                   pl.BlockSpec((1, nb, 4 * S5_STATE), g3)),
        scratch_shapes=[pltpu.VMEM((rows, 2 * S5_STATE), F32)] * 6 + [
            pltpu.VMEM((S5_BLOCK_GROUPS, rows, S5_ROW), BF16),
            pltpu.VMEM((S5_BLOCK_GROUPS, rows, S5_ROW), BF16)],
        compiler_params=_params(("arbitrary", "arbitrary")),
        name="s5",
    )(u, perm, m, w_in, w_out, coef, h0)


def _dft_tables(seq):
    k = jnp.arange(seq, dtype=jnp.int32)
    na = seq // FNET_GROUP
    ang_a = (2.0 * math.pi / na) * ((jnp.arange(na, dtype=jnp.int32)[:, None] * k[None, :]) % na).astype(F32)
    ang_b = (2.0 * math.pi / seq) * ((jnp.arange(FNET_GROUP, dtype=jnp.int32)[:, None] * k[None, :]) % seq).astype(F32)
    ca, sa = jnp.cos(ang_a)[:, None, :], jnp.sin(ang_a)[:, None, :]
    cb, sb = jnp.cos(ang_b)[None, :, :], jnp.sin(ang_b)[None, :, :]
    cos_jk = (ca * cb - sa * sb).reshape(seq, seq)
    sin_jk = (sa * cb + ca * sb).reshape(seq, seq)
    cs = jnp.concatenate([cos_jk, -sin_jk], axis=1).astype(BF16)
    c = jnp.arange(D_FNET, dtype=jnp.int32)
    same = (c[:, None] // FNET_GROUP) == (c[None, :] // FNET_GROUP)
    angc = (2.0 * math.pi / FNET_GROUP) * (((c[:, None] % FNET_GROUP) * (c[None, :] % FNET_GROUP))
                                           % FNET_GROUP).astype(F32)
    scale = 1.0 / math.sqrt(seq * FNET_GROUP)
    bdc = jnp.where(same, jnp.cos(angc) * scale, 0.0).astype(BF16)
    bds = jnp.where(same, jnp.sin(angc) * scale, 0.0).astype(BF16)
    return cs, bdc, bds


def _fourier_kernel(seq, z_ref, cs_ref, bdc_ref, bds_ref, o_ref, zz_ref):
    @pl.when(pl.program_id(1) == 0)
    def _():
        z = z_ref[...]
        zz_ref[0:seq, :] = jnp.dot(z, bdc_ref[...], preferred_element_type=F32).astype(BF16)
        zz_ref[seq:, :] = jnp.dot(z, bds_ref[...], preferred_element_type=F32).astype(BF16)

    o_ref[...] = jnp.dot(cs_ref[...], zz_ref[...], preferred_element_type=F32).astype(BF16)


def _fourier(z, nb, seq, tl):
    cs, bdc, bds = _dft_tables(seq)
    nt = seq // tl
    return pl.pallas_call(
        functools.partial(_fourier_kernel, seq),
        out_shape=jax.ShapeDtypeStruct(z.shape, BF16),
        grid=(nb, nt),
        in_specs=[pl.BlockSpec((seq, D_FNET), lambda b, i: (b, 0)),
                  pl.BlockSpec((tl, 2 * seq), lambda b, i: (i, 0)),
                  pl.BlockSpec((D_FNET, D_FNET), lambda b, i: (0, 0)),
                  pl.BlockSpec((D_FNET, D_FNET), lambda b, i: (0, 0))],
        out_specs=pl.BlockSpec((tl, D_FNET), lambda b, i: (b * nt + i, 0)),
        scratch_shapes=[pltpu.VMEM((2 * seq, D_FNET), BF16)],
        compiler_params=_params(("arbitrary", "arbitrary")),
        name="fourier",
    )(z, cs, bdc, bds)


def _first_argmax_mask(v, iota, size):
    m = jnp.max(v, axis=0, keepdims=True)
    first = jnp.min(jnp.where(v == m, iota, size), axis=0, keepdims=True)
    return iota == first


def _route(logits_t, bias_col):
    tm = logits_t.shape[1]
    neg = -jnp.inf
    s = jax.nn.sigmoid(logits_t)
    biased = s + bias_col
    io8 = lax.broadcasted_iota(jnp.int32, (EXPERTS_PER_GROUP, tm), 0)
    gs_rows = []
    for g in range(N_EXPERT_GROUPS):
        blk = biased[g * EXPERTS_PER_GROUP:(g + 1) * EXPERTS_PER_GROUP, :]
        m1 = jnp.max(blk, axis=0, keepdims=True)
        rest = jnp.where(_first_argmax_mask(blk, io8, EXPERTS_PER_GROUP), neg, blk)
        gs_rows.append(m1 + jnp.max(rest, axis=0, keepdims=True))
    gs = jnp.concatenate(gs_rows, axis=0)
    iog = lax.broadcasted_iota(jnp.int32, (N_EXPERT_GROUPS, tm), 0)
    gsel = jnp.zeros((N_EXPERT_GROUPS, tm), F32)
    for _ in range(TOPK_GROUPS):
        sel = _first_argmax_mask(gs, iog, N_EXPERT_GROUPS)
        gsel = jnp.where(sel, 1.0, gsel)
        gs = jnp.where(sel, neg, gs)
    emask = jnp.concatenate(
        [jnp.broadcast_to(gsel[g:g + 1, :], (EXPERTS_PER_GROUP, tm)) for g in range(N_EXPERT_GROUPS)], axis=0)
    v = jnp.where(emask > 0.0, biased, neg)
    ioe = lax.broadcasted_iota(jnp.int32, (N_EXPERTS, tm), 0)
    idx_rows, s_rows = [], []
    for _ in range(TOP_K):
        sel = _first_argmax_mask(v, ioe, N_EXPERTS)
        idx_rows.append(jnp.sum(jnp.where(sel, ioe, 0), axis=0, keepdims=True))
        s_rows.append(jnp.sum(jnp.where(sel, s, 0.0), axis=0, keepdims=True))
        v = jnp.where(sel, neg, v)
    denom = s_rows[0]
    for r in s_rows[1:]:
        denom = denom + r
    pad = ROUTE_ROWS - TOP_K
    idx = jnp.concatenate(idx_rows + [jnp.zeros((pad, tm), jnp.int32)], axis=0)
    w = jnp.concatenate([r / denom * ROUTED_SCALE for r in s_rows] + [jnp.zeros((pad, tm), F32)], axis=0)
    return idx, w


def _merge_kernel(has_pos, *refs):
    if has_pos:
        (ys_ref, yf_ref, gt_ref, x_ref, pos_ref, mod_ref, n2_ref, wglu_ref, wps_ref, wpf_ref, wout_ref,
         wrt_ref, rb_ref, ws1_ref, ws3_ref, ws2_ref, xs_ref, h2_ref, ridx_ref, rw_ref) = refs
        x = x_ref[...] + pos_ref[...]
    else:
        (ys_ref, yf_ref, gt_ref, x_ref, mod_ref, n2_ref, wglu_ref, wps_ref, wpf_ref, wout_ref,
         wrt_ref, rb_ref, ws1_ref, ws3_ref, ws2_ref, xs_ref, h2_ref, ridx_ref, rw_ref) = refs
        x = x_ref[...]
    m = mod_ref[0]
    g = jax.nn.gelu(ys_ref[...].astype(F32))
    a = g * jax.nn.sigmoid(jnp.dot(g.astype(BF16), wglu_ref[...], preferred_element_type=F32))
    pa = jnp.dot(a.astype(BF16), wps_ref[...], preferred_element_type=F32)
    pb = jnp.dot(yf_ref[...], wpf_ref[...], preferred_element_type=F32)
    gt = gt_ref[...].astype(F32)
    merged = gt[:, :D_MODEL] * pa + gt[:, D_MODEL:] * pb
    x1 = x + m[2:3, :] * jnp.dot(merged.astype(BF16), wout_ref[...], preferred_element_type=F32)
    h2 = _rms(x1, n2_ref[...]) * (1.0 + m[4:5, :]) + m[3:4, :]
    hb = h2.astype(BF16)
    for j in range(ROW_SUB):
        h2_ref[pl.ds(j, h2.shape[0], stride=ROW_SUB), :] = h2[:, j * LANES:(j + 1) * LANES]
    wr = wrt_ref[...]
    wr_hi = wr.astype(BF16)
    wr_lo = (wr - wr_hi.astype(F32)).astype(BF16)
    h_lo = (h2 - hb.astype(F32)).astype(BF16)
    dn = (((1,), (1,)), ((), ()))
    logits_t = (lax.dot_general(wr_hi, hb, dn, preferred_element_type=F32)
                + lax.dot_general(wr_hi, h_lo, dn, preferred_element_type=F32)
                + lax.dot_general(wr_lo, hb, dn, preferred_element_type=F32))
    ridx_ref[...], rw_ref[...] = _route(logits_t, rb_ref[...])
    s1 = jnp.dot(hb, ws1_ref[...], preferred_element_type=F32)
    s3 = jnp.dot(hb, ws3_ref[...], preferred_element_type=F32)
    shared = jnp.dot((s1 * jax.nn.sigmoid(s1) * s3).astype(BF16), ws2_ref[...], preferred_element_type=F32)
    xs_ref[...] = x1 + m[5:6, :] * shared


def _merge(ys, yf, gt, x, pos, mod, mod_row, n2, weights, tm):
    n = x.shape[0]
    has_pos = pos is not None
    row = lambda i: (i, 0)
    const = lambda a: pl.BlockSpec(a.shape, lambda i: (0,) * a.ndim)
    in_specs = [pl.BlockSpec((tm, D_S5), row), pl.BlockSpec((tm, D_FNET), row),
                pl.BlockSpec((tm, 2 * D_MODEL), row), pl.BlockSpec((tm, D_MODEL), row)]
    args = [ys, yf, gt, x]
    if has_pos:
        nper = pos.shape[0] // tm
        in_specs.append(pl.BlockSpec((tm, D_MODEL), lambda i: (i % nper, 0)))
        args.append(pos)
    in_specs += [pl.BlockSpec((1, N_MOD, D_MODEL), lambda i: (mod_row(i, tm), 0, 0)), const(n2)]
    args += [mod, n2]
    in_specs += [const(w) for w in weights]
    args += list(weights)
    return pl.pallas_call(
        functools.partial(_merge_kernel, has_pos),
        out_shape=(jax.ShapeDtypeStruct((n, D_MODEL), F32),
                   jax.ShapeDtypeStruct((n * ROW_SUB, LANES), F32),
                   jax.ShapeDtypeStruct((ROUTE_ROWS, n), jnp.int32),
                   jax.ShapeDtypeStruct((ROUTE_ROWS, n), F32)),
        grid=(n // tm,),
        in_specs=in_specs,
        out_specs=(pl.BlockSpec((tm, D_MODEL), row), pl.BlockSpec((tm * ROW_SUB, LANES), row),
                   pl.BlockSpec((ROUTE_ROWS, tm), lambda i: (0, i)),
                   pl.BlockSpec((ROUTE_ROWS, tm), lambda i: (0, i))),
        compiler_params=_params(("arbitrary",)),
        name="merge",
    )(*args)


MOE_SUB = 4096
MOE_TM = 128
MOE_TMAX = MOE_SUB * TOP_K // MOE_TM + N_EXPERTS
MOE_PAD = 2
MOE_TS = MOE_TMAX + 2 * MOE_PAD
MOE_DUMMY = 256
MOE_RMW = 16
ROW_TILE = MOE_TM * ROW_SUB


def _moe_plan(ridx, rw):
    n = ridx.shape[1]
    nsub = n // MOE_SUB
    npair = n * TOP_K
    t = jnp.arange(n, dtype=jnp.int32)
    key = ((t // MOE_SUB) * N_EXPERTS)[None] + ridx[:TOP_K]
    key = (key * MOE_SUB + (t % MOE_SUB)[None]).reshape(-1)
    skey, sw = lax.sort((key, rw[:TOP_K].reshape(-1)), num_keys=1)
    stok = jnp.concatenate([(skey % MOE_SUB) * ROW_SUB, jnp.zeros((MOE_TM,), jnp.int32)])
    sw_rows = jnp.concatenate([sw, jnp.zeros((MOE_TM,), F32)]).reshape((npair + MOE_TM) // LANES, 1, LANES)
    hits = ridx[:TOP_K].reshape(TOP_K, nsub, 1, MOE_SUB) == jnp.arange(N_EXPERTS, dtype=jnp.int32)[None, None, :, None]
    cnt = jnp.sum(hits.astype(jnp.int32), axis=(0, 3))
    poff = (jnp.cumsum(cnt.reshape(-1)) - cnt.reshape(-1)).reshape(nsub, N_EXPERTS)
    ntile = (cnt + MOE_TM - 1) // MOE_TM
    tcum = jnp.cumsum(ntile, axis=1)
    toff = tcum - ntile
    tstart = jnp.concatenate([toff, tcum[:, -1:]], axis=1).reshape(-1).astype(jnp.int32)
    j = jnp.arange(MOE_TS, dtype=jnp.int32) - MOE_PAD
    valid = (j[None] >= 0) & (j[None] < tcum[:, -1:])
    te = jnp.minimum(jnp.sum(j[None, :, None] >= tcum[:, None, :], axis=-1), N_EXPERTS - 1)
    pick = lambda a: jnp.take_along_axis(a, te, axis=1)
    first = (j[None] - pick(toff)) * MOE_TM
    p0 = jnp.where(valid, pick(poff) + first, 0).reshape(-1).astype(jnp.int32)
    nv = jnp.where(valid, jnp.minimum(pick(cnt) - first, MOE_TM), 0).reshape(-1).astype(jnp.int32)
    return tstart, p0, nv, stok, sw_rows


def _moe_kernel(ts_ref, p0_ref, nv_ref, tok_ref, sw_ref, src_ref, w1_ref, w3_ref, w2_ref, y_ref,
                xt_ref, xb_ref, act_ref, ot_ref, w1b_ref, w3b_ref, w2b_ref, slot_ref):
    sub = pl.program_id(0)
    e = pl.program_id(1)
    base = sub * MOE_TS + MOE_PAD
    first = ts_ref[sub * (N_EXPERTS + 1) + e]
    last = ts_ref[sub * (N_EXPERTS + 1) + e + 1]

    def gather(p0):
        for mi in range(MOE_TM):
            tok = pl.multiple_of(tok_ref[p0 + mi], ROW_SUB)
            xt_ref[mi * ROW_SUB:(mi + 1) * ROW_SUB, :] = src_ref[pl.ds(tok, ROW_SUB), :]
        for j in range(ROW_SUB):
            xb_ref[:, j * LANES:(j + 1) * LANES] = xt_ref[pl.ds(j, MOE_TM, stride=ROW_SUB), :].astype(BF16)

    def scatter(p0, nv, masked):
        for u in range(0, MOE_TM, MOE_RMW):
            new = []
            for i in range(MOE_RMW):
                tok = tok_ref[p0 + u + i]
                if masked:
                    tok = jnp.where(u + i < nv, tok, MOE_SUB * ROW_SUB)
                tok = pl.multiple_of(tok, ROW_SUB)
                new.append((tok, y_ref[pl.ds(tok, ROW_SUB), :]
                            + ot_ref[(u + i) * ROW_SUB:(u + i + 1) * ROW_SUB, :]))
            for tok, v in new:
                y_ref[pl.ds(tok, ROW_SUB), :] = v

    @pl.when(e == 0)
    def _():
        y_ref[...] = jnp.zeros_like(y_ref)
        ot_ref[...] = jnp.zeros_like(ot_ref)
        act_ref[...] = jnp.zeros_like(act_ref)
        w2b_ref[...] = jnp.zeros_like(w2b_ref)
        slot_ref[0] = 0
        gather(p0_ref[base])

    @pl.when(last > first)
    def _():
        slot_ref[0] = 1 - slot_ref[0]
        w1b_ref[...] = w1_ref[0].astype(BF16)
        w3b_ref[...] = w3_ref[0].astype(BF16)
        w2b_ref[slot_ref[0]] = w2_ref[0].astype(BF16)

    slot = slot_ref[0]

    def down_proj(w2_slot):
        o = jnp.dot(act_ref[...], w2b_ref[w2_slot], preferred_element_type=F32)
        for j in range(ROW_SUB):
            ot_ref[pl.ds(j, MOE_TM, stride=ROW_SUB), :] = o[:, j * LANES:(j + 1) * LANES]

    def step(i, masked):
        cur = base + i
        scatter(p0_ref[cur - 2], nv_ref[cur - 2], masked)
        down_proj(jnp.where(i > first, slot, 1 - slot))
        p0 = p0_ref[cur]
        nv = nv_ref[cur]
        x = xb_ref[...]
        a = jnp.dot(x, w1b_ref[...], preferred_element_type=F32)
        b = jnp.dot(x, w3b_ref[...], preferred_element_type=F32)
        r0 = p0 // LANES
        c = p0 % LANES
        lane = lax.broadcasted_iota(jnp.int32, (1, LANES), 1)
        rows = lax.broadcasted_iota(jnp.int32, (LANES, LANES), 0)
        cols = lax.broadcasted_iota(jnp.int32, (LANES, LANES), 1)
        gparts = []
        for hh in range(MOE_TM // LANES):
            ga = pltpu.roll(sw_ref[r0 + hh], LANES - c, axis=1)
            gb = pltpu.roll(sw_ref[r0 + hh + 1], LANES - c, axis=1)
            g = jnp.where(lane + hh * LANES < nv, jnp.where(lane < LANES - c, ga, gb), 0.0)
            gparts.append(jnp.sum(jnp.where(rows == cols, jnp.broadcast_to(g, (LANES, LANES)), 0.0),
                                  axis=1, keepdims=True))
        gcol = jnp.concatenate(gparts, axis=0)
        act_ref[...] = (a * jax.nn.sigmoid(a) * b * gcol).astype(BF16)
        gather(p0_ref[cur + 1])

    def body(i, carry):
        full = nv_ref[base + i - 2] == MOE_TM

        @pl.when(full)
        def _():
            step(i, False)

        @pl.when(jnp.logical_not(full))
        def _():
            step(i, True)

        return carry

    lax.fori_loop(first, last, body, 0)

    @pl.when(e == N_EXPERTS - 1)
    def _():
        scatter(p0_ref[base + last - 2], nv_ref[base + last - 2], True)
        down_proj(slot)
        scatter(p0_ref[base + last - 1], nv_ref[base + last - 1], True)


def _moe(h2_rows, plan, w1, w3, w2):
    tstart, p0, nv, stok, sw_rows = plan
    nsub = h2_rows.shape[0] // (MOE_SUB * ROW_SUB)
    wmap = lambda s, e, ts, p0, nv: (e, 0, 0)
    sub2 = lambda s, e, ts, p0, nv: (s, 0)
    grid_spec = pltpu.PrefetchScalarGridSpec(
        num_scalar_prefetch=3,
        grid=(nsub, N_EXPERTS),
        in_specs=[pl.BlockSpec(memory_space=pltpu.SMEM),
                  pl.BlockSpec(sw_rows.shape, lambda s, e, ts, p0, nv: (0, 0, 0)),
                  pl.BlockSpec((MOE_SUB * ROW_SUB, LANES), sub2, pipeline_mode=pl.Buffered(1)),
                  pl.BlockSpec((1, D_MODEL, D_EXPERT), wmap),
                  pl.BlockSpec((1, D_MODEL, D_EXPERT), wmap),
                  pl.BlockSpec((1, D_EXPERT, D_MODEL), wmap)],
        out_specs=pl.BlockSpec(((MOE_SUB + MOE_DUMMY) * ROW_SUB, LANES), sub2, pipeline_mode=pl.Buffered(1)),
        scratch_shapes=[pltpu.VMEM((ROW_TILE, LANES), F32), pltpu.VMEM((MOE_TM, D_MODEL), BF16),
                        pltpu.VMEM((MOE_TM, D_EXPERT), BF16), pltpu.VMEM((ROW_TILE, LANES), F32),
                        pltpu.VMEM((D_MODEL, D_EXPERT), BF16), pltpu.VMEM((D_MODEL, D_EXPERT), BF16),
                        pltpu.VMEM((2, D_EXPERT, D_MODEL), BF16), pltpu.SMEM((1,), jnp.int32)])
    return pl.pallas_call(
        _moe_kernel,
        grid_spec=grid_spec,
        out_shape=jax.ShapeDtypeStruct((nsub * (MOE_SUB + MOE_DUMMY) * ROW_SUB, LANES), F32),
        compiler_params=_params(("arbitrary", "arbitrary")),
        name="moe",
    )(tstart, p0, nv, stok, sw_rows, h2_rows, w1, w3, w2)


def _final_kernel(xs_ref, y_ref, mod_ref, fg_ref, o_ref):
    tm = xs_ref.shape[0]
    y = jnp.concatenate([y_ref[pl.ds(j, tm, stride=ROW_SUB), :] for j in range(ROW_SUB)], axis=1)
    x2 = xs_ref[...] + mod_ref[0][5:6, :] * y
    o_ref[...] = _rms(x2, fg_ref[...])


def _final(xs, y_rows, mod, mod_row, fg, tm):
    n = xs.shape[0]
    per_sub = MOE_SUB // tm
    stride = (MOE_SUB + MOE_DUMMY) // tm
    row = lambda i: (i, 0)
    return pl.pallas_call(
        _final_kernel,
        out_shape=jax.ShapeDtypeStruct((n, D_MODEL), F32),
        grid=(n // tm,),
        in_specs=[pl.BlockSpec((tm, D_MODEL), row),
                  pl.BlockSpec((tm * ROW_SUB, LANES), lambda i: ((i // per_sub) * stride + i % per_sub, 0)),
                  pl.BlockSpec((1, N_MOD, D_MODEL), lambda i: (mod_row(i, tm), 0, 0)),
                  pl.BlockSpec((1, D_MODEL), lambda i: (0, 0))],
        out_specs=pl.BlockSpec((tm, D_MODEL), row),
        compiler_params=_params(("arbitrary",)),
        name="final",
    )(xs, y_rows, mod, fg)


def _grid_pos_embed(n_tokens):
    rows = n_tokens // GRID_W
    r, col = jnp.meshgrid(jnp.arange(rows, dtype=F32), jnp.arange(GRID_W, dtype=F32), indexing="ij")
    quarter = D_MODEL // 4
    omega = 1.0 / (10000.0 ** (jnp.arange(quarter, dtype=F32) / quarter))

    def emb(p):
        a = p.reshape(-1)[:, None] * omega
        return jnp.concatenate([jnp.sin(a), jnp.cos(a)], axis=-1)

    return jnp.concatenate([emb(r), emb(col)], axis=-1)


def _stream(x3, pos, mod, first_row, h0, s5_ops, p):
    nb, seq, _ = x3.shape
    n = nb * seq
    nk = seq // S5_CHUNK
    x = x3.reshape(n, D_MODEL)
    per_seq_mod = first_row > 0

    def mod_row(i, tm):
        return first_row + (i * tm) // seq if per_seq_mod else 0

    us, uf, gt = _inproj(x, pos, mod, mod_row, p["norm1_g"], p["w_in"], 512)
    ys, fin = _s5(us, s5_ops, h0, nb, nk)
    yf = _fourier(uf, nb, seq, min(seq, 512))
    xs, h2_rows, ridx, rw = _merge(ys, yf, gt, x, pos, mod, mod_row, p["norm2_g"], p["merge_w"], 256)
    y_rows = _moe(h2_rows, _moe_plan(ridx, rw), p["w1"], p["w3"], p["w2"])
    out = _final(xs, y_rows, mod, mod_row, p["final_g"], 256)
    return out.reshape(nb, seq, D_MODEL), fin


def kernel(x_prompt, x_sample, state_s5_re, state_s5_im, c, c_ctx, w_ada, b_ada, norm1_g, norm2_g, w_in,
           lam_re, lam_im, log_dt, b_re, b_im, c_re, c_im, d_skip, w_glu, w_proj_s5, w_proj_f, w_out,
           w_router, router_bias, w1, w3, w2, ws1, ws3, ws2, final_norm_g):
    nb_ctx = x_prompt.shape[0]
    nb_lat, seq_lat, _ = x_sample.shape
    half = 2 * S5_STATE

    cond = jnp.concatenate([c_ctx[None], c, jnp.zeros((MOD_ROWS - 1 - nb_lat, D_MODEL), F32)], axis=0)
    mod = _adaln(cond, w_ada[0], b_ada[0]).reshape(MOD_ROWS, N_MOD, D_MODEL)

    s5_ops = _s5ops(lam_re[0], lam_im[0], log_dt[0], b_re[0], b_im[0], c_re[0], c_im[0], d_skip[0])
    p = dict(
        norm1_g=norm1_g[0][None], norm2_g=norm2_g[0][None], final_g=final_norm_g[None],
        w_in=w_in[0].astype(BF16), w1=w1[0], w3=w3[0], w2=w2[0],
        merge_w=(w_glu[0].astype(BF16), w_proj_s5[0].astype(BF16), w_proj_f[0].astype(BF16),
                 w_out[0].astype(BF16), w_router[0].T, router_bias[0][:, None],
                 ws1[0].astype(BF16), ws3[0].astype(BF16), ws2[0].astype(BF16)))

    def pack_state(sr, si):
        f = lambda a: a.astype(F32).transpose(2, 0, 1, 3).reshape(S5_GROUPS, a.shape[0], half)
        return jnp.concatenate([f(sr), f(si)], axis=-1)

    def unpack_state(fin, lo):
        nb = fin.shape[1]
        return fin[..., lo:lo + half].reshape(S5_GROUPS, nb, 2, S5_STATE).transpose(1, 2, 0, 3)[:, None]

    h0_ctx = jnp.zeros((S5_GROUPS, nb_ctx, 2 * half), F32)
    y_prompt, fin = _stream(x_prompt, None, mod, 0, h0_ctx, s5_ops, p)
    h0_lat = pack_state(state_s5_re[:, 0], state_s5_im[:, 0])
    y_sample, _ = _stream(x_sample, _grid_pos_embed(seq_lat), mod, 1, h0_lat, s5_ops, p)
    return (y_prompt, y_sample, unpack_state(fin, 0).astype(x_prompt.dtype),
            unpack_state(fin, half).astype(x_prompt.dtype))
```

```python
import functools
import math

import jax
import jax.numpy as jnp
from jax import lax
from jax.experimental import pallas as pl
from jax.experimental.pallas import tpu as pltpu

D_MODEL = 1024
GRID_W = 64
D_S5 = 768
S5_GROUP = 16
S5_GROUPS = 48
S5_STATE = 64
D_FNET = 256
FNET_GROUP = 64
N_EXPERTS = 64
TOP_K = 6
N_EXPERT_GROUPS = 8
EXPERTS_PER_GROUP = N_EXPERTS // N_EXPERT_GROUPS
TOPK_GROUPS = 4
D_EXPERT = 256
ROUTED_SCALE = 2.5
N_MOD = 6
EPS = 1e-6

S5_CHUNK = 16
S5_ROW = S5_CHUNK * S5_GROUP
MOD_ROWS = 8
ROUTE_ROWS = 8
LANES = 128
ROW_SUB = D_MODEL // LANES
VMEM_LIMIT = 56 * 1024 * 1024

BF16 = jnp.bfloat16
F32 = jnp.float32


def _params(sem, vmem=VMEM_LIMIT):
    return pltpu.CompilerParams(dimension_semantics=sem, vmem_limit_bytes=vmem)


def _rms(x, g):
    return x * lax.rsqrt(jnp.mean(x * x, axis=-1, keepdims=True) + EPS) * g


def _adaln_kernel(c_ref, w_ref, b_ref, o_ref):
    c = c_ref[...]
    o_ref[...] = jnp.dot(c * jax.nn.sigmoid(c), w_ref[...], precision=lax.Precision.HIGHEST,
                         preferred_element_type=F32) + b_ref[...]


def _adaln(cond, w_ada, b_ada):
    n_out = N_MOD * D_MODEL
    return pl.pallas_call(
        _adaln_kernel,
        out_shape=jax.ShapeDtypeStruct((MOD_ROWS, n_out), F32),
        grid=(N_MOD,),
        in_specs=[pl.BlockSpec((MOD_ROWS, D_MODEL), lambda i: (0, 0)),
                  pl.BlockSpec((D_MODEL, D_MODEL), lambda i: (0, i)),
                  pl.BlockSpec((1, D_MODEL), lambda i: (0, i))],
        out_specs=pl.BlockSpec((MOD_ROWS, D_MODEL), lambda i: (0, i)),
        compiler_params=_params(("arbitrary",)),
        name="adaln",
    )(cond, w_ada, b_ada.reshape(1, n_out))


def _inproj_kernel(has_pos, *refs):
    if has_pos:
        x_ref, pos_ref, mod_ref, g_ref, w_ref, us_ref, uf_ref, gt_ref = refs
        x = x_ref[...] + pos_ref[...]
    else:
        x_ref, mod_ref, g_ref, w_ref, us_ref, uf_ref, gt_ref = refs
        x = x_ref[...]
    m = mod_ref[0]
    h = _rms(x, g_ref[...]) * (1.0 + m[1:2, :]) + m[0:1, :]
    p = jnp.dot(h.astype(BF16), w_ref[...], preferred_element_type=F32)
    us_ref[...] = p[:, :D_S5]
    uf_ref[...] = p[:, D_S5:D_MODEL].astype(BF16)
    gt_ref[...] = jax.nn.sigmoid(p[:, D_MODEL:]).astype(BF16)


def _inproj(x, pos, mod, mod_row, norm_g, w_in_bf, tm):
    n = x.shape[0]
    has_pos = pos is not None
    row = lambda i: (i, 0)
    in_specs = [pl.BlockSpec((tm, D_MODEL), row)]
    args = [x]
    if has_pos:
        nper = pos.shape[0] // tm
        in_specs.append(pl.BlockSpec((tm, D_MODEL), lambda i: (i % nper, 0)))
        args.append(pos)
    in_specs += [pl.BlockSpec((1, N_MOD, D_MODEL), lambda i: (mod_row(i, tm), 0, 0)),
                 pl.BlockSpec((1, D_MODEL), lambda i: (0, 0)),
                 pl.BlockSpec(w_in_bf.shape, lambda i: (0, 0))]
    args += [mod, norm_g, w_in_bf]
    return pl.pallas_call(
        functools.partial(_inproj_kernel, has_pos),
        out_shape=(jax.ShapeDtypeStruct((n, D_S5), F32),
                   jax.ShapeDtypeStruct((n, D_FNET), BF16),
                   jax.ShapeDtypeStruct((n, 2 * D_MODEL), BF16)),
        grid=(n // tm,),
        in_specs=in_specs,
        out_specs=(pl.BlockSpec((tm, D_S5), row), pl.BlockSpec((tm, D_FNET), row),
                   pl.BlockSpec((tm, 2 * D_MODEL), row)),
        compiler_params=_params(("arbitrary",)),
        name="inproj",
    )(*args)


def _shift_lanes(x, k):
    if k == 0:
        return x
    z = jnp.zeros((x.shape[0], abs(k)), x.dtype)
    if k > 0:
        return jnp.concatenate([z, x[:, :x.shape[1] - k]], axis=1)
    return jnp.concatenate([x[:, -k:], z], axis=1)


def _s5ops_kernel(lam_re_ref, lam_im_ref, dt_ref, btr_ref, bti_ref, cr_ref, ci_ref, d_ref,
                  m_ref, wi_ref, wot_ref, coef_ref):
    hi = lax.Precision.HIGHEST
    lr = jnp.minimum(lam_re_ref[0], -1e-4)
    li = lam_im_ref[0]
    dt = jnp.exp(dt_ref[0])
    mag = jnp.exp(lr * dt)
    ar = mag * jnp.cos(li * dt)
    ai = mag * jnp.sin(li * dt)
    den = lr * lr + li * li
    nr = ar - 1.0
    qr = (nr * lr + ai * li) / den
    qi = (ai * lr - nr * li) / den
    pr, pi = [], []
    for n in range(S5_CHUNK + 1):
        pm = jnp.exp(float(n) * (lr * dt))
        pr.append(pm * jnp.cos(float(n) * (li * dt)))
        pi.append(pm * jnp.sin(float(n) * (li * dt)))
    bbr, bbi, car, cai = [], [], [], []
    for d in range(2):
        btr = btr_ref[0, d]
        bti = bti_ref[0, d]
        bbr.append(qr[d:d + 1] * btr - qi[d:d + 1] * bti)
        bbi.append(qr[d:d + 1] * bti + qi[d:d + 1] * btr)
        cr = cr_ref[0, d]
        ci = ci_ref[0, d]
        car.append([cr * pr[n][d:d + 1] - ci * pi[n][d:d + 1] for n in range(S5_CHUNK + 1)])
        cai.append([cr * pi[n][d:d + 1] + ci * pr[n][d:d + 1] for n in range(S5_CHUNK + 1)])

    def lag_kernels(d, order):
        a = jnp.concatenate([car[d][n] for n in order], axis=0)
        b = jnp.concatenate([cai[d][n] for n in order], axis=0)
        dn = (((1,), (1,)), ((), ()))
        return (lax.dot_general(bbr[d], a, dn, precision=hi, preferred_element_type=F32)
                - lax.dot_general(bbi[d], b, dn, precision=hi, preferred_element_type=F32))

    ktf = lag_kernels(0, range(S5_CHUNK))
    ktb = lag_kernels(1, range(S5_CHUNK - 1, -1, -1))
    row = lax.broadcasted_iota(jnp.int32, (S5_GROUP, S5_ROW), 0)
    lane = lax.broadcasted_iota(jnp.int32, (S5_GROUP, S5_ROW), 1)
    dcol = d_ref[0]
    for j in range(S5_CHUNK):
        rows = slice(j * S5_GROUP, (j + 1) * S5_GROUP)
        blk = _shift_lanes(ktf, S5_GROUP * j) + _shift_lanes(ktb, -S5_GROUP * (S5_CHUNK - 1 - j))
        blk = blk + jnp.where(lane == S5_GROUP * j + row, dcol, 0.0)
        m_ref[0, rows, :] = blk.astype(BF16)
        nf = S5_CHUNK - 1 - j
        wi = jnp.concatenate([pr[nf][0:1] * bbr[0] - pi[nf][0:1] * bbi[0],
                              pr[j][1:2] * bbr[1] - pi[j][1:2] * bbi[1],
                              pr[nf][0:1] * bbi[0] + pi[nf][0:1] * bbr[0],
                              pr[j][1:2] * bbi[1] + pi[j][1:2] * bbr[1]], axis=1)
        wi_ref[0, rows, :] = wi.astype(BF16)
        wot = jnp.concatenate([car[0][j + 1], car[1][S5_CHUNK - j],
                               -cai[0][j + 1], -cai[1][S5_CHUNK - j]], axis=1)
        wot_ref[0, rows, :] = wot.astype(BF16)
    coef_ref[0, 0:1, :] = jnp.concatenate([pr[S5_CHUNK][0:1], pr[S5_CHUNK][1:2]], axis=1)
    coef_ref[0, 1:2, :] = jnp.concatenate([pi[S5_CHUNK][0:1], pi[S5_CHUNK][1:2]], axis=1)


def _s5ops(lam_re, lam_im, log_dt, b_re, b_im, c_re, c_im, d_skip):
    g3 = lambda g: (g, 0, 0)
    g4 = lambda g: (g, 0, 0, 0)
    sw = lambda a: jnp.swapaxes(a.astype(F32), 0, 1)
    dt = jnp.broadcast_to(sw(log_dt)[..., None], (S5_GROUPS, 2, S5_STATE))
    args = (sw(lam_re), sw(lam_im), dt, sw(jnp.swapaxes(b_re, 2, 3)), sw(jnp.swapaxes(b_im, 2, 3)),
            sw(c_re), sw(c_im), d_skip.astype(F32).reshape(S5_GROUPS, S5_GROUP, 1))
    vec = pl.BlockSpec((1, 2, S5_STATE), g3)
    mat = pl.BlockSpec((1, 2, S5_GROUP, S5_STATE), g4)
    op = pl.BlockSpec((1, S5_ROW, S5_ROW), g3)
    return pl.pallas_call(
        _s5ops_kernel,
        out_shape=(jax.ShapeDtypeStruct((S5_GROUPS, S5_ROW, S5_ROW), BF16),) * 3
        + (jax.ShapeDtypeStruct((S5_GROUPS, 2, 2 * S5_STATE), F32),),
        grid=(S5_GROUPS,),
        in_specs=[vec, vec, vec, mat, mat, mat, mat, pl.BlockSpec((1, S5_GROUP, 1), g3)],
        out_specs=(op, op, op, pl.BlockSpec((1, 2, 2 * S5_STATE), g3)),
        compiler_params=_params(("arbitrary",)),
        name="s5ops",
    )(*args)


S5_BLOCK_GROUPS = LANES // S5_GROUP


def _s5_perm():
    a = jnp.arange(S5_CHUNK * LANES, dtype=jnp.int32)
    dst = ((a % LANES) // S5_GROUP) * S5_ROW + (a // LANES) * S5_GROUP + a % S5_GROUP
    return (dst[:, None] == a[None, :]).astype(BF16)


def _s5_kernel(nb, nk, u_ref, perm_ref, m_ref, wi_ref, wot_ref, coef_ref, h0_ref, y_ref, fin_ref,
               sr_ref, si_ref, efr_ref, ebr_ref, efi_ref, ebi_ref, uall_ref, yall_ref):
    gl = pl.program_id(1)
    rows = nb * nk
    half = 2 * S5_STATE

    @pl.when(gl == 0)
    def _():
        xcat = jnp.concatenate([u_ref[pl.ds(t, rows, stride=S5_CHUNK), :].astype(BF16)
                                for t in range(S5_CHUNK)], axis=1)
        uall = jnp.dot(xcat, perm_ref[...], preferred_element_type=F32).astype(BF16)
        for g in range(S5_BLOCK_GROUPS):
            uall_ref[g] = uall[:, g * S5_ROW:(g + 1) * S5_ROW]

    u = uall_ref[gl]
    s = jnp.dot(u, wi_ref[0], preferred_element_type=F32)
    sr_ref[...] = s[:, :half]
    si_ref[...] = s[:, half:]
    c_r = coef_ref[0, 0:1, :]
    c_i = coef_ref[0, 1:2, :]
    e_r = h0_ref[0, :, :half]
    e_i = h0_ref[0, :, half:]
    is_fwd = lax.broadcasted_iota(jnp.int32, (nb, half), 1) < S5_STATE
    for j in range(nk):
        rf = pl.ds(j, nb, stride=nk)
        rb = pl.ds(nk - 1 - j, nb, stride=nk)
        efr_ref[rf, :] = e_r
        ebr_ref[rb, :] = e_r
        efi_ref[rf, :] = e_i
        ebi_ref[rb, :] = e_i
        s_r = jnp.where(is_fwd, sr_ref[rf, :], sr_ref[rb, :])
        s_i = jnp.where(is_fwd, si_ref[rf, :], si_ref[rb, :])
        e_r, e_i = c_r * e_r - c_i * e_i + s_r, c_r * e_i + c_i * e_r + s_i
    fin_ref[0, :, :half] = e_r
    fin_ref[0, :, half:] = e_i
    y = jnp.dot(u, m_ref[0], preferred_element_type=F32)
    fwd_rows = lax.broadcasted_iota(jnp.int32, (rows, half), 1) < S5_STATE
    e = jnp.concatenate([jnp.where(fwd_rows, efr_ref[...], ebr_ref[...]),
                         jnp.where(fwd_rows, efi_ref[...], ebi_ref[...])], axis=1).astype(BF16)
    y = y + lax.dot_general(e, wot_ref[0], (((1,), (1,)), ((), ())), preferred_element_type=F32)
    yall_ref[gl] = y.astype(BF16)

    @pl.when(gl == S5_BLOCK_GROUPS - 1)
    def _():
        ycat = jnp.concatenate([yall_ref[g] for g in range(S5_BLOCK_GROUPS)], axis=1)
        out = lax.dot_general(ycat, perm_ref[...], (((1,), (1,)), ((), ())), preferred_element_type=F32)
        for t in range(S5_CHUNK):
            y_ref[pl.ds(t, rows, stride=S5_CHUNK), :] = out[:, t * LANES:(t + 1) * LANES]


def _s5(u, ops, h0, nb, nk):
    m, w_in, w_out, coef = ops
    n = u.shape[0]
    rows = nb * nk
    perm = _s5_perm()
    g3 = lambda b, g: (b * S5_BLOCK_GROUPS + g, 0, 0)
    blk = lambda b, g: (0, b)
    return pl.pallas_call(
        functools.partial(_s5_kernel, nb, nk),
        out_shape=(jax.ShapeDtypeStruct((n, D_S5), F32),
                   jax.ShapeDtypeStruct((S5_GROUPS, nb, 4 * S5_STATE), F32)),
        grid=(S5_GROUPS // S5_BLOCK_GROUPS, S5_BLOCK_GROUPS),
        in_specs=[pl.BlockSpec((n, LANES), blk),
                  pl.BlockSpec(perm.shape, lambda b, g: (0, 0)),
                  pl.BlockSpec((1, S5_ROW, S5_ROW), g3),
                  pl.BlockSpec((1, S5_ROW, 4 * S5_STATE), g3),
                  pl.BlockSpec((1, 4 * S5_STATE, S5_ROW), g3),
                  pl.BlockSpec((1, 2, 2 * S5_STATE), g3),
                  pl.BlockSpec((1, nb, 4 * S5_STATE), g3)],
        out_specs=(pl.BlockSpec((n, LANES), blk),
                   pl.BlockSpec((1, nb, 4 * S5_STATE), g3)),
        scratch_shapes=[pltpu.VMEM((rows, 2 * S5_STATE), F32)] * 6 + [
            pltpu.VMEM((S5_BLOCK_GROUPS, rows, S5_ROW), BF16),
            pltpu.VMEM((S5_BLOCK_GROUPS, rows, S5_ROW), BF16)],
        compiler_params=_params(("arbitrary", "arbitrary")),
        name="s5",
    )(u, perm, m, w_in, w_out, coef, h0)


def _dft_tables(seq):
    k = jnp.arange(seq, dtype=jnp.int32)
    na = seq // FNET_GROUP
    ang_a = (2.0 * math.pi / na) * ((jnp.arange(na, dtype=jnp.int32)[:, None] * k[None, :]) % na).astype(F32)
    ang_b = (2.0 * math.pi / seq) * ((jnp.arange(FNET_GROUP, dtype=jnp.int32)[:, None] * k[None, :]) % seq).astype(F32)
    ca, sa = jnp.cos(ang_a)[:, None, :], jnp.sin(ang_a)[:, None, :]
    cb, sb = jnp.cos(ang_b)[None, :, :], jnp.sin(ang_b)[None, :, :]
    cos_jk = (ca * cb - sa * sb).reshape(seq, seq)
    sin_jk = (sa * cb + ca * sb).reshape(seq, seq)
    cs = jnp.concatenate([cos_jk, -sin_jk], axis=1).astype(BF16)
    c = jnp.arange(D_FNET, dtype=jnp.int32)
    same = (c[:, None] // FNET_GROUP) == (c[None, :] // FNET_GROUP)
    angc = (2.0 * math.pi / FNET_GROUP) * (((c[:, None] % FNET_GROUP) * (c[None, :] % FNET_GROUP))
                                           % FNET_GROUP).astype(F32)
    scale = 1.0 / math.sqrt(seq * FNET_GROUP)
    bdc = jnp.where(same, jnp.cos(angc) * scale, 0.0).astype(BF16)
    bds = jnp.where(same, jnp.sin(angc) * scale, 0.0).astype(BF16)
    return cs, bdc, bds


def _fourier_kernel(seq, z_ref, cs_ref, bdc_ref, bds_ref, o_ref, zz_ref):
    @pl.when(pl.program_id(1) == 0)
    def _():
        z = z_ref[...]
        zz_ref[0:seq, :] = jnp.dot(z, bdc_ref[...], preferred_element_type=F32).astype(BF16)
        zz_ref[seq:, :] = jnp.dot(z, bds_ref[...], preferred_element_type=F32).astype(BF16)

    o_ref[...] = jnp.dot(cs_ref[...], zz_ref[...], preferred_element_type=F32).astype(BF16)


def _fourier(z, nb, seq, tl):
    cs, bdc, bds = _dft_tables(seq)
    nt = seq // tl
    return pl.pallas_call(
        functools.partial(_fourier_kernel, seq),
        out_shape=jax.ShapeDtypeStruct(z.shape, BF16),
        grid=(nb, nt),
        in_specs=[pl.BlockSpec((seq, D_FNET), lambda b, i: (b, 0)),
                  pl.BlockSpec((tl, 2 * seq), lambda b, i: (i, 0)),
                  pl.BlockSpec((D_FNET, D_FNET), lambda b, i: (0, 0)),
                  pl.BlockSpec((D_FNET, D_FNET), lambda b, i: (0, 0))],
        out_specs=pl.BlockSpec((tl, D_FNET), lambda b, i: (b * nt + i, 0)),
        scratch_shapes=[pltpu.VMEM((2 * seq, D_FNET), BF16)],
        compiler_params=_params(("arbitrary", "arbitrary")),
        name="fourier",
    )(z, cs, bdc, bds)


def _first_argmax_mask(v, iota, size):
    m = jnp.max(v, axis=0, keepdims=True)
    first = jnp.min(jnp.where(v == m, iota, size), axis=0, keepdims=True)
    return iota == first


def _route(logits_t, bias_col):
    tm = logits_t.shape[1]
    neg = -jnp.inf
    s = jax.nn.sigmoid(logits_t)
    biased = s + bias_col
    io8 = lax.broadcasted_iota(jnp.int32, (EXPERTS_PER_GROUP, tm), 0)
    gs_rows = []
    for g in range(N_EXPERT_GROUPS):
        blk = biased[g * EXPERTS_PER_GROUP:(g + 1) * EXPERTS_PER_GROUP, :]
        m1 = jnp.max(blk, axis=0, keepdims=True)
        rest = jnp.where(_first_argmax_mask(blk, io8, EXPERTS_PER_GROUP), neg, blk)
        gs_rows.append(m1 + jnp.max(rest, axis=0, keepdims=True))
    gs = jnp.concatenate(gs_rows, axis=0)
    iog = lax.broadcasted_iota(jnp.int32, (N_EXPERT_GROUPS, tm), 0)
    gsel = jnp.zeros((N_EXPERT_GROUPS, tm), F32)
    for _ in range(TOPK_GROUPS):
        sel = _first_argmax_mask(gs, iog, N_EXPERT_GROUPS)
        gsel = jnp.where(sel, 1.0, gsel)
        gs = jnp.where(sel, neg, gs)
    emask = jnp.concatenate(
        [jnp.broadcast_to(gsel[g:g + 1, :], (EXPERTS_PER_GROUP, tm)) for g in range(N_EXPERT_GROUPS)], axis=0)
    v = jnp.where(emask > 0.0, biased, neg)
    ioe = lax.broadcasted_iota(jnp.int32, (N_EXPERTS, tm), 0)
    idx_rows, s_rows = [], []
    for _ in range(TOP_K):
        sel = _first_argmax_mask(v, ioe, N_EXPERTS)
        idx_rows.append(jnp.sum(jnp.where(sel, ioe, 0), axis=0, keepdims=True))
        s_rows.append(jnp.sum(jnp.where(sel, s, 0.0), axis=0, keepdims=True))
        v = jnp.where(sel, neg, v)
    denom = s_rows[0]
    for r in s_rows[1:]:
        denom = denom + r
    pad = ROUTE_ROWS - TOP_K
    idx = jnp.concatenate(idx_rows + [jnp.zeros((pad, tm), jnp.int32)], axis=0)
    w = jnp.concatenate([r / denom * ROUTED_SCALE for r in s_rows] + [jnp.zeros((pad, tm), F32)], axis=0)
    return idx, w


def _merge_kernel(has_pos, *refs):
    if has_pos:
        (ys_ref, yf_ref, gt_ref, x_ref, pos_ref, mod_ref, n2_ref, wglu_ref, wps_ref, wpf_ref, wout_ref,
         wrt_ref, rb_ref, ws1_ref, ws3_ref, ws2_ref, xs_ref, h2_ref, ridx_ref, rw_ref) = refs
        x = x_ref[...] + pos_ref[...]
    else:
        (ys_ref, yf_ref, gt_ref, x_ref, mod_ref, n2_ref, wglu_ref, wps_ref, wpf_ref, wout_ref,
         wrt_ref, rb_ref, ws1_ref, ws3_ref, ws2_ref, xs_ref, h2_ref, ridx_ref, rw_ref) = refs
        x = x_ref[...]
    m = mod_ref[0]
    g = jax.nn.gelu(ys_ref[...].astype(F32))
    a = g * jax.nn.sigmoid(jnp.dot(g.astype(BF16), wglu_ref[...], preferred_element_type=F32))
    pa = jnp.dot(a.astype(BF16), wps_ref[...], preferred_element_type=F32)
    pb = jnp.dot(yf_ref[...], wpf_ref[...], preferred_element_type=F32)
    gt = gt_ref[...].astype(F32)
    merged = gt[:, :D_MODEL] * pa + gt[:, D_MODEL:] * pb
    x1 = x + m[2:3, :] * jnp.dot(merged.astype(BF16), wout_ref[...], preferred_element_type=F32)
    h2 = _rms(x1, n2_ref[...]) * (1.0 + m[4:5, :]) + m[3:4, :]
    hb = h2.astype(BF16)
    for j in range(ROW_SUB):
        h2_ref[pl.ds(j, h2.shape[0], stride=ROW_SUB), :] = h2[:, j * LANES:(j + 1) * LANES]
    wr = wrt_ref[...]
    wr_hi = wr.astype(BF16)
    wr_lo = (wr - wr_hi.astype(F32)).astype(BF16)
    h_lo = (h2 - hb.astype(F32)).astype(BF16)
    dn = (((1,), (1,)), ((), ()))
    logits_t = (lax.dot_general(wr_hi, hb, dn, preferred_element_type=F32)
                + lax.dot_general(wr_hi, h_lo, dn, preferred_element_type=F32)
                + lax.dot_general(wr_lo, hb, dn, preferred_element_type=F32))
    ridx_ref[...], rw_ref[...] = _route(logits_t, rb_ref[...])
    s1 = jnp.dot(hb, ws1_ref[...], preferred_element_type=F32)
    s3 = jnp.dot(hb, ws3_ref[...], preferred_element_type=F32)
    shared = jnp.dot((s1 * jax.nn.sigmoid(s1) * s3).astype(BF16), ws2_ref[...], preferred_element_type=F32)
    xs_ref[...] = x1 + m[5:6, :] * shared


def _merge(ys, yf, gt, x, pos, mod, mod_row, n2, weights, tm):
    n = x.shape[0]
    has_pos = pos is not None
    row = lambda i: (i, 0)
    const = lambda a: pl.BlockSpec(a.shape, lambda i: (0,) * a.ndim)
    in_specs = [pl.BlockSpec((tm, D_S5), row), pl.BlockSpec((tm, D_FNET), row),
                pl.BlockSpec((tm, 2 * D_MODEL), row), pl.BlockSpec((tm, D_MODEL), row)]
    args = [ys, yf, gt, x]
    if has_pos:
        nper = pos.shape[0] // tm
        in_specs.append(pl.BlockSpec((tm, D_MODEL), lambda i: (i % nper, 0)))
        args.append(pos)
    in_specs += [pl.BlockSpec((1, N_MOD, D_MODEL), lambda i: (mod_row(i, tm), 0, 0)), const(n2)]
    args += [mod, n2]
    in_specs += [const(w) for w in weights]
    args += list(weights)
    return pl.pallas_call(
        functools.partial(_merge_kernel, has_pos),
        out_shape=(jax.ShapeDtypeStruct((n, D_MODEL), F32),
                   jax.ShapeDtypeStruct((n * ROW_SUB, LANES), F32),
                   jax.ShapeDtypeStruct((ROUTE_ROWS, n), jnp.int32),
                   jax.ShapeDtypeStruct((ROUTE_ROWS, n), F32)),
        grid=(n // tm,),
        in_specs=in_specs,
        out_specs=(pl.BlockSpec((tm, D_MODEL), row), pl.BlockSpec((tm * ROW_SUB, LANES), row),
                   pl.BlockSpec((ROUTE_ROWS, tm), lambda i: (0, i)),
                   pl.BlockSpec((ROUTE_ROWS, tm), lambda i: (0, i))),
        compiler_params=_params(("arbitrary",)),
        name="merge",
    )(*args)


MOE_SUB = 4096
MOE_TM = 128
MOE_TMAX = MOE_SUB * TOP_K // MOE_TM + N_EXPERTS
MOE_PAD = 2
MOE_TS = MOE_TMAX + 2 * MOE_PAD
MOE_DUMMY = 256
MOE_RMW = 16
ROW_TILE = MOE_TM * ROW_SUB


def _moe_plan(ridx, rw):
    n = ridx.shape[1]
    nsub = n // MOE_SUB
    npair = n * TOP_K
    t = jnp.arange(n, dtype=jnp.int32)
    key = ((t // MOE_SUB) * N_EXPERTS)[None] + ridx[:TOP_K]
    key = (key * MOE_SUB + (t % MOE_SUB)[None]).reshape(-1)
    skey, sw = lax.sort((key, rw[:TOP_K].reshape(-1)), num_keys=1)
    stok = jnp.concatenate([(skey % MOE_SUB) * ROW_SUB, jnp.zeros((MOE_TM,), jnp.int32)])
    sw_rows = jnp.concatenate([sw, jnp.zeros((MOE_TM,), F32)]).reshape((npair + MOE_TM) // LANES, 1, LANES)
    hits = ridx[:TOP_K].reshape(TOP_K, nsub, 1, MOE_SUB) == jnp.arange(N_EXPERTS, dtype=jnp.int32)[None, None, :, None]
    cnt = jnp.sum(hits.astype(jnp.int32), axis=(0, 3))
    poff = (jnp.cumsum(cnt.reshape(-1)) - cnt.reshape(-1)).reshape(nsub, N_EXPERTS)
    ntile = (cnt + MOE_TM - 1) // MOE_TM
    tcum = jnp.cumsum(ntile, axis=1)
    toff = tcum - ntile
    tstart = jnp.concatenate([toff, tcum[:, -1:]], axis=1).reshape(-1).astype(jnp.int32)
    j = jnp.arange(MOE_TS, dtype=jnp.int32) - MOE_PAD
    valid = (j[None] >= 0) & (j[None] < tcum[:, -1:])
    te = jnp.minimum(jnp.sum(j[None, :, None] >= tcum[:, None, :], axis=-1), N_EXPERTS - 1)
    pick = lambda a: jnp.take_along_axis(a, te, axis=1)
    first = (j[None] - pick(toff)) * MOE_TM
    p0 = jnp.where(valid, pick(poff) + first, 0).reshape(-1).astype(jnp.int32)
    nv = jnp.where(valid, jnp.minimum(pick(cnt) - first, MOE_TM), 0).reshape(-1).astype(jnp.int32)
    return tstart, p0, nv, stok, sw_rows


def _moe_kernel(ts_ref, p0_ref, nv_ref, tok_ref, sw_ref, src_ref, w1_ref, w3_ref, w2_ref, y_ref,
                xt_ref, xb_ref, act_ref, ot_ref, w1b_ref, w3b_ref, w2b_ref, slot_ref):
    sub = pl.program_id(0)
    e = pl.program_id(1)
    base = sub * MOE_TS + MOE_PAD
    first = ts_ref[sub * (N_EXPERTS + 1) + e]
    last = ts_ref[sub * (N_EXPERTS + 1) + e + 1]

    def gather(p0):
        for mi in range(MOE_TM):
            tok = pl.multiple_of(tok_ref[p0 + mi], ROW_SUB)
            xt_ref[mi * ROW_SUB:(mi + 1) * ROW_SUB, :] = src_ref[pl.ds(tok, ROW_SUB), :]
        for j in range(ROW_SUB):
            xb_ref[:, j * LANES:(j + 1) * LANES] = xt_ref[pl.ds(j, MOE_TM, stride=ROW_SUB), :].astype(BF16)

    def scatter(p0, nv, masked):
        for u in range(0, MOE_TM, MOE_RMW):
            new = []
            for i in range(MOE_RMW):
                tok = tok_ref[p0 + u + i]
                if masked:
                    tok = jnp.where(u + i < nv, tok, MOE_SUB * ROW_SUB)
                tok = pl.multiple_of(tok, ROW_SUB)
                new.append((tok, y_ref[pl.ds(tok, ROW_SUB), :]
                            + ot_ref[(u + i) * ROW_SUB:(u + i + 1) * ROW_SUB, :]))
            for tok, v in new:
                y_ref[pl.ds(tok, ROW_SUB), :] = v

    @pl.when(e == 0)
    def _():
        y_ref[...] = jnp.zeros_like(y_ref)
        ot_ref[...] = jnp.zeros_like(ot_ref)
        act_ref[...] = jnp.zeros_like(act_ref)
        w2b_ref[...] = jnp.zeros_like(w2b_ref)
        slot_ref[0] = 0
        gather(p0_ref[base])

    @pl.when(last > first)
    def _():
        slot_ref[0] = 1 - slot_ref[0]
        w1b_ref[...] = w1_ref[0].astype(BF16)
        w3b_ref[...] = w3_ref[0].astype(BF16)
        w2b_ref[slot_ref[0]] = w2_ref[0].astype(BF16)

    slot = slot_ref[0]

    def down_proj(w2_slot):
        o = jnp.dot(act_ref[...], w2b_ref[w2_slot], preferred_element_type=F32)
        for j in range(ROW_SUB):
            ot_ref[pl.ds(j, MOE_TM, stride=ROW_SUB), :] = o[:, j * LANES:(j + 1) * LANES]

    def step(i, masked):
        cur = base + i
        scatter(p0_ref[cur - 2], nv_ref[cur - 2], masked)
        down_proj(jnp.where(i > first, slot, 1 - slot))
        p0 = p0_ref[cur]
        nv = nv_ref[cur]
        x = xb_ref[...]
        a = jnp.dot(x, w1b_ref[...], preferred_element_type=F32)
        b = jnp.dot(x, w3b_ref[...], preferred_element_type=F32)
        r0 = p0 // LANES
        c = p0 % LANES
        lane = lax.broadcasted_iota(jnp.int32, (1, LANES), 1)
        rows = lax.broadcasted_iota(jnp.int32, (LANES, LANES), 0)
        cols = lax.broadcasted_iota(jnp.int32, (LANES, LANES), 1)
        gparts = []
        for hh in range(MOE_TM // LANES):
            ga = pltpu.roll(sw_ref[r0 + hh], LANES - c, axis=1)
            gb = pltpu.roll(sw_ref[r0 + hh + 1], LANES - c, axis=1)
            g = jnp.where(lane + hh * LANES < nv, jnp.where(lane < LANES - c, ga, gb), 0.0)
            gparts.append(jnp.sum(jnp.where(rows == cols, jnp.broadcast_to(g, (LANES, LANES)), 0.0),
                                  axis=1, keepdims=True))
        gcol = jnp.concatenate(gparts, axis=0)
        act_ref[...] = (a * jax.nn.sigmoid(a) * b * gcol).astype(BF16)
        gather(p0_ref[cur + 1])

    def body(i, carry):
        step(i, True)
        return carry

    lax.fori_loop(first, last, body, 0)

    @pl.when(e == N_EXPERTS - 1)
    def _():
        scatter(p0_ref[base + last - 2], nv_ref[base + last - 2], True)
        down_proj(slot)
        scatter(p0_ref[base + last - 1], nv_ref[base + last - 1], True)


def _moe(h2_rows, plan, w1, w3, w2):
    tstart, p0, nv, stok, sw_rows = plan
    nsub = h2_rows.shape[0] // (MOE_SUB * ROW_SUB)
    wmap = lambda s, e, ts, p0, nv: (e, 0, 0)
    sub2 = lambda s, e, ts, p0, nv: (s, 0)
    grid_spec = pltpu.PrefetchScalarGridSpec(
        num_scalar_prefetch=3,
        grid=(nsub, N_EXPERTS),
        in_specs=[pl.BlockSpec(memory_space=pltpu.SMEM),
                  pl.BlockSpec(sw_rows.shape, lambda s, e, ts, p0, nv: (0, 0, 0)),
                  pl.BlockSpec((MOE_SUB * ROW_SUB, LANES), sub2, pipeline_mode=pl.Buffered(1)),
                  pl.BlockSpec((1, D_MODEL, D_EXPERT), wmap),
                  pl.BlockSpec((1, D_MODEL, D_EXPERT), wmap),
                  pl.BlockSpec((1, D_EXPERT, D_MODEL), wmap)],
        out_specs=pl.BlockSpec(((MOE_SUB + MOE_DUMMY) * ROW_SUB, LANES), sub2, pipeline_mode=pl.Buffered(1)),
        scratch_shapes=[pltpu.VMEM((ROW_TILE, LANES), F32), pltpu.VMEM((MOE_TM, D_MODEL), BF16),
                        pltpu.VMEM((MOE_TM, D_EXPERT), BF16), pltpu.VMEM((ROW_TILE, LANES), F32),
                        pltpu.VMEM((D_MODEL, D_EXPERT), BF16), pltpu.VMEM((D_MODEL, D_EXPERT), BF16),
                        pltpu.VMEM((2, D_EXPERT, D_MODEL), BF16), pltpu.SMEM((1,), jnp.int32)])
    return pl.pallas_call(
        _moe_kernel,
        grid_spec=grid_spec,
        out_shape=jax.ShapeDtypeStruct((nsub * (MOE_SUB + MOE_DUMMY) * ROW_SUB, LANES), F32),
        compiler_params=_params(("arbitrary", "arbitrary")),
        name="moe",
    )(tstart, p0, nv, stok, sw_rows, h2_rows, w1, w3, w2)


def _final_kernel(xs_ref, y_ref, mod_ref, fg_ref, o_ref):
    tm = xs_ref.shape[0]
    y = jnp.concatenate([y_ref[pl.ds(j, tm, stride=ROW_SUB), :] for j in range(ROW_SUB)], axis=1)
    x2 = xs_ref[...] + mod_ref[0][5:6, :] * y
    o_ref[...] = _rms(x2, fg_ref[...])


def _final(xs, y_rows, mod, mod_row, fg, tm):
    n = xs.shape[0]
    per_sub = MOE_SUB // tm
    stride = (MOE_SUB + MOE_DUMMY) // tm
    row = lambda i: (i, 0)
    return pl.pallas_call(
        _final_kernel,
        out_shape=jax.ShapeDtypeStruct((n, D_MODEL), F32),
        grid=(n // tm,),
        in_specs=[pl.BlockSpec((tm, D_MODEL), row),
                  pl.BlockSpec((tm * ROW_SUB, LANES), lambda i: ((i // per_sub) * stride + i % per_sub, 0)),
                  pl.BlockSpec((1, N_MOD, D_MODEL), lambda i: (mod_row(i, tm), 0, 0)),
                  pl.BlockSpec((1, D_MODEL), lambda i: (0, 0))],
        out_specs=pl.BlockSpec((tm, D_MODEL), row),
        compiler_params=_params(("arbitrary",)),
        name="final",
    )(xs, y_rows, mod, fg)


def _grid_pos_embed(n_tokens):
    rows = n_tokens // GRID_W
    r, col = jnp.meshgrid(jnp.arange(rows, dtype=F32), jnp.arange(GRID_W, dtype=F32), indexing="ij")
    quarter = D_MODEL // 4
    omega = 1.0 / (10000.0 ** (jnp.arange(quarter, dtype=F32) / quarter))

    def emb(p):
        a = p.reshape(-1)[:, None] * omega
        return jnp.concatenate([jnp.sin(a), jnp.cos(a)], axis=-1)

    return jnp.concatenate([emb(r), emb(col)], axis=-1)


def _stream(x3, pos, mod, first_row, h0, s5_ops, p):
    nb, seq, _ = x3.shape
    n = nb * seq
    nk = seq // S5_CHUNK
    x = x3.reshape(n, D_MODEL)
    per_seq_mod = first_row > 0

    def mod_row(i, tm):
        return first_row + (i * tm) // seq if per_seq_mod else 0

    us, uf, gt = _inproj(x, pos, mod, mod_row, p["norm1_g"], p["w_in"], 512)
    ys, fin = _s5(us, s5_ops, h0, nb, nk)
    yf = _fourier(uf, nb, seq, min(seq, 512))
    xs, h2_rows, ridx, rw = _merge(ys, yf, gt, x, pos, mod, mod_row, p["norm2_g"], p["merge_w"], 256)
    y_rows = _moe(h2_rows, _moe_plan(ridx, rw), p["w1"], p["w3"], p["w2"])
    out = _final(xs, y_rows, mod, mod_row, p["final_g"], 256)
    return out.reshape(nb, seq, D_MODEL), fin


def kernel(x_prompt, x_sample, state_s5_re, state_s5_im, c, c_ctx, w_ada, b_ada, norm1_g, norm2_g, w_in,
           lam_re, lam_im, log_dt, b_re, b_im, c_re, c_im, d_skip, w_glu, w_proj_s5, w_proj_f, w_out,
           w_router, router_bias, w1, w3, w2, ws1, ws3, ws2, final_norm_g):
    nb_ctx = x_prompt.shape[0]
    nb_lat, seq_lat, _ = x_sample.shape
    half = 2 * S5_STATE

    cond = jnp.concatenate([c_ctx[None], c, jnp.zeros((MOD_ROWS - 1 - nb_lat, D_MODEL), F32)], axis=0)
    mod = _adaln(cond, w_ada[0], b_ada[0]).reshape(MOD_ROWS, N_MOD, D_MODEL)

    s5_ops = _s5ops(lam_re[0], lam_im[0], log_dt[0], b_re[0], b_im[0], c_re[0], c_im[0], d_skip[0])
    p = dict(
        norm1_g=norm1_g[0][None], norm2_g=norm2_g[0][None], final_g=final_norm_g[None],
        w_in=w_in[0].astype(BF16), w1=w1[0], w3=w3[0], w2=w2[0],
        merge_w=(w_glu[0].astype(BF16), w_proj_s5[0].astype(BF16), w_proj_f[0].astype(BF16),
                 w_out[0].astype(BF16), w_router[0].T, router_bias[0][:, None],
                 ws1[0].astype(BF16), ws3[0].astype(BF16), ws2[0].astype(BF16)))

    def pack_state(sr, si):
        f = lambda a: a.astype(F32).transpose(2, 0, 1, 3).reshape(S5_GROUPS, a.shape[0], half)
        return jnp.concatenate([f(sr), f(si)], axis=-1)

    def unpack_state(fin, lo):
        nb = fin.shape[1]
        return fin[..., lo:lo + half].reshape(S5_GROUPS, nb, 2, S5_STATE).transpose(1, 2, 0, 3)[:, None]

    h0_ctx = jnp.zeros((S5_GROUPS, nb_ctx, 2 * half), F32)
    y_prompt, fin = _stream(x_prompt, None, mod, 0, h0_ctx, s5_ops, p)
    h0_lat = pack_state(state_s5_re[:, 0], state_s5_im[:, 0])
    y_sample, _ = _stream(x_sample, _grid_pos_embed(seq_lat), mod, 1, h0_lat, s5_ops, p)
    return (y_prompt, y_sample, unpack_state(fin, 0).astype(x_prompt.dtype),
            unpack_state(fin, half).astype(x_prompt.dtype))
```

```python
import functools
import math

import jax
import jax.numpy as jnp
from jax import lax
from jax.experimental import pallas as pl
from jax.experimental.pallas import tpu as pltpu

D_MODEL = 1024
GRID_W = 64
D_S5 = 768
S5_GROUP = 16
S5_GROUPS = 48
S5_STATE = 64
D_FNET = 256
FNET_GROUP = 64
N_EXPERTS = 64
TOP_K = 6
N_EXPERT_GROUPS = 8
EXPERTS_PER_GROUP = N_EXPERTS // N_EXPERT_GROUPS
TOPK_GROUPS = 4
D_EXPERT = 256
ROUTED_SCALE = 2.5
N_MOD = 6
EPS = 1e-6

S5_CHUNK = 16
S5_ROW = S5_CHUNK * S5_GROUP
MOD_ROWS = 8
ROUTE_ROWS = 8
LANES = 128
ROW_SUB = D_MODEL // LANES
VMEM_LIMIT = 56 * 1024 * 1024

BF16 = jnp.bfloat16
F32 = jnp.float32


def _params(sem, vmem=VMEM_LIMIT):
    return pltpu.CompilerParams(dimension_semantics=sem, vmem_limit_bytes=vmem)


def _rms(x, g):
    return x * lax.rsqrt(jnp.mean(x * x, axis=-1, keepdims=True) + EPS) * g


def _adaln_kernel(c_ref, w_ref, b_ref, o_ref):
    c = c_ref[...]
    o_ref[...] = jnp.dot(c * jax.nn.sigmoid(c), w_ref[...], precision=lax.Precision.HIGHEST,
                         preferred_element_type=F32) + b_ref[...]


def _adaln(cond, w_ada, b_ada):
    n_out = N_MOD * D_MODEL
    return pl.pallas_call(
        _adaln_kernel,
        out_shape=jax.ShapeDtypeStruct((MOD_ROWS, n_out), F32),
        grid=(N_MOD,),
        in_specs=[pl.BlockSpec((MOD_ROWS, D_MODEL), lambda i: (0, 0)),
                  pl.BlockSpec((D_MODEL, D_MODEL), lambda i: (0, i)),
                  pl.BlockSpec((1, D_MODEL), lambda i: (0, i))],
        out_specs=pl.BlockSpec((MOD_ROWS, D_MODEL), lambda i: (0, i)),
        compiler_params=_params(("arbitrary",)),
        name="adaln",
    )(cond, w_ada, b_ada.reshape(1, n_out))


def _inproj_kernel(has_pos, *refs):
    if has_pos:
        x_ref, pos_ref, mod_ref, g_ref, w_ref, us_ref, uf_ref, gt_ref = refs
        x = x_ref[...] + pos_ref[...]
    else:
        x_ref, mod_ref, g_ref, w_ref, us_ref, uf_ref, gt_ref = refs
        x = x_ref[...]
    m = mod_ref[0]
    h = _rms(x, g_ref[...]) * (1.0 + m[1:2, :]) + m[0:1, :]
    p = jnp.dot(h.astype(BF16), w_ref[...], preferred_element_type=F32)
    us_ref[...] = p[:, :D_S5]
    uf_ref[...] = p[:, D_S5:D_MODEL].astype(BF16)
    gt_ref[...] = jax.nn.sigmoid(p[:, D_MODEL:]).astype(BF16)


def _inproj(x, pos, mod, mod_row, norm_g, w_in_bf, tm):
    n = x.shape[0]
    has_pos = pos is not None
    row = lambda i: (i, 0)
    in_specs = [pl.BlockSpec((tm, D_MODEL), row)]
    args = [x]
    if has_pos:
        nper = pos.shape[0] // tm
        in_specs.append(pl.BlockSpec((tm, D_MODEL), lambda i: (i % nper, 0)))
        args.append(pos)
    in_specs += [pl.BlockSpec((1, N_MOD, D_MODEL), lambda i: (mod_row(i, tm), 0, 0)),
                 pl.BlockSpec((1, D_MODEL), lambda i: (0, 0)),
                 pl.BlockSpec(w_in_bf.shape, lambda i: (0, 0))]
    args += [mod, norm_g, w_in_bf]
    return pl.pallas_call(
        functools.partial(_inproj_kernel, has_pos),
        out_shape=(jax.ShapeDtypeStruct((n, D_S5), F32),
                   jax.ShapeDtypeStruct((n, D_FNET), BF16),
                   jax.ShapeDtypeStruct((n, 2 * D_MODEL), BF16)),
        grid=(n // tm,),
        in_specs=in_specs,
        out_specs=(pl.BlockSpec((tm, D_S5), row), pl.BlockSpec((tm, D_FNET), row),
                   pl.BlockSpec((tm, 2 * D_MODEL), row)),
        compiler_params=_params(("arbitrary",)),
        name="inproj",
    )(*args)


def _shift_lanes(x, k):
    if k == 0:
        return x
    z = jnp.zeros((x.shape[0], abs(k)), x.dtype)
    if k > 0:
        return jnp.concatenate([z, x[:, :x.shape[1] - k]], axis=1)
    return jnp.concatenate([x[:, -k:], z], axis=1)


def _s5ops_kernel(lam_re_ref, lam_im_ref, dt_ref, btr_ref, bti_ref, cr_ref, ci_ref, d_ref,
                  m_ref, wi_ref, wot_ref, coef_ref):
    hi = lax.Precision.HIGHEST
    lr = jnp.minimum(lam_re_ref[0], -1e-4)
    li = lam_im_ref[0]
    dt = jnp.exp(dt_ref[0])
    mag = jnp.exp(lr * dt)
    ar = mag * jnp.cos(li * dt)
    ai = mag * jnp.sin(li * dt)
    den = lr * lr + li * li
    nr = ar - 1.0
    qr = (nr * lr + ai * li) / den
    qi = (ai * lr - nr * li) / den
    pr, pi = [], []
    for n in range(S5_CHUNK + 1):
        pm = jnp.exp(float(n) * (lr * dt))
        pr.append(pm * jnp.cos(float(n) * (li * dt)))
        pi.append(pm * jnp.sin(float(n) * (li * dt)))
    bbr, bbi, car, cai = [], [], [], []
    for d in range(2):
        btr = btr_ref[0, d]
        bti = bti_ref[0, d]
        bbr.append(qr[d:d + 1] * btr - qi[d:d + 1] * bti)
        bbi.append(qr[d:d + 1] * bti + qi[d:d + 1] * btr)
        cr = cr_ref[0, d]
        ci = ci_ref[0, d]
        car.append([cr * pr[n][d:d + 1] - ci * pi[n][d:d + 1] for n in range(S5_CHUNK + 1)])
        cai.append([cr * pi[n][d:d + 1] + ci * pr[n][d:d + 1] for n in range(S5_CHUNK + 1)])

    def lag_kernels(d, order):
        a = jnp.concatenate([car[d][n] for n in order], axis=0)
        b = jnp.concatenate([cai[d][n] for n in order], axis=0)
        dn = (((1,), (1,)), ((), ()))
        return (lax.dot_general(bbr[d], a, dn, precision=hi, preferred_element_type=F32)
                - lax.dot_general(bbi[d], b, dn, precision=hi, preferred_element_type=F32))

    ktf = lag_kernels(0, range(S5_CHUNK))
    ktb = lag_kernels(1, range(S5_CHUNK - 1, -1, -1))
    row = lax.broadcasted_iota(jnp.int32, (S5_GROUP, S5_ROW), 0)
    lane = lax.broadcasted_iota(jnp.int32, (S5_GROUP, S5_ROW), 1)
    dcol = d_ref[0]
    for j in range(S5_CHUNK):
        rows = slice(j * S5_GROUP, (j + 1) * S5_GROUP)
        blk = _shift_lanes(ktf, S5_GROUP * j) + _shift_lanes(ktb, -S5_GROUP * (S5_CHUNK - 1 - j))
        blk = blk + jnp.where(lane == S5_GROUP * j + row, dcol, 0.0)
        m_ref[0, rows, :] = blk.astype(BF16)
        nf = S5_CHUNK - 1 - j
        wi = jnp.concatenate([pr[nf][0:1] * bbr[0] - pi[nf][0:1] * bbi[0],
                              pr[j][1:2] * bbr[1] - pi[j][1:2] * bbi[1],
                              pr[nf][0:1] * bbi[0] + pi[nf][0:1] * bbr[0],
                              pr[j][1:2] * bbi[1] + pi[j][1:2] * bbr[1]], axis=1)
        wi_ref[0, rows, :] = wi.astype(BF16)
        wot = jnp.concatenate([car[0][j + 1], car[1][S5_CHUNK - j],
                               -cai[0][j + 1], -cai[1][S5_CHUNK - j]], axis=1)
        wot_ref[0, rows, :] = wot.astype(BF16)
    coef_ref[0, 0:1, :] = jnp.concatenate([pr[S5_CHUNK][0:1], pr[S5_CHUNK][1:2]], axis=1)
    coef_ref[0, 1:2, :] = jnp.concatenate([pi[S5_CHUNK][0:1], pi[S5_CHUNK][1:2]], axis=1)


def _s5ops(lam_re, lam_im, log_dt, b_re, b_im, c_re, c_im, d_skip):
    g3 = lambda g: (g, 0, 0)
    g4 = lambda g: (g, 0, 0, 0)
    sw = lambda a: jnp.swapaxes(a.astype(F32), 0, 1)
    dt = jnp.broadcast_to(sw(log_dt)[..., None], (S5_GROUPS, 2, S5_STATE))
    args = (sw(lam_re), sw(lam_im), dt, sw(jnp.swapaxes(b_re, 2, 3)), sw(jnp.swapaxes(b_im, 2, 3)),
            sw(c_re), sw(c_im), d_skip.astype(F32).reshape(S5_GROUPS, S5_GROUP, 1))
    vec = pl.BlockSpec((1, 2, S5_STATE), g3)
    mat = pl.BlockSpec((1, 2, S5_GROUP, S5_STATE), g4)
    op = pl.BlockSpec((1, S5_ROW, S5_ROW), g3)
    return pl.pallas_call(
        _s5ops_kernel,
        out_shape=(jax.ShapeDtypeStruct((S5_GROUPS, S5_ROW, S5_ROW), BF16),) * 3
        + (jax.ShapeDtypeStruct((S5_GROUPS, 2, 2 * S5_STATE), F32),),
        grid=(S5_GROUPS,),
        in_specs=[vec, vec, vec, mat, mat, mat, mat, pl.BlockSpec((1, S5_GROUP, 1), g3)],
        out_specs=(op, op, op, pl.BlockSpec((1, 2, 2 * S5_STATE), g3)),
        compiler_params=_params(("arbitrary",)),
        name="s5ops",
    )(*args)


S5_BLOCK_GROUPS = LANES // S5_GROUP


S5_HALF_T = LANES // S5_GROUP


def _s5_perm():
    a = jnp.arange(S5_HALF_T * LANES, dtype=jnp.int32)
    dst = ((a % LANES) // S5_GROUP) * LANES + (a // LANES) * S5_GROUP + a % S5_GROUP
    return (dst[:, None] == a[None, :]).astype(BF16)


def _s5_kernel(nb, nk, u_ref, perm_ref, m_ref, wi_ref, wot_ref, coef_ref, h0_ref, y_ref, fin_ref,
               sr_ref, si_ref, efr_ref, ebr_ref, efi_ref, ebi_ref, uall_ref, yall_ref):
    gl = pl.program_id(1)
    rows = nb * nk
    half = 2 * S5_STATE

    @pl.when(gl == 0)
    def _():
        for hh in range(S5_CHUNK // S5_HALF_T):
            xcat = jnp.concatenate([u_ref[pl.ds(hh * S5_HALF_T + tt, rows, stride=S5_CHUNK), :].astype(BF16)
                                    for tt in range(S5_HALF_T)], axis=1)
            uh = jnp.dot(xcat, perm_ref[...], preferred_element_type=F32).astype(BF16)
            for g in range(S5_BLOCK_GROUPS):
                uall_ref[g, :, hh * LANES:(hh + 1) * LANES] = uh[:, g * LANES:(g + 1) * LANES]

    u = uall_ref[gl]
    s = jnp.dot(u, wi_ref[0], preferred_element_type=F32)
    sr_ref[...] = s[:, :half]
    si_ref[...] = s[:, half:]
    c_r = coef_ref[0, 0:1, :]
    c_i = coef_ref[0, 1:2, :]
    e_r = h0_ref[0, :, :half]
    e_i = h0_ref[0, :, half:]
    is_fwd = lax.broadcasted_iota(jnp.int32, (nb, half), 1) < S5_STATE
    for j in range(nk):
        rf = pl.ds(j, nb, stride=nk)
        rb = pl.ds(nk - 1 - j, nb, stride=nk)
        efr_ref[rf, :] = e_r
        ebr_ref[rb, :] = e_r
        efi_ref[rf, :] = e_i
        ebi_ref[rb, :] = e_i
        s_r = jnp.where(is_fwd, sr_ref[rf, :], sr_ref[rb, :])
        s_i = jnp.where(is_fwd, si_ref[rf, :], si_ref[rb, :])
        e_r, e_i = c_r * e_r - c_i * e_i + s_r, c_r * e_i + c_i * e_r + s_i
    fin_ref[0, :, :half] = e_r
    fin_ref[0, :, half:] = e_i
    y = jnp.dot(u, m_ref[0], preferred_element_type=F32)
    fwd_rows = lax.broadcasted_iota(jnp.int32, (rows, half), 1) < S5_STATE
    e = jnp.concatenate([jnp.where(fwd_rows, efr_ref[...], ebr_ref[...]),
                         jnp.where(fwd_rows, efi_ref[...], ebi_ref[...])], axis=1).astype(BF16)
    y = y + lax.dot_general(e, wot_ref[0], (((1,), (1,)), ((), ())), preferred_element_type=F32)
    yall_ref[gl] = y.astype(BF16)

    @pl.when(gl == S5_BLOCK_GROUPS - 1)
    def _():
        for hh in range(S5_CHUNK // S5_HALF_T):
            ycat = jnp.concatenate([yall_ref[g, :, hh * LANES:(hh + 1) * LANES]
                                    for g in range(S5_BLOCK_GROUPS)], axis=1)
            out = lax.dot_general(ycat, perm_ref[...], (((1,), (1,)), ((), ())), preferred_element_type=F32)
            for tt in range(S5_HALF_T):
                y_ref[pl.ds(hh * S5_HALF_T + tt, rows, stride=S5_CHUNK), :] = out[:, tt * LANES:(tt + 1) * LANES]


def _s5(u, ops, h0, nb, nk):
    m, w_in, w_out, coef = ops
    n = u.shape[0]
    rows = nb * nk
    perm = _s5_perm()
    g3 = lambda b, g: (b * S5_BLOCK_GROUPS + g, 0, 0)
    blk = lambda b, g: (0, b)
    return pl.pallas_call(
        functools.partial(_s5_kernel, nb, nk),
        out_shape=(jax.ShapeDtypeStruct((n, D_S5), F32),
                   jax.ShapeDtypeStruct((S5_GROUPS, nb, 4 * S5_STATE), F32)),
        grid=(S5_GROUPS // S5_BLOCK_GROUPS, S5_BLOCK_GROUPS),
        in_specs=[pl.BlockSpec((n, LANES), blk),
                  pl.BlockSpec(perm.shape, lambda b, g: (0, 0)),
                  pl.BlockSpec((1, S5_ROW, S5_ROW), g3),
                  pl.BlockSpec((1, S5_ROW, 4 * S5_STATE), g3),
                  pl.BlockSpec((1, 4 * S5_STATE, S5_ROW), g3),
                  pl.BlockSpec((1, 2, 2 * S5_STATE), g3),
                  pl.BlockSpec((1, nb, 4 * S5_STATE), g3)],
        out_specs=(pl.BlockSpec((n, LANES), blk),
                   pl.BlockSpec((1, nb, 4 * S5_STATE), g3)),
        scratch_shapes=[pltpu.VMEM((rows, 2 * S5_STATE), F32)] * 6 + [
            pltpu.VMEM((S5_BLOCK_GROUPS, rows, S5_ROW), BF16),
            pltpu.VMEM((S5_BLOCK_GROUPS, rows, S5_ROW), BF16)],
        compiler_params=_params(("arbitrary", "arbitrary")),
        name="s5",
    )(u, perm, m, w_in, w_out, coef, h0)


def _dft_tables(seq):
    k = jnp.arange(seq, dtype=jnp.int32)
    na = seq // FNET_GROUP
    ang_a = (2.0 * math.pi / na) * ((jnp.arange(na, dtype=jnp.int32)[:, None] * k[None, :]) % na).astype(F32)
    ang_b = (2.0 * math.pi / seq) * ((jnp.arange(FNET_GROUP, dtype=jnp.int32)[:, None] * k[None, :]) % seq).astype(F32)
    ca, sa = jnp.cos(ang_a)[:, None, :], jnp.sin(ang_a)[:, None, :]
    cb, sb = jnp.cos(ang_b)[None, :, :], jnp.sin(ang_b)[None, :, :]
    cos_jk = (ca * cb - sa * sb).reshape(seq, seq)
    sin_jk = (sa * cb + ca * sb).reshape(seq, seq)
    cs = jnp.concatenate([cos_jk, -sin_jk], axis=1).astype(BF16)
    c = jnp.arange(D_FNET, dtype=jnp.int32)
    same = (c[:, None] // FNET_GROUP) == (c[None, :] // FNET_GROUP)
    angc = (2.0 * math.pi / FNET_GROUP) * (((c[:, None] % FNET_GROUP) * (c[None, :] % FNET_GROUP))
                                           % FNET_GROUP).astype(F32)
    scale = 1.0 / math.sqrt(seq * FNET_GROUP)
    bdc = jnp.where(same, jnp.cos(angc) * scale, 0.0).astype(BF16)
    bds = jnp.where(same, jnp.sin(angc) * scale, 0.0).astype(BF16)
    return cs, bdc, bds


def _fourier_kernel(seq, z_ref, cs_ref, bdc_ref, bds_ref, o_ref, zz_ref):
    @pl.when(pl.program_id(1) == 0)
    def _():
        z = z_ref[...]
        zz_ref[0:seq, :] = jnp.dot(z, bdc_ref[...], preferred_element_type=F32).astype(BF16)
        zz_ref[seq:, :] = jnp.dot(z, bds_ref[...], preferred_element_type=F32).astype(BF16)

    o_ref[...] = jnp.dot(cs_ref[...], zz_ref[...], preferred_element_type=F32).astype(BF16)


def _fourier(z, nb, seq, tl):
    cs, bdc, bds = _dft_tables(seq)
    nt = seq // tl
    return pl.pallas_call(
        functools.partial(_fourier_kernel, seq),
        out_shape=jax.ShapeDtypeStruct(z.shape, BF16),
        grid=(nb, nt),
        in_specs=[pl.BlockSpec((seq, D_FNET), lambda b, i: (b, 0)),
                  pl.BlockSpec((tl, 2 * seq), lambda b, i: (i, 0)),
                  pl.BlockSpec((D_FNET, D_FNET), lambda b, i: (0, 0)),
                  pl.BlockSpec((D_FNET, D_FNET), lambda b, i: (0, 0))],
        out_specs=pl.BlockSpec((tl, D_FNET), lambda b, i: (b * nt + i, 0)),
        scratch_shapes=[pltpu.VMEM((2 * seq, D_FNET), BF16)],
        compiler_params=_params(("arbitrary", "arbitrary")),
        name="fourier",
    )(z, cs, bdc, bds)


def _first_argmax_mask(v, iota, size):
    m = jnp.max(v, axis=0, keepdims=True)
    first = jnp.min(jnp.where(v == m, iota, size), axis=0, keepdims=True)
    return iota == first


def _route(logits_t, bias_col):
    tm = logits_t.shape[1]
    neg = -jnp.inf
    s = jax.nn.sigmoid(logits_t)
    biased = s + bias_col
    io8 = lax.broadcasted_iota(jnp.int32, (EXPERTS_PER_GROUP, tm), 0)
    gs_rows = []
    for g in range(N_EXPERT_GROUPS):
        blk = biased[g * EXPERTS_PER_GROUP:(g + 1) * EXPERTS_PER_GROUP, :]
        m1 = jnp.max(blk, axis=0, keepdims=True)
        rest = jnp.where(_first_argmax_mask(blk, io8, EXPERTS_PER_GROUP), neg, blk)
        gs_rows.append(m1 + jnp.max(rest, axis=0, keepdims=True))
    gs = jnp.concatenate(gs_rows, axis=0)
    iog = lax.broadcasted_iota(jnp.int32, (N_EXPERT_GROUPS, tm), 0)
    gsel = jnp.zeros((N_EXPERT_GROUPS, tm), F32)
    for _ in range(TOPK_GROUPS):
        sel = _first_argmax_mask(gs, iog, N_EXPERT_GROUPS)
        gsel = jnp.where(sel, 1.0, gsel)
        gs = jnp.where(sel, neg, gs)
    emask = jnp.concatenate(
        [jnp.broadcast_to(gsel[g:g + 1, :], (EXPERTS_PER_GROUP, tm)) for g in range(N_EXPERT_GROUPS)], axis=0)
    v = jnp.where(emask > 0.0, biased, neg)
    ioe = lax.broadcasted_iota(jnp.int32, (N_EXPERTS, tm), 0)
    idx_rows, s_rows = [], []
    for _ in range(TOP_K):
        sel = _first_argmax_mask(v, ioe, N_EXPERTS)
        idx_rows.append(jnp.sum(jnp.where(sel, ioe, 0), axis=0, keepdims=True))
        s_rows.append(jnp.sum(jnp.where(sel, s, 0.0), axis=0, keepdims=True))
        v = jnp.where(sel, neg, v)
    denom = s_rows[0]
    for r in s_rows[1:]:
        denom = denom + r
    pad = ROUTE_ROWS - TOP_K
    idx = jnp.concatenate(idx_rows + [jnp.zeros((pad, tm), jnp.int32)], axis=0)
    w = jnp.concatenate([r / denom * ROUTED_SCALE for r in s_rows] + [jnp.zeros((pad, tm), F32)], axis=0)
    return idx, w


def _merge_kernel(has_pos, *refs):
    if has_pos:
        (ys_ref, yf_ref, gt_ref, x_ref, pos_ref, mod_ref, n2_ref, wglu_ref, wps_ref, wpf_ref, wout_ref,
         wrt_ref, rb_ref, ws1_ref, ws3_ref, ws2_ref, xs_ref, h2_ref, ridx_ref, rw_ref) = refs
        x = x_ref[...] + pos_ref[...]
    else:
        (ys_ref, yf_ref, gt_ref, x_ref, mod_ref, n2_ref, wglu_ref, wps_ref, wpf_ref, wout_ref,
         wrt_ref, rb_ref, ws1_ref, ws3_ref, ws2_ref, xs_ref, h2_ref, ridx_ref, rw_ref) = refs
        x = x_ref[...]
    m = mod_ref[0]
    g = jax.nn.gelu(ys_ref[...].astype(F32))
    a = g * jax.nn.sigmoid(jnp.dot(g.astype(BF16), wglu_ref[...], preferred_element_type=F32))
    pa = jnp.dot(a.astype(BF16), wps_ref[...], preferred_element_type=F32)
    pb = jnp.dot(yf_ref[...], wpf_ref[...], preferred_element_type=F32)
    gt = gt_ref[...].astype(F32)
    merged = gt[:, :D_MODEL] * pa + gt[:, D_MODEL:] * pb
    x1 = x + m[2:3, :] * jnp.dot(merged.astype(BF16), wout_ref[...], preferred_element_type=F32)
    h2 = _rms(x1, n2_ref[...]) * (1.0 + m[4:5, :]) + m[3:4, :]
    hb = h2.astype(BF16)
    for j in range(ROW_SUB):
        h2_ref[pl.ds(j, h2.shape[0], stride=ROW_SUB), :] = h2[:, j * LANES:(j + 1) * LANES]
    wr = wrt_ref[...]
    wr_hi = wr.astype(BF16)
    wr_lo = (wr - wr_hi.astype(F32)).astype(BF16)
    h_lo = (h2 - hb.astype(F32)).astype(BF16)
    dn = (((1,), (1,)), ((), ()))
    logits_t = (lax.dot_general(wr_hi, hb, dn, preferred_element_type=F32)
                + lax.dot_general(wr_hi, h_lo, dn, preferred_element_type=F32)
                + lax.dot_general(wr_lo, hb, dn, preferred_element_type=F32))
    ridx_ref[...], rw_ref[...] = _route(logits_t, rb_ref[...])
    s1 = jnp.dot(hb, ws1_ref[...], preferred_element_type=F32)
    s3 = jnp.dot(hb, ws3_ref[...], preferred_element_type=F32)
    shared = jnp.dot((s1 * jax.nn.sigmoid(s1) * s3).astype(BF16), ws2_ref[...], preferred_element_type=F32)
    xs_ref[...] = x1 + m[5:6, :] * shared


def _merge(ys, yf, gt, x, pos, mod, mod_row, n2, weights, tm):
    n = x.shape[0]
    has_pos = pos is not None
    row = lambda i: (i, 0)
    const = lambda a: pl.BlockSpec(a.shape, lambda i: (0,) * a.ndim)
    in_specs = [pl.BlockSpec((tm, D_S5), row), pl.BlockSpec((tm, D_FNET), row),
                pl.BlockSpec((tm, 2 * D_MODEL), row), pl.BlockSpec((tm, D_MODEL), row)]
    args = [ys, yf, gt, x]
    if has_pos:
        nper = pos.shape[0] // tm
        in_specs.append(pl.BlockSpec((tm, D_MODEL), lambda i: (i % nper, 0)))
        args.append(pos)
    in_specs += [pl.BlockSpec((1, N_MOD, D_MODEL), lambda i: (mod_row(i, tm), 0, 0)), const(n2)]
    args += [mod, n2]
    in_specs += [const(w) for w in weights]
    args += list(weights)
    return pl.pallas_call(
        functools.partial(_merge_kernel, has_pos),
        out_shape=(jax.ShapeDtypeStruct((n, D_MODEL), F32),
                   jax.ShapeDtypeStruct((n * ROW_SUB, LANES), F32),
                   jax.ShapeDtypeStruct((ROUTE_ROWS, n), jnp.int32),
                   jax.ShapeDtypeStruct((ROUTE_ROWS, n), F32)),
        grid=(n // tm,),
        in_specs=in_specs,
        out_specs=(pl.BlockSpec((tm, D_MODEL), row), pl.BlockSpec((tm * ROW_SUB, LANES), row),
                   pl.BlockSpec((ROUTE_ROWS, tm), lambda i: (0, i)),
                   pl.BlockSpec((ROUTE_ROWS, tm), lambda i: (0, i))),
        compiler_params=_params(("arbitrary",)),
        name="merge",
    )(*args)


MOE_SUB = 4096
MOE_TM = 128
MOE_TMAX = MOE_SUB * TOP_K // MOE_TM + N_EXPERTS
MOE_PAD = 2
MOE_TS = MOE_TMAX + 2 * MOE_PAD
MOE_DUMMY = 256
MOE_RMW = 16
ROW_TILE = MOE_TM * ROW_SUB


def _moe_plan(ridx, rw):
    n = ridx.shape[1]
    nsub = n // MOE_SUB
    npair = n * TOP_K
    t = jnp.arange(n, dtype=jnp.int32)
    key = ((t // MOE_SUB) * N_EXPERTS)[None] + ridx[:TOP_K]
    key = (key * MOE_SUB + (t % MOE_SUB)[None]).reshape(-1)
    skey, sw = lax.sort((key, rw[:TOP_K].reshape(-1)), num_keys=1)
    stok = jnp.concatenate([(skey % MOE_SUB) * ROW_SUB, jnp.zeros((MOE_TM,), jnp.int32)])
    sw_rows = jnp.concatenate([sw, jnp.zeros((MOE_TM,), F32)]).reshape((npair + MOE_TM) // LANES, 1, LANES)
    hits = ridx[:TOP_K].reshape(TOP_K, nsub, 1, MOE_SUB) == jnp.arange(N_EXPERTS, dtype=jnp.int32)[None, None, :, None]
    cnt = jnp.sum(hits.astype(jnp.int32), axis=(0, 3))
    poff = (jnp.cumsum(cnt.reshape(-1)) - cnt.reshape(-1)).reshape(nsub, N_EXPERTS)
    ntile = (cnt + MOE_TM - 1) // MOE_TM
    tcum = jnp.cumsum(ntile, axis=1)
    toff = tcum - ntile
    tstart = jnp.concatenate([toff, tcum[:, -1:]], axis=1).reshape(-1).astype(jnp.int32)
    j = jnp.arange(MOE_TS, dtype=jnp.int32) - MOE_PAD
    valid = (j[None] >= 0) & (j[None] < tcum[:, -1:])
    te = jnp.minimum(jnp.sum(j[None, :, None] >= tcum[:, None, :], axis=-1), N_EXPERTS - 1)
    pick = lambda a: jnp.take_along_axis(a, te, axis=1)
    first = (j[None] - pick(toff)) * MOE_TM
    p0 = jnp.where(valid, pick(poff) + first, 0).reshape(-1).astype(jnp.int32)
    nv = jnp.where(valid, jnp.minimum(pick(cnt) - first, MOE_TM), 0).reshape(-1).astype(jnp.int32)
    return tstart, p0, nv, stok, sw_rows


def _moe_kernel(ts_ref, p0_ref, nv_ref, tok_ref, sw_ref, src_ref, w1_ref, w3_ref, w2_ref, y_ref,
                xt_ref, xb_ref, act_ref, ot_ref, w1b_ref, w3b_ref, w2b_ref, slot_ref):
    sub = pl.program_id(0)
    e = pl.program_id(1)
    base = sub * MOE_TS + MOE_PAD
    first = ts_ref[sub * (N_EXPERTS + 1) + e]
    last = ts_ref[sub * (N_EXPERTS + 1) + e + 1]

    def gather(p0):
        for mi in range(MOE_TM):
            tok = pl.multiple_of(tok_ref[p0 + mi], ROW_SUB)
            xt_ref[mi * ROW_SUB:(mi + 1) * ROW_SUB, :] = src_ref[pl.ds(tok, ROW_SUB), :]
        for j in range(ROW_SUB):
            xb_ref[:, j * LANES:(j + 1) * LANES] = xt_ref[pl.ds(j, MOE_TM, stride=ROW_SUB), :].astype(BF16)

    def scatter(p0, nv, masked):
        for u in range(0, MOE_TM, MOE_RMW):
            new = []
            for i in range(MOE_RMW):
                tok = tok_ref[p0 + u + i]
                if masked:
                    tok = jnp.where(u + i < nv, tok, MOE_SUB * ROW_SUB)
                tok = pl.multiple_of(tok, ROW_SUB)
                new.append((tok, y_ref[pl.ds(tok, ROW_SUB), :]
                            + ot_ref[(u + i) * ROW_SUB:(u + i + 1) * ROW_SUB, :]))
            for tok, v in new:
                y_ref[pl.ds(tok, ROW_SUB), :] = v

    @pl.when(e == 0)
    def _():
        y_ref[...] = jnp.zeros_like(y_ref)
        ot_ref[...] = jnp.zeros_like(ot_ref)
        act_ref[...] = jnp.zeros_like(act_ref)
        w2b_ref[...] = jnp.zeros_like(w2b_ref)
        slot_ref[0] = 0
        gather(p0_ref[base])

    @pl.when(last > first)
    def _():
        slot_ref[0] = 1 - slot_ref[0]
        w1b_ref[...] = w1_ref[0].astype(BF16)
        w3b_ref[...] = w3_ref[0].astype(BF16)
        w2b_ref[slot_ref[0]] = w2_ref[0].astype(BF16)

    slot = slot_ref[0]

    def down_proj(w2_slot):
        o = jnp.dot(act_ref[...], w2b_ref[w2_slot], preferred_element_type=F32)
        for j in range(ROW_SUB):
            ot_ref[pl.ds(j, MOE_TM, stride=ROW_SUB), :] = o[:, j * LANES:(j + 1) * LANES]

    def step(i, masked):
        cur = base + i
        scatter(p0_ref[cur - 2], nv_ref[cur - 2], masked)
        down_proj(jnp.where(i > first, slot, 1 - slot))
        p0 = p0_ref[cur]
        nv = nv_ref[cur]
        x = xb_ref[...]
        a = jnp.dot(x, w1b_ref[...], preferred_element_type=F32)
        b = jnp.dot(x, w3b_ref[...], preferred_element_type=F32)
        r0 = p0 // LANES
        c = p0 % LANES
        lane = lax.broadcasted_iota(jnp.int32, (1, LANES), 1)
        rows = lax.broadcasted_iota(jnp.int32, (LANES, LANES), 0)
        cols = lax.broadcasted_iota(jnp.int32, (LANES, LANES), 1)
        gparts = []
        for hh in range(MOE_TM // LANES):
            ga = pltpu.roll(sw_ref[r0 + hh], LANES - c, axis=1)
            gb = pltpu.roll(sw_ref[r0 + hh + 1], LANES - c, axis=1)
            g = jnp.where(lane + hh * LANES < nv, jnp.where(lane < LANES - c, ga, gb), 0.0)
            gparts.append(jnp.sum(jnp.where(rows == cols, jnp.broadcast_to(g, (LANES, LANES)), 0.0),
                                  axis=1, keepdims=True))
        gcol = jnp.concatenate(gparts, axis=0)
        act_ref[...] = (a * jax.nn.sigmoid(a) * b * gcol).astype(BF16)
        gather(p0_ref[cur + 1])

    def body(i, carry):
        step(i, True)
        return carry

    lax.fori_loop(first, last, body, 0)

    @pl.when(e == N_EXPERTS - 1)
    def _():
        scatter(p0_ref[base + last - 2], nv_ref[base + last - 2], True)
        down_proj(slot)
        scatter(p0_ref[base + last - 1], nv_ref[base + last - 1], True)


def _moe(h2_rows, plan, w1, w3, w2):
    tstart, p0, nv, stok, sw_rows = plan
    nsub = h2_rows.shape[0] // (MOE_SUB * ROW_SUB)
    wmap = lambda s, e, ts, p0, nv: (e, 0, 0)
    sub2 = lambda s, e, ts, p0, nv: (s, 0)
    grid_spec = pltpu.PrefetchScalarGridSpec(
        num_scalar_prefetch=3,
        grid=(nsub, N_EXPERTS),
        in_specs=[pl.BlockSpec(memory_space=pltpu.SMEM),
                  pl.BlockSpec(sw_rows.shape, lambda s, e, ts, p0, nv: (0, 0, 0)),
                  pl.BlockSpec((MOE_SUB * ROW_SUB, LANES), sub2, pipeline_mode=pl.Buffered(1)),
                  pl.BlockSpec((1, D_MODEL, D_EXPERT), wmap),
                  pl.BlockSpec((1, D_MODEL, D_EXPERT), wmap),
                  pl.BlockSpec((1, D_EXPERT, D_MODEL), wmap)],
        out_specs=pl.BlockSpec(((MOE_SUB + MOE_DUMMY) * ROW_SUB, LANES), sub2, pipeline_mode=pl.Buffered(1)),
        scratch_shapes=[pltpu.VMEM((ROW_TILE, LANES), F32), pltpu.VMEM((MOE_TM, D_MODEL), BF16),
                        pltpu.VMEM((MOE_TM, D_EXPERT), BF16), pltpu.VMEM((ROW_TILE, LANES), F32),
                        pltpu.VMEM((D_MODEL, D_EXPERT), BF16), pltpu.VMEM((D_MODEL, D_EXPERT), BF16),
                        pltpu.VMEM((2, D_EXPERT, D_MODEL), BF16), pltpu.SMEM((1,), jnp.int32)])
    return pl.pallas_call(
        _moe_kernel,
        grid_spec=grid_spec,
        out_shape=jax.ShapeDtypeStruct((nsub * (MOE_SUB + MOE_DUMMY) * ROW_SUB, LANES), F32),
        compiler_params=_params(("arbitrary", "arbitrary")),
        name="moe",
    )(tstart, p0, nv, stok, sw_rows, h2_rows, w1, w3, w2)


def _final_kernel(xs_ref, y_ref, mod_ref, fg_ref, o_ref):
    tm = xs_ref.shape[0]
    y = jnp.concatenate([y_ref[pl.ds(j, tm, stride=ROW_SUB), :] for j in range(ROW_SUB)], axis=1)
    x2 = xs_ref[...] + mod_ref[0][5:6, :] * y
    o_ref[...] = _rms(x2, fg_ref[...])


def _final(xs, y_rows, mod, mod_row, fg, tm):
    n = xs.shape[0]
    per_sub = MOE_SUB // tm
    stride = (MOE_SUB + MOE_DUMMY) // tm
    row = lambda i: (i, 0)
    return pl.pallas_call(
        _final_kernel,
        out_shape=jax.ShapeDtypeStruct((n, D_MODEL), F32),
        grid=(n // tm,),
        in_specs=[pl.BlockSpec((tm, D_MODEL), row),
                  pl.BlockSpec((tm * ROW_SUB, LANES), lambda i: ((i // per_sub) * stride + i % per_sub, 0)),
                  pl.BlockSpec((1, N_MOD, D_MODEL), lambda i: (mod_row(i, tm), 0, 0)),
                  pl.BlockSpec((1, D_MODEL), lambda i: (0, 0))],
        out_specs=pl.BlockSpec((tm, D_MODEL), row),
        compiler_params=_params(("arbitrary",)),
        name="final",
    )(xs, y_rows, mod, fg)


def _grid_pos_embed(n_tokens):
    rows = n_tokens // GRID_W
    r, col = jnp.meshgrid(jnp.arange(rows, dtype=F32), jnp.arange(GRID_W, dtype=F32), indexing="ij")
    quarter = D_MODEL // 4
    omega = 1.0 / (10000.0 ** (jnp.arange(quarter, dtype=F32) / quarter))

    def emb(p):
        a = p.reshape(-1)[:, None] * omega
        return jnp.concatenate([jnp.sin(a), jnp.cos(a)], axis=-1)

    return jnp.concatenate([emb(r), emb(col)], axis=-1)


def _stream(x3, pos, mod, first_row, h0, s5_ops, p):
    nb, seq, _ = x3.shape
    n = nb * seq
    nk = seq // S5_CHUNK
    x = x3.reshape(n, D_MODEL)
    per_seq_mod = first_row > 0

    def mod_row(i, tm):
        return first_row + (i * tm) // seq if per_seq_mod else 0

    us, uf, gt = _inproj(x, pos, mod, mod_row, p["norm1_g"], p["w_in"], 512)
    ys, fin = _s5(us, s5_ops, h0, nb, nk)
    yf = _fourier(uf, nb, seq, min(seq, 512))
    xs, h2_rows, ridx, rw = _merge(ys, yf, gt, x, pos, mod, mod_row, p["norm2_g"], p["merge_w"], 256)
    y_rows = _moe(h2_rows, _moe_plan(ridx, rw), p["w1"], p["w3"], p["w2"])
    out = _final(xs, y_rows, mod, mod_row, p["final_g"], 256)
    return out.reshape(nb, seq, D_MODEL), fin


def kernel(x_prompt, x_sample, state_s5_re, state_s5_im, c, c_ctx, w_ada, b_ada, norm1_g, norm2_g, w_in,
           lam_re, lam_im, log_dt, b_re, b_im, c_re, c_im, d_skip, w_glu, w_proj_s5, w_proj_f, w_out,
           w_router, router_bias, w1, w3, w2, ws1, ws3, ws2, final_norm_g):
    nb_ctx = x_prompt.shape[0]
    nb_lat, seq_lat, _ = x_sample.shape
    half = 2 * S5_STATE

    cond = jnp.concatenate([c_ctx[None], c, jnp.zeros((MOD_ROWS - 1 - nb_lat, D_MODEL), F32)], axis=0)
    mod = _adaln(cond, w_ada[0], b_ada[0]).reshape(MOD_ROWS, N_MOD, D_MODEL)

    s5_ops = _s5ops(lam_re[0], lam_im[0], log_dt[0], b_re[0], b_im[0], c_re[0], c_im[0], d_skip[0])
    p = dict(
        norm1_g=norm1_g[0][None], norm2_g=norm2_g[0][None], final_g=final_norm_g[None],
        w_in=w_in[0].astype(BF16), w1=w1[0], w3=w3[0], w2=w2[0],
        merge_w=(w_glu[0].astype(BF16), w_proj_s5[0].astype(BF16), w_proj_f[0].astype(BF16),
                 w_out[0].astype(BF16), w_router[0].T, router_bias[0][:, None],
                 ws1[0].astype(BF16), ws3[0].astype(BF16), ws2[0].astype(BF16)))

    def pack_state(sr, si):
        f = lambda a: a.astype(F32).transpose(2, 0, 1, 3).reshape(S5_GROUPS, a.shape[0], half)
        return jnp.concatenate([f(sr), f(si)], axis=-1)

    def unpack_state(fin, lo):
        nb = fin.shape[1]
        return fin[..., lo:lo + half].reshape(S5_GROUPS, nb, 2, S5_STATE).transpose(1, 2, 0, 3)[:, None]

    h0_ctx = jnp.zeros((S5_GROUPS, nb_ctx, 2 * half), F32)
    y_prompt, fin = _stream(x_prompt, None, mod, 0, h0_ctx, s5_ops, p)
    h0_lat = pack_state(state_s5_re[:, 0], state_s5_im[:, 0])
    y_sample, _ = _stream(x_sample, _grid_pos_embed(seq_lat), mod, 1, h0_lat, s5_ops, p)
    return (y_prompt, y_sample, unpack_state(fin, 0).astype(x_prompt.dtype),
            unpack_state(fin, half).astype(x_prompt.dtype))
```

```python
import functools
import math

import jax
import jax.numpy as jnp
from jax import lax
from jax.experimental import pallas as pl
from jax.experimental.pallas import tpu as pltpu

D_MODEL = 1024
GRID_W = 64
D_S5 = 768
S5_GROUP = 16
S5_GROUPS = 48
S5_STATE = 64
D_FNET = 256
FNET_GROUP = 64
N_EXPERTS = 64
TOP_K = 6
N_EXPERT_GROUPS = 8
EXPERTS_PER_GROUP = N_EXPERTS // N_EXPERT_GROUPS
TOPK_GROUPS = 4
D_EXPERT = 256
ROUTED_SCALE = 2.5
N_MOD = 6
EPS = 1e-6

S5_CHUNK = 16
S5_ROW = S5_CHUNK * S5_GROUP
MOD_ROWS = 8
ROUTE_ROWS = 8
LANES = 128
ROW_SUB = D_MODEL // LANES
VMEM_LIMIT = 56 * 1024 * 1024

BF16 = jnp.bfloat16
F32 = jnp.float32


def _params(sem, vmem=VMEM_LIMIT):
    return pltpu.CompilerParams(dimension_semantics=sem, vmem_limit_bytes=vmem)


def _rms(x, g):
    return x * lax.rsqrt(jnp.mean(x * x, axis=-1, keepdims=True) + EPS) * g


def _adaln_kernel(c_ref, w_ref, b_ref, o_ref):
    c = c_ref[...]
    o_ref[...] = jnp.dot(c * jax.nn.sigmoid(c), w_ref[...], precision=lax.Precision.HIGHEST,
                         preferred_element_type=F32) + b_ref[...]


def _adaln(cond, w_ada, b_ada):
    n_out = N_MOD * D_MODEL
    return pl.pallas_call(
        _adaln_kernel,
        out_shape=jax.ShapeDtypeStruct((MOD_ROWS, n_out), F32),
        grid=(N_MOD,),
        in_specs=[pl.BlockSpec((MOD_ROWS, D_MODEL), lambda i: (0, 0)),
                  pl.BlockSpec((D_MODEL, D_MODEL), lambda i: (0, i)),
                  pl.BlockSpec((1, D_MODEL), lambda i: (0, i))],
        out_specs=pl.BlockSpec((MOD_ROWS, D_MODEL), lambda i: (0, i)),
        compiler_params=_params(("arbitrary",)),
        name="adaln",
    )(cond, w_ada, b_ada.reshape(1, n_out))


def _inproj_kernel(has_pos, *refs):
    if has_pos:
        x_ref, pos_ref, mod_ref, g_ref, w_ref, us_ref, uf_ref, gt_ref = refs
        x = x_ref[...] + pos_ref[...]
    else:
        x_ref, mod_ref, g_ref, w_ref, us_ref, uf_ref, gt_ref = refs
        x = x_ref[...]
    m = mod_ref[0]
    h = _rms(x, g_ref[...]) * (1.0 + m[1:2, :]) + m[0:1, :]
    p = jnp.dot(h.astype(BF16), w_ref[...], preferred_element_type=F32)
    us_ref[...] = p[:, :D_S5]
    uf_ref[...] = p[:, D_S5:D_MODEL].astype(BF16)
    gt_ref[...] = jax.nn.sigmoid(p[:, D_MODEL:]).astype(BF16)


def _inproj(x, pos, mod, mod_row, norm_g, w_in_bf, tm):
    n = x.shape[0]
    has_pos = pos is not None
    row = lambda i: (i, 0)
    in_specs = [pl.BlockSpec((tm, D_MODEL), row)]
    args = [x]
    if has_pos:
        nper = pos.shape[0] // tm
        in_specs.append(pl.BlockSpec((tm, D_MODEL), lambda i: (i % nper, 0)))
        args.append(pos)
    in_specs += [pl.BlockSpec((1, N_MOD, D_MODEL), lambda i: (mod_row(i, tm), 0, 0)),
                 pl.BlockSpec((1, D_MODEL), lambda i: (0, 0)),
                 pl.BlockSpec(w_in_bf.shape, lambda i: (0, 0))]
    args += [mod, norm_g, w_in_bf]
    return pl.pallas_call(
        functools.partial(_inproj_kernel, has_pos),
        out_shape=(jax.ShapeDtypeStruct((n, D_S5), F32),
                   jax.ShapeDtypeStruct((n, D_FNET), BF16),
                   jax.ShapeDtypeStruct((n, 2 * D_MODEL), BF16)),
        grid=(n // tm,),
        in_specs=in_specs,
        out_specs=(pl.BlockSpec((tm, D_S5), row), pl.BlockSpec((tm, D_FNET), row),
                   pl.BlockSpec((tm, 2 * D_MODEL), row)),
        compiler_params=_params(("arbitrary",)),
        name="inproj",
    )(*args)


def _shift_lanes(x, k):
    if k == 0:
        return x
    z = jnp.zeros((x.shape[0], abs(k)), x.dtype)
    if k > 0:
        return jnp.concatenate([z, x[:, :x.shape[1] - k]], axis=1)
    return jnp.concatenate([x[:, -k:], z], axis=1)


def _s5ops_kernel(lam_re_ref, lam_im_ref, dt_ref, btr_ref, bti_ref, cr_ref, ci_ref, d_ref,
                  m_ref, wi_ref, wot_ref, coef_ref):
    hi = lax.Precision.HIGHEST
    lr = jnp.minimum(lam_re_ref[0], -1e-4)
    li = lam_im_ref[0]
    dt = jnp.exp(dt_ref[0])
    mag = jnp.exp(lr * dt)
    ar = mag * jnp.cos(li * dt)
    ai = mag * jnp.sin(li * dt)
    den = lr * lr + li * li
    nr = ar - 1.0
    qr = (nr * lr + ai * li) / den
    qi = (ai * lr - nr * li) / den
    pr, pi = [], []
    for n in range(S5_CHUNK + 1):
        pm = jnp.exp(float(n) * (lr * dt))
        pr.append(pm * jnp.cos(float(n) * (li * dt)))
        pi.append(pm * jnp.sin(float(n) * (li * dt)))
    bbr, bbi, car, cai = [], [], [], []
    for d in range(2):
        btr = btr_ref[0, d]
        bti = bti_ref[0, d]
        bbr.append(qr[d:d + 1] * btr - qi[d:d + 1] * bti)
        bbi.append(qr[d:d + 1] * bti + qi[d:d + 1] * btr)
        cr = cr_ref[0, d]
        ci = ci_ref[0, d]
        car.append([cr * pr[n][d:d + 1] - ci * pi[n][d:d + 1] for n in range(S5_CHUNK + 1)])
        cai.append([cr * pi[n][d:d + 1] + ci * pr[n][d:d + 1] for n in range(S5_CHUNK + 1)])

    def lag_kernels(d, order):
        a = jnp.concatenate([car[d][n] for n in order], axis=0)
        b = jnp.concatenate([cai[d][n] for n in order], axis=0)
        dn = (((1,), (1,)), ((), ()))
        return (lax.dot_general(bbr[d], a, dn, precision=hi, preferred_element_type=F32)
                - lax.dot_general(bbi[d], b, dn, precision=hi, preferred_element_type=F32))

    ktf = lag_kernels(0, range(S5_CHUNK))
    ktb = lag_kernels(1, range(S5_CHUNK - 1, -1, -1))
    row = lax.broadcasted_iota(jnp.int32, (S5_GROUP, S5_ROW), 0)
    lane = lax.broadcasted_iota(jnp.int32, (S5_GROUP, S5_ROW), 1)
    dcol = d_ref[0]
    for j in range(S5_CHUNK):
        rows = slice(j * S5_GROUP, (j + 1) * S5_GROUP)
        blk = _shift_lanes(ktf, S5_GROUP * j) + _shift_lanes(ktb, -S5_GROUP * (S5_CHUNK - 1 - j))
        blk = blk + jnp.where(lane == S5_GROUP * j + row, dcol, 0.0)
        m_ref[0, rows, :] = blk.astype(BF16)
        nf = S5_CHUNK - 1 - j
        wi = jnp.concatenate([pr[nf][0:1] * bbr[0] - pi[nf][0:1] * bbi[0],
                              pr[j][1:2] * bbr[1] - pi[j][1:2] * bbi[1],
                              pr[nf][0:1] * bbi[0] + pi[nf][0:1] * bbr[0],
                              pr[j][1:2] * bbi[1] + pi[j][1:2] * bbr[1]], axis=1)
        wi_ref[0, rows, :] = wi.astype(BF16)
        wot = jnp.concatenate([car[0][j + 1], car[1][S5_CHUNK - j],
                               -cai[0][j + 1], -cai[1][S5_CHUNK - j]], axis=1)
        wot_ref[0, rows, :] = wot.astype(BF16)
    coef_ref[0, 0:1, :] = jnp.concatenate([pr[S5_CHUNK][0:1], pr[S5_CHUNK][1:2]], axis=1)
    coef_ref[0, 1:2, :] = jnp.concatenate([pi[S5_CHUNK][0:1], pi[S5_CHUNK][1:2]], axis=1)


def _s5ops(lam_re, lam_im, log_dt, b_re, b_im, c_re, c_im, d_skip):
    g3 = lambda g: (g, 0, 0)
    g4 = lambda g: (g, 0, 0, 0)
    sw = lambda a: jnp.swapaxes(a.astype(F32), 0, 1)
    dt = jnp.broadcast_to(sw(log_dt)[..., None], (S5_GROUPS, 2, S5_STATE))
    args = (sw(lam_re), sw(lam_im), dt, sw(jnp.swapaxes(b_re, 2, 3)), sw(jnp.swapaxes(b_im, 2, 3)),
            sw(c_re), sw(c_im), d_skip.astype(F32).reshape(S5_GROUPS, S5_GROUP, 1))
    vec = pl.BlockSpec((1, 2, S5_STATE), g3)
    mat = pl.BlockSpec((1, 2, S5_GROUP, S5_STATE), g4)
    op = pl.BlockSpec((1, S5_ROW, S5_ROW), g3)
    return pl.pallas_call(
        _s5ops_kernel,
        out_shape=(jax.ShapeDtypeStruct((S5_GROUPS, S5_ROW, S5_ROW), BF16),) * 3
        + (jax.ShapeDtypeStruct((S5_GROUPS, 2, 2 * S5_STATE), F32),),
        grid=(S5_GROUPS,),
        in_specs=[vec, vec, vec, mat, mat, mat, mat, pl.BlockSpec((1, S5_GROUP, 1), g3)],
        out_specs=(op, op, op, pl.BlockSpec((1, 2, 2 * S5_STATE), g3)),
        compiler_params=_params(("arbitrary",)),
        name="s5ops",
    )(*args)


S5_BLOCK_GROUPS = LANES // S5_GROUP


S5_HALF_T = LANES // S5_GROUP


def _s5_perm():
    a = jnp.arange(S5_HALF_T * LANES, dtype=jnp.int32)
    dst = ((a % LANES) // S5_GROUP) * LANES + (a // LANES) * S5_GROUP + a % S5_GROUP
    return (dst[:, None] == a[None, :]).astype(BF16)


def _s5_kernel(nb, nk, u_ref, perm_ref, m_ref, wi_ref, wot_ref, coef_ref, h0_ref, y_ref, fin_ref,
               sr_ref, si_ref, efr_ref, ebr_ref, efi_ref, ebi_ref, uall_ref, yall_ref):
    gl = pl.program_id(1)
    rows = nb * nk
    half = 2 * S5_STATE

    @pl.when(gl == 0)
    def _():
        for hh in range(S5_CHUNK // S5_HALF_T):
            xcat = jnp.concatenate([u_ref[pl.ds(hh * S5_HALF_T + tt, rows, stride=S5_CHUNK), :].astype(BF16)
                                    for tt in range(S5_HALF_T)], axis=1)
            uh = jnp.dot(xcat, perm_ref[...], preferred_element_type=F32).astype(BF16)
            for g in range(S5_BLOCK_GROUPS):
                uall_ref[g, :, hh * LANES:(hh + 1) * LANES] = uh[:, g * LANES:(g + 1) * LANES]

    u = uall_ref[gl]
    s = jnp.dot(u, wi_ref[0], preferred_element_type=F32)
    sr_ref[...] = s[:, :half]
    si_ref[...] = s[:, half:]
    c_r = coef_ref[0, 0:1, :]
    c_i = coef_ref[0, 1:2, :]
    e_r = h0_ref[0, :, :half]
    e_i = h0_ref[0, :, half:]
    is_fwd = lax.broadcasted_iota(jnp.int32, (nb, half), 1) < S5_STATE
    for j in range(nk):
        rf = pl.ds(j, nb, stride=nk)
        rb = pl.ds(nk - 1 - j, nb, stride=nk)
        efr_ref[rf, :] = e_r
        ebr_ref[rb, :] = e_r
        efi_ref[rf, :] = e_i
        ebi_ref[rb, :] = e_i
        s_r = jnp.where(is_fwd, sr_ref[rf, :], sr_ref[rb, :])
        s_i = jnp.where(is_fwd, si_ref[rf, :], si_ref[rb, :])
        e_r, e_i = c_r * e_r - c_i * e_i + s_r, c_r * e_i + c_i * e_r + s_i
    fin_ref[0, :, :half] = e_r
    fin_ref[0, :, half:] = e_i
    y = jnp.dot(u, m_ref[0], preferred_element_type=F32)
    fwd_rows = lax.broadcasted_iota(jnp.int32, (rows, half), 1) < S5_STATE
    e = jnp.concatenate([jnp.where(fwd_rows, efr_ref[...], ebr_ref[...]),
                         jnp.where(fwd_rows, efi_ref[...], ebi_ref[...])], axis=1).astype(BF16)
    y = y + lax.dot_general(e, wot_ref[0], (((1,), (1,)), ((), ())), preferred_element_type=F32)
    yall_ref[gl] = y.astype(BF16)

    @pl.when(gl == S5_BLOCK_GROUPS - 1)
    def _():
        for hh in range(S5_CHUNK // S5_HALF_T):
            ycat = jnp.concatenate([yall_ref[g, :, hh * LANES:(hh + 1) * LANES]
                                    for g in range(S5_BLOCK_GROUPS)], axis=1)
            out = lax.dot_general(ycat, perm_ref[...], (((1,), (1,)), ((), ())), preferred_element_type=F32)
            for tt in range(S5_HALF_T):
                y_ref[pl.ds(hh * S5_HALF_T + tt, rows, stride=S5_CHUNK), :] = out[:, tt * LANES:(tt + 1) * LANES]


def _s5(u, ops, h0, nb, nk):
    m, w_in, w_out, coef = ops
    n = u.shape[0]
    rows = nb * nk
    perm = _s5_perm()
    g3 = lambda b, g: (b * S5_BLOCK_GROUPS + g, 0, 0)
    blk = lambda b, g: (0, b)
    return pl.pallas_call(
        functools.partial(_s5_kernel, nb, nk),
        out_shape=(jax.ShapeDtypeStruct((n, D_S5), F32),
                   jax.ShapeDtypeStruct((S5_GROUPS, nb, 4 * S5_STATE), F32)),
        grid=(S5_GROUPS // S5_BLOCK_GROUPS, S5_BLOCK_GROUPS),
        in_specs=[pl.BlockSpec((n, LANES), blk),
                  pl.BlockSpec(perm.shape, lambda b, g: (0, 0)),
                  pl.BlockSpec((1, S5_ROW, S5_ROW), g3),
                  pl.BlockSpec((1, S5_ROW, 4 * S5_STATE), g3),
                  pl.BlockSpec((1, 4 * S5_STATE, S5_ROW), g3),
                  pl.BlockSpec((1, 2, 2 * S5_STATE), g3),
                  pl.BlockSpec((1, nb, 4 * S5_STATE), g3)],
        out_specs=(pl.BlockSpec((n, LANES), blk),
                   pl.BlockSpec((1, nb, 4 * S5_STATE), g3)),
        scratch_shapes=[pltpu.VMEM((rows, 2 * S5_STATE), F32)] * 6 + [
            pltpu.VMEM((S5_BLOCK_GROUPS, rows, S5_ROW), BF16),
            pltpu.VMEM((S5_BLOCK_GROUPS, rows, S5_ROW), BF16)],
        compiler_params=_params(("arbitrary", "arbitrary")),
        name="s5",
    )(u, perm, m, w_in, w_out, coef, h0)


def _dft_tables(seq):
    k = jnp.arange(seq, dtype=jnp.int32)
    na = seq // FNET_GROUP
    ang_a = (2.0 * math.pi / na) * ((jnp.arange(na, dtype=jnp.int32)[:, None] * k[None, :]) % na).astype(F32)
    ang_b = (2.0 * math.pi / seq) * ((jnp.arange(FNET_GROUP, dtype=jnp.int32)[:, None] * k[None, :]) % seq).astype(F32)
    ca, sa = jnp.cos(ang_a)[:, None, :], jnp.sin(ang_a)[:, None, :]
    cb, sb = jnp.cos(ang_b)[None, :, :], jnp.sin(ang_b)[None, :, :]
    cos_jk = (ca * cb - sa * sb).reshape(seq, seq)
    sin_jk = (sa * cb + ca * sb).reshape(seq, seq)
    cs = jnp.concatenate([cos_jk, -sin_jk], axis=1).astype(BF16)
    c = jnp.arange(D_FNET, dtype=jnp.int32)
    same = (c[:, None] // FNET_GROUP) == (c[None, :] // FNET_GROUP)
    angc = (2.0 * math.pi / FNET_GROUP) * (((c[:, None] % FNET_GROUP) * (c[None, :] % FNET_GROUP))
                                           % FNET_GROUP).astype(F32)
    scale = 1.0 / math.sqrt(seq * FNET_GROUP)
    bdc = jnp.where(same, jnp.cos(angc) * scale, 0.0).astype(BF16)
    bds = jnp.where(same, jnp.sin(angc) * scale, 0.0).astype(BF16)
    return cs, bdc, bds


def _fourier_kernel(seq, z_ref, cs_ref, bdc_ref, bds_ref, o_ref, zz_ref):
    @pl.when(pl.program_id(1) == 0)
    def _():
        z = z_ref[...]
        zz_ref[0:seq, :] = jnp.dot(z, bdc_ref[...], preferred_element_type=F32).astype(BF16)
        zz_ref[seq:, :] = jnp.dot(z, bds_ref[...], preferred_element_type=F32).astype(BF16)

    o_ref[...] = jnp.dot(cs_ref[...], zz_ref[...], preferred_element_type=F32).astype(BF16)


def _fourier(z, nb, seq, tl):
    cs, bdc, bds = _dft_tables(seq)
    nt = seq // tl
    return pl.pallas_call(
        functools.partial(_fourier_kernel, seq),
        out_shape=jax.ShapeDtypeStruct(z.shape, BF16),
        grid=(nb, nt),
        in_specs=[pl.BlockSpec((seq, D_FNET), lambda b, i: (b, 0)),
                  pl.BlockSpec((tl, 2 * seq), lambda b, i: (i, 0)),
                  pl.BlockSpec((D_FNET, D_FNET), lambda b, i: (0, 0)),
                  pl.BlockSpec((D_FNET, D_FNET), lambda b, i: (0, 0))],
        out_specs=pl.BlockSpec((tl, D_FNET), lambda b, i: (b * nt + i, 0)),
        scratch_shapes=[pltpu.VMEM((2 * seq, D_FNET), BF16)],
        compiler_params=_params(("arbitrary", "arbitrary")),
        name="fourier",
    )(z, cs, bdc, bds)


def _first_argmax_mask(v, iota, size):
    m = jnp.max(v, axis=0, keepdims=True)
    first = jnp.min(jnp.where(v == m, iota, size), axis=0, keepdims=True)
    return iota == first


def _route(logits_t, bias_col):
    tm = logits_t.shape[1]
    neg = -jnp.inf
    s = jax.nn.sigmoid(logits_t)
    biased = s + bias_col
    io8 = lax.broadcasted_iota(jnp.int32, (EXPERTS_PER_GROUP, tm), 0)
    gs_rows = []
    for g in range(N_EXPERT_GROUPS):
        blk = biased[g * EXPERTS_PER_GROUP:(g + 1) * EXPERTS_PER_GROUP, :]
        m1 = jnp.max(blk, axis=0, keepdims=True)
        rest = jnp.where(_first_argmax_mask(blk, io8, EXPERTS_PER_GROUP), neg, blk)
        gs_rows.append(m1 + jnp.max(rest, axis=0, keepdims=True))
    gs = jnp.concatenate(gs_rows, axis=0)
    iog = lax.broadcasted_iota(jnp.int32, (N_EXPERT_GROUPS, tm), 0)
    gsel = jnp.zeros((N_EXPERT_GROUPS, tm), F32)
    for _ in range(TOPK_GROUPS):
        sel = _first_argmax_mask(gs, iog, N_EXPERT_GROUPS)
        gsel = jnp.where(sel, 1.0, gsel)
        gs = jnp.where(sel, neg, gs)
    emask = jnp.concatenate(
        [jnp.broadcast_to(gsel[g:g + 1, :], (EXPERTS_PER_GROUP, tm)) for g in range(N_EXPERT_GROUPS)], axis=0)
    v = jnp.where(emask > 0.0, biased, neg)
    ioe = lax.broadcasted_iota(jnp.int32, (N_EXPERTS, tm), 0)
    idx_rows, s_rows = [], []
    for _ in range(TOP_K):
        sel = _first_argmax_mask(v, ioe, N_EXPERTS)
        idx_rows.append(jnp.sum(jnp.where(sel, ioe, 0), axis=0, keepdims=True))
        s_rows.append(jnp.sum(jnp.where(sel, s, 0.0), axis=0, keepdims=True))
        v = jnp.where(sel, neg, v)
    denom = s_rows[0]
    for r in s_rows[1:]:
        denom = denom + r
    pad = ROUTE_ROWS - TOP_K
    idx = jnp.concatenate(idx_rows + [jnp.zeros((pad, tm), jnp.int32)], axis=0)
    w = jnp.concatenate([r / denom * ROUTED_SCALE for r in s_rows] + [jnp.zeros((pad, tm), F32)], axis=0)
    return idx, w


def _merge_kernel(has_pos, *refs):
    if has_pos:
        (ys_ref, yf_ref, gt_ref, x_ref, pos_ref, mod_ref, n2_ref, wglu_ref, wps_ref, wpf_ref, wout_ref,
         wrt_ref, rb_ref, ws1_ref, ws3_ref, ws2_ref, xs_ref, h2_ref, ridx_ref, rw_ref) = refs
        x = x_ref[...] + pos_ref[...]
    else:
        (ys_ref, yf_ref, gt_ref, x_ref, mod_ref, n2_ref, wglu_ref, wps_ref, wpf_ref, wout_ref,
         wrt_ref, rb_ref, ws1_ref, ws3_ref, ws2_ref, xs_ref, h2_ref, ridx_ref, rw_ref) = refs
        x = x_ref[...]
    m = mod_ref[0]
    g = jax.nn.gelu(ys_ref[...].astype(F32))
    a = g * jax.nn.sigmoid(jnp.dot(g.astype(BF16), wglu_ref[...], preferred_element_type=F32))
    pa = jnp.dot(a.astype(BF16), wps_ref[...], preferred_element_type=F32)
    pb = jnp.dot(yf_ref[...], wpf_ref[...], preferred_element_type=F32)
    gt = gt_ref[...].astype(F32)
    merged = gt[:, :D_MODEL] * pa + gt[:, D_MODEL:] * pb
    x1 = x + m[2:3, :] * jnp.dot(merged.astype(BF16), wout_ref[...], preferred_element_type=F32)
    h2 = _rms(x1, n2_ref[...]) * (1.0 + m[4:5, :]) + m[3:4, :]
    hb = h2.astype(BF16)
    for j in range(ROW_SUB):
        h2_ref[pl.ds(j, h2.shape[0], stride=ROW_SUB), :] = h2[:, j * LANES:(j + 1) * LANES]
    wr = wrt_ref[...]
    wr_hi = wr.astype(BF16)
    wr_lo = (wr - wr_hi.astype(F32)).astype(BF16)
    h_lo = (h2 - hb.astype(F32)).astype(BF16)
    dn = (((1,), (1,)), ((), ()))
    logits_t = (lax.dot_general(wr_hi, hb, dn, preferred_element_type=F32)
                + lax.dot_general(wr_hi, h_lo, dn, preferred_element_type=F32)
                + lax.dot_general(wr_lo, hb, dn, preferred_element_type=F32))
    ridx_ref[...], rw_ref[...] = _route(logits_t, rb_ref[...])
    s1 = jnp.dot(hb, ws1_ref[...], preferred_element_type=F32)
    s3 = jnp.dot(hb, ws3_ref[...], preferred_element_type=F32)
    shared = jnp.dot((s1 * jax.nn.sigmoid(s1) * s3).astype(BF16), ws2_ref[...], preferred_element_type=F32)
    xs_ref[...] = x1 + m[5:6, :] * shared


def _merge(ys, yf, gt, x, pos, mod, mod_row, n2, weights, tm):
    n = x.shape[0]
    has_pos = pos is not None
    row = lambda i: (i, 0)
    const = lambda a: pl.BlockSpec(a.shape, lambda i: (0,) * a.ndim)
    in_specs = [pl.BlockSpec((tm, D_S5), row), pl.BlockSpec((tm, D_FNET), row),
                pl.BlockSpec((tm, 2 * D_MODEL), row), pl.BlockSpec((tm, D_MODEL), row)]
    args = [ys, yf, gt, x]
    if has_pos:
        nper = pos.shape[0] // tm
        in_specs.append(pl.BlockSpec((tm, D_MODEL), lambda i: (i % nper, 0)))
        args.append(pos)
    in_specs += [pl.BlockSpec((1, N_MOD, D_MODEL), lambda i: (mod_row(i, tm), 0, 0)), const(n2)]
    args += [mod, n2]
    in_specs += [const(w) for w in weights]
    args += list(weights)
    return pl.pallas_call(
        functools.partial(_merge_kernel, has_pos),
        out_shape=(jax.ShapeDtypeStruct((n, D_MODEL), F32),
                   jax.ShapeDtypeStruct((n * ROW_SUB, LANES), F32),
                   jax.ShapeDtypeStruct((ROUTE_ROWS, n), jnp.int32),
                   jax.ShapeDtypeStruct((ROUTE_ROWS, n), F32)),
        grid=(n // tm,),
        in_specs=in_specs,
        out_specs=(pl.BlockSpec((tm, D_MODEL), row), pl.BlockSpec((tm * ROW_SUB, LANES), row),
                   pl.BlockSpec((ROUTE_ROWS, tm), lambda i: (0, i)),
                   pl.BlockSpec((ROUTE_ROWS, tm), lambda i: (0, i))),
        compiler_params=_params(("arbitrary",)),
        name="merge",
    )(*args)


MOE_SUB = 4096
MOE_TM = 128
MOE_TMAX = MOE_SUB * TOP_K // MOE_TM + N_EXPERTS
MOE_PAD = 2
MOE_TS = MOE_TMAX + 2 * MOE_PAD
MOE_DUMMY = 256
MOE_RMW = 16
ROW_TILE = MOE_TM * ROW_SUB


def _moe_plan(ridx, rw):
    n = ridx.shape[1]
    nsub = n // MOE_SUB
    npair = n * TOP_K
    t = jnp.arange(n, dtype=jnp.int32)
    key = ((t // MOE_SUB) * N_EXPERTS)[None] + ridx[:TOP_K]
    key = (key * MOE_SUB + (t % MOE_SUB)[None]).reshape(-1)
    skey, sw = lax.sort((key, rw[:TOP_K].reshape(-1)), num_keys=1)
    stok = jnp.concatenate([(skey % MOE_SUB) * ROW_SUB, jnp.zeros((MOE_TM,), jnp.int32)])
    sw_rows = jnp.concatenate([sw, jnp.zeros((MOE_TM,), F32)]).reshape((npair + MOE_TM) // LANES, 1, LANES)
    hits = ridx[:TOP_K].reshape(TOP_K, nsub, 1, MOE_SUB) == jnp.arange(N_EXPERTS, dtype=jnp.int32)[None, None, :, None]
    cnt = jnp.sum(hits.astype(jnp.int32), axis=(0, 3))
    poff = (jnp.cumsum(cnt.reshape(-1)) - cnt.reshape(-1)).reshape(nsub, N_EXPERTS)
    ntile = (cnt + MOE_TM - 1) // MOE_TM
    tcum = jnp.cumsum(ntile, axis=1)
    toff = tcum - ntile
    tstart = jnp.concatenate([toff, tcum[:, -1:]], axis=1).reshape(-1).astype(jnp.int32)
    j = jnp.arange(MOE_TS, dtype=jnp.int32) - MOE_PAD
    valid = (j[None] >= 0) & (j[None] < tcum[:, -1:])
    te = jnp.minimum(jnp.sum(j[None, :, None] >= tcum[:, None, :], axis=-1), N_EXPERTS - 1)
    pick = lambda a: jnp.take_along_axis(a, te, axis=1)
    first = (j[None] - pick(toff)) * MOE_TM
    p0 = jnp.where(valid, pick(poff) + first, 0).reshape(-1).astype(jnp.int32)
    nv = jnp.where(valid, jnp.minimum(pick(cnt) - first, MOE_TM), 0).reshape(-1).astype(jnp.int32)
    return tstart, p0, nv, stok, sw_rows


def _moe_kernel(ts_ref, p0_ref, nv_ref, tok_ref, sw_ref, src_ref, w1_ref, w3_ref, w2_ref, y_ref,
                xt_ref, xb_ref, act_ref, ot_ref, w1b_ref, w3b_ref, w2b_ref, slot_ref):
    sub = pl.program_id(0)
    e = pl.program_id(1)
    base = sub * MOE_TS + MOE_PAD
    first = ts_ref[sub * (N_EXPERTS + 1) + e]
    last = ts_ref[sub * (N_EXPERTS + 1) + e + 1]

    def gather(p0):
        for mi in range(MOE_TM):
            tok = pl.multiple_of(tok_ref[p0 + mi], ROW_SUB)
            xt_ref[mi * ROW_SUB:(mi + 1) * ROW_SUB, :] = src_ref[pl.ds(tok, ROW_SUB), :]
        for j in range(ROW_SUB):
            xb_ref[:, j * LANES:(j + 1) * LANES] = xt_ref[pl.ds(j, MOE_TM, stride=ROW_SUB), :].astype(BF16)

    def scatter(p0, nv, masked):
        for u in range(0, MOE_TM, MOE_RMW):
            new = []
            for i in range(MOE_RMW):
                tok = tok_ref[p0 + u + i]
                if masked:
                    tok = jnp.where(u + i < nv, tok, MOE_SUB * ROW_SUB)
                tok = pl.multiple_of(tok, ROW_SUB)
                new.append((tok, y_ref[pl.ds(tok, ROW_SUB), :]
                            + ot_ref[(u + i) * ROW_SUB:(u + i + 1) * ROW_SUB, :]))
            for tok, v in new:
                y_ref[pl.ds(tok, ROW_SUB), :] = v

    @pl.when(e == 0)
    def _():
        y_ref[...] = jnp.zeros_like(y_ref)
        ot_ref[...] = jnp.zeros_like(ot_ref)
        act_ref[...] = jnp.zeros_like(act_ref)
        w2b_ref[...] = jnp.zeros_like(w2b_ref)
        slot_ref[0] = 0
        gather(p0_ref[base])

    @pl.when(last > first)
    def _():
        slot_ref[0] = 1 - slot_ref[0]
        w1b_ref[...] = w1_ref[0].astype(BF16)
        w3b_ref[...] = w3_ref[0].astype(BF16)
        w2b_ref[slot_ref[0]] = w2_ref[0].astype(BF16)

    slot = slot_ref[0]

    def down_proj(w2_slot):
        o = jnp.dot(act_ref[...], w2b_ref[w2_slot], preferred_element_type=F32)
        for j in range(ROW_SUB):
            ot_ref[pl.ds(j, MOE_TM, stride=ROW_SUB), :] = o[:, j * LANES:(j + 1) * LANES]

    def step(i, masked):
        cur = base + i
        scatter(p0_ref[cur - 2], nv_ref[cur - 2], masked)
        down_proj(jnp.where(i > first, slot, 1 - slot))
        p0 = p0_ref[cur]
        nv = nv_ref[cur]
        x = xb_ref[...]
        a = jnp.dot(x, w1b_ref[...], preferred_element_type=F32)
        b = jnp.dot(x, w3b_ref[...], preferred_element_type=F32)
        r0 = p0 // LANES
        c = p0 % LANES
        lane = lax.broadcasted_iota(jnp.int32, (1, LANES), 1)
        rows = lax.broadcasted_iota(jnp.int32, (LANES, LANES), 0)
        cols = lax.broadcasted_iota(jnp.int32, (LANES, LANES), 1)
        gparts = []
        for hh in range(MOE_TM // LANES):
            ga = pltpu.roll(sw_ref[r0 + hh], LANES - c, axis=1)
            gb = pltpu.roll(sw_ref[r0 + hh + 1], LANES - c, axis=1)
            g = jnp.where(lane + hh * LANES < nv, jnp.where(lane < LANES - c, ga, gb), 0.0)
            gparts.append(jnp.sum(jnp.where(rows == cols, jnp.broadcast_to(g, (LANES, LANES)), 0.0),
                                  axis=1, keepdims=True))
        gcol = jnp.concatenate(gparts, axis=0)
        act_ref[...] = (a * jax.nn.sigmoid(a) * b * gcol).astype(BF16)
        gather(p0_ref[cur + 1])

    def body(i, carry):
        step(i, True)
        return carry

    lax.fori_loop(first, last, body, 0)

    @pl.when(e == N_EXPERTS - 1)
    def _():
        scatter(p0_ref[base + last - 2], nv_ref[base + last - 2], True)
        down_proj(slot)
        scatter(p0_ref[base + last - 1], nv_ref[base + last - 1], True)


def _moe(h2_rows, plan, w1, w3, w2):
    tstart, p0, nv, stok, sw_rows = plan
    nsub = h2_rows.shape[0] // (MOE_SUB * ROW_SUB)
    wmap = lambda s, e, ts, p0, nv: (e, 0, 0)
    sub2 = lambda s, e, ts, p0, nv: (s, 0)
    grid_spec = pltpu.PrefetchScalarGridSpec(
        num_scalar_prefetch=3,
        grid=(nsub, N_EXPERTS),
        in_specs=[pl.BlockSpec(memory_space=pltpu.SMEM),
                  pl.BlockSpec(sw_rows.shape, lambda s, e, ts, p0, nv: (0, 0, 0)),
                  pl.BlockSpec((MOE_SUB * ROW_SUB, LANES), sub2, pipeline_mode=pl.Buffered(1)),
                  pl.BlockSpec((1, D_MODEL, D_EXPERT), wmap),
                  pl.BlockSpec((1, D_MODEL, D_EXPERT), wmap),
                  pl.BlockSpec((1, D_EXPERT, D_MODEL), wmap)],
        out_specs=pl.BlockSpec(((MOE_SUB + MOE_DUMMY) * ROW_SUB, LANES), sub2, pipeline_mode=pl.Buffered(1)),
        scratch_shapes=[pltpu.VMEM((ROW_TILE, LANES), F32), pltpu.VMEM((MOE_TM, D_MODEL), BF16),
                        pltpu.VMEM((MOE_TM, D_EXPERT), BF16), pltpu.VMEM((ROW_TILE, LANES), F32),
                        pltpu.VMEM((D_MODEL, D_EXPERT), BF16), pltpu.VMEM((D_MODEL, D_EXPERT), BF16),
                        pltpu.VMEM((2, D_EXPERT, D_MODEL), BF16), pltpu.SMEM((1,), jnp.int32)])
    return pl.pallas_call(
        _moe_kernel,
        grid_spec=grid_spec,
        out_shape=jax.ShapeDtypeStruct((nsub * (MOE_SUB + MOE_DUMMY) * ROW_SUB, LANES), F32),
        compiler_params=_params(("arbitrary", "arbitrary")),
        name="moe",
    )(tstart, p0, nv, stok, sw_rows, h2_rows, w1, w3, w2)


def _final_kernel(xs_ref, y_ref, mod_ref, fg_ref, o_ref):
    tm = xs_ref.shape[0]
    y = jnp.concatenate([y_ref[pl.ds(j, tm, stride=ROW_SUB), :] for j in range(ROW_SUB)], axis=1)
    x2 = xs_ref[...] + mod_ref[0][5:6, :] * y
    o_ref[...] = _rms(x2, fg_ref[...])


def _final(xs, y_rows, mod, mod_row, fg, tm):
    n = xs.shape[0]
    per_sub = MOE_SUB // tm
    stride = (MOE_SUB + MOE_DUMMY) // tm
    row = lambda i: (i, 0)
    return pl.pallas_call(
        _final_kernel,
        out_shape=jax.ShapeDtypeStruct((n, D_MODEL), F32),
        grid=(n // tm,),
        in_specs=[pl.BlockSpec((tm, D_MODEL), row),
                  pl.BlockSpec((tm * ROW_SUB, LANES), lambda i: ((i // per_sub) * stride + i % per_sub, 0)),
                  pl.BlockSpec((1, N_MOD, D_MODEL), lambda i: (mod_row(i, tm), 0, 0)),
                  pl.BlockSpec((1, D_MODEL), lambda i: (0, 0))],
        out_specs=pl.BlockSpec((tm, D_MODEL), row),
        compiler_params=_params(("arbitrary",)),
        name="final",
    )(xs, y_rows, mod, fg)


def _grid_pos_embed(n_tokens):
    rows = n_tokens // GRID_W
    r, col = jnp.meshgrid(jnp.arange(rows, dtype=F32), jnp.arange(GRID_W, dtype=F32), indexing="ij")
    quarter = D_MODEL // 4
    omega = 1.0 / (10000.0 ** (jnp.arange(quarter, dtype=F32) / quarter))

    def emb(p):
        a = p.reshape(-1)[:, None] * omega
        return jnp.concatenate([jnp.sin(a), jnp.cos(a)], axis=-1)

    return jnp.concatenate([emb(r), emb(col)], axis=-1)


def _mixers(x3, pos, mod, first_row, h0, s5_ops, p):
    nb, seq, _ = x3.shape
    n = nb * seq
    nk = seq // S5_CHUNK
    x = x3.reshape(n, D_MODEL)
    per_seq_mod = first_row > 0

    def mod_row(i, tm):
        return first_row + (i * tm) // seq if per_seq_mod else 0

    us, uf, gt = _inproj(x, pos, mod, mod_row, p["norm1_g"], p["w_in"], 512)
    ys, fin = _s5(us, s5_ops, h0, nb, nk)
    yf = _fourier(uf, nb, seq, min(seq, 512))
    xs, h2_rows, ridx, rw = _merge(ys, yf, gt, x, pos, mod, mod_row, p["norm2_g"], p["merge_w"], 512)
    return xs, h2_rows, ridx, rw, fin, mod_row


def _plan_of_stream(plan, s, n_tokens):
    tstart, p0, nv, stok, sw_rows = plan
    nsub = n_tokens // MOE_SUB
    npair = n_tokens * TOP_K
    off = s * npair
    tiles = slice(s * nsub * MOE_TS, (s + 1) * nsub * MOE_TS)
    nv_s = nv[tiles]
    p0_s = jnp.where(nv_s > 0, p0[tiles] - off, 0)
    return (tstart[s * nsub * (N_EXPERTS + 1):(s + 1) * nsub * (N_EXPERTS + 1)], p0_s, nv_s,
            stok[off:off + npair + MOE_TM], sw_rows[off // LANES:(off + npair + MOE_TM) // LANES])


def kernel(x_prompt, x_sample, state_s5_re, state_s5_im, c, c_ctx, w_ada, b_ada, norm1_g, norm2_g, w_in,
           lam_re, lam_im, log_dt, b_re, b_im, c_re, c_im, d_skip, w_glu, w_proj_s5, w_proj_f, w_out,
           w_router, router_bias, w1, w3, w2, ws1, ws3, ws2, final_norm_g):
    nb_ctx = x_prompt.shape[0]
    nb_lat, seq_lat, _ = x_sample.shape
    half = 2 * S5_STATE

    cond = jnp.concatenate([c_ctx[None], c, jnp.zeros((MOD_ROWS - 1 - nb_lat, D_MODEL), F32)], axis=0)
    mod = _adaln(cond, w_ada[0], b_ada[0]).reshape(MOD_ROWS, N_MOD, D_MODEL)

    s5_ops = _s5ops(lam_re[0], lam_im[0], log_dt[0], b_re[0], b_im[0], c_re[0], c_im[0], d_skip[0])
    p = dict(
        norm1_g=norm1_g[0][None], norm2_g=norm2_g[0][None], final_g=final_norm_g[None],
        w_in=w_in[0].astype(BF16), w1=w1[0], w3=w3[0], w2=w2[0],
        merge_w=(w_glu[0].astype(BF16), w_proj_s5[0].astype(BF16), w_proj_f[0].astype(BF16),
                 w_out[0].astype(BF16), w_router[0].T, router_bias[0][:, None],
                 ws1[0].astype(BF16), ws3[0].astype(BF16), ws2[0].astype(BF16)))

    def pack_state(sr, si):
        f = lambda a: a.astype(F32).transpose(2, 0, 1, 3).reshape(S5_GROUPS, a.shape[0], half)
        return jnp.concatenate([f(sr), f(si)], axis=-1)

    def unpack_state(fin, lo):
        nb = fin.shape[1]
        return fin[..., lo:lo + half].reshape(S5_GROUPS, nb, 2, S5_STATE).transpose(1, 2, 0, 3)[:, None]

    h0_ctx = jnp.zeros((S5_GROUPS, nb_ctx, 2 * half), F32)
    h0_lat = pack_state(state_s5_re[:, 0], state_s5_im[:, 0])
    streams = [(x_prompt, _mixers(x_prompt, None, mod, 0, h0_ctx, s5_ops, p)),
               (x_sample, _mixers(x_sample, _grid_pos_embed(seq_lat), mod, 1, h0_lat, s5_ops, p))]
    n_tokens = streams[0][1][0].shape[0]
    plan = _moe_plan(jnp.concatenate([m[2] for _, m in streams], axis=1),
                     jnp.concatenate([m[3] for _, m in streams], axis=1))
    outs = []
    for s, (x3, (xs, h2_rows, _, _, _, mod_row)) in enumerate(streams):
        y_rows = _moe(h2_rows, _plan_of_stream(plan, s, n_tokens), p["w1"], p["w3"], p["w2"])
        outs.append(_final(xs, y_rows, mod, mod_row, p["final_g"], 256).reshape(x3.shape))
    fin = streams[0][1][4]
    return (outs[0], outs[1], unpack_state(fin, 0).astype(x_prompt.dtype),
            unpack_state(fin, half).astype(x_prompt.dtype))
```

```python
import functools
import math

import jax
import jax.numpy as jnp
from jax import lax
from jax.experimental import pallas as pl
from jax.experimental.pallas import tpu as pltpu

D_MODEL = 1024
GRID_W = 64
D_S5 = 768
S5_GROUP = 16
S5_GROUPS = 48
S5_STATE = 64
D_FNET = 256
FNET_GROUP = 64
N_EXPERTS = 64
TOP_K = 6
N_EXPERT_GROUPS = 8
EXPERTS_PER_GROUP = N_EXPERTS // N_EXPERT_GROUPS
TOPK_GROUPS = 4
D_EXPERT = 256
ROUTED_SCALE = 2.5
N_MOD = 6
EPS = 1e-6

S5_CHUNK = 16
S5_ROW = S5_CHUNK * S5_GROUP
MOD_ROWS = 8
ROUTE_ROWS = 8
LANES = 128
ROW_SUB = D_MODEL // LANES
VMEM_LIMIT = 56 * 1024 * 1024

BF16 = jnp.bfloat16
F32 = jnp.float32


def _params(sem, vmem=VMEM_LIMIT):
    return pltpu.CompilerParams(dimension_semantics=sem, vmem_limit_bytes=vmem)


def _rms(x, g):
    return x * lax.rsqrt(jnp.mean(x * x, axis=-1, keepdims=True) + EPS) * g


def _adaln_kernel(c_ref, w_ref, b_ref, o_ref):
    c = c_ref[...]
    o_ref[...] = jnp.dot(c * jax.nn.sigmoid(c), w_ref[...], precision=lax.Precision.HIGHEST,
                         preferred_element_type=F32) + b_ref[...]


def _adaln(cond, w_ada, b_ada):
    n_out = N_MOD * D_MODEL
    return pl.pallas_call(
        _adaln_kernel,
        out_shape=jax.ShapeDtypeStruct((MOD_ROWS, n_out), F32),
        grid=(N_MOD,),
        in_specs=[pl.BlockSpec((MOD_ROWS, D_MODEL), lambda i: (0, 0)),
                  pl.BlockSpec((D_MODEL, D_MODEL), lambda i: (0, i)),
                  pl.BlockSpec((1, D_MODEL), lambda i: (0, i))],
        out_specs=pl.BlockSpec((MOD_ROWS, D_MODEL), lambda i: (0, i)),
        compiler_params=_params(("arbitrary",)),
        name="adaln",
    )(cond, w_ada, b_ada.reshape(1, n_out))


def _inproj_kernel(has_pos, *refs):
    if has_pos:
        x_ref, pos_ref, mod_ref, g_ref, w_ref, us_ref, uf_ref, gt_ref = refs
        x = x_ref[...] + pos_ref[...]
    else:
        x_ref, mod_ref, g_ref, w_ref, us_ref, uf_ref, gt_ref = refs
        x = x_ref[...]
    m = mod_ref[0]
    h = _rms(x, g_ref[...]) * (1.0 + m[1:2, :]) + m[0:1, :]
    p = jnp.dot(h.astype(BF16), w_ref[...], preferred_element_type=F32)
    us_ref[...] = p[:, :D_S5]
    uf_ref[...] = p[:, D_S5:D_MODEL].astype(BF16)
    gt_ref[...] = jax.nn.sigmoid(p[:, D_MODEL:]).astype(BF16)


def _inproj(x, pos, mod, mod_row, norm_g, w_in_bf, tm):
    n = x.shape[0]
    has_pos = pos is not None
    row = lambda i: (i, 0)
    in_specs = [pl.BlockSpec((tm, D_MODEL), row)]
    args = [x]
    if has_pos:
        nper = pos.shape[0] // tm
        in_specs.append(pl.BlockSpec((tm, D_MODEL), lambda i: (i % nper, 0)))
        args.append(pos)
    in_specs += [pl.BlockSpec((1, N_MOD, D_MODEL), lambda i: (mod_row(i, tm), 0, 0)),
                 pl.BlockSpec((1, D_MODEL), lambda i: (0, 0)),
                 pl.BlockSpec(w_in_bf.shape, lambda i: (0, 0))]
    args += [mod, norm_g, w_in_bf]
    return pl.pallas_call(
        functools.partial(_inproj_kernel, has_pos),
        out_shape=(jax.ShapeDtypeStruct((n, D_S5), F32),
                   jax.ShapeDtypeStruct((n, D_FNET), BF16),
                   jax.ShapeDtypeStruct((n, 2 * D_MODEL), BF16)),
        grid=(n // tm,),
        in_specs=in_specs,
        out_specs=(pl.BlockSpec((tm, D_S5), row), pl.BlockSpec((tm, D_FNET), row),
                   pl.BlockSpec((tm, 2 * D_MODEL), row)),
        compiler_params=_params(("arbitrary",)),
        name="inproj",
    )(*args)


def _shift_lanes(x, k):
    if k == 0:
        return x
    z = jnp.zeros((x.shape[0], abs(k)), x.dtype)
    if k > 0:
        return jnp.concatenate([z, x[:, :x.shape[1] - k]], axis=1)
    return jnp.concatenate([x[:, -k:], z], axis=1)


def _s5ops_kernel(lam_re_ref, lam_im_ref, dt_ref, btr_ref, bti_ref, cr_ref, ci_ref, d_ref,
                  m_ref, wi_ref, wot_ref, coef_ref):
    hi = lax.Precision.HIGHEST
    lr = jnp.minimum(lam_re_ref[0], -1e-4)
    li = lam_im_ref[0]
    dt = jnp.exp(dt_ref[0])
    mag = jnp.exp(lr * dt)
    ar = mag * jnp.cos(li * dt)
    ai = mag * jnp.sin(li * dt)
    den = lr * lr + li * li
    nr = ar - 1.0
    qr = (nr * lr + ai * li) / den
    qi = (ai * lr - nr * li) / den
    pr, pi = [], []
    for n in range(S5_CHUNK + 1):
        pm = jnp.exp(float(n) * (lr * dt))
        pr.append(pm * jnp.cos(float(n) * (li * dt)))
        pi.append(pm * jnp.sin(float(n) * (li * dt)))
    bbr, bbi, car, cai = [], [], [], []
    for d in range(2):
        btr = btr_ref[0, d]
        bti = bti_ref[0, d]
        bbr.append(qr[d:d + 1] * btr - qi[d:d + 1] * bti)
        bbi.append(qr[d:d + 1] * bti + qi[d:d + 1] * btr)
        cr = cr_ref[0, d]
        ci = ci_ref[0, d]
        car.append([cr * pr[n][d:d + 1] - ci * pi[n][d:d + 1] for n in range(S5_CHUNK + 1)])
        cai.append([cr * pi[n][d:d + 1] + ci * pr[n][d:d + 1] for n in range(S5_CHUNK + 1)])

    def lag_kernels(d, order):
        a = jnp.concatenate([car[d][n] for n in order], axis=0)
        b = jnp.concatenate([cai[d][n] for n in order], axis=0)
        dn = (((1,), (1,)), ((), ()))
        return (lax.dot_general(bbr[d], a, dn, precision=hi, preferred_element_type=F32)
                - lax.dot_general(bbi[d], b, dn, precision=hi, preferred_element_type=F32))

    ktf = lag_kernels(0, range(S5_CHUNK))
    ktb = lag_kernels(1, range(S5_CHUNK - 1, -1, -1))
    row = lax.broadcasted_iota(jnp.int32, (S5_GROUP, S5_ROW), 0)
    lane = lax.broadcasted_iota(jnp.int32, (S5_GROUP, S5_ROW), 1)
    dcol = d_ref[0]
    for j in range(S5_CHUNK):
        rows = slice(j * S5_GROUP, (j + 1) * S5_GROUP)
        blk = _shift_lanes(ktf, S5_GROUP * j) + _shift_lanes(ktb, -S5_GROUP * (S5_CHUNK - 1 - j))
        blk = blk + jnp.where(lane == S5_GROUP * j + row, dcol, 0.0)
        m_ref[0, rows, :] = blk.astype(BF16)
        nf = S5_CHUNK - 1 - j
        wi = jnp.concatenate([pr[nf][0:1] * bbr[0] - pi[nf][0:1] * bbi[0],
                              pr[j][1:2] * bbr[1] - pi[j][1:2] * bbi[1],
                              pr[nf][0:1] * bbi[0] + pi[nf][0:1] * bbr[0],
                              pr[j][1:2] * bbi[1] + pi[j][1:2] * bbr[1]], axis=1)
        wi_ref[0, rows, :] = wi.astype(BF16)
        wot = jnp.concatenate([car[0][j + 1], car[1][S5_CHUNK - j],
                               -cai[0][j + 1], -cai[1][S5_CHUNK - j]], axis=1)
        wot_ref[0, rows, :] = wot.astype(BF16)
    coef_ref[0, 0:1, :] = jnp.concatenate([pr[S5_CHUNK][0:1], pr[S5_CHUNK][1:2]], axis=1)
    coef_ref[0, 1:2, :] = jnp.concatenate([pi[S5_CHUNK][0:1], pi[S5_CHUNK][1:2]], axis=1)


def _s5ops(lam_re, lam_im, log_dt, b_re, b_im, c_re, c_im, d_skip):
    g3 = lambda g: (g, 0, 0)
    g4 = lambda g: (g, 0, 0, 0)
    sw = lambda a: jnp.swapaxes(a.astype(F32), 0, 1)
    dt = jnp.broadcast_to(sw(log_dt)[..., None], (S5_GROUPS, 2, S5_STATE))
    args = (sw(lam_re), sw(lam_im), dt, sw(jnp.swapaxes(b_re, 2, 3)), sw(jnp.swapaxes(b_im, 2, 3)),
            sw(c_re), sw(c_im), d_skip.astype(F32).reshape(S5_GROUPS, S5_GROUP, 1))
    vec = pl.BlockSpec((1, 2, S5_STATE), g3)
    mat = pl.BlockSpec((1, 2, S5_GROUP, S5_STATE), g4)
    op = pl.BlockSpec((1, S5_ROW, S5_ROW), g3)
    return pl.pallas_call(
        _s5ops_kernel,
        out_shape=(jax.ShapeDtypeStruct((S5_GROUPS, S5_ROW, S5_ROW), BF16),) * 3
        + (jax.ShapeDtypeStruct((S5_GROUPS, 2, 2 * S5_STATE), F32),),
        grid=(S5_GROUPS,),
        in_specs=[vec, vec, vec, mat, mat, mat, mat, pl.BlockSpec((1, S5_GROUP, 1), g3)],
        out_specs=(op, op, op, pl.BlockSpec((1, 2, 2 * S5_STATE), g3)),
        compiler_params=_params(("arbitrary",)),
        name="s5ops",
    )(*args)


S5_BLOCK_GROUPS = LANES // S5_GROUP


S5_HALF_T = LANES // S5_GROUP


def _s5_perm():
    a = jnp.arange(S5_HALF_T * LANES, dtype=jnp.int32)
    dst = ((a % LANES) // S5_GROUP) * LANES + (a // LANES) * S5_GROUP + a % S5_GROUP
    return (dst[:, None] == a[None, :]).astype(BF16)


def _s5_kernel(nb, nk, u_ref, perm_ref, m_ref, wi_ref, wot_ref, coef_ref, h0_ref, y_ref, fin_ref,
               sr_ref, si_ref, efr_ref, ebr_ref, efi_ref, ebi_ref, uall_ref, yall_ref):
    gl = pl.program_id(1)
    rows = nb * nk
    half = 2 * S5_STATE

    @pl.when(gl == 0)
    def _():
        for hh in range(S5_CHUNK // S5_HALF_T):
            xcat = jnp.concatenate([u_ref[pl.ds(hh * S5_HALF_T + tt, rows, stride=S5_CHUNK), :].astype(BF16)
                                    for tt in range(S5_HALF_T)], axis=1)
            uh = jnp.dot(xcat, perm_ref[...], preferred_element_type=F32).astype(BF16)
            for g in range(S5_BLOCK_GROUPS):
                uall_ref[g, :, hh * LANES:(hh + 1) * LANES] = uh[:, g * LANES:(g + 1) * LANES]

    u = uall_ref[gl]
    s = jnp.dot(u, wi_ref[0], preferred_element_type=F32)
    sr_ref[...] = s[:, :half]
    si_ref[...] = s[:, half:]
    c_r = coef_ref[0, 0:1, :]
    c_i = coef_ref[0, 1:2, :]
    e_r = h0_ref[0, :, :half]
    e_i = h0_ref[0, :, half:]
    is_fwd = lax.broadcasted_iota(jnp.int32, (nb, half), 1) < S5_STATE
    for j in range(nk):
        rf = pl.ds(j, nb, stride=nk)
        rb = pl.ds(nk - 1 - j, nb, stride=nk)
        efr_ref[rf, :] = e_r
        ebr_ref[rb, :] = e_r
        efi_ref[rf, :] = e_i
        ebi_ref[rb, :] = e_i
        s_r = jnp.where(is_fwd, sr_ref[rf, :], sr_ref[rb, :])
        s_i = jnp.where(is_fwd, si_ref[rf, :], si_ref[rb, :])
        e_r, e_i = c_r * e_r - c_i * e_i + s_r, c_r * e_i + c_i * e_r + s_i
    fin_ref[0, :, :half] = e_r
    fin_ref[0, :, half:] = e_i
    y = jnp.dot(u, m_ref[0], preferred_element_type=F32)
    fwd_rows = lax.broadcasted_iota(jnp.int32, (rows, half), 1) < S5_STATE
    e = jnp.concatenate([jnp.where(fwd_rows, efr_ref[...], ebr_ref[...]),
                         jnp.where(fwd_rows, efi_ref[...], ebi_ref[...])], axis=1).astype(BF16)
    y = y + lax.dot_general(e, wot_ref[0], (((1,), (1,)), ((), ())), preferred_element_type=F32)
    yall_ref[gl] = y.astype(BF16)

    @pl.when(gl == S5_BLOCK_GROUPS - 1)
    def _():
        for hh in range(S5_CHUNK // S5_HALF_T):
            ycat = jnp.concatenate([yall_ref[g, :, hh * LANES:(hh + 1) * LANES]
                                    for g in range(S5_BLOCK_GROUPS)], axis=1)
            out = lax.dot_general(ycat, perm_ref[...], (((1,), (1,)), ((), ())), preferred_element_type=F32)
            for tt in range(S5_HALF_T):
                y_ref[pl.ds(hh * S5_HALF_T + tt, rows, stride=S5_CHUNK), :] = out[:, tt * LANES:(tt + 1) * LANES]


def _s5(u, ops, h0, nb, nk):
    m, w_in, w_out, coef = ops
    n = u.shape[0]
    rows = nb * nk
    perm = _s5_perm()
    g3 = lambda b, g: (b * S5_BLOCK_GROUPS + g, 0, 0)
    blk = lambda b, g: (0, b)
    return pl.pallas_call(
        functools.partial(_s5_kernel, nb, nk),
        out_shape=(jax.ShapeDtypeStruct((n, D_S5), F32),
                   jax.ShapeDtypeStruct((S5_GROUPS, nb, 4 * S5_STATE), F32)),
        grid=(S5_GROUPS // S5_BLOCK_GROUPS, S5_BLOCK_GROUPS),
        in_specs=[pl.BlockSpec((n, LANES), blk),
                  pl.BlockSpec(perm.shape, lambda b, g: (0, 0)),
                  pl.BlockSpec((1, S5_ROW, S5_ROW), g3),
                  pl.BlockSpec((1, S5_ROW, 4 * S5_STATE), g3),
                  pl.BlockSpec((1, 4 * S5_STATE, S5_ROW), g3),
                  pl.BlockSpec((1, 2, 2 * S5_STATE), g3),
                  pl.BlockSpec((1, nb, 4 * S5_STATE), g3)],
        out_specs=(pl.BlockSpec((n, LANES), blk),
                   pl.BlockSpec((1, nb, 4 * S5_STATE), g3)),
        scratch_shapes=[pltpu.VMEM((rows, 2 * S5_STATE), F32)] * 6 + [
            pltpu.VMEM((S5_BLOCK_GROUPS, rows, S5_ROW), BF16),
            pltpu.VMEM((S5_BLOCK_GROUPS, rows, S5_ROW), BF16)],
        compiler_params=_params(("arbitrary", "arbitrary")),
        name="s5",
    )(u, perm, m, w_in, w_out, coef, h0)


def _dft_tables(seq):
    k = jnp.arange(seq, dtype=jnp.int32)
    na = seq // FNET_GROUP
    ang_a = (2.0 * math.pi / na) * ((jnp.arange(na, dtype=jnp.int32)[:, None] * k[None, :]) % na).astype(F32)
    ang_b = (2.0 * math.pi / seq) * ((jnp.arange(FNET_GROUP, dtype=jnp.int32)[:, None] * k[None, :]) % seq).astype(F32)
    ca, sa = jnp.cos(ang_a)[:, None, :], jnp.sin(ang_a)[:, None, :]
    cb, sb = jnp.cos(ang_b)[None, :, :], jnp.sin(ang_b)[None, :, :]
    cos_jk = (ca * cb - sa * sb).reshape(seq, seq)
    sin_jk = (sa * cb + ca * sb).reshape(seq, seq)
    cs = jnp.concatenate([cos_jk, -sin_jk], axis=1).astype(BF16)
    c = jnp.arange(D_FNET, dtype=jnp.int32)
    same = (c[:, None] // FNET_GROUP) == (c[None, :] // FNET_GROUP)
    angc = (2.0 * math.pi / FNET_GROUP) * (((c[:, None] % FNET_GROUP) * (c[None, :] % FNET_GROUP))
                                           % FNET_GROUP).astype(F32)
    scale = 1.0 / math.sqrt(seq * FNET_GROUP)
    bdc = jnp.where(same, jnp.cos(angc) * scale, 0.0).astype(BF16)
    bds = jnp.where(same, jnp.sin(angc) * scale, 0.0).astype(BF16)
    return cs, bdc, bds


def _fourier_kernel(seq, z_ref, cs_ref, bdc_ref, bds_ref, o_ref, zz_ref):
    @pl.when(pl.program_id(1) == 0)
    def _():
        z = z_ref[...]
        zz_ref[0:seq, :] = jnp.dot(z, bdc_ref[...], preferred_element_type=F32).astype(BF16)
        zz_ref[seq:, :] = jnp.dot(z, bds_ref[...], preferred_element_type=F32).astype(BF16)

    o_ref[...] = jnp.dot(cs_ref[...], zz_ref[...], preferred_element_type=F32).astype(BF16)


def _fourier(z, nb, seq, tl):
    cs, bdc, bds = _dft_tables(seq)
    nt = seq // tl
    return pl.pallas_call(
        functools.partial(_fourier_kernel, seq),
        out_shape=jax.ShapeDtypeStruct(z.shape, BF16),
        grid=(nb, nt),
        in_specs=[pl.BlockSpec((seq, D_FNET), lambda b, i: (b, 0)),
                  pl.BlockSpec((tl, 2 * seq), lambda b, i: (i, 0)),
                  pl.BlockSpec((D_FNET, D_FNET), lambda b, i: (0, 0)),
                  pl.BlockSpec((D_FNET, D_FNET), lambda b, i: (0, 0))],
        out_specs=pl.BlockSpec((tl, D_FNET), lambda b, i: (b * nt + i, 0)),
        scratch_shapes=[pltpu.VMEM((2 * seq, D_FNET), BF16)],
        compiler_params=_params(("arbitrary", "arbitrary")),
        name="fourier",
    )(z, cs, bdc, bds)


def _first_argmax_mask(v, iota, size):
    m = jnp.max(v, axis=0, keepdims=True)
    first = jnp.min(jnp.where(v == m, iota, size), axis=0, keepdims=True)
    return iota == first


def _route(logits_t, bias_col):
    tm = logits_t.shape[1]
    neg = -jnp.inf
    s = jax.nn.sigmoid(logits_t)
    biased = s + bias_col
    io8 = lax.broadcasted_iota(jnp.int32, (EXPERTS_PER_GROUP, tm), 0)
    gs_rows = []
    for g in range(N_EXPERT_GROUPS):
        blk = biased[g * EXPERTS_PER_GROUP:(g + 1) * EXPERTS_PER_GROUP, :]
        m1 = jnp.max(blk, axis=0, keepdims=True)
        rest = jnp.where(_first_argmax_mask(blk, io8, EXPERTS_PER_GROUP), neg, blk)
        gs_rows.append(m1 + jnp.max(rest, axis=0, keepdims=True))
    gs = jnp.concatenate(gs_rows, axis=0)
    iog = lax.broadcasted_iota(jnp.int32, (N_EXPERT_GROUPS, tm), 0)
    gsel = jnp.zeros((N_EXPERT_GROUPS, tm), F32)
    for _ in range(TOPK_GROUPS):
        sel = _first_argmax_mask(gs, iog, N_EXPERT_GROUPS)
        gsel = jnp.where(sel, 1.0, gsel)
        gs = jnp.where(sel, neg, gs)
    emask = jnp.concatenate(
        [jnp.broadcast_to(gsel[g:g + 1, :], (EXPERTS_PER_GROUP, tm)) for g in range(N_EXPERT_GROUPS)], axis=0)
    v = jnp.where(emask > 0.0, biased, neg)
    ioe = lax.broadcasted_iota(jnp.int32, (N_EXPERTS, tm), 0)
    idx_rows, s_rows = [], []
    for _ in range(TOP_K):
        sel = _first_argmax_mask(v, ioe, N_EXPERTS)
        idx_rows.append(jnp.sum(jnp.where(sel, ioe, 0), axis=0, keepdims=True))
        s_rows.append(jnp.sum(jnp.where(sel, s, 0.0), axis=0, keepdims=True))
        v = jnp.where(sel, neg, v)
    denom = s_rows[0]
    for r in s_rows[1:]:
        denom = denom + r
    pad = ROUTE_ROWS - TOP_K
    idx = jnp.concatenate(idx_rows + [jnp.zeros((pad, tm), jnp.int32)], axis=0)
    w = jnp.concatenate([r / denom * ROUTED_SCALE for r in s_rows] + [jnp.zeros((pad, tm), F32)], axis=0)
    return idx, w


def _merge_kernel(has_pos, *refs):
    if has_pos:
        (ys_ref, yf_ref, gt_ref, x_ref, pos_ref, mod_ref, n2_ref, wglu_ref, wps_ref, wpf_ref, wout_ref,
         wrt_ref, rb_ref, ws1_ref, ws3_ref, ws2_ref, xs_ref, h2_ref, ridx_ref, rw_ref) = refs
        x = x_ref[...] + pos_ref[...]
    else:
        (ys_ref, yf_ref, gt_ref, x_ref, mod_ref, n2_ref, wglu_ref, wps_ref, wpf_ref, wout_ref,
         wrt_ref, rb_ref, ws1_ref, ws3_ref, ws2_ref, xs_ref, h2_ref, ridx_ref, rw_ref) = refs
        x = x_ref[...]
    m = mod_ref[0]
    g = jax.nn.gelu(ys_ref[...].astype(F32))
    a = g * jax.nn.sigmoid(jnp.dot(g.astype(BF16), wglu_ref[...], preferred_element_type=F32))
    pa = jnp.dot(a.astype(BF16), wps_ref[...], preferred_element_type=F32)
    pb = jnp.dot(yf_ref[...], wpf_ref[...], preferred_element_type=F32)
    gt = gt_ref[...].astype(F32)
    merged = gt[:, :D_MODEL] * pa + gt[:, D_MODEL:] * pb
    x1 = x + m[2:3, :] * jnp.dot(merged.astype(BF16), wout_ref[...], preferred_element_type=F32)
    h2 = _rms(x1, n2_ref[...]) * (1.0 + m[4:5, :]) + m[3:4, :]
    hb = h2.astype(BF16)
    for j in range(ROW_SUB):
        h2_ref[pl.ds(j, h2.shape[0], stride=ROW_SUB), :] = h2[:, j * LANES:(j + 1) * LANES]
    wr = wrt_ref[...]
    wr_hi = wr.astype(BF16)
    wr_lo = (wr - wr_hi.astype(F32)).astype(BF16)
    h_lo = (h2 - hb.astype(F32)).astype(BF16)
    dn = (((1,), (1,)), ((), ()))
    logits_t = (lax.dot_general(wr_hi, hb, dn, preferred_element_type=F32)
                + lax.dot_general(wr_hi, h_lo, dn, preferred_element_type=F32)
                + lax.dot_general(wr_lo, hb, dn, preferred_element_type=F32))
    ridx_ref[...], rw_ref[...] = _route(logits_t, rb_ref[...])
    s1 = jnp.dot(hb, ws1_ref[...], preferred_element_type=F32)
    s3 = jnp.dot(hb, ws3_ref[...], preferred_element_type=F32)
    shared = jnp.dot((s1 * jax.nn.sigmoid(s1) * s3).astype(BF16), ws2_ref[...], preferred_element_type=F32)
    xs_ref[...] = x1 + m[5:6, :] * shared


def _merge(ys, yf, gt, x, pos, mod, mod_row, n2, weights, tm):
    n = x.shape[0]
    has_pos = pos is not None
    row = lambda i: (i, 0)
    const = lambda a: pl.BlockSpec(a.shape, lambda i: (0,) * a.ndim)
    in_specs = [pl.BlockSpec((tm, D_S5), row), pl.BlockSpec((tm, D_FNET), row),
                pl.BlockSpec((tm, 2 * D_MODEL), row), pl.BlockSpec((tm, D_MODEL), row)]
    args = [ys, yf, gt, x]
    if has_pos:
        nper = pos.shape[0] // tm
        in_specs.append(pl.BlockSpec((tm, D_MODEL), lambda i: (i % nper, 0)))
        args.append(pos)
    in_specs += [pl.BlockSpec((1, N_MOD, D_MODEL), lambda i: (mod_row(i, tm), 0, 0)), const(n2)]
    args += [mod, n2]
    in_specs += [const(w) for w in weights]
    args += list(weights)
    return pl.pallas_call(
        functools.partial(_merge_kernel, has_pos),
        out_shape=(jax.ShapeDtypeStruct((n, D_MODEL), F32),
                   jax.ShapeDtypeStruct((n * ROW_SUB, LANES), F32),
                   jax.ShapeDtypeStruct((ROUTE_ROWS, n), jnp.int32),
                   jax.ShapeDtypeStruct((ROUTE_ROWS, n), F32)),
        grid=(n // tm,),
        in_specs=in_specs,
        out_specs=(pl.BlockSpec((tm, D_MODEL), row), pl.BlockSpec((tm * ROW_SUB, LANES), row),
                   pl.BlockSpec((ROUTE_ROWS, tm), lambda i: (0, i)),
                   pl.BlockSpec((ROUTE_ROWS, tm), lambda i: (0, i))),
        compiler_params=_params(("arbitrary",)),
        name="merge",
    )(*args)


MOE_SUB = 4096
MOE_TM = 128
MOE_TMAX = MOE_SUB * TOP_K // MOE_TM + N_EXPERTS
MOE_PAD = 2
MOE_TS = MOE_TMAX + 2 * MOE_PAD
MOE_DUMMY = 256
MOE_RMW = 16
ROW_TILE = MOE_TM * ROW_SUB


def _moe_plan(ridx, rw):
    n = ridx.shape[1]
    nsub = n // MOE_SUB
    npair = n * TOP_K
    by_sub = lambda a: a[:TOP_K].reshape(TOP_K, nsub, MOE_SUB).transpose(1, 0, 2).reshape(nsub, TOP_K * MOE_SUB)
    tok = jnp.tile(jnp.arange(MOE_SUB, dtype=jnp.int32), TOP_K)[None]
    skey, sw = lax.sort((by_sub(ridx) * MOE_SUB + tok, by_sub(rw)), dimension=1, num_keys=1)
    sw = sw.reshape(-1)
    stok = jnp.concatenate([(skey.reshape(-1) % MOE_SUB) * ROW_SUB, jnp.zeros((MOE_TM,), jnp.int32)])
    sw_rows = jnp.concatenate([sw, jnp.zeros((MOE_TM,), F32)]).reshape((npair + MOE_TM) // LANES, 1, LANES)
    hits = ridx[:TOP_K].reshape(TOP_K, nsub, 1, MOE_SUB) == jnp.arange(N_EXPERTS, dtype=jnp.int32)[None, None, :, None]
    cnt = jnp.sum(hits.astype(jnp.int32), axis=(0, 3))
    poff = (jnp.cumsum(cnt.reshape(-1)) - cnt.reshape(-1)).reshape(nsub, N_EXPERTS)
    ntile = (cnt + MOE_TM - 1) // MOE_TM
    tcum = jnp.cumsum(ntile, axis=1)
    toff = tcum - ntile
    tstart = jnp.concatenate([toff, tcum[:, -1:]], axis=1).reshape(-1).astype(jnp.int32)
    j = jnp.arange(MOE_TS, dtype=jnp.int32) - MOE_PAD
    valid = (j[None] >= 0) & (j[None] < tcum[:, -1:])
    te = jnp.minimum(jnp.sum(j[None, :, None] >= tcum[:, None, :], axis=-1), N_EXPERTS - 1)
    pick = lambda a: jnp.take_along_axis(a, te, axis=1)
    first = (j[None] - pick(toff)) * MOE_TM
    p0 = jnp.where(valid, pick(poff) + first, 0).reshape(-1).astype(jnp.int32)
    nv = jnp.where(valid, jnp.minimum(pick(cnt) - first, MOE_TM), 0).reshape(-1).astype(jnp.int32)
    return tstart, p0, nv, stok, sw_rows


def _moe_kernel(ts_ref, p0_ref, nv_ref, tok_ref, sw_ref, src_ref, w1_ref, w3_ref, w2_ref, y_ref,
                xt_ref, xb_ref, act_ref, ot_ref, w1b_ref, w3b_ref, w2b_ref, slot_ref):
    sub = pl.program_id(0)
    e = pl.program_id(1)
    base = sub * MOE_TS + MOE_PAD
    first = ts_ref[sub * (N_EXPERTS + 1) + e]
    last = ts_ref[sub * (N_EXPERTS + 1) + e + 1]

    def gather(p0):
        for mi in range(MOE_TM):
            tok = pl.multiple_of(tok_ref[p0 + mi], ROW_SUB)
            xt_ref[mi * ROW_SUB:(mi + 1) * ROW_SUB, :] = src_ref[pl.ds(tok, ROW_SUB), :]
        for j in range(ROW_SUB):
            xb_ref[:, j * LANES:(j + 1) * LANES] = xt_ref[pl.ds(j, MOE_TM, stride=ROW_SUB), :].astype(BF16)

    def scatter(p0, nv, masked):
        for u in range(0, MOE_TM, MOE_RMW):
            new = []
            for i in range(MOE_RMW):
                tok = tok_ref[p0 + u + i]
                if masked:
                    tok = jnp.where(u + i < nv, tok, MOE_SUB * ROW_SUB)
                tok = pl.multiple_of(tok, ROW_SUB)
                new.append((tok, y_ref[pl.ds(tok, ROW_SUB), :]
                            + ot_ref[(u + i) * ROW_SUB:(u + i + 1) * ROW_SUB, :]))
            for tok, v in new:
                y_ref[pl.ds(tok, ROW_SUB), :] = v

    @pl.when(e == 0)
    def _():
        y_ref[...] = jnp.zeros_like(y_ref)
        ot_ref[...] = jnp.zeros_like(ot_ref)
        act_ref[...] = jnp.zeros_like(act_ref)
        w2b_ref[...] = jnp.zeros_like(w2b_ref)
        slot_ref[0] = 0
        gather(p0_ref[base])

    @pl.when(last > first)
    def _():
        slot_ref[0] = 1 - slot_ref[0]
        w1b_ref[...] = w1_ref[0].astype(BF16)
        w3b_ref[...] = w3_ref[0].astype(BF16)
        w2b_ref[slot_ref[0]] = w2_ref[0].astype(BF16)

    slot = slot_ref[0]

    def down_proj(w2_slot):
        o = jnp.dot(act_ref[...], w2b_ref[w2_slot], preferred_element_type=F32)
        for j in range(ROW_SUB):
            ot_ref[pl.ds(j, MOE_TM, stride=ROW_SUB), :] = o[:, j * LANES:(j + 1) * LANES]

    def step(i, masked):
        cur = base + i
        scatter(p0_ref[cur - 2], nv_ref[cur - 2], masked)
        down_proj(jnp.where(i > first, slot, 1 - slot))
        p0 = p0_ref[cur]
        nv = nv_ref[cur]
        x = xb_ref[...]
        a = jnp.dot(x, w1b_ref[...], preferred_element_type=F32)
        b = jnp.dot(x, w3b_ref[...], preferred_element_type=F32)
        r0 = p0 // LANES
        c = p0 % LANES
        lane = lax.broadcasted_iota(jnp.int32, (1, LANES), 1)
        rows = lax.broadcasted_iota(jnp.int32, (LANES, LANES), 0)
        cols = lax.broadcasted_iota(jnp.int32, (LANES, LANES), 1)
        gparts = []
        for hh in range(MOE_TM // LANES):
            ga = pltpu.roll(sw_ref[r0 + hh], LANES - c, axis=1)
            gb = pltpu.roll(sw_ref[r0 + hh + 1], LANES - c, axis=1)
            g = jnp.where(lane + hh * LANES < nv, jnp.where(lane < LANES - c, ga, gb), 0.0)
            gparts.append(jnp.sum(jnp.where(rows == cols, jnp.broadcast_to(g, (LANES, LANES)), 0.0),
                                  axis=1, keepdims=True))
        gcol = jnp.concatenate(gparts, axis=0)
        act_ref[...] = (a * jax.nn.sigmoid(a) * b * gcol).astype(BF16)
        gather(p0_ref[cur + 1])

    def body(i, carry):
        step(i, True)
        return carry

    lax.fori_loop(first, last, body, 0)

    @pl.when(e == N_EXPERTS - 1)
    def _():
        scatter(p0_ref[base + last - 2], nv_ref[base + last - 2], True)
        down_proj(slot)
        scatter(p0_ref[base + last - 1], nv_ref[base + last - 1], True)


def _moe(h2_rows, plan, w1, w3, w2):
    tstart, p0, nv, stok, sw_rows = plan
    nsub = h2_rows.shape[0] // (MOE_SUB * ROW_SUB)
    wmap = lambda s, e, ts, p0, nv: (e, 0, 0)
    sub2 = lambda s, e, ts, p0, nv: (s, 0)
    grid_spec = pltpu.PrefetchScalarGridSpec(
        num_scalar_prefetch=3,
        grid=(nsub, N_EXPERTS),
        in_specs=[pl.BlockSpec(memory_space=pltpu.SMEM),
                  pl.BlockSpec(sw_rows.shape, lambda s, e, ts, p0, nv: (0, 0, 0)),
                  pl.BlockSpec((MOE_SUB * ROW_SUB, LANES), sub2, pipeline_mode=pl.Buffered(1)),
                  pl.BlockSpec((1, D_MODEL, D_EXPERT), wmap),
                  pl.BlockSpec((1, D_MODEL, D_EXPERT), wmap),
                  pl.BlockSpec((1, D_EXPERT, D_MODEL), wmap)],
        out_specs=pl.BlockSpec(((MOE_SUB + MOE_DUMMY) * ROW_SUB, LANES), sub2, pipeline_mode=pl.Buffered(1)),
        scratch_shapes=[pltpu.VMEM((ROW_TILE, LANES), F32), pltpu.VMEM((MOE_TM, D_MODEL), BF16),
                        pltpu.VMEM((MOE_TM, D_EXPERT), BF16), pltpu.VMEM((ROW_TILE, LANES), F32),
                        pltpu.VMEM((D_MODEL, D_EXPERT), BF16), pltpu.VMEM((D_MODEL, D_EXPERT), BF16),
                        pltpu.VMEM((2, D_EXPERT, D_MODEL), BF16), pltpu.SMEM((1,), jnp.int32)])
    return pl.pallas_call(
        _moe_kernel,
        grid_spec=grid_spec,
        out_shape=jax.ShapeDtypeStruct((nsub * (MOE_SUB + MOE_DUMMY) * ROW_SUB, LANES), F32),
        compiler_params=_params(("arbitrary", "arbitrary")),
        name="moe",
    )(tstart, p0, nv, stok, sw_rows, h2_rows, w1, w3, w2)


def _final_kernel(xs_ref, y_ref, mod_ref, fg_ref, o_ref):
    tm = xs_ref.shape[0]
    y = jnp.concatenate([y_ref[pl.ds(j, tm, stride=ROW_SUB), :] for j in range(ROW_SUB)], axis=1)
    x2 = xs_ref[...] + mod_ref[0][5:6, :] * y
    o_ref[...] = _rms(x2, fg_ref[...])


def _final(xs, y_rows, mod, mod_row, fg, tm):
    n = xs.shape[0]
    per_sub = MOE_SUB // tm
    stride = (MOE_SUB + MOE_DUMMY) // tm
    row = lambda i: (i, 0)
    return pl.pallas_call(
        _final_kernel,
        out_shape=jax.ShapeDtypeStruct((n, D_MODEL), F32),
        grid=(n // tm,),
        in_specs=[pl.BlockSpec((tm, D_MODEL), row),
                  pl.BlockSpec((tm * ROW_SUB, LANES), lambda i: ((i // per_sub) * stride + i % per_sub, 0)),
                  pl.BlockSpec((1, N_MOD, D_MODEL), lambda i: (mod_row(i, tm), 0, 0)),
                  pl.BlockSpec((1, D_MODEL), lambda i: (0, 0))],
        out_specs=pl.BlockSpec((tm, D_MODEL), row),
        compiler_params=_params(("arbitrary",)),
        name="final",
    )(xs, y_rows, mod, fg)


def _grid_pos_embed(n_tokens):
    rows = n_tokens // GRID_W
    r, col = jnp.meshgrid(jnp.arange(rows, dtype=F32), jnp.arange(GRID_W, dtype=F32), indexing="ij")
    quarter = D_MODEL // 4
    omega = 1.0 / (10000.0 ** (jnp.arange(quarter, dtype=F32) / quarter))

    def emb(p):
        a = p.reshape(-1)[:, None] * omega
        return jnp.concatenate([jnp.sin(a), jnp.cos(a)], axis=-1)

    return jnp.concatenate([emb(r), emb(col)], axis=-1)


def _mixers(x3, pos, mod, first_row, h0, s5_ops, p):
    nb, seq, _ = x3.shape
    n = nb * seq
    nk = seq // S5_CHUNK
    x = x3.reshape(n, D_MODEL)
    per_seq_mod = first_row > 0

    def mod_row(i, tm):
        return first_row + (i * tm) // seq if per_seq_mod else 0

    us, uf, gt = _inproj(x, pos, mod, mod_row, p["norm1_g"], p["w_in"], 512)
    ys, fin = _s5(us, s5_ops, h0, nb, nk)
    yf = _fourier(uf, nb, seq, min(seq, 512))
    xs, h2_rows, ridx, rw = _merge(ys, yf, gt, x, pos, mod, mod_row, p["norm2_g"], p["merge_w"], 512)
    return xs, h2_rows, ridx, rw, fin, mod_row


def _plan_of_stream(plan, s, n_tokens):
    tstart, p0, nv, stok, sw_rows = plan
    nsub = n_tokens // MOE_SUB
    npair = n_tokens * TOP_K
    off = s * npair
    tiles = slice(s * nsub * MOE_TS, (s + 1) * nsub * MOE_TS)
    nv_s = nv[tiles]
    p0_s = jnp.where(nv_s > 0, p0[tiles] - off, 0)
    return (tstart[s * nsub * (N_EXPERTS + 1):(s + 1) * nsub * (N_EXPERTS + 1)], p0_s, nv_s,
            stok[off:off + npair + MOE_TM], sw_rows[off // LANES:(off + npair + MOE_TM) // LANES])


def kernel(x_prompt, x_sample, state_s5_re, state_s5_im, c, c_ctx, w_ada, b_ada, norm1_g, norm2_g, w_in,
           lam_re, lam_im, log_dt, b_re, b_im, c_re, c_im, d_skip, w_glu, w_proj_s5, w_proj_f, w_out,
           w_router, router_bias, w1, w3, w2, ws1, ws3, ws2, final_norm_g):
    nb_ctx = x_prompt.shape[0]
    nb_lat, seq_lat, _ = x_sample.shape
    half = 2 * S5_STATE

    cond = jnp.concatenate([c_ctx[None], c, jnp.zeros((MOD_ROWS - 1 - nb_lat, D_MODEL), F32)], axis=0)
    mod = _adaln(cond, w_ada[0], b_ada[0]).reshape(MOD_ROWS, N_MOD, D_MODEL)

    s5_ops = _s5ops(lam_re[0], lam_im[0], log_dt[0], b_re[0], b_im[0], c_re[0], c_im[0], d_skip[0])
    p = dict(
        norm1_g=norm1_g[0][None], norm2_g=norm2_g[0][None], final_g=final_norm_g[None],
        w_in=w_in[0].astype(BF16), w1=w1[0], w3=w3[0], w2=w2[0],
        merge_w=(w_glu[0].astype(BF16), w_proj_s5[0].astype(BF16), w_proj_f[0].astype(BF16),
                 w_out[0].astype(BF16), w_router[0].T, router_bias[0][:, None],
                 ws1[0].astype(BF16), ws3[0].astype(BF16), ws2[0].astype(BF16)))

    def pack_state(sr, si):
        f = lambda a: a.astype(F32).transpose(2, 0, 1, 3).reshape(S5_GROUPS, a.shape[0], half)
        return jnp.concatenate([f(sr), f(si)], axis=-1)

    def unpack_state(fin, lo):
        nb = fin.shape[1]
        return fin[..., lo:lo + half].reshape(S5_GROUPS, nb, 2, S5_STATE).transpose(1, 2, 0, 3)[:, None]

    h0_ctx = jnp.zeros((S5_GROUPS, nb_ctx, 2 * half), F32)
    h0_lat = pack_state(state_s5_re[:, 0], state_s5_im[:, 0])
    streams = [(x_prompt, _mixers(x_prompt, None, mod, 0, h0_ctx, s5_ops, p)),
               (x_sample, _mixers(x_sample, _grid_pos_embed(seq_lat), mod, 1, h0_lat, s5_ops, p))]
    n_tokens = streams[0][1][0].shape[0]
    plan = _moe_plan(jnp.concatenate([m[2] for _, m in streams], axis=1),
                     jnp.concatenate([m[3] for _, m in streams], axis=1))
    outs = []
    for s, (x3, (xs, h2_rows, _, _, _, mod_row)) in enumerate(streams):
        y_rows = _moe(h2_rows, _plan_of_stream(plan, s, n_tokens), p["w1"], p["w3"], p["w2"])
        outs.append(_final(xs, y_rows, mod, mod_row, p["final_g"], 256).reshape(x3.shape))
    fin = streams[0][1][4]
    return (outs[0], outs[1], unpack_state(fin, 0).astype(x_prompt.dtype),
            unpack_state(fin, half).astype(x_prompt.dtype))
```

```python
import functools
import math

import jax
import jax.numpy as jnp
from jax import lax
from jax.experimental import pallas as pl
from jax.experimental.pallas import tpu as pltpu

D_MODEL = 1024
GRID_W = 64
D_S5 = 768
S5_GROUP = 16
S5_GROUPS = 48
S5_STATE = 64
D_FNET = 256
FNET_GROUP = 64
N_EXPERTS = 64
TOP_K = 6
N_EXPERT_GROUPS = 8
EXPERTS_PER_GROUP = N_EXPERTS // N_EXPERT_GROUPS
TOPK_GROUPS = 4
D_EXPERT = 256
ROUTED_SCALE = 2.5
N_MOD = 6
EPS = 1e-6

S5_CHUNK = 16
S5_ROW = S5_CHUNK * S5_GROUP
MOD_ROWS = 8
ROUTE_ROWS = 8
LANES = 128
ROW_SUB = D_MODEL // LANES
VMEM_LIMIT = 56 * 1024 * 1024

BF16 = jnp.bfloat16
F32 = jnp.float32


def _params(sem, vmem=VMEM_LIMIT):
    return pltpu.CompilerParams(dimension_semantics=sem, vmem_limit_bytes=vmem)


def _rms(x, g):
    return x * lax.rsqrt(jnp.mean(x * x, axis=-1, keepdims=True) + EPS) * g


def _adaln_kernel(c_ref, w_ref, b_ref, o_ref):
    c = c_ref[...]
    o_ref[...] = jnp.dot(c * jax.nn.sigmoid(c), w_ref[...], precision=lax.Precision.HIGHEST,
                         preferred_element_type=F32) + b_ref[...]


def _adaln(cond, w_ada, b_ada):
    n_out = N_MOD * D_MODEL
    return pl.pallas_call(
        _adaln_kernel,
        out_shape=jax.ShapeDtypeStruct((MOD_ROWS, n_out), F32),
        grid=(N_MOD,),
        in_specs=[pl.BlockSpec((MOD_ROWS, D_MODEL), lambda i: (0, 0)),
                  pl.BlockSpec((D_MODEL, D_MODEL), lambda i: (0, i)),
                  pl.BlockSpec((1, D_MODEL), lambda i: (0, i))],
        out_specs=pl.BlockSpec((MOD_ROWS, D_MODEL), lambda i: (0, i)),
        compiler_params=_params(("arbitrary",)),
        name="adaln",
    )(cond, w_ada, b_ada.reshape(1, n_out))


def _inproj_kernel(has_pos, *refs):
    if has_pos:
        x_ref, pos_ref, mod_ref, g_ref, w_ref, us_ref, uf_ref, gt_ref = refs
        x = x_ref[...] + pos_ref[...]
    else:
        x_ref, mod_ref, g_ref, w_ref, us_ref, uf_ref, gt_ref = refs
        x = x_ref[...]
    m = mod_ref[0]
    h = _rms(x, g_ref[...]) * (1.0 + m[1:2, :]) + m[0:1, :]
    p = jnp.dot(h.astype(BF16), w_ref[...], preferred_element_type=F32)
    us_ref[...] = p[:, :D_S5]
    uf_ref[...] = p[:, D_S5:D_MODEL].astype(BF16)
    gt_ref[...] = jax.nn.sigmoid(p[:, D_MODEL:]).astype(BF16)


def _inproj(x, pos, mod, mod_row, norm_g, w_in_bf, tm):
    n = x.shape[0]
    has_pos = pos is not None
    row = lambda i: (i, 0)
    in_specs = [pl.BlockSpec((tm, D_MODEL), row)]
    args = [x]
    if has_pos:
        nper = pos.shape[0] // tm
        in_specs.append(pl.BlockSpec((tm, D_MODEL), lambda i: (i % nper, 0)))
        args.append(pos)
    in_specs += [pl.BlockSpec((1, N_MOD, D_MODEL), lambda i: (mod_row(i, tm), 0, 0)),
                 pl.BlockSpec((1, D_MODEL), lambda i: (0, 0)),
                 pl.BlockSpec(w_in_bf.shape, lambda i: (0, 0))]
    args += [mod, norm_g, w_in_bf]
    return pl.pallas_call(
        functools.partial(_inproj_kernel, has_pos),
        out_shape=(jax.ShapeDtypeStruct((n, D_S5), F32),
                   jax.ShapeDtypeStruct((n, D_FNET), BF16),
                   jax.ShapeDtypeStruct((n, 2 * D_MODEL), BF16)),
        grid=(n // tm,),
        in_specs=in_specs,
        out_specs=(pl.BlockSpec((tm, D_S5), row), pl.BlockSpec((tm, D_FNET), row),
                   pl.BlockSpec((tm, 2 * D_MODEL), row)),
        compiler_params=_params(("arbitrary",)),
        name="inproj",
    )(*args)


def _shift_lanes(x, k):
    if k == 0:
        return x
    z = jnp.zeros((x.shape[0], abs(k)), x.dtype)
    if k > 0:
        return jnp.concatenate([z, x[:, :x.shape[1] - k]], axis=1)
    return jnp.concatenate([x[:, -k:], z], axis=1)


def _s5ops_kernel(lam_re_ref, lam_im_ref, dt_ref, btr_ref, bti_ref, cr_ref, ci_ref, d_ref,
                  m_ref, wi_ref, wot_ref, coef_ref):
    hi = lax.Precision.HIGHEST
    lr = jnp.minimum(lam_re_ref[0], -1e-4)
    li = lam_im_ref[0]
    dt = jnp.exp(dt_ref[0])
    mag = jnp.exp(lr * dt)
    ar = mag * jnp.cos(li * dt)
    ai = mag * jnp.sin(li * dt)
    den = lr * lr + li * li
    nr = ar - 1.0
    qr = (nr * lr + ai * li) / den
    qi = (ai * lr - nr * li) / den
    pr, pi = [], []
    for n in range(S5_CHUNK + 1):
        pm = jnp.exp(float(n) * (lr * dt))
        pr.append(pm * jnp.cos(float(n) * (li * dt)))
        pi.append(pm * jnp.sin(float(n) * (li * dt)))
    bbr, bbi, car, cai = [], [], [], []
    for d in range(2):
        btr = btr_ref[0, d]
        bti = bti_ref[0, d]
        bbr.append(qr[d:d + 1] * btr - qi[d:d + 1] * bti)
        bbi.append(qr[d:d + 1] * bti + qi[d:d + 1] * btr)
        cr = cr_ref[0, d]
        ci = ci_ref[0, d]
        car.append([cr * pr[n][d:d + 1] - ci * pi[n][d:d + 1] for n in range(S5_CHUNK + 1)])
        cai.append([cr * pi[n][d:d + 1] + ci * pr[n][d:d + 1] for n in range(S5_CHUNK + 1)])

    def lag_kernels(d, order):
        a = jnp.concatenate([car[d][n] for n in order], axis=0)
        b = jnp.concatenate([cai[d][n] for n in order], axis=0)
        dn = (((1,), (1,)), ((), ()))
        return (lax.dot_general(bbr[d], a, dn, precision=hi, preferred_element_type=F32)
                - lax.dot_general(bbi[d], b, dn, precision=hi, preferred_element_type=F32))

    ktf = lag_kernels(0, range(S5_CHUNK))
    ktb = lag_kernels(1, range(S5_CHUNK - 1, -1, -1))
    row = lax.broadcasted_iota(jnp.int32, (S5_GROUP, S5_ROW), 0)
    lane = lax.broadcasted_iota(jnp.int32, (S5_GROUP, S5_ROW), 1)
    dcol = d_ref[0]
    for j in range(S5_CHUNK):
        rows = slice(j * S5_GROUP, (j + 1) * S5_GROUP)
        blk = _shift_lanes(ktf, S5_GROUP * j) + _shift_lanes(ktb, -S5_GROUP * (S5_CHUNK - 1 - j))
        blk = blk + jnp.where(lane == S5_GROUP * j + row, dcol, 0.0)
        m_ref[0, rows, :] = blk.astype(BF16)
        nf = S5_CHUNK - 1 - j
        wi = jnp.concatenate([pr[nf][0:1] * bbr[0] - pi[nf][0:1] * bbi[0],
                              pr[j][1:2] * bbr[1] - pi[j][1:2] * bbi[1],
                              pr[nf][0:1] * bbi[0] + pi[nf][0:1] * bbr[0],
                              pr[j][1:2] * bbi[1] + pi[j][1:2] * bbr[1]], axis=1)
        wi_ref[0, rows, :] = wi.astype(BF16)
        wot = jnp.concatenate([car[0][j + 1], car[1][S5_CHUNK - j],
                               -cai[0][j + 1], -cai[1][S5_CHUNK - j]], axis=1)
        wot_ref[0, rows, :] = wot.astype(BF16)
    coef_ref[0, 0:1, :] = jnp.concatenate([pr[S5_CHUNK][0:1], pr[S5_CHUNK][1:2]], axis=1)
    coef_ref[0, 1:2, :] = jnp.concatenate([pi[S5_CHUNK][0:1], pi[S5_CHUNK][1:2]], axis=1)


def _s5ops(lam_re, lam_im, log_dt, b_re, b_im, c_re, c_im, d_skip):
    g3 = lambda g: (g, 0, 0)
    g4 = lambda g: (g, 0, 0, 0)
    sw = lambda a: jnp.swapaxes(a.astype(F32), 0, 1)
    dt = jnp.broadcast_to(sw(log_dt)[..., None], (S5_GROUPS, 2, S5_STATE))
    args = (sw(lam_re), sw(lam_im), dt, sw(jnp.swapaxes(b_re, 2, 3)), sw(jnp.swapaxes(b_im, 2, 3)),
            sw(c_re), sw(c_im), d_skip.astype(F32).reshape(S5_GROUPS, S5_GROUP, 1))
    vec = pl.BlockSpec((1, 2, S5_STATE), g3)
    mat = pl.BlockSpec((1, 2, S5_GROUP, S5_STATE), g4)
    op = pl.BlockSpec((1, S5_ROW, S5_ROW), g3)
    return pl.pallas_call(
        _s5ops_kernel,
        out_shape=(jax.ShapeDtypeStruct((S5_GROUPS, S5_ROW, S5_ROW), BF16),) * 3
        + (jax.ShapeDtypeStruct((S5_GROUPS, 2, 2 * S5_STATE), F32),),
        grid=(S5_GROUPS,),
        in_specs=[vec, vec, vec, mat, mat, mat, mat, pl.BlockSpec((1, S5_GROUP, 1), g3)],
        out_specs=(op, op, op, pl.BlockSpec((1, 2, 2 * S5_STATE), g3)),
        compiler_params=_params(("arbitrary",)),
        name="s5ops",
    )(*args)


S5_BLOCK_GROUPS = LANES // S5_GROUP


S5_HALF_T = LANES // S5_GROUP


def _s5_perm():
    a = jnp.arange(S5_HALF_T * LANES, dtype=jnp.int32)
    dst = ((a % LANES) // S5_GROUP) * LANES + (a // LANES) * S5_GROUP + a % S5_GROUP
    return (dst[:, None] == a[None, :]).astype(BF16)


def _s5_kernel(nb, nk, u_ref, perm_ref, m_ref, wi_ref, wot_ref, coef_ref, h0_ref, y_ref, fin_ref,
               sr_ref, si_ref, efr_ref, ebr_ref, efi_ref, ebi_ref, uall_ref, yall_ref):
    gl = pl.program_id(1)
    rows = nb * nk
    half = 2 * S5_STATE

    @pl.when(gl == 0)
    def _():
        for hh in range(S5_CHUNK // S5_HALF_T):
            xcat = jnp.concatenate([u_ref[pl.ds(hh * S5_HALF_T + tt, rows, stride=S5_CHUNK), :].astype(BF16)
                                    for tt in range(S5_HALF_T)], axis=1)
            uh = jnp.dot(xcat, perm_ref[...], preferred_element_type=F32).astype(BF16)
            for g in range(S5_BLOCK_GROUPS):
                uall_ref[g, :, hh * LANES:(hh + 1) * LANES] = uh[:, g * LANES:(g + 1) * LANES]

    u = uall_ref[gl]
    s = jnp.dot(u, wi_ref[0], preferred_element_type=F32)
    sr_ref[...] = s[:, :half]
    si_ref[...] = s[:, half:]
    c_r = coef_ref[0, 0:1, :]
    c_i = coef_ref[0, 1:2, :]
    e_r = h0_ref[0, :, :half]
    e_i = h0_ref[0, :, half:]
    is_fwd = lax.broadcasted_iota(jnp.int32, (nb, half), 1) < S5_STATE
    for j in range(nk):
        rf = pl.ds(j, nb, stride=nk)
        rb = pl.ds(nk - 1 - j, nb, stride=nk)
        efr_ref[rf, :] = e_r
        ebr_ref[rb, :] = e_r
        efi_ref[rf, :] = e_i
        ebi_ref[rb, :] = e_i
        s_r = jnp.where(is_fwd, sr_ref[rf, :], sr_ref[rb, :])
        s_i = jnp.where(is_fwd, si_ref[rf, :], si_ref[rb, :])
        e_r, e_i = c_r * e_r - c_i * e_i + s_r, c_r * e_i + c_i * e_r + s_i
    fin_ref[0, :, :half] = e_r
    fin_ref[0, :, half:] = e_i
    y = jnp.dot(u, m_ref[0], preferred_element_type=F32)
    fwd_rows = lax.broadcasted_iota(jnp.int32, (rows, half), 1) < S5_STATE
    e = jnp.concatenate([jnp.where(fwd_rows, efr_ref[...], ebr_ref[...]),
                         jnp.where(fwd_rows, efi_ref[...], ebi_ref[...])], axis=1).astype(BF16)
    y = y + lax.dot_general(e, wot_ref[0], (((1,), (1,)), ((), ())), preferred_element_type=F32)
    yall_ref[gl] = y.astype(BF16)

    @pl.when(gl == S5_BLOCK_GROUPS - 1)
    def _():
        for hh in range(S5_CHUNK // S5_HALF_T):
            ycat = jnp.concatenate([yall_ref[g, :, hh * LANES:(hh + 1) * LANES]
                                    for g in range(S5_BLOCK_GROUPS)], axis=1)
            out = lax.dot_general(ycat, perm_ref[...], (((1,), (1,)), ((), ())), preferred_element_type=F32)
            for tt in range(S5_HALF_T):
                y_ref[pl.ds(hh * S5_HALF_T + tt, rows, stride=S5_CHUNK), :] = out[:, tt * LANES:(tt + 1) * LANES]


def _s5(u, ops, h0, nb, nk):
    m, w_in, w_out, coef = ops
    n = u.shape[0]
    rows = nb * nk
    perm = _s5_perm()
    g3 = lambda b, g: (b * S5_BLOCK_GROUPS + g, 0, 0)
    blk = lambda b, g: (0, b)
    return pl.pallas_call(
        functools.partial(_s5_kernel, nb, nk),
        out_shape=(jax.ShapeDtypeStruct((n, D_S5), F32),
                   jax.ShapeDtypeStruct((S5_GROUPS, nb, 4 * S5_STATE), F32)),
        grid=(S5_GROUPS // S5_BLOCK_GROUPS, S5_BLOCK_GROUPS),
        in_specs=[pl.BlockSpec((n, LANES), blk),
                  pl.BlockSpec(perm.shape, lambda b, g: (0, 0)),
                  pl.BlockSpec((1, S5_ROW, S5_ROW), g3),
                  pl.BlockSpec((1, S5_ROW, 4 * S5_STATE), g3),
                  pl.BlockSpec((1, 4 * S5_STATE, S5_ROW), g3),
                  pl.BlockSpec((1, 2, 2 * S5_STATE), g3),
                  pl.BlockSpec((1, nb, 4 * S5_STATE), g3)],
        out_specs=(pl.BlockSpec((n, LANES), blk),
                   pl.BlockSpec((1, nb, 4 * S5_STATE), g3)),
        scratch_shapes=[pltpu.VMEM((rows, 2 * S5_STATE), F32)] * 6 + [
            pltpu.VMEM((S5_BLOCK_GROUPS, rows, S5_ROW), BF16),
            pltpu.VMEM((S5_BLOCK_GROUPS, rows, S5_ROW), BF16)],
        compiler_params=_params(("arbitrary", "arbitrary")),
        name="s5",
    )(u, perm, m, w_in, w_out, coef, h0)


def _dft_tables(seq):
    k = jnp.arange(seq, dtype=jnp.int32)
    na = seq // FNET_GROUP
    ang_a = (2.0 * math.pi / na) * ((jnp.arange(na, dtype=jnp.int32)[:, None] * k[None, :]) % na).astype(F32)
    ang_b = (2.0 * math.pi / seq) * ((jnp.arange(FNET_GROUP, dtype=jnp.int32)[:, None] * k[None, :]) % seq).astype(F32)
    ca, sa = jnp.cos(ang_a)[:, None, :], jnp.sin(ang_a)[:, None, :]
    cb, sb = jnp.cos(ang_b)[None, :, :], jnp.sin(ang_b)[None, :, :]
    cos_jk = (ca * cb - sa * sb).reshape(seq, seq)
    sin_jk = (sa * cb + ca * sb).reshape(seq, seq)
    cs = jnp.concatenate([cos_jk, -sin_jk], axis=1).astype(BF16)
    c = jnp.arange(D_FNET, dtype=jnp.int32)
    same = (c[:, None] // FNET_GROUP) == (c[None, :] // FNET_GROUP)
    angc = (2.0 * math.pi / FNET_GROUP) * (((c[:, None] % FNET_GROUP) * (c[None, :] % FNET_GROUP))
                                           % FNET_GROUP).astype(F32)
    scale = 1.0 / math.sqrt(seq * FNET_GROUP)
    bdc = jnp.where(same, jnp.cos(angc) * scale, 0.0).astype(BF16)
    bds = jnp.where(same, jnp.sin(angc) * scale, 0.0).astype(BF16)
    return cs, bdc, bds


def _fourier_kernel(seq, z_ref, cs_ref, bdc_ref, bds_ref, o_ref, zz_ref):
    @pl.when(pl.program_id(1) == 0)
    def _():
        z = z_ref[...]
        zz_ref[0:seq, :] = jnp.dot(z, bdc_ref[...], preferred_element_type=F32).astype(BF16)
        zz_ref[seq:, :] = jnp.dot(z, bds_ref[...], preferred_element_type=F32).astype(BF16)

    o_ref[...] = jnp.dot(cs_ref[...], zz_ref[...], preferred_element_type=F32).astype(BF16)


def _fourier(z, nb, seq, tl):
    cs, bdc, bds = _dft_tables(seq)
    nt = seq // tl
    return pl.pallas_call(
        functools.partial(_fourier_kernel, seq),
        out_shape=jax.ShapeDtypeStruct(z.shape, BF16),
        grid=(nb, nt),
        in_specs=[pl.BlockSpec((seq, D_FNET), lambda b, i: (b, 0)),
                  pl.BlockSpec((tl, 2 * seq), lambda b, i: (i, 0)),
                  pl.BlockSpec((D_FNET, D_FNET), lambda b, i: (0, 0)),
                  pl.BlockSpec((D_FNET, D_FNET), lambda b, i: (0, 0))],
        out_specs=pl.BlockSpec((tl, D_FNET), lambda b, i: (b * nt + i, 0)),
        scratch_shapes=[pltpu.VMEM((2 * seq, D_FNET), BF16)],
        compiler_params=_params(("arbitrary", "arbitrary")),
        name="fourier",
    )(z, cs, bdc, bds)


def _first_argmax_mask(v, iota, size):
    m = jnp.max(v, axis=0, keepdims=True)
    first = jnp.min(jnp.where(v == m, iota, size), axis=0, keepdims=True)
    return iota == first


def _route(logits_t, bias_col):
    tm = logits_t.shape[1]
    neg = -jnp.inf
    s = jax.nn.sigmoid(logits_t)
    biased = s + bias_col
    io8 = lax.broadcasted_iota(jnp.int32, (EXPERTS_PER_GROUP, tm), 0)
    gs_rows = []
    for g in range(N_EXPERT_GROUPS):
        blk = biased[g * EXPERTS_PER_GROUP:(g + 1) * EXPERTS_PER_GROUP, :]
        m1 = jnp.max(blk, axis=0, keepdims=True)
        rest = jnp.where(_first_argmax_mask(blk, io8, EXPERTS_PER_GROUP), neg, blk)
        gs_rows.append(m1 + jnp.max(rest, axis=0, keepdims=True))
    gs = jnp.concatenate(gs_rows, axis=0)
    iog = lax.broadcasted_iota(jnp.int32, (N_EXPERT_GROUPS, tm), 0)
    gsel = jnp.zeros((N_EXPERT_GROUPS, tm), F32)
    for _ in range(TOPK_GROUPS):
        sel = _first_argmax_mask(gs, iog, N_EXPERT_GROUPS)
        gsel = jnp.where(sel, 1.0, gsel)
        gs = jnp.where(sel, neg, gs)
    emask = jnp.concatenate(
        [jnp.broadcast_to(gsel[g:g + 1, :], (EXPERTS_PER_GROUP, tm)) for g in range(N_EXPERT_GROUPS)], axis=0)
    v = jnp.where(emask > 0.0, biased, neg)
    ioe = lax.broadcasted_iota(jnp.int32, (N_EXPERTS, tm), 0)
    idx_rows, s_rows = [], []
    for _ in range(TOP_K):
        sel = _first_argmax_mask(v, ioe, N_EXPERTS)
        idx_rows.append(jnp.sum(jnp.where(sel, ioe, 0), axis=0, keepdims=True))
        s_rows.append(jnp.sum(jnp.where(sel, s, 0.0), axis=0, keepdims=True))
        v = jnp.where(sel, neg, v)
    denom = s_rows[0]
    for r in s_rows[1:]:
        denom = denom + r
    pad = ROUTE_ROWS - TOP_K
    idx = jnp.concatenate(idx_rows + [jnp.zeros((pad, tm), jnp.int32)], axis=0)
    w = jnp.concatenate([r / denom * ROUTED_SCALE for r in s_rows] + [jnp.zeros((pad, tm), F32)], axis=0)
    return idx, w


def _merge_kernel(has_pos, *refs):
    if has_pos:
        (ys_ref, yf_ref, gt_ref, x_ref, pos_ref, mod_ref, n2_ref, wglu_ref, wps_ref, wpf_ref, wout_ref,
         wrt_ref, rb_ref, ws1_ref, ws3_ref, ws2_ref, xs_ref, h2_ref, ridx_ref, rw_ref) = refs
        x = x_ref[...] + pos_ref[...]
    else:
        (ys_ref, yf_ref, gt_ref, x_ref, mod_ref, n2_ref, wglu_ref, wps_ref, wpf_ref, wout_ref,
         wrt_ref, rb_ref, ws1_ref, ws3_ref, ws2_ref, xs_ref, h2_ref, ridx_ref, rw_ref) = refs
        x = x_ref[...]
    m = mod_ref[0]
    g = jax.nn.gelu(ys_ref[...].astype(F32))
    a = g * jax.nn.sigmoid(jnp.dot(g.astype(BF16), wglu_ref[...], preferred_element_type=F32))
    pa = jnp.dot(a.astype(BF16), wps_ref[...], preferred_element_type=F32)
    pb = jnp.dot(yf_ref[...], wpf_ref[...], preferred_element_type=F32)
    gt = gt_ref[...].astype(F32)
    merged = gt[:, :D_MODEL] * pa + gt[:, D_MODEL:] * pb
    x1 = x + m[2:3, :] * jnp.dot(merged.astype(BF16), wout_ref[...], preferred_element_type=F32)
    h2 = _rms(x1, n2_ref[...]) * (1.0 + m[4:5, :]) + m[3:4, :]
    hb = h2.astype(BF16)
    for j in range(ROW_SUB):
        h2_ref[pl.ds(j, h2.shape[0], stride=ROW_SUB), :] = h2[:, j * LANES:(j + 1) * LANES]
    wr = wrt_ref[...]
    wr_hi = wr.astype(BF16)
    wr_lo = (wr - wr_hi.astype(F32)).astype(BF16)
    h_lo = (h2 - hb.astype(F32)).astype(BF16)
    dn = (((1,), (1,)), ((), ()))
    logits_t = (lax.dot_general(wr_hi, hb, dn, preferred_element_type=F32)
                + lax.dot_general(wr_hi, h_lo, dn, preferred_element_type=F32)
                + lax.dot_general(wr_lo, hb, dn, preferred_element_type=F32))
    ridx_ref[...], rw_ref[...] = _route(logits_t, rb_ref[...])
    s1 = jnp.dot(hb, ws1_ref[...], preferred_element_type=F32)
    s3 = jnp.dot(hb, ws3_ref[...], preferred_element_type=F32)
    shared = jnp.dot((s1 * jax.nn.sigmoid(s1) * s3).astype(BF16), ws2_ref[...], preferred_element_type=F32)
    xs_ref[...] = x1 + m[5:6, :] * shared


def _merge(ys, yf, gt, x, pos, mod, mod_row, n2, weights, tm):
    n = x.shape[0]
    has_pos = pos is not None
    row = lambda i: (i, 0)
    const = lambda a: pl.BlockSpec(a.shape, lambda i: (0,) * a.ndim)
    in_specs = [pl.BlockSpec((tm, D_S5), row), pl.BlockSpec((tm, D_FNET), row),
                pl.BlockSpec((tm, 2 * D_MODEL), row), pl.BlockSpec((tm, D_MODEL), row)]
    args = [ys, yf, gt, x]
    if has_pos:
        nper = pos.shape[0] // tm
        in_specs.append(pl.BlockSpec((tm, D_MODEL), lambda i: (i % nper, 0)))
        args.append(pos)
    in_specs += [pl.BlockSpec((1, N_MOD, D_MODEL), lambda i: (mod_row(i, tm), 0, 0)), const(n2)]
    args += [mod, n2]
    in_specs += [const(w) for w in weights]
    args += list(weights)
    return pl.pallas_call(
        functools.partial(_merge_kernel, has_pos),
        out_shape=(jax.ShapeDtypeStruct((n, D_MODEL), F32),
                   jax.ShapeDtypeStruct((n * ROW_SUB, LANES), F32),
                   jax.ShapeDtypeStruct((ROUTE_ROWS, n), jnp.int32),
                   jax.ShapeDtypeStruct((ROUTE_ROWS, n), F32)),
        grid=(n // tm,),
        in_specs=in_specs,
        out_specs=(pl.BlockSpec((tm, D_MODEL), row), pl.BlockSpec((tm * ROW_SUB, LANES), row),
                   pl.BlockSpec((ROUTE_ROWS, tm), lambda i: (0, i)),
                   pl.BlockSpec((ROUTE_ROWS, tm), lambda i: (0, i))),
        compiler_params=_params(("arbitrary",)),
        name="merge",
    )(*args)


MOE_SUB = 4096
MOE_TM = 128
MOE_TMAX = MOE_SUB * TOP_K // MOE_TM + N_EXPERTS
MOE_SORT_SUBS = 2
MOE_PAD = 2
MOE_TS = MOE_TMAX + 2 * MOE_PAD
MOE_DUMMY = 256
MOE_RMW = 16
ROW_TILE = MOE_TM * ROW_SUB


def _moe_plan(ridx, rw):
    n = ridx.shape[1]
    nsub = n // MOE_SUB
    npair = n * TOP_K
    t = jnp.arange(n, dtype=jnp.int32)
    key = (((t // MOE_SUB) * N_EXPERTS)[None] + ridx[:TOP_K]) * MOE_SUB + (t % MOE_SUB)[None]
    group = MOE_SORT_SUBS * MOE_SUB
    parts = [lax.sort((key[:, g:g + group].reshape(-1), rw[:TOP_K, g:g + group].reshape(-1)), num_keys=1)
             for g in range(0, n, group)]
    skey = jnp.concatenate([pk for pk, _ in parts])
    sw = jnp.concatenate([pw for _, pw in parts])
    stok = jnp.concatenate([(skey % MOE_SUB) * ROW_SUB, jnp.zeros((MOE_TM,), jnp.int32)])
    sw_rows = jnp.concatenate([sw, jnp.zeros((MOE_TM,), F32)]).reshape((npair + MOE_TM) // LANES, 1, LANES)
    hits = ridx[:TOP_K].reshape(TOP_K, nsub, 1, MOE_SUB) == jnp.arange(N_EXPERTS, dtype=jnp.int32)[None, None, :, None]
    cnt = jnp.sum(hits.astype(jnp.int32), axis=(0, 3))
    poff = (jnp.cumsum(cnt.reshape(-1)) - cnt.reshape(-1)).reshape(nsub, N_EXPERTS)
    ntile = (cnt + MOE_TM - 1) // MOE_TM
    tcum = jnp.cumsum(ntile, axis=1)
    toff = tcum - ntile
    tstart = jnp.concatenate([toff, tcum[:, -1:]], axis=1).reshape(-1).astype(jnp.int32)
    j = jnp.arange(MOE_TS, dtype=jnp.int32) - MOE_PAD
    valid = (j[None] >= 0) & (j[None] < tcum[:, -1:])
    te = jnp.minimum(jnp.sum(j[None, :, None] >= tcum[:, None, :], axis=-1), N_EXPERTS - 1)
    pick = lambda a: jnp.take_along_axis(a, te, axis=1)
    first = (j[None] - pick(toff)) * MOE_TM
    p0 = jnp.where(valid, pick(poff) + first, 0).reshape(-1).astype(jnp.int32)
    nv = jnp.where(valid, jnp.minimum(pick(cnt) - first, MOE_TM), 0).reshape(-1).astype(jnp.int32)
    return tstart, p0, nv, stok, sw_rows


def _moe_kernel(ts_ref, p0_ref, nv_ref, tok_ref, sw_ref, src_ref, w1_ref, w3_ref, w2_ref, y_ref,
                xt_ref, xb_ref, act_ref, ot_ref, w1b_ref, w3b_ref, w2b_ref, slot_ref):
    sub = pl.program_id(0)
    e = pl.program_id(1)
    base = sub * MOE_TS + MOE_PAD
    first = ts_ref[sub * (N_EXPERTS + 1) + e]
    last = ts_ref[sub * (N_EXPERTS + 1) + e + 1]

    def gather(p0):
        for mi in range(MOE_TM):
            tok = pl.multiple_of(tok_ref[p0 + mi], ROW_SUB)
            xt_ref[mi * ROW_SUB:(mi + 1) * ROW_SUB, :] = src_ref[pl.ds(tok, ROW_SUB), :]
        for j in range(ROW_SUB):
            xb_ref[:, j * LANES:(j + 1) * LANES] = xt_ref[pl.ds(j, MOE_TM, stride=ROW_SUB), :].astype(BF16)

    def scatter(p0, nv, masked):
        for u in range(0, MOE_TM, MOE_RMW):
            new = []
            for i in range(MOE_RMW):
                tok = tok_ref[p0 + u + i]
                if masked:
                    tok = jnp.where(u + i < nv, tok, MOE_SUB * ROW_SUB)
                tok = pl.multiple_of(tok, ROW_SUB)
                new.append((tok, y_ref[pl.ds(tok, ROW_SUB), :]
                            + ot_ref[(u + i) * ROW_SUB:(u + i + 1) * ROW_SUB, :]))
            for tok, v in new:
                y_ref[pl.ds(tok, ROW_SUB), :] = v

    @pl.when(e == 0)
    def _():
        y_ref[...] = jnp.zeros_like(y_ref)
        ot_ref[...] = jnp.zeros_like(ot_ref)
        act_ref[...] = jnp.zeros_like(act_ref)
        w2b_ref[...] = jnp.zeros_like(w2b_ref)
        slot_ref[0] = 0
        gather(p0_ref[base])

    @pl.when(last > first)
    def _():
        slot_ref[0] = 1 - slot_ref[0]
        w1b_ref[...] = w1_ref[0].astype(BF16)
        w3b_ref[...] = w3_ref[0].astype(BF16)
        w2b_ref[slot_ref[0]] = w2_ref[0].astype(BF16)

    slot = slot_ref[0]

    def down_proj(w2_slot):
        o = jnp.dot(act_ref[...], w2b_ref[w2_slot], preferred_element_type=F32)
        for j in range(ROW_SUB):
            ot_ref[pl.ds(j, MOE_TM, stride=ROW_SUB), :] = o[:, j * LANES:(j + 1) * LANES]

    def step(i, masked):
        cur = base + i
        scatter(p0_ref[cur - 2], nv_ref[cur - 2], masked)
        down_proj(jnp.where(i > first, slot, 1 - slot))
        p0 = p0_ref[cur]
        nv = nv_ref[cur]
        x = xb_ref[...]
        a = jnp.dot(x, w1b_ref[...], preferred_element_type=F32)
        b = jnp.dot(x, w3b_ref[...], preferred_element_type=F32)
        r0 = p0 // LANES
        c = p0 % LANES
        lane = lax.broadcasted_iota(jnp.int32, (1, LANES), 1)
        rows = lax.broadcasted_iota(jnp.int32, (LANES, LANES), 0)
        cols = lax.broadcasted_iota(jnp.int32, (LANES, LANES), 1)
        gparts = []
        for hh in range(MOE_TM // LANES):
            ga = pltpu.roll(sw_ref[r0 + hh], LANES - c, axis=1)
            gb = pltpu.roll(sw_ref[r0 + hh + 1], LANES - c, axis=1)
            g = jnp.where(lane + hh * LANES < nv, jnp.where(lane < LANES - c, ga, gb), 0.0)
            gparts.append(jnp.sum(jnp.where(rows == cols, jnp.broadcast_to(g, (LANES, LANES)), 0.0),
                                  axis=1, keepdims=True))
        gcol = jnp.concatenate(gparts, axis=0)
        act_ref[...] = (a * jax.nn.sigmoid(a) * b * gcol).astype(BF16)
        gather(p0_ref[cur + 1])

    def body(i, carry):
        step(i, True)
        return carry

    lax.fori_loop(first, last, body, 0)

    @pl.when(e == N_EXPERTS - 1)
    def _():
        scatter(p0_ref[base + last - 2], nv_ref[base + last - 2], True)
        down_proj(slot)
        scatter(p0_ref[base + last - 1], nv_ref[base + last - 1], True)


def _moe(h2_rows, plan, w1, w3, w2):
    tstart, p0, nv, stok, sw_rows = plan
    nsub = h2_rows.shape[0] // (MOE_SUB * ROW_SUB)
    wmap = lambda s, e, ts, p0, nv: (e, 0, 0)
    sub2 = lambda s, e, ts, p0, nv: (s, 0)
    grid_spec = pltpu.PrefetchScalarGridSpec(
        num_scalar_prefetch=3,
        grid=(nsub, N_EXPERTS),
        in_specs=[pl.BlockSpec(memory_space=pltpu.SMEM),
                  pl.BlockSpec(sw_rows.shape, lambda s, e, ts, p0, nv: (0, 0, 0)),
                  pl.BlockSpec((MOE_SUB * ROW_SUB, LANES), sub2, pipeline_mode=pl.Buffered(1)),
                  pl.BlockSpec((1, D_MODEL, D_EXPERT), wmap),
                  pl.BlockSpec((1, D_MODEL, D_EXPERT), wmap),
                  pl.BlockSpec((1, D_EXPERT, D_MODEL), wmap)],
        out_specs=pl.BlockSpec(((MOE_SUB + MOE_DUMMY) * ROW_SUB, LANES), sub2, pipeline_mode=pl.Buffered(1)),
        scratch_shapes=[pltpu.VMEM((ROW_TILE, LANES), F32), pltpu.VMEM((MOE_TM, D_MODEL), BF16),
                        pltpu.VMEM((MOE_TM, D_EXPERT), BF16), pltpu.VMEM((ROW_TILE, LANES), F32),
                        pltpu.VMEM((D_MODEL, D_EXPERT), BF16), pltpu.VMEM((D_MODEL, D_EXPERT), BF16),
                        pltpu.VMEM((2, D_EXPERT, D_MODEL), BF16), pltpu.SMEM((1,), jnp.int32)])
    return pl.pallas_call(
        _moe_kernel,
        grid_spec=grid_spec,
        out_shape=jax.ShapeDtypeStruct((nsub * (MOE_SUB + MOE_DUMMY) * ROW_SUB, LANES), F32),
        compiler_params=_params(("arbitrary", "arbitrary")),
        name="moe",
    )(tstart, p0, nv, stok, sw_rows, h2_rows, w1, w3, w2)


def _final_kernel(xs_ref, y_ref, mod_ref, fg_ref, o_ref):
    tm = xs_ref.shape[0]
    y = jnp.concatenate([y_ref[pl.ds(j, tm, stride=ROW_SUB), :] for j in range(ROW_SUB)], axis=1)
    x2 = xs_ref[...] + mod_ref[0][5:6, :] * y
    o_ref[...] = _rms(x2, fg_ref[...])


def _final(xs, y_rows, mod, mod_row, fg, tm):
    n = xs.shape[0]
    per_sub = MOE_SUB // tm
    stride = (MOE_SUB + MOE_DUMMY) // tm
    row = lambda i: (i, 0)
    return pl.pallas_call(
        _final_kernel,
        out_shape=jax.ShapeDtypeStruct((n, D_MODEL), F32),
        grid=(n // tm,),
        in_specs=[pl.BlockSpec((tm, D_MODEL), row),
                  pl.BlockSpec((tm * ROW_SUB, LANES), lambda i: ((i // per_sub) * stride + i % per_sub, 0)),
                  pl.BlockSpec((1, N_MOD, D_MODEL), lambda i: (mod_row(i, tm), 0, 0)),
                  pl.BlockSpec((1, D_MODEL), lambda i: (0, 0))],
        out_specs=pl.BlockSpec((tm, D_MODEL), row),
        compiler_params=_params(("arbitrary",)),
        name="final",
    )(xs, y_rows, mod, fg)


def _grid_pos_embed(n_tokens):
    rows = n_tokens // GRID_W
    r, col = jnp.meshgrid(jnp.arange(rows, dtype=F32), jnp.arange(GRID_W, dtype=F32), indexing="ij")
    quarter = D_MODEL // 4
    omega = 1.0 / (10000.0 ** (jnp.arange(quarter, dtype=F32) / quarter))

    def emb(p):
        a = p.reshape(-1)[:, None] * omega
        return jnp.concatenate([jnp.sin(a), jnp.cos(a)], axis=-1)

    return jnp.concatenate([emb(r), emb(col)], axis=-1)


def _mixers(x3, pos, mod, first_row, h0, s5_ops, p):
    nb, seq, _ = x3.shape
    n = nb * seq
    nk = seq // S5_CHUNK
    x = x3.reshape(n, D_MODEL)
    per_seq_mod = first_row > 0

    def mod_row(i, tm):
        return first_row + (i * tm) // seq if per_seq_mod else 0

    us, uf, gt = _inproj(x, pos, mod, mod_row, p["norm1_g"], p["w_in"], 512)
    ys, fin = _s5(us, s5_ops, h0, nb, nk)
    yf = _fourier(uf, nb, seq, min(seq, 512))
    xs, h2_rows, ridx, rw = _merge(ys, yf, gt, x, pos, mod, mod_row, p["norm2_g"], p["merge_w"], 512)
    return xs, h2_rows, ridx, rw, fin, mod_row


def _plan_of_stream(plan, s, n_tokens):
    tstart, p0, nv, stok, sw_rows = plan
    nsub = n_tokens // MOE_SUB
    npair = n_tokens * TOP_K
    off = s * npair
    tiles = slice(s * nsub * MOE_TS, (s + 1) * nsub * MOE_TS)
    nv_s = nv[tiles]
    p0_s = jnp.where(nv_s > 0, p0[tiles] - off, 0)
    return (tstart[s * nsub * (N_EXPERTS + 1):(s + 1) * nsub * (N_EXPERTS + 1)], p0_s, nv_s,
            stok[off:off + npair + MOE_TM], sw_rows[off // LANES:(off + npair + MOE_TM) // LANES])


def kernel(x_prompt, x_sample, state_s5_re, state_s5_im, c, c_ctx, w_ada, b_ada, norm1_g, norm2_g, w_in,
           lam_re, lam_im, log_dt, b_re, b_im, c_re, c_im, d_skip, w_glu, w_proj_s5, w_proj_f, w_out,
           w_router, router_bias, w1, w3, w2, ws1, ws3, ws2, final_norm_g):
    nb_ctx = x_prompt.shape[0]
    nb_lat, seq_lat, _ = x_sample.shape
    half = 2 * S5_STATE

    cond = jnp.concatenate([c_ctx[None], c, jnp.zeros((MOD_ROWS - 1 - nb_lat, D_MODEL), F32)], axis=0)
    mod = _adaln(cond, w_ada[0], b_ada[0]).reshape(MOD_ROWS, N_MOD, D_MODEL)

    s5_ops = _s5ops(lam_re[0], lam_im[0], log_dt[0], b_re[0], b_im[0], c_re[0], c_im[0], d_skip[0])
    p = dict(
        norm1_g=norm1_g[0][None], norm2_g=norm2_g[0][None], final_g=final_norm_g[None],
        w_in=w_in[0].astype(BF16), w1=w1[0], w3=w3[0], w2=w2[0],
        merge_w=(w_glu[0].astype(BF16), w_proj_s5[0].astype(BF16), w_proj_f[0].astype(BF16),
                 w_out[0].astype(BF16), w_router[0].T, router_bias[0][:, None],
                 ws1[0].astype(BF16), ws3[0].astype(BF16), ws2[0].astype(BF16)))

    def pack_state(sr, si):
        f = lambda a: a.astype(F32).transpose(2, 0, 1, 3).reshape(S5_GROUPS, a.shape[0], half)
        return jnp.concatenate([f(sr), f(si)], axis=-1)

    def unpack_state(fin, lo):
        nb = fin.shape[1]
        return fin[..., lo:lo + half].reshape(S5_GROUPS, nb, 2, S5_STATE).transpose(1, 2, 0, 3)[:, None]

    h0_ctx = jnp.zeros((S5_GROUPS, nb_ctx, 2 * half), F32)
    h0_lat = pack_state(state_s5_re[:, 0], state_s5_im[:, 0])
    streams = [(x_prompt, _mixers(x_prompt, None, mod, 0, h0_ctx, s5_ops, p)),
               (x_sample, _mixers(x_sample, _grid_pos_embed(seq_lat), mod, 1, h0_lat, s5_ops, p))]
    n_tokens = streams[0][1][0].shape[0]
    plan = _moe_plan(jnp.concatenate([m[2] for _, m in streams], axis=1),
                     jnp.concatenate([m[3] for _, m in streams], axis=1))
    outs = []
    for s, (x3, (xs, h2_rows, _, _, _, mod_row)) in enumerate(streams):
        y_rows = _moe(h2_rows, _plan_of_stream(plan, s, n_tokens), p["w1"], p["w3"], p["w2"])
        outs.append(_final(xs, y_rows, mod, mod_row, p["final_g"], 256).reshape(x3.shape))
    fin = streams[0][1][4]
    return (outs[0], outs[1], unpack_state(fin, 0).astype(x_prompt.dtype),
            unpack_state(fin, half).astype(x_prompt.dtype))
```

```python
import functools
import math

import jax
import jax.numpy as jnp
from jax import lax
from jax.experimental import pallas as pl
from jax.experimental.pallas import tpu as pltpu

D_MODEL = 1024
GRID_W = 64
D_S5 = 768
S5_GROUP = 16
S5_GROUPS = 48
S5_STATE = 64
D_FNET = 256
FNET_GROUP = 64
N_EXPERTS = 64
TOP_K = 6
N_EXPERT_GROUPS = 8
EXPERTS_PER_GROUP = N_EXPERTS // N_EXPERT_GROUPS
TOPK_GROUPS = 4
D_EXPERT = 256
ROUTED_SCALE = 2.5
N_MOD = 6
EPS = 1e-6

S5_CHUNK = 16
S5_ROW = S5_CHUNK * S5_GROUP
MOD_ROWS = 8
ROUTE_ROWS = 8
LANES = 128
ROW_SUB = D_MODEL // LANES
VMEM_LIMIT = 56 * 1024 * 1024

BF16 = jnp.bfloat16
F32 = jnp.float32


def _params(sem, vmem=VMEM_LIMIT):
    return pltpu.CompilerParams(dimension_semantics=sem, vmem_limit_bytes=vmem)


def _rms(x, g):
    return x * lax.rsqrt(jnp.mean(x * x, axis=-1, keepdims=True) + EPS) * g


def _adaln_kernel(c_ref, w_ref, b_ref, o_ref):
    c = c_ref[...]
    o_ref[...] = jnp.dot(c * jax.nn.sigmoid(c), w_ref[...], precision=lax.Precision.HIGHEST,
                         preferred_element_type=F32) + b_ref[...]


def _adaln(cond, w_ada, b_ada):
    n_out = N_MOD * D_MODEL
    return pl.pallas_call(
        _adaln_kernel,
        out_shape=jax.ShapeDtypeStruct((MOD_ROWS, n_out), F32),
        grid=(N_MOD,),
        in_specs=[pl.BlockSpec((MOD_ROWS, D_MODEL), lambda i: (0, 0)),
                  pl.BlockSpec((D_MODEL, D_MODEL), lambda i: (0, i)),
                  pl.BlockSpec((1, D_MODEL), lambda i: (0, i))],
        out_specs=pl.BlockSpec((MOD_ROWS, D_MODEL), lambda i: (0, i)),
        compiler_params=_params(("arbitrary",)),
        name="adaln",
    )(cond, w_ada, b_ada.reshape(1, n_out))


def _inproj_kernel(has_pos, *refs):
    if has_pos:
        x_ref, pos_ref, mod_ref, g_ref, w_ref, us_ref, uf_ref, gt_ref = refs
        x = x_ref[...] + pos_ref[...]
    else:
        x_ref, mod_ref, g_ref, w_ref, us_ref, uf_ref, gt_ref = refs
        x = x_ref[...]
    m = mod_ref[0]
    h = _rms(x, g_ref[...]) * (1.0 + m[1:2, :]) + m[0:1, :]
    p = jnp.dot(h.astype(BF16), w_ref[...], preferred_element_type=F32)
    us_ref[...] = p[:, :D_S5]
    uf_ref[...] = p[:, D_S5:D_MODEL].astype(BF16)
    gt_ref[...] = jax.nn.sigmoid(p[:, D_MODEL:]).astype(BF16)


def _inproj(x, pos, mod, mod_row, norm_g, w_in_bf, tm):
    n = x.shape[0]
    has_pos = pos is not None
    row = lambda i: (i, 0)
    in_specs = [pl.BlockSpec((tm, D_MODEL), row)]
    args = [x]
    if has_pos:
        nper = pos.shape[0] // tm
        in_specs.append(pl.BlockSpec((tm, D_MODEL), lambda i: (i % nper, 0)))
        args.append(pos)
    in_specs += [pl.BlockSpec((1, N_MOD, D_MODEL), lambda i: (mod_row(i, tm), 0, 0)),
                 pl.BlockSpec((1, D_MODEL), lambda i: (0, 0)),
                 pl.BlockSpec(w_in_bf.shape, lambda i: (0, 0))]
    args += [mod, norm_g, w_in_bf]
    return pl.pallas_call(
        functools.partial(_inproj_kernel, has_pos),
        out_shape=(jax.ShapeDtypeStruct((n, D_S5), F32),
                   jax.ShapeDtypeStruct((n, D_FNET), BF16),
                   jax.ShapeDtypeStruct((n, 2 * D_MODEL), BF16)),
        grid=(n // tm,),
        in_specs=in_specs,
        out_specs=(pl.BlockSpec((tm, D_S5), row), pl.BlockSpec((tm, D_FNET), row),
                   pl.BlockSpec((tm, 2 * D_MODEL), row)),
        compiler_params=_params(("arbitrary",)),
        name="inproj",
    )(*args)


def _shift_lanes(x, k):
    if k == 0:
        return x
    z = jnp.zeros((x.shape[0], abs(k)), x.dtype)
    if k > 0:
        return jnp.concatenate([z, x[:, :x.shape[1] - k]], axis=1)
    return jnp.concatenate([x[:, -k:], z], axis=1)


def _s5ops_kernel(lam_re_ref, lam_im_ref, dt_ref, btr_ref, bti_ref, cr_ref, ci_ref, d_ref,
                  m_ref, wi_ref, wot_ref, coef_ref):
    hi = lax.Precision.HIGHEST
    lr = jnp.minimum(lam_re_ref[0], -1e-4)
    li = lam_im_ref[0]
    dt = jnp.exp(dt_ref[0])
    mag = jnp.exp(lr * dt)
    ar = mag * jnp.cos(li * dt)
    ai = mag * jnp.sin(li * dt)
    den = lr * lr + li * li
    nr = ar - 1.0
    qr = (nr * lr + ai * li) / den
    qi = (ai * lr - nr * li) / den
    pr, pi = [], []
    for n in range(S5_CHUNK + 1):
        pm = jnp.exp(float(n) * (lr * dt))
        pr.append(pm * jnp.cos(float(n) * (li * dt)))
        pi.append(pm * jnp.sin(float(n) * (li * dt)))
    bbr, bbi, car, cai = [], [], [], []
    for d in range(2):
        btr = btr_ref[0, d]
        bti = bti_ref[0, d]
        bbr.append(qr[d:d + 1] * btr - qi[d:d + 1] * bti)
        bbi.append(qr[d:d + 1] * bti + qi[d:d + 1] * btr)
        cr = cr_ref[0, d]
        ci = ci_ref[0, d]
        car.append([cr * pr[n][d:d + 1] - ci * pi[n][d:d + 1] for n in range(S5_CHUNK + 1)])
        cai.append([cr * pi[n][d:d + 1] + ci * pr[n][d:d + 1] for n in range(S5_CHUNK + 1)])

    def lag_kernels(d, order):
        a = jnp.concatenate([car[d][n] for n in order], axis=0)
        b = jnp.concatenate([cai[d][n] for n in order], axis=0)
        dn = (((1,), (1,)), ((), ()))
        return (lax.dot_general(bbr[d], a, dn, precision=hi, preferred_element_type=F32)
                - lax.dot_general(bbi[d], b, dn, precision=hi, preferred_element_type=F32))

    ktf = lag_kernels(0, range(S5_CHUNK))
    ktb = lag_kernels(1, range(S5_CHUNK - 1, -1, -1))
    row = lax.broadcasted_iota(jnp.int32, (S5_GROUP, S5_ROW), 0)
    lane = lax.broadcasted_iota(jnp.int32, (S5_GROUP, S5_ROW), 1)
    dcol = d_ref[0]
    for j in range(S5_CHUNK):
        rows = slice(j * S5_GROUP, (j + 1) * S5_GROUP)
        blk = _shift_lanes(ktf, S5_GROUP * j) + _shift_lanes(ktb, -S5_GROUP * (S5_CHUNK - 1 - j))
        blk = blk + jnp.where(lane == S5_GROUP * j + row, dcol, 0.0)
        m_ref[0, rows, :] = blk.astype(BF16)
        nf = S5_CHUNK - 1 - j
        wi = jnp.concatenate([pr[nf][0:1] * bbr[0] - pi[nf][0:1] * bbi[0],
                              pr[j][1:2] * bbr[1] - pi[j][1:2] * bbi[1],
                              pr[nf][0:1] * bbi[0] + pi[nf][0:1] * bbr[0],
                              pr[j][1:2] * bbi[1] + pi[j][1:2] * bbr[1]], axis=1)
        wi_ref[0, rows, :] = wi.astype(BF16)
        wot = jnp.concatenate([car[0][j + 1], car[1][S5_CHUNK - j],
                               -cai[0][j + 1], -cai[1][S5_CHUNK - j]], axis=1)
        wot_ref[0, rows, :] = wot.astype(BF16)
    coef_ref[0, 0:1, :] = jnp.concatenate([pr[S5_CHUNK][0:1], pr[S5_CHUNK][1:2]], axis=1)
    coef_ref[0, 1:2, :] = jnp.concatenate([pi[S5_CHUNK][0:1], pi[S5_CHUNK][1:2]], axis=1)


def _s5ops(lam_re, lam_im, log_dt, b_re, b_im, c_re, c_im, d_skip):
    g3 = lambda g: (g, 0, 0)
    g4 = lambda g: (g, 0, 0, 0)
    sw = lambda a: jnp.swapaxes(a.astype(F32), 0, 1)
    dt = jnp.broadcast_to(sw(log_dt)[..., None], (S5_GROUPS, 2, S5_STATE))
    args = (sw(lam_re), sw(lam_im), dt, sw(jnp.swapaxes(b_re, 2, 3)), sw(jnp.swapaxes(b_im, 2, 3)),
            sw(c_re), sw(c_im), d_skip.astype(F32).reshape(S5_GROUPS, S5_GROUP, 1))
    vec = pl.BlockSpec((1, 2, S5_STATE), g3)
    mat = pl.BlockSpec((1, 2, S5_GROUP, S5_STATE), g4)
    op = pl.BlockSpec((1, S5_ROW, S5_ROW), g3)
    return pl.pallas_call(
        _s5ops_kernel,
        out_shape=(jax.ShapeDtypeStruct((S5_GROUPS, S5_ROW, S5_ROW), BF16),) * 3
        + (jax.ShapeDtypeStruct((S5_GROUPS, 2, 2 * S5_STATE), F32),),
        grid=(S5_GROUPS,),
        in_specs=[vec, vec, vec, mat, mat, mat, mat, pl.BlockSpec((1, S5_GROUP, 1), g3)],
        out_specs=(op, op, op, pl.BlockSpec((1, 2, 2 * S5_STATE), g3)),
        compiler_params=_params(("arbitrary",)),
        name="s5ops",
    )(*args)


S5_BLOCK_GROUPS = LANES // S5_GROUP


S5_HALF_T = LANES // S5_GROUP


def _s5_perm():
    a = jnp.arange(S5_HALF_T * LANES, dtype=jnp.int32)
    dst = ((a % LANES) // S5_GROUP) * LANES + (a // LANES) * S5_GROUP + a % S5_GROUP
    return (dst[:, None] == a[None, :]).astype(BF16)


def _s5_kernel(nb, nk, u_ref, perm_ref, m_ref, wi_ref, wot_ref, coef_ref, h0_ref, y_ref, fin_ref,
               sr_ref, si_ref, efr_ref, ebr_ref, efi_ref, ebi_ref, uall_ref, yall_ref):
    gl = pl.program_id(1)
    rows = nb * nk
    half = 2 * S5_STATE

    @pl.when(gl == 0)
    def _():
        for hh in range(S5_CHUNK // S5_HALF_T):
            xcat = jnp.concatenate([u_ref[pl.ds(hh * S5_HALF_T + tt, rows, stride=S5_CHUNK), :].astype(BF16)
                                    for tt in range(S5_HALF_T)], axis=1)
            uh = jnp.dot(xcat, perm_ref[...], preferred_element_type=F32).astype(BF16)
            for g in range(S5_BLOCK_GROUPS):
                uall_ref[g, :, hh * LANES:(hh + 1) * LANES] = uh[:, g * LANES:(g + 1) * LANES]

    u = uall_ref[gl]
    s = jnp.dot(u, wi_ref[0], preferred_element_type=F32)
    sr_ref[...] = s[:, :half]
    si_ref[...] = s[:, half:]
    c_r = coef_ref[0, 0:1, :]
    c_i = coef_ref[0, 1:2, :]
    e_r = h0_ref[0, :, :half]
    e_i = h0_ref[0, :, half:]
    is_fwd = lax.broadcasted_iota(jnp.int32, (nb, half), 1) < S5_STATE
    for j in range(nk):
        rf = pl.ds(j, nb, stride=nk)
        rb = pl.ds(nk - 1 - j, nb, stride=nk)
        efr_ref[rf, :] = e_r
        ebr_ref[rb, :] = e_r
        efi_ref[rf, :] = e_i
        ebi_ref[rb, :] = e_i
        s_r = jnp.where(is_fwd, sr_ref[rf, :], sr_ref[rb, :])
        s_i = jnp.where(is_fwd, si_ref[rf, :], si_ref[rb, :])
        e_r, e_i = c_r * e_r - c_i * e_i + s_r, c_r * e_i + c_i * e_r + s_i
    fin_ref[0, :, :half] = e_r
    fin_ref[0, :, half:] = e_i
    y = jnp.dot(u, m_ref[0], preferred_element_type=F32)
    fwd_rows = lax.broadcasted_iota(jnp.int32, (rows, half), 1) < S5_STATE
    e = jnp.concatenate([jnp.where(fwd_rows, efr_ref[...], ebr_ref[...]),
                         jnp.where(fwd_rows, efi_ref[...], ebi_ref[...])], axis=1).astype(BF16)
    y = y + lax.dot_general(e, wot_ref[0], (((1,), (1,)), ((), ())), preferred_element_type=F32)
    yall_ref[gl] = y.astype(BF16)

    @pl.when(gl == S5_BLOCK_GROUPS - 1)
    def _():
        for hh in range(S5_CHUNK // S5_HALF_T):
            ycat = jnp.concatenate([yall_ref[g, :, hh * LANES:(hh + 1) * LANES]
                                    for g in range(S5_BLOCK_GROUPS)], axis=1)
            out = lax.dot_general(ycat, perm_ref[...], (((1,), (1,)), ((), ())), preferred_element_type=F32)
            for tt in range(S5_HALF_T):
                y_ref[pl.ds(hh * S5_HALF_T + tt, rows, stride=S5_CHUNK), :] = out[:, tt * LANES:(tt + 1) * LANES]


def _s5(u, ops, h0, nb, nk):
    m, w_in, w_out, coef = ops
    n = u.shape[0]
    rows = nb * nk
    perm = _s5_perm()
    g3 = lambda b, g: (b * S5_BLOCK_GROUPS + g, 0, 0)
    blk = lambda b, g: (0, b)
    return pl.pallas_call(
        functools.partial(_s5_kernel, nb, nk),
        out_shape=(jax.ShapeDtypeStruct((n, D_S5), F32),
                   jax.ShapeDtypeStruct((S5_GROUPS, nb, 4 * S5_STATE), F32)),
        grid=(S5_GROUPS // S5_BLOCK_GROUPS, S5_BLOCK_GROUPS),
        in_specs=[pl.BlockSpec((n, LANES), blk),
                  pl.BlockSpec(perm.shape, lambda b, g: (0, 0)),
                  pl.BlockSpec((1, S5_ROW, S5_ROW), g3),
                  pl.BlockSpec((1, S5_ROW, 4 * S5_STATE), g3),
                  pl.BlockSpec((1, 4 * S5_STATE, S5_ROW), g3),
                  pl.BlockSpec((1, 2, 2 * S5_STATE), g3),
                  pl.BlockSpec((1, nb, 4 * S5_STATE), g3)],
        out_specs=(pl.BlockSpec((n, LANES), blk),
                   pl.BlockSpec((1, nb, 4 * S5_STATE), g3)),
        scratch_shapes=[pltpu.VMEM((rows, 2 * S5_STATE), F32)] * 6 + [
            pltpu.VMEM((S5_BLOCK_GROUPS, rows, S5_ROW), BF16),
            pltpu.VMEM((S5_BLOCK_GROUPS, rows, S5_ROW), BF16)],
        compiler_params=_params(("arbitrary", "arbitrary")),
        name="s5",
    )(u, perm, m, w_in, w_out, coef, h0)


def _dft_tables(seq):
    k = jnp.arange(seq, dtype=jnp.int32)
    na = seq // FNET_GROUP
    ang_a = (2.0 * math.pi / na) * ((jnp.arange(na, dtype=jnp.int32)[:, None] * k[None, :]) % na).astype(F32)
    ang_b = (2.0 * math.pi / seq) * ((jnp.arange(FNET_GROUP, dtype=jnp.int32)[:, None] * k[None, :]) % seq).astype(F32)
    ca, sa = jnp.cos(ang_a)[:, None, :], jnp.sin(ang_a)[:, None, :]
    cb, sb = jnp.cos(ang_b)[None, :, :], jnp.sin(ang_b)[None, :, :]
    cos_jk = (ca * cb - sa * sb).reshape(seq, seq)
    sin_jk = (sa * cb + ca * sb).reshape(seq, seq)
    cs = jnp.concatenate([cos_jk, -sin_jk], axis=1).astype(BF16)
    c = jnp.arange(D_FNET, dtype=jnp.int32)
    same = (c[:, None] // FNET_GROUP) == (c[None, :] // FNET_GROUP)
    angc = (2.0 * math.pi / FNET_GROUP) * (((c[:, None] % FNET_GROUP) * (c[None, :] % FNET_GROUP))
                                           % FNET_GROUP).astype(F32)
    scale = 1.0 / math.sqrt(seq * FNET_GROUP)
    bdc = jnp.where(same, jnp.cos(angc) * scale, 0.0).astype(BF16)
    bds = jnp.where(same, jnp.sin(angc) * scale, 0.0).astype(BF16)
    return cs, bdc, bds


def _fourier_kernel(seq, z_ref, cs_ref, bdc_ref, bds_ref, o_ref, zz_ref):
    @pl.when(pl.program_id(1) == 0)
    def _():
        z = z_ref[...]
        zz_ref[0:seq, :] = jnp.dot(z, bdc_ref[...], preferred_element_type=F32).astype(BF16)
        zz_ref[seq:, :] = jnp.dot(z, bds_ref[...], preferred_element_type=F32).astype(BF16)

    o_ref[...] = jnp.dot(cs_ref[...], zz_ref[...], preferred_element_type=F32).astype(BF16)


def _fourier(z, nb, seq, tl):
    cs, bdc, bds = _dft_tables(seq)
    nt = seq // tl
    return pl.pallas_call(
        functools.partial(_fourier_kernel, seq),
        out_shape=jax.ShapeDtypeStruct(z.shape, BF16),
        grid=(nb, nt),
        in_specs=[pl.BlockSpec((seq, D_FNET), lambda b, i: (b, 0)),
                  pl.BlockSpec((tl, 2 * seq), lambda b, i: (i, 0)),
                  pl.BlockSpec((D_FNET, D_FNET), lambda b, i: (0, 0)),
                  pl.BlockSpec((D_FNET, D_FNET), lambda b, i: (0, 0))],
        out_specs=pl.BlockSpec((tl, D_FNET), lambda b, i: (b * nt + i, 0)),
        scratch_shapes=[pltpu.VMEM((2 * seq, D_FNET), BF16)],
        compiler_params=_params(("arbitrary", "arbitrary")),
        name="fourier",
    )(z, cs, bdc, bds)


def _first_argmax_mask(v, iota, size):
    m = jnp.max(v, axis=0, keepdims=True)
    first = jnp.min(jnp.where(v == m, iota, size), axis=0, keepdims=True)
    return iota == first


def _route(logits_t, bias_col):
    tm = logits_t.shape[1]
    neg = -jnp.inf
    s = jax.nn.sigmoid(logits_t)
    biased = s + bias_col
    io8 = lax.broadcasted_iota(jnp.int32, (EXPERTS_PER_GROUP, tm), 0)
    gs_rows = []
    for g in range(N_EXPERT_GROUPS):
        blk = biased[g * EXPERTS_PER_GROUP:(g + 1) * EXPERTS_PER_GROUP, :]
        m1 = jnp.max(blk, axis=0, keepdims=True)
        rest = jnp.where(_first_argmax_mask(blk, io8, EXPERTS_PER_GROUP), neg, blk)
        gs_rows.append(m1 + jnp.max(rest, axis=0, keepdims=True))
    gs = jnp.concatenate(gs_rows, axis=0)
    iog = lax.broadcasted_iota(jnp.int32, (N_EXPERT_GROUPS, tm), 0)
    gsel = jnp.zeros((N_EXPERT_GROUPS, tm), F32)
    for _ in range(TOPK_GROUPS):
        sel = _first_argmax_mask(gs, iog, N_EXPERT_GROUPS)
        gsel = jnp.where(sel, 1.0, gsel)
        gs = jnp.where(sel, neg, gs)
    emask = jnp.concatenate(
        [jnp.broadcast_to(gsel[g:g + 1, :], (EXPERTS_PER_GROUP, tm)) for g in range(N_EXPERT_GROUPS)], axis=0)
    v = jnp.where(emask > 0.0, biased, neg)
    ioe = lax.broadcasted_iota(jnp.int32, (N_EXPERTS, tm), 0)
    idx_rows, s_rows = [], []
    for _ in range(TOP_K):
        sel = _first_argmax_mask(v, ioe, N_EXPERTS)
        idx_rows.append(jnp.sum(jnp.where(sel, ioe, 0), axis=0, keepdims=True))
        s_rows.append(jnp.sum(jnp.where(sel, s, 0.0), axis=0, keepdims=True))
        v = jnp.where(sel, neg, v)
    denom = s_rows[0]
    for r in s_rows[1:]:
        denom = denom + r
    pad = ROUTE_ROWS - TOP_K
    idx = jnp.concatenate(idx_rows + [jnp.zeros((pad, tm), jnp.int32)], axis=0)
    w = jnp.concatenate([r / denom * ROUTED_SCALE for r in s_rows] + [jnp.zeros((pad, tm), F32)], axis=0)
    return idx, w


def _merge_kernel(has_pos, *refs):
    if has_pos:
        (ys_ref, yf_ref, gt_ref, x_ref, pos_ref, mod_ref, n2_ref, wglu_ref, wps_ref, wpf_ref, wout_ref,
         wrt_ref, rb_ref, ws1_ref, ws3_ref, ws2_ref, xs_ref, h2_ref, ridx_ref, rw_ref) = refs
        x = x_ref[...] + pos_ref[...]
    else:
        (ys_ref, yf_ref, gt_ref, x_ref, mod_ref, n2_ref, wglu_ref, wps_ref, wpf_ref, wout_ref,
         wrt_ref, rb_ref, ws1_ref, ws3_ref, ws2_ref, xs_ref, h2_ref, ridx_ref, rw_ref) = refs
        x = x_ref[...]
    m = mod_ref[0]
    g = jax.nn.gelu(ys_ref[...].astype(F32))
    a = g * jax.nn.sigmoid(jnp.dot(g.astype(BF16), wglu_ref[...], preferred_element_type=F32))
    pa = jnp.dot(a.astype(BF16), wps_ref[...], preferred_element_type=F32)
    pb = jnp.dot(yf_ref[...], wpf_ref[...], preferred_element_type=F32)
    gt = gt_ref[...].astype(F32)
    merged = gt[:, :D_MODEL] * pa + gt[:, D_MODEL:] * pb
    x1 = x + m[2:3, :] * jnp.dot(merged.astype(BF16), wout_ref[...], preferred_element_type=F32)
    h2 = _rms(x1, n2_ref[...]) * (1.0 + m[4:5, :]) + m[3:4, :]
    hb = h2.astype(BF16)
    for j in range(ROW_SUB):
        h2_ref[pl.ds(j, h2.shape[0], stride=ROW_SUB), :] = h2[:, j * LANES:(j + 1) * LANES]
    wr = wrt_ref[...]
    wr_hi = wr.astype(BF16)
    wr_lo = (wr - wr_hi.astype(F32)).astype(BF16)
    h_lo = (h2 - hb.astype(F32)).astype(BF16)
    dn = (((1,), (1,)), ((), ()))
    logits_t = (lax.dot_general(wr_hi, hb, dn, preferred_element_type=F32)
                + lax.dot_general(wr_hi, h_lo, dn, preferred_element_type=F32)
                + lax.dot_general(wr_lo, hb, dn, preferred_element_type=F32))
    ridx_ref[...], rw_ref[...] = _route(logits_t, rb_ref[...])
    s1 = jnp.dot(hb, ws1_ref[...], preferred_element_type=F32)
    s3 = jnp.dot(hb, ws3_ref[...], preferred_element_type=F32)
    shared = jnp.dot((s1 * jax.nn.sigmoid(s1) * s3).astype(BF16), ws2_ref[...], preferred_element_type=F32)
    xs_ref[...] = x1 + m[5:6, :] * shared


def _merge(ys, yf, gt, x, pos, mod, mod_row, n2, weights, tm):
    n = x.shape[0]
    has_pos = pos is not None
    row = lambda i: (i, 0)
    const = lambda a: pl.BlockSpec(a.shape, lambda i: (0,) * a.ndim)
    in_specs = [pl.BlockSpec((tm, D_S5), row), pl.BlockSpec((tm, D_FNET), row),
                pl.BlockSpec((tm, 2 * D_MODEL), row), pl.BlockSpec((tm, D_MODEL), row)]
    args = [ys, yf, gt, x]
    if has_pos:
        nper = pos.shape[0] // tm
        in_specs.append(pl.BlockSpec((tm, D_MODEL), lambda i: (i % nper, 0)))
        args.append(pos)
    in_specs += [pl.BlockSpec((1, N_MOD, D_MODEL), lambda i: (mod_row(i, tm), 0, 0)), const(n2)]
    args += [mod, n2]
    in_specs += [const(w) for w in weights]
    args += list(weights)
    return pl.pallas_call(
        functools.partial(_merge_kernel, has_pos),
        out_shape=(jax.ShapeDtypeStruct((n, D_MODEL), F32),
                   jax.ShapeDtypeStruct((n * ROW_SUB, LANES), F32),
                   jax.ShapeDtypeStruct((ROUTE_ROWS, n), jnp.int32),
                   jax.ShapeDtypeStruct((ROUTE_ROWS, n), F32)),
        grid=(n // tm,),
        in_specs=in_specs,
        out_specs=(pl.BlockSpec((tm, D_MODEL), row), pl.BlockSpec((tm * ROW_SUB, LANES), row),
                   pl.BlockSpec((ROUTE_ROWS, tm), lambda i: (0, i)),
                   pl.BlockSpec((ROUTE_ROWS, tm), lambda i: (0, i))),
        compiler_params=_params(("arbitrary",)),
        name="merge",
    )(*args)


MOE_SUB = 4096
MOE_TM = 128
MOE_TMAX = MOE_SUB * TOP_K // MOE_TM + N_EXPERTS
MOE_SORT_SUBS = 2
MOE_PAD = 2
MOE_TS = MOE_TMAX + 2 * MOE_PAD
MOE_DUMMY = 256
MOE_RMW = 16
MOE_FT = 256
ROW_TILE = MOE_TM * ROW_SUB


def _moe_plan(ridx, rw):
    n = ridx.shape[1]
    nsub = n // MOE_SUB
    npair = n * TOP_K
    t = jnp.arange(n, dtype=jnp.int32)
    key = (((t // MOE_SUB) * N_EXPERTS)[None] + ridx[:TOP_K]) * MOE_SUB + (t % MOE_SUB)[None]
    group = MOE_SORT_SUBS * MOE_SUB
    parts = [lax.sort((key[:, g:g + group].reshape(-1), rw[:TOP_K, g:g + group].reshape(-1)), num_keys=1)
             for g in range(0, n, group)]
    skey = jnp.concatenate([pk for pk, _ in parts])
    sw = jnp.concatenate([pw for _, pw in parts])
    stok = jnp.concatenate([(skey % MOE_SUB) * ROW_SUB, jnp.zeros((MOE_TM,), jnp.int32)])
    sw_rows = jnp.concatenate([sw, jnp.zeros((MOE_TM,), F32)]).reshape((npair + MOE_TM) // LANES, 1, LANES)
    hits = ridx[:TOP_K].reshape(TOP_K, nsub, 1, MOE_SUB) == jnp.arange(N_EXPERTS, dtype=jnp.int32)[None, None, :, None]
    cnt = jnp.sum(hits.astype(jnp.int32), axis=(0, 3))
    poff = (jnp.cumsum(cnt.reshape(-1)) - cnt.reshape(-1)).reshape(nsub, N_EXPERTS)
    ntile = (cnt + MOE_TM - 1) // MOE_TM
    tcum = jnp.cumsum(ntile, axis=1)
    toff = tcum - ntile
    tstart = jnp.concatenate([toff, tcum[:, -1:]], axis=1).reshape(-1).astype(jnp.int32)
    j = jnp.arange(MOE_TS, dtype=jnp.int32) - MOE_PAD
    valid = (j[None] >= 0) & (j[None] < tcum[:, -1:])
    te = jnp.minimum(jnp.sum(j[None, :, None] >= tcum[:, None, :], axis=-1), N_EXPERTS - 1)
    pick = lambda a: jnp.take_along_axis(a, te, axis=1)
    first = (j[None] - pick(toff)) * MOE_TM
    p0 = jnp.where(valid, pick(poff) + first, 0).reshape(-1).astype(jnp.int32)
    nv = jnp.where(valid, jnp.minimum(pick(cnt) - first, MOE_TM), 0).reshape(-1).astype(jnp.int32)
    return tstart, p0, nv, stok, sw_rows


def _moe_kernel(ts_ref, p0_ref, nv_ref, tok_ref, sw_ref, src_ref, w1_ref, w3_ref, w2_ref, xs_ref, mod_ref,
                fg_ref, o_ref, y_ref, xt_ref, xb_ref, act_ref, ot_ref, w1b_ref, w3b_ref, w2b_ref, slot_ref):
    sub = pl.program_id(0)
    e = pl.program_id(1)
    base = sub * MOE_TS + MOE_PAD
    ec = jnp.minimum(e, N_EXPERTS - 1)
    first = ts_ref[sub * (N_EXPERTS + 1) + ec]
    last = jnp.where(e < N_EXPERTS, ts_ref[sub * (N_EXPERTS + 1) + ec + 1], first)

    def gather(p0):
        for mi in range(MOE_TM):
            tok = pl.multiple_of(tok_ref[p0 + mi], ROW_SUB)
            xt_ref[mi * ROW_SUB:(mi + 1) * ROW_SUB, :] = src_ref[pl.ds(tok, ROW_SUB), :]
        for j in range(ROW_SUB):
            xb_ref[:, j * LANES:(j + 1) * LANES] = xt_ref[pl.ds(j, MOE_TM, stride=ROW_SUB), :].astype(BF16)

    def scatter(p0, nv, masked):
        for u in range(0, MOE_TM, MOE_RMW):
            new = []
            for i in range(MOE_RMW):
                tok = tok_ref[p0 + u + i]
                if masked:
                    tok = jnp.where(u + i < nv, tok, MOE_SUB * ROW_SUB)
                tok = pl.multiple_of(tok, ROW_SUB)
                new.append((tok, y_ref[pl.ds(tok, ROW_SUB), :]
                            + ot_ref[(u + i) * ROW_SUB:(u + i + 1) * ROW_SUB, :]))
            for tok, v in new:
                y_ref[pl.ds(tok, ROW_SUB), :] = v

    @pl.when(e == 0)
    def _():
        y_ref[...] = jnp.zeros_like(y_ref)
        ot_ref[...] = jnp.zeros_like(ot_ref)
        act_ref[...] = jnp.zeros_like(act_ref)
        w2b_ref[...] = jnp.zeros_like(w2b_ref)
        slot_ref[0] = 0
        gather(p0_ref[base])

    @pl.when(last > first)
    def _():
        slot_ref[0] = 1 - slot_ref[0]
        w1b_ref[...] = w1_ref[0].astype(BF16)
        w3b_ref[...] = w3_ref[0].astype(BF16)
        w2b_ref[slot_ref[0]] = w2_ref[0].astype(BF16)

    slot = slot_ref[0]

    def down_proj(w2_slot):
        o = jnp.dot(act_ref[...], w2b_ref[w2_slot], preferred_element_type=F32)
        for j in range(ROW_SUB):
            ot_ref[pl.ds(j, MOE_TM, stride=ROW_SUB), :] = o[:, j * LANES:(j + 1) * LANES]

    def step(i, masked):
        cur = base + i
        scatter(p0_ref[cur - 2], nv_ref[cur - 2], masked)
        down_proj(jnp.where(i > first, slot, 1 - slot))
        p0 = p0_ref[cur]
        nv = nv_ref[cur]
        x = xb_ref[...]
        a = jnp.dot(x, w1b_ref[...], preferred_element_type=F32)
        b = jnp.dot(x, w3b_ref[...], preferred_element_type=F32)
        r0 = p0 // LANES
        c = p0 % LANES
        lane = lax.broadcasted_iota(jnp.int32, (1, LANES), 1)
        rows = lax.broadcasted_iota(jnp.int32, (LANES, LANES), 0)
        cols = lax.broadcasted_iota(jnp.int32, (LANES, LANES), 1)
        gparts = []
        for hh in range(MOE_TM // LANES):
            ga = pltpu.roll(sw_ref[r0 + hh], LANES - c, axis=1)
            gb = pltpu.roll(sw_ref[r0 + hh + 1], LANES - c, axis=1)
            g = jnp.where(lane + hh * LANES < nv, jnp.where(lane < LANES - c, ga, gb), 0.0)
            gparts.append(jnp.sum(jnp.where(rows == cols, jnp.broadcast_to(g, (LANES, LANES)), 0.0),
                                  axis=1, keepdims=True))
        gcol = jnp.concatenate(gparts, axis=0)
        act_ref[...] = (a * jax.nn.sigmoid(a) * b * gcol).astype(BF16)
        gather(p0_ref[cur + 1])

    def body(i, carry):
        step(i, True)
        return carry

    lax.fori_loop(first, last, body, 0)

    @pl.when(e == N_EXPERTS - 1)
    def _():
        scatter(p0_ref[base + last - 2], nv_ref[base + last - 2], True)
        down_proj(slot)
        scatter(p0_ref[base + last - 1], nv_ref[base + last - 1], True)

    @pl.when(e >= N_EXPERTS)
    def _():
        row0 = pl.multiple_of((e - N_EXPERTS) * (MOE_FT * ROW_SUB), ROW_SUB)
        y = jnp.concatenate([y_ref[pl.ds(row0 + j, MOE_FT, stride=ROW_SUB), :] for j in range(ROW_SUB)],
                            axis=1)
        x2 = xs_ref[...] + mod_ref[0][5:6, :] * y
        o_ref[...] = _rms(x2, fg_ref[...])


def _moe(h2_rows, plan, w1, w3, w2, xs, mod, mod_row, fg):
    tstart, p0, nv, stok, sw_rows = plan
    nsub = h2_rows.shape[0] // (MOE_SUB * ROW_SUB)
    per_sub = MOE_SUB // MOE_FT
    wmap = lambda s, e, ts, p0, nv: (jnp.minimum(e, N_EXPERTS - 1), 0, 0)
    sub2 = lambda s, e, ts, p0, nv: (s, 0)
    out_tile = lambda s, e: s * per_sub + jnp.clip(e - N_EXPERTS, 0, per_sub - 1)
    grid_spec = pltpu.PrefetchScalarGridSpec(
        num_scalar_prefetch=3,
        grid=(nsub, N_EXPERTS + per_sub),
        in_specs=[pl.BlockSpec(memory_space=pltpu.SMEM),
                  pl.BlockSpec(sw_rows.shape, lambda s, e, ts, p0, nv: (0, 0, 0)),
                  pl.BlockSpec((MOE_SUB * ROW_SUB, LANES), sub2, pipeline_mode=pl.Buffered(1)),
                  pl.BlockSpec((1, D_MODEL, D_EXPERT), wmap),
                  pl.BlockSpec((1, D_MODEL, D_EXPERT), wmap),
                  pl.BlockSpec((1, D_EXPERT, D_MODEL), wmap),
                  pl.BlockSpec((MOE_FT, D_MODEL), lambda s, e, ts, p0, nv: (out_tile(s, e), 0)),
                  pl.BlockSpec((1, N_MOD, D_MODEL),
                               lambda s, e, ts, p0, nv: (mod_row(out_tile(s, e), MOE_FT), 0, 0)),
                  pl.BlockSpec((1, D_MODEL), lambda s, e, ts, p0, nv: (0, 0))],
        out_specs=pl.BlockSpec((MOE_FT, D_MODEL), lambda s, e, ts, p0, nv: (out_tile(s, e), 0)),
        scratch_shapes=[pltpu.VMEM(((MOE_SUB + MOE_DUMMY) * ROW_SUB, LANES), F32),
                        pltpu.VMEM((ROW_TILE, LANES), F32), pltpu.VMEM((MOE_TM, D_MODEL), BF16),
                        pltpu.VMEM((MOE_TM, D_EXPERT), BF16), pltpu.VMEM((ROW_TILE, LANES), F32),
                        pltpu.VMEM((D_MODEL, D_EXPERT), BF16), pltpu.VMEM((D_MODEL, D_EXPERT), BF16),
                        pltpu.VMEM((2, D_EXPERT, D_MODEL), BF16), pltpu.SMEM((1,), jnp.int32)])
    return pl.pallas_call(
        _moe_kernel,
        grid_spec=grid_spec,
        out_shape=jax.ShapeDtypeStruct(xs.shape, F32),
        compiler_params=_params(("arbitrary", "arbitrary")),
        name="moe",
    )(tstart, p0, nv, stok, sw_rows, h2_rows, w1, w3, w2, xs, mod, fg)


def _grid_pos_embed(n_tokens):
    rows = n_tokens // GRID_W
    r, col = jnp.meshgrid(jnp.arange(rows, dtype=F32), jnp.arange(GRID_W, dtype=F32), indexing="ij")
    quarter = D_MODEL // 4
    omega = 1.0 / (10000.0 ** (jnp.arange(quarter, dtype=F32) / quarter))

    def emb(p):
        a = p.reshape(-1)[:, None] * omega
        return jnp.concatenate([jnp.sin(a), jnp.cos(a)], axis=-1)

    return jnp.concatenate([emb(r), emb(col)], axis=-1)


def _mixers(x3, pos, mod, first_row, h0, s5_ops, p):
    nb, seq, _ = x3.shape
    n = nb * seq
    nk = seq // S5_CHUNK
    x = x3.reshape(n, D_MODEL)
    per_seq_mod = first_row > 0

    def mod_row(i, tm):
        return first_row + (i * tm) // seq if per_seq_mod else 0

    us, uf, gt = _inproj(x, pos, mod, mod_row, p["norm1_g"], p["w_in"], 512)
    ys, fin = _s5(us, s5_ops, h0, nb, nk)
    yf = _fourier(uf, nb, seq, min(seq, 512))
    xs, h2_rows, ridx, rw = _merge(ys, yf, gt, x, pos, mod, mod_row, p["norm2_g"], p["merge_w"], 512)
    return xs, h2_rows, ridx, rw, fin, mod_row


def _plan_of_stream(plan, s, n_tokens):
    tstart, p0, nv, stok, sw_rows = plan
    nsub = n_tokens // MOE_SUB
    npair = n_tokens * TOP_K
    off = s * npair
    tiles = slice(s * nsub * MOE_TS, (s + 1) * nsub * MOE_TS)
    nv_s = nv[tiles]
    p0_s = jnp.where(nv_s > 0, p0[tiles] - off, 0)
    return (tstart[s * nsub * (N_EXPERTS + 1):(s + 1) * nsub * (N_EXPERTS + 1)], p0_s, nv_s,
            stok[off:off + npair + MOE_TM], sw_rows[off // LANES:(off + npair + MOE_TM) // LANES])


def kernel(x_prompt, x_sample, state_s5_re, state_s5_im, c, c_ctx, w_ada, b_ada, norm1_g, norm2_g, w_in,
           lam_re, lam_im, log_dt, b_re, b_im, c_re, c_im, d_skip, w_glu, w_proj_s5, w_proj_f, w_out,
           w_router, router_bias, w1, w3, w2, ws1, ws3, ws2, final_norm_g):
    nb_ctx = x_prompt.shape[0]
    nb_lat, seq_lat, _ = x_sample.shape
    half = 2 * S5_STATE

    cond = jnp.concatenate([c_ctx[None], c, jnp.zeros((MOD_ROWS - 1 - nb_lat, D_MODEL), F32)], axis=0)
    mod = _adaln(cond, w_ada[0], b_ada[0]).reshape(MOD_ROWS, N_MOD, D_MODEL)

    s5_ops = _s5ops(lam_re[0], lam_im[0], log_dt[0], b_re[0], b_im[0], c_re[0], c_im[0], d_skip[0])
    p = dict(
        norm1_g=norm1_g[0][None], norm2_g=norm2_g[0][None], final_g=final_norm_g[None],
        w_in=w_in[0].astype(BF16), w1=w1[0], w3=w3[0], w2=w2[0],
        merge_w=(w_glu[0].astype(BF16), w_proj_s5[0].astype(BF16), w_proj_f[0].astype(BF16),
                 w_out[0].astype(BF16), w_router[0].T, router_bias[0][:, None],
                 ws1[0].astype(BF16), ws3[0].astype(BF16), ws2[0].astype(BF16)))

    def pack_state(sr, si):
        f = lambda a: a.astype(F32).transpose(2, 0, 1, 3).reshape(S5_GROUPS, a.shape[0], half)
        return jnp.concatenate([f(sr), f(si)], axis=-1)

    def unpack_state(fin, lo):
        nb = fin.shape[1]
        return fin[..., lo:lo + half].reshape(S5_GROUPS, nb, 2, S5_STATE).transpose(1, 2, 0, 3)[:, None]

    h0_ctx = jnp.zeros((S5_GROUPS, nb_ctx, 2 * half), F32)
    h0_lat = pack_state(state_s5_re[:, 0], state_s5_im[:, 0])
    streams = [(x_prompt, _mixers(x_prompt, None, mod, 0, h0_ctx, s5_ops, p)),
               (x_sample, _mixers(x_sample, _grid_pos_embed(seq_lat), mod, 1, h0_lat, s5_ops, p))]
    n_tokens = streams[0][1][0].shape[0]
    plan = _moe_plan(jnp.concatenate([m[2] for _, m in streams], axis=1),
                     jnp.concatenate([m[3] for _, m in streams], axis=1))
    outs = []
    for s, (x3, (xs, h2_rows, _, _, _, mod_row)) in enumerate(streams):
        out = _moe(h2_rows, _plan_of_stream(plan, s, n_tokens), p["w1"], p["w3"], p["w2"],
                   xs, mod, mod_row, p["final_g"])
        outs.append(out.reshape(x3.shape))
    fin = streams[0][1][4]
    return (outs[0], outs[1], unpack_state(fin, 0).astype(x_prompt.dtype),
            unpack_state(fin, half).astype(x_prompt.dtype))
```

```python
import functools
import math

import jax
import jax.numpy as jnp
from jax import lax
from jax.experimental import pallas as pl
from jax.experimental.pallas import tpu as pltpu

D_MODEL = 1024
GRID_W = 64
D_S5 = 768
S5_GROUP = 16
S5_GROUPS = 48
S5_STATE = 64
D_FNET = 256
FNET_GROUP = 64
N_EXPERTS = 64
TOP_K = 6
N_EXPERT_GROUPS = 8
EXPERTS_PER_GROUP = N_EXPERTS // N_EXPERT_GROUPS
TOPK_GROUPS = 4
D_EXPERT = 256
ROUTED_SCALE = 2.5
N_MOD = 6
EPS = 1e-6

S5_CHUNK = 16
S5_ROW = S5_CHUNK * S5_GROUP
MOD_ROWS = 8
ROUTE_ROWS = 8
LANES = 128
ROW_SUB = D_MODEL // LANES
VMEM_LIMIT = 56 * 1024 * 1024

BF16 = jnp.bfloat16
F32 = jnp.float32


def _params(sem, vmem=VMEM_LIMIT):
    return pltpu.CompilerParams(dimension_semantics=sem, vmem_limit_bytes=vmem)


def _rms(x, g):
    return x * lax.rsqrt(jnp.mean(x * x, axis=-1, keepdims=True) + EPS) * g


def _adaln_kernel(c_ref, w_ref, b_ref, o_ref):
    c = c_ref[...]
    o_ref[...] = jnp.dot(c * jax.nn.sigmoid(c), w_ref[...], precision=lax.Precision.HIGHEST,
                         preferred_element_type=F32) + b_ref[...]


def _adaln(cond, w_ada, b_ada):
    n_out = N_MOD * D_MODEL
    return pl.pallas_call(
        _adaln_kernel,
        out_shape=jax.ShapeDtypeStruct((MOD_ROWS, n_out), F32),
        grid=(N_MOD,),
        in_specs=[pl.BlockSpec((MOD_ROWS, D_MODEL), lambda i: (0, 0)),
                  pl.BlockSpec((D_MODEL, D_MODEL), lambda i: (0, i)),
                  pl.BlockSpec((1, D_MODEL), lambda i: (0, i))],
        out_specs=pl.BlockSpec((MOD_ROWS, D_MODEL), lambda i: (0, i)),
        compiler_params=_params(("arbitrary",)),
        name="adaln",
    )(cond, w_ada, b_ada.reshape(1, n_out))


def _inproj_kernel(has_pos, *refs):
    if has_pos:
        x_ref, pos_ref, mod_ref, g_ref, w_ref, us_ref, uf_ref, gt_ref = refs
        x = x_ref[...] + pos_ref[...]
    else:
        x_ref, mod_ref, g_ref, w_ref, us_ref, uf_ref, gt_ref = refs
        x = x_ref[...]
    m = mod_ref[0]
    h = _rms(x, g_ref[...]) * (1.0 + m[1:2, :]) + m[0:1, :]
    p = jnp.dot(h.astype(BF16), w_ref[...], preferred_element_type=F32)
    us_ref[...] = p[:, :D_S5]
    uf_ref[...] = p[:, D_S5:D_MODEL].astype(BF16)
    gt_ref[...] = jax.nn.sigmoid(p[:, D_MODEL:]).astype(BF16)


def _inproj(x, pos, mod, mod_row, norm_g, w_in_bf, tm):
    n = x.shape[0]
    has_pos = pos is not None
    row = lambda i: (i, 0)
    in_specs = [pl.BlockSpec((tm, D_MODEL), row)]
    args = [x]
    if has_pos:
        nper = pos.shape[0] // tm
        in_specs.append(pl.BlockSpec((tm, D_MODEL), lambda i: (i % nper, 0)))
        args.append(pos)
    in_specs += [pl.BlockSpec((1, N_MOD, D_MODEL), lambda i: (mod_row(i, tm), 0, 0)),
                 pl.BlockSpec((1, D_MODEL), lambda i: (0, 0)),
                 pl.BlockSpec(w_in_bf.shape, lambda i: (0, 0))]
    args += [mod, norm_g, w_in_bf]
    return pl.pallas_call(
        functools.partial(_inproj_kernel, has_pos),
        out_shape=(jax.ShapeDtypeStruct((n, D_S5), F32),
                   jax.ShapeDtypeStruct((n, D_FNET), BF16),
                   jax.ShapeDtypeStruct((n, 2 * D_MODEL), BF16)),
        grid=(n // tm,),
        in_specs=in_specs,
        out_specs=(pl.BlockSpec((tm, D_S5), row), pl.BlockSpec((tm, D_FNET), row),
                   pl.BlockSpec((tm, 2 * D_MODEL), row)),
        compiler_params=_params(("arbitrary",)),
        name="inproj",
    )(*args)


def _shift_lanes(x, k):
    if k == 0:
        return x
    z = jnp.zeros((x.shape[0], abs(k)), x.dtype)
    if k > 0:
        return jnp.concatenate([z, x[:, :x.shape[1] - k]], axis=1)
    return jnp.concatenate([x[:, -k:], z], axis=1)


def _s5ops_kernel(lam_re_ref, lam_im_ref, dt_ref, btr_ref, bti_ref, cr_ref, ci_ref, d_ref,
                  m_ref, wi_ref, wot_ref, coef_ref):
    hi = lax.Precision.HIGHEST
    lr = jnp.minimum(lam_re_ref[0], -1e-4)
    li = lam_im_ref[0]
    dt = jnp.exp(dt_ref[0])
    mag = jnp.exp(lr * dt)
    ar = mag * jnp.cos(li * dt)
    ai = mag * jnp.sin(li * dt)
    den = lr * lr + li * li
    nr = ar - 1.0
    qr = (nr * lr + ai * li) / den
    qi = (ai * lr - nr * li) / den
    pr, pi = [], []
    for n in range(S5_CHUNK + 1):
        pm = jnp.exp(float(n) * (lr * dt))
        pr.append(pm * jnp.cos(float(n) * (li * dt)))
        pi.append(pm * jnp.sin(float(n) * (li * dt)))
    bbr, bbi, car, cai = [], [], [], []
    for d in range(2):
        btr = btr_ref[0, d]
        bti = bti_ref[0, d]
        bbr.append(qr[d:d + 1] * btr - qi[d:d + 1] * bti)
        bbi.append(qr[d:d + 1] * bti + qi[d:d + 1] * btr)
        cr = cr_ref[0, d]
        ci = ci_ref[0, d]
        car.append([cr * pr[n][d:d + 1] - ci * pi[n][d:d + 1] for n in range(S5_CHUNK + 1)])
        cai.append([cr * pi[n][d:d + 1] + ci * pr[n][d:d + 1] for n in range(S5_CHUNK + 1)])

    def lag_kernels(d, order):
        a = jnp.concatenate([car[d][n] for n in order], axis=0)
        b = jnp.concatenate([cai[d][n] for n in order], axis=0)
        dn = (((1,), (1,)), ((), ()))
        return (lax.dot_general(bbr[d], a, dn, precision=hi, preferred_element_type=F32)
                - lax.dot_general(bbi[d], b, dn, precision=hi, preferred_element_type=F32))

    ktf = lag_kernels(0, range(S5_CHUNK))
    ktb = lag_kernels(1, range(S5_CHUNK - 1, -1, -1))
    row = lax.broadcasted_iota(jnp.int32, (S5_GROUP, S5_ROW), 0)
    lane = lax.broadcasted_iota(jnp.int32, (S5_GROUP, S5_ROW), 1)
    dcol = d_ref[0]
    for j in range(S5_CHUNK):
        rows = slice(j * S5_GROUP, (j + 1) * S5_GROUP)
        blk = _shift_lanes(ktf, S5_GROUP * j) + _shift_lanes(ktb, -S5_GROUP * (S5_CHUNK - 1 - j))
        blk = blk + jnp.where(lane == S5_GROUP * j + row, dcol, 0.0)
        m_ref[0, rows, :] = blk.astype(BF16)
        nf = S5_CHUNK - 1 - j
        wi = jnp.concatenate([pr[nf][0:1] * bbr[0] - pi[nf][0:1] * bbi[0],
                              pr[j][1:2] * bbr[1] - pi[j][1:2] * bbi[1],
                              pr[nf][0:1] * bbi[0] + pi[nf][0:1] * bbr[0],
                              pr[j][1:2] * bbi[1] + pi[j][1:2] * bbr[1]], axis=1)
        wi_ref[0, rows, :] = wi.astype(BF16)
        wot = jnp.concatenate([car[0][j + 1], car[1][S5_CHUNK - j],
                               -cai[0][j + 1], -cai[1][S5_CHUNK - j]], axis=1)
        wot_ref[0, rows, :] = wot.astype(BF16)
    coef_ref[0, 0:1, :] = jnp.concatenate([pr[S5_CHUNK][0:1], pr[S5_CHUNK][1:2]], axis=1)
    coef_ref[0, 1:2, :] = jnp.concatenate([pi[S5_CHUNK][0:1], pi[S5_CHUNK][1:2]], axis=1)


def _s5ops(lam_re, lam_im, log_dt, b_re, b_im, c_re, c_im, d_skip):
    g3 = lambda g: (g, 0, 0)
    g4 = lambda g: (g, 0, 0, 0)
    sw = lambda a: jnp.swapaxes(a.astype(F32), 0, 1)
    dt = jnp.broadcast_to(sw(log_dt)[..., None], (S5_GROUPS, 2, S5_STATE))
    args = (sw(lam_re), sw(lam_im), dt, sw(jnp.swapaxes(b_re, 2, 3)), sw(jnp.swapaxes(b_im, 2, 3)),
            sw(c_re), sw(c_im), d_skip.astype(F32).reshape(S5_GROUPS, S5_GROUP, 1))
    vec = pl.BlockSpec((1, 2, S5_STATE), g3)
    mat = pl.BlockSpec((1, 2, S5_GROUP, S5_STATE), g4)
    op = pl.BlockSpec((1, S5_ROW, S5_ROW), g3)
    return pl.pallas_call(
        _s5ops_kernel,
        out_shape=(jax.ShapeDtypeStruct((S5_GROUPS, S5_ROW, S5_ROW), BF16),) * 3
        + (jax.ShapeDtypeStruct((S5_GROUPS, 2, 2 * S5_STATE), F32),),
        grid=(S5_GROUPS,),
        in_specs=[vec, vec, vec, mat, mat, mat, mat, pl.BlockSpec((1, S5_GROUP, 1), g3)],
        out_specs=(op, op, op, pl.BlockSpec((1, 2, 2 * S5_STATE), g3)),
        compiler_params=_params(("arbitrary",)),
        name="s5ops",
    )(*args)


S5_BLOCK_GROUPS = LANES // S5_GROUP


S5_HALF_T = LANES // S5_GROUP


def _s5_perm():
    a = jnp.arange(S5_HALF_T * LANES, dtype=jnp.int32)
    dst = ((a % LANES) // S5_GROUP) * LANES + (a // LANES) * S5_GROUP + a % S5_GROUP
    return (dst[:, None] == a[None, :]).astype(BF16)


def _s5_kernel(nb, nk, u_ref, perm_ref, m_ref, wi_ref, wot_ref, coef_ref, h0_ref, y_ref, fin_ref,
               sr_ref, si_ref, efr_ref, ebr_ref, efi_ref, ebi_ref, uall_ref, yall_ref):
    gl = pl.program_id(1)
    rows = nb * nk
    half = 2 * S5_STATE

    @pl.when(gl == 0)
    def _():
        for hh in range(S5_CHUNK // S5_HALF_T):
            xcat = jnp.concatenate([u_ref[pl.ds(hh * S5_HALF_T + tt, rows, stride=S5_CHUNK), :].astype(BF16)
                                    for tt in range(S5_HALF_T)], axis=1)
            uh = jnp.dot(xcat, perm_ref[...], preferred_element_type=F32).astype(BF16)
            for g in range(S5_BLOCK_GROUPS):
                uall_ref[g, :, hh * LANES:(hh + 1) * LANES] = uh[:, g * LANES:(g + 1) * LANES]

    u = uall_ref[gl]
    s = jnp.dot(u, wi_ref[0], preferred_element_type=F32)
    sr_ref[...] = s[:, :half]
    si_ref[...] = s[:, half:]
    c_r = coef_ref[0, 0:1, :]
    c_i = coef_ref[0, 1:2, :]
    e_r = h0_ref[0, :, :half]
    e_i = h0_ref[0, :, half:]
    is_fwd = lax.broadcasted_iota(jnp.int32, (nb, half), 1) < S5_STATE
    for j in range(nk):
        rf = pl.ds(j, nb, stride=nk)
        rb = pl.ds(nk - 1 - j, nb, stride=nk)
        efr_ref[rf, :] = e_r
        ebr_ref[rb, :] = e_r
        efi_ref[rf, :] = e_i
        ebi_ref[rb, :] = e_i
        s_r = jnp.where(is_fwd, sr_ref[rf, :], sr_ref[rb, :])
        s_i = jnp.where(is_fwd, si_ref[rf, :], si_ref[rb, :])
        e_r, e_i = c_r * e_r - c_i * e_i + s_r, c_r * e_i + c_i * e_r + s_i
    fin_ref[0, :, :half] = e_r
    fin_ref[0, :, half:] = e_i
    y = jnp.dot(u, m_ref[0], preferred_element_type=F32)
    fwd_rows = lax.broadcasted_iota(jnp.int32, (rows, half), 1) < S5_STATE
    e = jnp.concatenate([jnp.where(fwd_rows, efr_ref[...], ebr_ref[...]),
                         jnp.where(fwd_rows, efi_ref[...], ebi_ref[...])], axis=1).astype(BF16)
    y = y + lax.dot_general(e, wot_ref[0], (((1,), (1,)), ((), ())), preferred_element_type=F32)
    yall_ref[gl] = y.astype(BF16)

    @pl.when(gl == S5_BLOCK_GROUPS - 1)
    def _():
        for hh in range(S5_CHUNK // S5_HALF_T):
            ycat = jnp.concatenate([yall_ref[g, :, hh * LANES:(hh + 1) * LANES]
                                    for g in range(S5_BLOCK_GROUPS)], axis=1)
            out = lax.dot_general(ycat, perm_ref[...], (((1,), (1,)), ((), ())), preferred_element_type=F32)
            for tt in range(S5_HALF_T):
                y_ref[pl.ds(hh * S5_HALF_T + tt, rows, stride=S5_CHUNK), :] = out[:, tt * LANES:(tt + 1) * LANES]


def _s5(u, ops, h0, nb, nk):
    m, w_in, w_out, coef = ops
    n = u.shape[0]
    rows = nb * nk
    perm = _s5_perm()
    g3 = lambda b, g: (b * S5_BLOCK_GROUPS + g, 0, 0)
    blk = lambda b, g: (0, b)
    return pl.pallas_call(
        functools.partial(_s5_kernel, nb, nk),
        out_shape=(jax.ShapeDtypeStruct((n, D_S5), F32),
                   jax.ShapeDtypeStruct((S5_GROUPS, nb, 4 * S5_STATE), F32)),
        grid=(S5_GROUPS // S5_BLOCK_GROUPS, S5_BLOCK_GROUPS),
        in_specs=[pl.BlockSpec((n, LANES), blk),
                  pl.BlockSpec(perm.shape, lambda b, g: (0, 0)),
                  pl.BlockSpec((1, S5_ROW, S5_ROW), g3),
                  pl.BlockSpec((1, S5_ROW, 4 * S5_STATE), g3),
                  pl.BlockSpec((1, 4 * S5_STATE, S5_ROW), g3),
                  pl.BlockSpec((1, 2, 2 * S5_STATE), g3),
                  pl.BlockSpec((1, nb, 4 * S5_STATE), g3)],
        out_specs=(pl.BlockSpec((n, LANES), blk),
                   pl.BlockSpec((1, nb, 4 * S5_STATE), g3)),
        scratch_shapes=[pltpu.VMEM((rows, 2 * S5_STATE), F32)] * 6 + [
            pltpu.VMEM((S5_BLOCK_GROUPS, rows, S5_ROW), BF16),
            pltpu.VMEM((S5_BLOCK_GROUPS, rows, S5_ROW), BF16)],
        compiler_params=_params(("arbitrary", "arbitrary")),
        name="s5",
    )(u, perm, m, w_in, w_out, coef, h0)


def _dft_tables(seq):
    k = jnp.arange(seq, dtype=jnp.int32)
    na = seq // FNET_GROUP
    ang_a = (2.0 * math.pi / na) * ((jnp.arange(na, dtype=jnp.int32)[:, None] * k[None, :]) % na).astype(F32)
    ang_b = (2.0 * math.pi / seq) * ((jnp.arange(FNET_GROUP, dtype=jnp.int32)[:, None] * k[None, :]) % seq).astype(F32)
    ca, sa = jnp.cos(ang_a)[:, None, :], jnp.sin(ang_a)[:, None, :]
    cb, sb = jnp.cos(ang_b)[None, :, :], jnp.sin(ang_b)[None, :, :]
    cos_jk = (ca * cb - sa * sb).reshape(seq, seq)
    sin_jk = (sa * cb + ca * sb).reshape(seq, seq)
    cs = jnp.concatenate([cos_jk, -sin_jk], axis=1).astype(BF16)
    c = jnp.arange(D_FNET, dtype=jnp.int32)
    same = (c[:, None] // FNET_GROUP) == (c[None, :] // FNET_GROUP)
    angc = (2.0 * math.pi / FNET_GROUP) * (((c[:, None] % FNET_GROUP) * (c[None, :] % FNET_GROUP))
                                           % FNET_GROUP).astype(F32)
    scale = 1.0 / math.sqrt(seq * FNET_GROUP)
    bdc = jnp.where(same, jnp.cos(angc) * scale, 0.0).astype(BF16)
    bds = jnp.where(same, jnp.sin(angc) * scale, 0.0).astype(BF16)
    return cs, bdc, bds


def _fourier_kernel(seq, z_ref, cs_ref, bdc_ref, bds_ref, o_ref, zz_ref):
    @pl.when(pl.program_id(1) == 0)
    def _():
        z = z_ref[...]
        zz_ref[0:seq, :] = jnp.dot(z, bdc_ref[...], preferred_element_type=F32).astype(BF16)
        zz_ref[seq:, :] = jnp.dot(z, bds_ref[...], preferred_element_type=F32).astype(BF16)

    o_ref[...] = jnp.dot(cs_ref[...], zz_ref[...], preferred_element_type=F32).astype(BF16)


def _fourier(z, nb, seq, tl):
    cs, bdc, bds = _dft_tables(seq)
    nt = seq // tl
    return pl.pallas_call(
        functools.partial(_fourier_kernel, seq),
        out_shape=jax.ShapeDtypeStruct(z.shape, BF16),
        grid=(nb, nt),
        in_specs=[pl.BlockSpec((seq, D_FNET), lambda b, i: (b, 0)),
                  pl.BlockSpec((tl, 2 * seq), lambda b, i: (i, 0)),
                  pl.BlockSpec((D_FNET, D_FNET), lambda b, i: (0, 0)),
                  pl.BlockSpec((D_FNET, D_FNET), lambda b, i: (0, 0))],
        out_specs=pl.BlockSpec((tl, D_FNET), lambda b, i: (b * nt + i, 0)),
        scratch_shapes=[pltpu.VMEM((2 * seq, D_FNET), BF16)],
        compiler_params=_params(("arbitrary", "arbitrary")),
        name="fourier",
    )(z, cs, bdc, bds)


def _first_argmax_mask(v, iota, size):
    m = jnp.max(v, axis=0, keepdims=True)
    first = jnp.min(jnp.where(v == m, iota, size), axis=0, keepdims=True)
    return iota == first


def _route(logits_t, bias_col):
    tm = logits_t.shape[1]
    neg = -jnp.inf
    s = jax.nn.sigmoid(logits_t)
    biased = s + bias_col
    io8 = lax.broadcasted_iota(jnp.int32, (EXPERTS_PER_GROUP, tm), 0)
    gs_rows = []
    for g in range(N_EXPERT_GROUPS):
        blk = biased[g * EXPERTS_PER_GROUP:(g + 1) * EXPERTS_PER_GROUP, :]
        m1 = jnp.max(blk, axis=0, keepdims=True)
        rest = jnp.where(_first_argmax_mask(blk, io8, EXPERTS_PER_GROUP), neg, blk)
        gs_rows.append(m1 + jnp.max(rest, axis=0, keepdims=True))
    gs = jnp.concatenate(gs_rows, axis=0)
    iog = lax.broadcasted_iota(jnp.int32, (N_EXPERT_GROUPS, tm), 0)
    gsel = jnp.zeros((N_EXPERT_GROUPS, tm), F32)
    for _ in range(TOPK_GROUPS):
        sel = _first_argmax_mask(gs, iog, N_EXPERT_GROUPS)
        gsel = jnp.where(sel, 1.0, gsel)
        gs = jnp.where(sel, neg, gs)
    emask = jnp.concatenate(
        [jnp.broadcast_to(gsel[g:g + 1, :], (EXPERTS_PER_GROUP, tm)) for g in range(N_EXPERT_GROUPS)], axis=0)
    v = jnp.where(emask > 0.0, biased, neg)
    ioe = lax.broadcasted_iota(jnp.int32, (N_EXPERTS, tm), 0)
    idx_rows, s_rows = [], []
    for _ in range(TOP_K):
        sel = _first_argmax_mask(v, ioe, N_EXPERTS)
        idx_rows.append(jnp.sum(jnp.where(sel, ioe, 0), axis=0, keepdims=True))
        s_rows.append(jnp.sum(jnp.where(sel, s, 0.0), axis=0, keepdims=True))
        v = jnp.where(sel, neg, v)
    denom = s_rows[0]
    for r in s_rows[1:]:
        denom = denom + r
    pad = ROUTE_ROWS - TOP_K
    idx = jnp.concatenate(idx_rows + [jnp.zeros((pad, tm), jnp.int32)], axis=0)
    w = jnp.concatenate([r / denom * ROUTED_SCALE for r in s_rows] + [jnp.zeros((pad, tm), F32)], axis=0)
    return idx, w


def _merge_kernel(has_pos, *refs):
    if has_pos:
        (ys_ref, yf_ref, gt_ref, x_ref, pos_ref, mod_ref, n2_ref, wglu_ref, wps_ref, wpf_ref, wout_ref,
         wrt_ref, rb_ref, ws1_ref, ws3_ref, ws2_ref, xs_ref, h2_ref, ridx_ref, rw_ref) = refs
        x = x_ref[...] + pos_ref[...]
    else:
        (ys_ref, yf_ref, gt_ref, x_ref, mod_ref, n2_ref, wglu_ref, wps_ref, wpf_ref, wout_ref,
         wrt_ref, rb_ref, ws1_ref, ws3_ref, ws2_ref, xs_ref, h2_ref, ridx_ref, rw_ref) = refs
        x = x_ref[...]
    m = mod_ref[0]
    g = jax.nn.gelu(ys_ref[...].astype(F32))
    a = g * jax.nn.sigmoid(jnp.dot(g.astype(BF16), wglu_ref[...], preferred_element_type=F32))
    pa = jnp.dot(a.astype(BF16), wps_ref[...], preferred_element_type=F32)
    pb = jnp.dot(yf_ref[...], wpf_ref[...], preferred_element_type=F32)
    gt = gt_ref[...].astype(F32)
    merged = gt[:, :D_MODEL] * pa + gt[:, D_MODEL:] * pb
    x1 = x + m[2:3, :] * jnp.dot(merged.astype(BF16), wout_ref[...], preferred_element_type=F32)
    h2 = _rms(x1, n2_ref[...]) * (1.0 + m[4:5, :]) + m[3:4, :]
    hb = h2.astype(BF16)
    for j in range(ROW_SUB):
        h2_ref[pl.ds(j, h2.shape[0], stride=ROW_SUB), :] = h2[:, j * LANES:(j + 1) * LANES]
    wr = wrt_ref[...]
    wr_hi = wr.astype(BF16)
    wr_lo = (wr - wr_hi.astype(F32)).astype(BF16)
    h_lo = (h2 - hb.astype(F32)).astype(BF16)
    dn = (((1,), (1,)), ((), ()))
    logits_t = (lax.dot_general(wr_hi, hb, dn, preferred_element_type=F32)
                + lax.dot_general(wr_hi, h_lo, dn, preferred_element_type=F32)
                + lax.dot_general(wr_lo, hb, dn, preferred_element_type=F32))
    ridx_ref[...], rw_ref[...] = _route(logits_t, rb_ref[...])
    s1 = jnp.dot(hb, ws1_ref[...], preferred_element_type=F32)
    s3 = jnp.dot(hb, ws3_ref[...], preferred_element_type=F32)
    shared = jnp.dot((s1 * jax.nn.sigmoid(s1) * s3).astype(BF16), ws2_ref[...], preferred_element_type=F32)
    xs_ref[...] = x1 + m[5:6, :] * shared


def _merge(ys, yf, gt, x, pos, mod, mod_row, n2, weights, tm):
    n = x.shape[0]
    has_pos = pos is not None
    row = lambda i: (i, 0)
    const = lambda a: pl.BlockSpec(a.shape, lambda i: (0,) * a.ndim)
    in_specs = [pl.BlockSpec((tm, D_S5), row), pl.BlockSpec((tm, D_FNET), row),
                pl.BlockSpec((tm, 2 * D_MODEL), row), pl.BlockSpec((tm, D_MODEL), row)]
    args = [ys, yf, gt, x]
    if has_pos:
        nper = pos.shape[0] // tm
        in_specs.append(pl.BlockSpec((tm, D_MODEL), lambda i: (i % nper, 0)))
        args.append(pos)
    in_specs += [pl.BlockSpec((1, N_MOD, D_MODEL), lambda i: (mod_row(i, tm), 0, 0)), const(n2)]
    args += [mod, n2]
    in_specs += [const(w) for w in weights]
    args += list(weights)
    return pl.pallas_call(
        functools.partial(_merge_kernel, has_pos),
        out_shape=(jax.ShapeDtypeStruct((n, D_MODEL), F32),
                   jax.ShapeDtypeStruct((n * ROW_SUB, LANES), F32),
                   jax.ShapeDtypeStruct((ROUTE_ROWS, n), jnp.int32),
                   jax.ShapeDtypeStruct((ROUTE_ROWS, n), F32)),
        grid=(n // tm,),
        in_specs=in_specs,
        out_specs=(pl.BlockSpec((tm, D_MODEL), row), pl.BlockSpec((tm * ROW_SUB, LANES), row),
                   pl.BlockSpec((ROUTE_ROWS, tm), lambda i: (0, i)),
                   pl.BlockSpec((ROUTE_ROWS, tm), lambda i: (0, i))),
        compiler_params=_params(("arbitrary",)),
        name="merge",
    )(*args)


MOE_SUB = 4096
MOE_TM = 128
MOE_TMAX = MOE_SUB * TOP_K // MOE_TM + N_EXPERTS
MOE_SORT_SUBS = 2
MOE_PAD = 2
MOE_TS = MOE_TMAX + 2 * MOE_PAD
MOE_DUMMY = 256
MOE_RMW = 16
MOE_FT = 256
ROW_TILE = MOE_TM * ROW_SUB


def _moe_plan(ridx, rw):
    n = ridx.shape[1]
    nsub = n // MOE_SUB
    npair = n * TOP_K
    t = jnp.arange(n, dtype=jnp.int32)
    key = (((t // MOE_SUB) * N_EXPERTS)[None] + ridx[:TOP_K]) * MOE_SUB + (t % MOE_SUB)[None]
    group = MOE_SORT_SUBS * MOE_SUB
    parts = [lax.sort((key[:, g:g + group].reshape(-1), rw[:TOP_K, g:g + group].reshape(-1)), num_keys=1)
             for g in range(0, n, group)]
    skey = jnp.concatenate([pk for pk, _ in parts])
    sw = jnp.concatenate([pw for _, pw in parts])
    stok = jnp.concatenate([(skey % MOE_SUB) * ROW_SUB, jnp.zeros((MOE_TM,), jnp.int32)])
    sw_rows = jnp.concatenate([sw, jnp.zeros((MOE_TM,), F32)]).reshape((npair + MOE_TM) // LANES, 1, LANES)
    hits = ridx[:TOP_K].reshape(TOP_K, nsub, 1, MOE_SUB) == jnp.arange(N_EXPERTS, dtype=jnp.int32)[None, None, :, None]
    cnt = jnp.sum(hits.astype(jnp.int32), axis=(0, 3))
    poff = (jnp.cumsum(cnt.reshape(-1)) - cnt.reshape(-1)).reshape(nsub, N_EXPERTS)
    ntile = (cnt + MOE_TM - 1) // MOE_TM
    tcum = jnp.cumsum(ntile, axis=1)
    toff = tcum - ntile
    tstart = jnp.concatenate([toff, tcum[:, -1:]], axis=1).reshape(-1).astype(jnp.int32)
    j = jnp.arange(MOE_TS, dtype=jnp.int32) - MOE_PAD
    valid = (j[None] >= 0) & (j[None] < tcum[:, -1:])
    te = jnp.minimum(jnp.sum(j[None, :, None] >= tcum[:, None, :], axis=-1), N_EXPERTS - 1)
    pick = lambda a: jnp.take_along_axis(a, te, axis=1)
    first = (j[None] - pick(toff)) * MOE_TM
    p0 = jnp.where(valid, pick(poff) + first, 0).reshape(-1).astype(jnp.int32)
    nv = jnp.where(valid, jnp.minimum(pick(cnt) - first, MOE_TM), 0).reshape(-1).astype(jnp.int32)
    return tstart, p0, nv, stok, sw_rows


def _moe_kernel(ts_ref, p0_ref, nv_ref, tok_ref, sw_ref, src_ref, w1_ref, w3_ref, w2_ref, xs_ref, mod_ref,
                fg_ref, o_ref, y_ref, xt_ref, xb_ref, act_ref, ot_ref, w1b_ref, w3b_ref, w2b_ref, slot_ref):
    sub = pl.program_id(0)
    e = pl.program_id(1)
    base = sub * MOE_TS + MOE_PAD
    ec = jnp.minimum(e, N_EXPERTS - 1)
    first = ts_ref[sub * (N_EXPERTS + 1) + ec]
    last = jnp.where(e < N_EXPERTS, ts_ref[sub * (N_EXPERTS + 1) + ec + 1], first)

    def gather(p0):
        for mi in range(MOE_TM):
            tok = pl.multiple_of(tok_ref[p0 + mi], ROW_SUB)
            xt_ref[mi * ROW_SUB:(mi + 1) * ROW_SUB, :] = src_ref[pl.ds(tok, ROW_SUB), :]
        for j in range(ROW_SUB):
            xb_ref[:, j * LANES:(j + 1) * LANES] = xt_ref[pl.ds(j, MOE_TM, stride=ROW_SUB), :].astype(BF16)

    def scatter(p0, nv, masked):
        for u in range(0, MOE_TM, MOE_RMW):
            new = []
            for i in range(MOE_RMW):
                tok = tok_ref[p0 + u + i]
                if masked:
                    tok = jnp.where(u + i < nv, tok, MOE_SUB * ROW_SUB)
                tok = pl.multiple_of(tok, ROW_SUB)
                new.append((tok, y_ref[pl.ds(tok, ROW_SUB), :]
                            + ot_ref[(u + i) * ROW_SUB:(u + i + 1) * ROW_SUB, :]))
            for tok, v in new:
                y_ref[pl.ds(tok, ROW_SUB), :] = v

    @pl.when(e == 0)
    def _():
        y_ref[...] = jnp.zeros_like(y_ref)
        ot_ref[...] = jnp.zeros_like(ot_ref)
        act_ref[...] = jnp.zeros_like(act_ref)
        w2b_ref[...] = jnp.zeros_like(w2b_ref)
        slot_ref[0] = 0
        gather(p0_ref[base])

    @pl.when(last > first)
    def _():
        slot_ref[0] = 1 - slot_ref[0]
        w1b_ref[...] = w1_ref[0].astype(BF16)
        w3b_ref[...] = w3_ref[0].astype(BF16)
        w2b_ref[slot_ref[0]] = w2_ref[0].astype(BF16)

    slot = slot_ref[0]

    def down_proj(w2_slot):
        o = jnp.dot(act_ref[...], w2b_ref[w2_slot], preferred_element_type=F32)
        for j in range(ROW_SUB):
            ot_ref[pl.ds(j, MOE_TM, stride=ROW_SUB), :] = o[:, j * LANES:(j + 1) * LANES]

    def step(i, masked):
        cur = base + i
        scatter(p0_ref[cur - 2], nv_ref[cur - 2], masked)
        down_proj(jnp.where(i > first, slot, 1 - slot))
        p0 = p0_ref[cur]
        nv = nv_ref[cur]
        x = xb_ref[...]
        a = jnp.dot(x, w1b_ref[...], preferred_element_type=F32)
        b = jnp.dot(x, w3b_ref[...], preferred_element_type=F32)
        r0 = p0 // LANES
        c = p0 % LANES
        lane = lax.broadcasted_iota(jnp.int32, (1, LANES), 1)
        rows = lax.broadcasted_iota(jnp.int32, (LANES, LANES), 0)
        cols = lax.broadcasted_iota(jnp.int32, (LANES, LANES), 1)
        gparts = []
        for hh in range(MOE_TM // LANES):
            ga = pltpu.roll(sw_ref[r0 + hh], LANES - c, axis=1)
            gb = pltpu.roll(sw_ref[r0 + hh + 1], LANES - c, axis=1)
            g = jnp.where(lane + hh * LANES < nv, jnp.where(lane < LANES - c, ga, gb), 0.0)
            gparts.append(jnp.sum(jnp.where(rows == cols, jnp.broadcast_to(g, (LANES, LANES)), 0.0),
                                  axis=1, keepdims=True))
        gcol = jnp.concatenate(gparts, axis=0)
        act_ref[...] = (a * jax.nn.sigmoid(a) * b * gcol).astype(BF16)
        gather(p0_ref[cur + 1])

    def body(i, carry):
        step(i, True)
        return carry

    lax.fori_loop(first, last, body, 0)

    @pl.when(e == N_EXPERTS - 1)
    def _():
        scatter(p0_ref[base + last - 2], nv_ref[base + last - 2], True)
        down_proj(slot)
        scatter(p0_ref[base + last - 1], nv_ref[base + last - 1], True)

    @pl.when(e >= N_EXPERTS)
    def _():
        row0 = pl.multiple_of((e - N_EXPERTS) * (MOE_FT * ROW_SUB), ROW_SUB)
        y = jnp.concatenate([y_ref[pl.ds(row0 + j, MOE_FT, stride=ROW_SUB), :] for j in range(ROW_SUB)],
                            axis=1)
        x2 = xs_ref[...] + mod_ref[0][5:6, :] * y
        o_ref[...] = _rms(x2, fg_ref[...])


def _moe(h2_rows, plan, w1, w3, w2, xs, mod, mod_row, fg):
    tstart, p0, nv, stok, sw_rows = plan
    nsub = h2_rows.shape[0] // (MOE_SUB * ROW_SUB)
    per_sub = MOE_SUB // MOE_FT
    wmap = lambda s, e, ts, p0, nv: (jnp.minimum(e, N_EXPERTS - 1), 0, 0)
    sub2 = lambda s, e, ts, p0, nv: (s, 0)
    out_tile = lambda s, e: s * per_sub + jnp.clip(e - N_EXPERTS, 0, per_sub - 1)
    grid_spec = pltpu.PrefetchScalarGridSpec(
        num_scalar_prefetch=3,
        grid=(nsub, N_EXPERTS + per_sub),
        in_specs=[pl.BlockSpec(memory_space=pltpu.SMEM),
                  pl.BlockSpec(sw_rows.shape, lambda s, e, ts, p0, nv: (0, 0, 0)),
                  pl.BlockSpec((MOE_SUB * ROW_SUB, LANES), sub2, pipeline_mode=pl.Buffered(1)),
                  pl.BlockSpec((1, D_MODEL, D_EXPERT), wmap),
                  pl.BlockSpec((1, D_MODEL, D_EXPERT), wmap),
                  pl.BlockSpec((1, D_EXPERT, D_MODEL), wmap),
                  pl.BlockSpec((MOE_FT, D_MODEL), lambda s, e, ts, p0, nv: (out_tile(s, e), 0)),
                  pl.BlockSpec((1, N_MOD, D_MODEL),
                               lambda s, e, ts, p0, nv: (mod_row(out_tile(s, e), MOE_FT), 0, 0)),
                  pl.BlockSpec((1, D_MODEL), lambda s, e, ts, p0, nv: (0, 0))],
        out_specs=pl.BlockSpec((MOE_FT, D_MODEL), lambda s, e, ts, p0, nv: (out_tile(s, e), 0)),
        scratch_shapes=[pltpu.VMEM(((MOE_SUB + MOE_DUMMY) * ROW_SUB, LANES), F32),
                        pltpu.VMEM((ROW_TILE, LANES), F32), pltpu.VMEM((MOE_TM, D_MODEL), BF16),
                        pltpu.VMEM((MOE_TM, D_EXPERT), BF16), pltpu.VMEM((ROW_TILE, LANES), F32),
                        pltpu.VMEM((D_MODEL, D_EXPERT), BF16), pltpu.VMEM((D_MODEL, D_EXPERT), BF16),
                        pltpu.VMEM((2, D_EXPERT, D_MODEL), BF16), pltpu.SMEM((1,), jnp.int32)])
    return pl.pallas_call(
        _moe_kernel,
        grid_spec=grid_spec,
        out_shape=jax.ShapeDtypeStruct(xs.shape, F32),
        compiler_params=_params(("arbitrary", "arbitrary")),
        name="moe",
    )(tstart, p0, nv, stok, sw_rows, h2_rows, w1, w3, w2, xs, mod, fg)


def _grid_pos_embed(n_tokens):
    rows = n_tokens // GRID_W
    quarter = D_MODEL // 4
    omega = 1.0 / (10000.0 ** (jnp.arange(quarter, dtype=F32) / quarter))

    def emb(count):
        a = jnp.arange(count, dtype=F32)[:, None] * omega
        return jnp.concatenate([jnp.sin(a), jnp.cos(a)], axis=-1)

    by_row = jnp.repeat(emb(rows), GRID_W, axis=0)
    by_col = jnp.tile(emb(GRID_W), (rows, 1))
    return jnp.concatenate([by_row, by_col], axis=-1)


def _mixers(x3, pos, mod, first_row, h0, s5_ops, p):
    nb, seq, _ = x3.shape
    n = nb * seq
    nk = seq // S5_CHUNK
    x = x3.reshape(n, D_MODEL)
    per_seq_mod = first_row > 0

    def mod_row(i, tm):
        return first_row + (i * tm) // seq if per_seq_mod else 0

    us, uf, gt = _inproj(x, pos, mod, mod_row, p["norm1_g"], p["w_in"], 1024)
    ys, fin = _s5(us, s5_ops, h0, nb, nk)
    yf = _fourier(uf, nb, seq, min(seq, 512))
    xs, h2_rows, ridx, rw = _merge(ys, yf, gt, x, pos, mod, mod_row, p["norm2_g"], p["merge_w"], 512)
    return xs, h2_rows, ridx, rw, fin, mod_row


def _plan_of_stream(plan, s, n_tokens):
    tstart, p0, nv, stok, sw_rows = plan
    nsub = n_tokens // MOE_SUB
    npair = n_tokens * TOP_K
    off = s * npair
    tiles = slice(s * nsub * MOE_TS, (s + 1) * nsub * MOE_TS)
    nv_s = nv[tiles]
    p0_s = jnp.where(nv_s > 0, p0[tiles] - off, 0)
    return (tstart[s * nsub * (N_EXPERTS + 1):(s + 1) * nsub * (N_EXPERTS + 1)], p0_s, nv_s,
            stok[off:off + npair + MOE_TM], sw_rows[off // LANES:(off + npair + MOE_TM) // LANES])


def kernel(x_prompt, x_sample, state_s5_re, state_s5_im, c, c_ctx, w_ada, b_ada, norm1_g, norm2_g, w_in,
           lam_re, lam_im, log_dt, b_re, b_im, c_re, c_im, d_skip, w_glu, w_proj_s5, w_proj_f, w_out,
           w_router, router_bias, w1, w3, w2, ws1, ws3, ws2, final_norm_g):
    nb_ctx = x_prompt.shape[0]
    nb_lat, seq_lat, _ = x_sample.shape
    half = 2 * S5_STATE

    cond = jnp.concatenate([c_ctx[None], c, jnp.zeros((MOD_ROWS - 1 - nb_lat, D_MODEL), F32)], axis=0)
    mod = _adaln(cond, w_ada[0], b_ada[0]).reshape(MOD_ROWS, N_MOD, D_MODEL)

    s5_ops = _s5ops(lam_re[0], lam_im[0], log_dt[0], b_re[0], b_im[0], c_re[0], c_im[0], d_skip[0])
    p = dict(
        norm1_g=norm1_g[0][None], norm2_g=norm2_g[0][None], final_g=final_norm_g[None],
        w_in=w_in[0].astype(BF16), w1=w1[0], w3=w3[0], w2=w2[0],
        merge_w=(w_glu[0].astype(BF16), w_proj_s5[0].astype(BF16), w_proj_f[0].astype(BF16),
                 w_out[0].astype(BF16), w_router[0].T, router_bias[0][:, None],
                 ws1[0].astype(BF16), ws3[0].astype(BF16), ws2[0].astype(BF16)))

    def pack_state(sr, si):
        f = lambda a: a.astype(F32).transpose(2, 0, 1, 3).reshape(S5_GROUPS, a.shape[0], half)
        return jnp.concatenate([f(sr), f(si)], axis=-1)

    def unpack_state(fin, lo):
        nb = fin.shape[1]
        return fin[..., lo:lo + half].reshape(S5_GROUPS, nb, 2, S5_STATE).transpose(1, 2, 0, 3)[:, None]

    h0_ctx = jnp.zeros((S5_GROUPS, nb_ctx, 2 * half), F32)
    h0_lat = pack_state(state_s5_re[:, 0], state_s5_im[:, 0])
    streams = [(x_prompt, _mixers(x_prompt, None, mod, 0, h0_ctx, s5_ops, p)),
               (x_sample, _mixers(x_sample, _grid_pos_embed(seq_lat), mod, 1, h0_lat, s5_ops, p))]
    n_tokens = streams[0][1][0].shape[0]
    plan = _moe_plan(jnp.concatenate([m[2] for _, m in streams], axis=1),
                     jnp.concatenate([m[3] for _, m in streams], axis=1))
    outs = []
    for s, (x3, (xs, h2_rows, _, _, _, mod_row)) in enumerate(streams):
        out = _moe(h2_rows, _plan_of_stream(plan, s, n_tokens), p["w1"], p["w3"], p["w2"],
                   xs, mod, mod_row, p["final_g"])
        outs.append(out.reshape(x3.shape))
    fin = streams[0][1][4]
    return (outs[0], outs[1], unpack_state(fin, 0).astype(x_prompt.dtype),
            unpack_state(fin, half).astype(x_prompt.dtype))
```

```python
import functools
import math

import jax
import jax.numpy as jnp
from jax import lax
from jax.experimental import pallas as pl
from jax.experimental.pallas import tpu as pltpu

D_MODEL = 1024
GRID_W = 64
D_S5 = 768
S5_GROUP = 16
S5_GROUPS = 48
S5_STATE = 64
D_FNET = 256
FNET_GROUP = 64
N_EXPERTS = 64
TOP_K = 6
N_EXPERT_GROUPS = 8
EXPERTS_PER_GROUP = N_EXPERTS // N_EXPERT_GROUPS
TOPK_GROUPS = 4
D_EXPERT = 256
ROUTED_SCALE = 2.5
N_MOD = 6
EPS = 1e-6

S5_CHUNK = 16
S5_ROW = S5_CHUNK * S5_GROUP
MOD_ROWS = 8
ROUTE_ROWS = 8
LANES = 128
ROW_SUB = D_MODEL // LANES
VMEM_LIMIT = 56 * 1024 * 1024

BF16 = jnp.bfloat16
F32 = jnp.float32


def _params(sem, vmem=VMEM_LIMIT):
    return pltpu.CompilerParams(dimension_semantics=sem, vmem_limit_bytes=vmem)


def _rms(x, g):
    return x * lax.rsqrt(jnp.mean(x * x, axis=-1, keepdims=True) + EPS) * g


def _adaln_kernel(c_ref, w_ref, b_ref, o_ref):
    c = c_ref[...]
    o_ref[...] = jnp.dot(c * jax.nn.sigmoid(c), w_ref[...], precision=lax.Precision.HIGHEST,
                         preferred_element_type=F32) + b_ref[...]


def _adaln(cond, w_ada, b_ada):
    n_out = N_MOD * D_MODEL
    return pl.pallas_call(
        _adaln_kernel,
        out_shape=jax.ShapeDtypeStruct((MOD_ROWS, n_out), F32),
        grid=(N_MOD,),
        in_specs=[pl.BlockSpec((MOD_ROWS, D_MODEL), lambda i: (0, 0)),
                  pl.BlockSpec((D_MODEL, D_MODEL), lambda i: (0, i)),
                  pl.BlockSpec((1, D_MODEL), lambda i: (0, i))],
        out_specs=pl.BlockSpec((MOD_ROWS, D_MODEL), lambda i: (0, i)),
        compiler_params=_params(("arbitrary",)),
        name="adaln",
    )(cond, w_ada, b_ada.reshape(1, n_out))


def _inproj_kernel(has_pos, *refs):
    if has_pos:
        x_ref, pos_ref, mod_ref, g_ref, w_ref, us_ref, uf_ref, gt_ref = refs
        x = x_ref[...] + pos_ref[...]
    else:
        x_ref, mod_ref, g_ref, w_ref, us_ref, uf_ref, gt_ref = refs
        x = x_ref[...]
    m = mod_ref[0]
    h = _rms(x, g_ref[...]) * (1.0 + m[1:2, :]) + m[0:1, :]
    p = jnp.dot(h.astype(BF16), w_ref[...], preferred_element_type=F32)
    us_ref[...] = p[:, :D_S5]
    uf_ref[...] = p[:, D_S5:D_MODEL].astype(BF16)
    gt_ref[...] = jax.nn.sigmoid(p[:, D_MODEL:]).astype(BF16)


def _inproj(x, pos, mod, mod_row, norm_g, w_in_bf, tm):
    n = x.shape[0]
    has_pos = pos is not None
    row = lambda i: (i, 0)
    in_specs = [pl.BlockSpec((tm, D_MODEL), row)]
    args = [x]
    if has_pos:
        nper = pos.shape[0] // tm
        in_specs.append(pl.BlockSpec((tm, D_MODEL), lambda i: (i % nper, 0)))
        args.append(pos)
    in_specs += [pl.BlockSpec((1, N_MOD, D_MODEL), lambda i: (mod_row(i, tm), 0, 0)),
                 pl.BlockSpec((1, D_MODEL), lambda i: (0, 0)),
                 pl.BlockSpec(w_in_bf.shape, lambda i: (0, 0))]
    args += [mod, norm_g, w_in_bf]
    return pl.pallas_call(
        functools.partial(_inproj_kernel, has_pos),
        out_shape=(jax.ShapeDtypeStruct((n, D_S5), F32),
                   jax.ShapeDtypeStruct((n, D_FNET), BF16),
                   jax.ShapeDtypeStruct((n, 2 * D_MODEL), BF16)),
        grid=(n // tm,),
        in_specs=in_specs,
        out_specs=(pl.BlockSpec((tm, D_S5), row), pl.BlockSpec((tm, D_FNET), row),
                   pl.BlockSpec((tm, 2 * D_MODEL), row)),
        compiler_params=_params(("arbitrary",)),
        name="inproj",
    )(*args)


def _shift_lanes(x, k):
    if k == 0:
        return x
    z = jnp.zeros((x.shape[0], abs(k)), x.dtype)
    if k > 0:
        return jnp.concatenate([z, x[:, :x.shape[1] - k]], axis=1)
    return jnp.concatenate([x[:, -k:], z], axis=1)


def _s5ops_kernel(lam_re_ref, lam_im_ref, dt_ref, btr_ref, bti_ref, cr_ref, ci_ref, d_ref,
                  m_ref, wi_ref, wot_ref, coef_ref):
    hi = lax.Precision.HIGHEST
    lr = jnp.minimum(lam_re_ref[0], -1e-4)
    li = lam_im_ref[0]
    dt = jnp.exp(dt_ref[0])
    mag = jnp.exp(lr * dt)
    ar = mag * jnp.cos(li * dt)
    ai = mag * jnp.sin(li * dt)
    den = lr * lr + li * li
    nr = ar - 1.0
    qr = (nr * lr + ai * li) / den
    qi = (ai * lr - nr * li) / den
    npow = S5_CHUNK + 1
    nrow = (lax.broadcasted_iota(jnp.int32, (2 * npow, 1), 0) // 2).astype(F32)
    stack = lambda v: jnp.concatenate([v] * npow, axis=0)
    pmag = jnp.exp(nrow * stack(lr * dt))
    pang = nrow * stack(li * dt)
    pr_all = pmag * jnp.cos(pang)
    pi_all = pmag * jnp.sin(pang)
    pr = [pr_all[2 * n:2 * n + 2] for n in range(npow)]
    pi = [pi_all[2 * n:2 * n + 2] for n in range(npow)]
    bbr, bbi, car, cai = [], [], [], []
    for d in range(2):
        btr = btr_ref[0, d]
        bti = bti_ref[0, d]
        bbr.append(qr[d:d + 1] * btr - qi[d:d + 1] * bti)
        bbi.append(qr[d:d + 1] * bti + qi[d:d + 1] * btr)
        cr = cr_ref[0, d]
        ci = ci_ref[0, d]
        car.append([cr * pr[n][d:d + 1] - ci * pi[n][d:d + 1] for n in range(S5_CHUNK + 1)])
        cai.append([cr * pi[n][d:d + 1] + ci * pr[n][d:d + 1] for n in range(S5_CHUNK + 1)])

    def lag_kernels(d, order):
        a = jnp.concatenate([car[d][n] for n in order], axis=0)
        b = jnp.concatenate([cai[d][n] for n in order], axis=0)
        dn = (((1,), (1,)), ((), ()))
        return (lax.dot_general(bbr[d], a, dn, precision=hi, preferred_element_type=F32)
                - lax.dot_general(bbi[d], b, dn, precision=hi, preferred_element_type=F32))

    ktf = lag_kernels(0, range(S5_CHUNK))
    ktb = lag_kernels(1, range(S5_CHUNK - 1, -1, -1))
    row = lax.broadcasted_iota(jnp.int32, (S5_GROUP, S5_ROW), 0)
    lane = lax.broadcasted_iota(jnp.int32, (S5_GROUP, S5_ROW), 1)
    dcol = d_ref[0]
    for j in range(S5_CHUNK):
        rows = slice(j * S5_GROUP, (j + 1) * S5_GROUP)
        blk = _shift_lanes(ktf, S5_GROUP * j) + _shift_lanes(ktb, -S5_GROUP * (S5_CHUNK - 1 - j))
        blk = blk + jnp.where(lane == S5_GROUP * j + row, dcol, 0.0)
        m_ref[0, rows, :] = blk.astype(BF16)
        nf = S5_CHUNK - 1 - j
        wi = jnp.concatenate([pr[nf][0:1] * bbr[0] - pi[nf][0:1] * bbi[0],
                              pr[j][1:2] * bbr[1] - pi[j][1:2] * bbi[1],
                              pr[nf][0:1] * bbi[0] + pi[nf][0:1] * bbr[0],
                              pr[j][1:2] * bbi[1] + pi[j][1:2] * bbr[1]], axis=1)
        wi_ref[0, rows, :] = wi.astype(BF16)
        wot = jnp.concatenate([car[0][j + 1], car[1][S5_CHUNK - j],
                               -cai[0][j + 1], -cai[1][S5_CHUNK - j]], axis=1)
        wot_ref[0, rows, :] = wot.astype(BF16)
    coef_ref[0, 0:1, :] = jnp.concatenate([pr[S5_CHUNK][0:1], pr[S5_CHUNK][1:2]], axis=1)
    coef_ref[0, 1:2, :] = jnp.concatenate([pi[S5_CHUNK][0:1], pi[S5_CHUNK][1:2]], axis=1)


def _s5ops(lam_re, lam_im, log_dt, b_re, b_im, c_re, c_im, d_skip):
    g3 = lambda g: (g, 0, 0)
    g4 = lambda g: (g, 0, 0, 0)
    sw = lambda a: jnp.swapaxes(a.astype(F32), 0, 1)
    dt = jnp.broadcast_to(sw(log_dt)[..., None], (S5_GROUPS, 2, S5_STATE))
    args = (sw(lam_re), sw(lam_im), dt, sw(jnp.swapaxes(b_re, 2, 3)), sw(jnp.swapaxes(b_im, 2, 3)),
            sw(c_re), sw(c_im), d_skip.astype(F32).reshape(S5_GROUPS, S5_GROUP, 1))
    vec = pl.BlockSpec((1, 2, S5_STATE), g3)
    mat = pl.BlockSpec((1, 2, S5_GROUP, S5_STATE), g4)
    op = pl.BlockSpec((1, S5_ROW, S5_ROW), g3)
    return pl.pallas_call(
        _s5ops_kernel,
        out_shape=(jax.ShapeDtypeStruct((S5_GROUPS, S5_ROW, S5_ROW), BF16),) * 3
        + (jax.ShapeDtypeStruct((S5_GROUPS, 2, 2 * S5_STATE), F32),),
        grid=(S5_GROUPS,),
        in_specs=[vec, vec, vec, mat, mat, mat, mat, pl.BlockSpec((1, S5_GROUP, 1), g3)],
        out_specs=(op, op, op, pl.BlockSpec((1, 2, 2 * S5_STATE), g3)),
        compiler_params=_params(("arbitrary",)),
        name="s5ops",
    )(*args)


S5_BLOCK_GROUPS = LANES // S5_GROUP


S5_HALF_T = LANES // S5_GROUP


def _s5_perm():
    a = jnp.arange(S5_HALF_T * LANES, dtype=jnp.int32)
    dst = ((a % LANES) // S5_GROUP) * LANES + (a // LANES) * S5_GROUP + a % S5_GROUP
    return (dst[:, None] == a[None, :]).astype(BF16)


def _s5_kernel(nb, nk, u_ref, perm_ref, m_ref, wi_ref, wot_ref, coef_ref, h0_ref, y_ref, fin_ref,
               sr_ref, si_ref, efr_ref, ebr_ref, efi_ref, ebi_ref, uall_ref, yall_ref):
    gl = pl.program_id(1)
    rows = nb * nk
    half = 2 * S5_STATE

    @pl.when(gl == 0)
    def _():
        for hh in range(S5_CHUNK // S5_HALF_T):
            xcat = jnp.concatenate([u_ref[pl.ds(hh * S5_HALF_T + tt, rows, stride=S5_CHUNK), :].astype(BF16)
                                    for tt in range(S5_HALF_T)], axis=1)
            uh = jnp.dot(xcat, perm_ref[...], preferred_element_type=F32).astype(BF16)
            for g in range(S5_BLOCK_GROUPS):
                uall_ref[g, :, hh * LANES:(hh + 1) * LANES] = uh[:, g * LANES:(g + 1) * LANES]

    u = uall_ref[gl]
    s = jnp.dot(u, wi_ref[0], preferred_element_type=F32)
    sr_ref[...] = s[:, :half]
    si_ref[...] = s[:, half:]
    c_r = coef_ref[0, 0:1, :]
    c_i = coef_ref[0, 1:2, :]
    e_r = h0_ref[0, :, :half]
    e_i = h0_ref[0, :, half:]
    is_fwd = lax.broadcasted_iota(jnp.int32, (nb, half), 1) < S5_STATE
    for j in range(nk):
        rf = pl.ds(j, nb, stride=nk)
        rb = pl.ds(nk - 1 - j, nb, stride=nk)
        efr_ref[rf, :] = e_r
        ebr_ref[rb, :] = e_r
        efi_ref[rf, :] = e_i
        ebi_ref[rb, :] = e_i
        s_r = jnp.where(is_fwd, sr_ref[rf, :], sr_ref[rb, :])
        s_i = jnp.where(is_fwd, si_ref[rf, :], si_ref[rb, :])
        e_r, e_i = c_r * e_r - c_i * e_i + s_r, c_r * e_i + c_i * e_r + s_i
    fin_ref[0, :, :half] = e_r
    fin_ref[0, :, half:] = e_i
    y = jnp.dot(u, m_ref[0], preferred_element_type=F32)
    fwd_rows = lax.broadcasted_iota(jnp.int32, (rows, half), 1) < S5_STATE
    e = jnp.concatenate([jnp.where(fwd_rows, efr_ref[...], ebr_ref[...]),
                         jnp.where(fwd_rows, efi_ref[...], ebi_ref[...])], axis=1).astype(BF16)
    y = y + lax.dot_general(e, wot_ref[0], (((1,), (1,)), ((), ())), preferred_element_type=F32)
    yall_ref[gl] = y.astype(BF16)

    @pl.when(gl == S5_BLOCK_GROUPS - 1)
    def _():
        for hh in range(S5_CHUNK // S5_HALF_T):
            ycat = jnp.concatenate([yall_ref[g, :, hh * LANES:(hh + 1) * LANES]
                                    for g in range(S5_BLOCK_GROUPS)], axis=1)
            out = lax.dot_general(ycat, perm_ref[...], (((1,), (1,)), ((), ())), preferred_element_type=F32)
            for tt in range(S5_HALF_T):
                y_ref[pl.ds(hh * S5_HALF_T + tt, rows, stride=S5_CHUNK), :] = out[:, tt * LANES:(tt + 1) * LANES]


def _s5(u, ops, h0, nb, nk):
    m, w_in, w_out, coef = ops
    n = u.shape[0]
    rows = nb * nk
    perm = _s5_perm()
    g3 = lambda b, g: (b * S5_BLOCK_GROUPS + g, 0, 0)
    blk = lambda b, g: (0, b)
    return pl.pallas_call(
        functools.partial(_s5_kernel, nb, nk),
        out_shape=(jax.ShapeDtypeStruct((n, D_S5), F32),
                   jax.ShapeDtypeStruct((S5_GROUPS, nb, 4 * S5_STATE), F32)),
        grid=(S5_GROUPS // S5_BLOCK_GROUPS, S5_BLOCK_GROUPS),
        in_specs=[pl.BlockSpec((n, LANES), blk),
                  pl.BlockSpec(perm.shape, lambda b, g: (0, 0)),
                  pl.BlockSpec((1, S5_ROW, S5_ROW), g3),
                  pl.BlockSpec((1, S5_ROW, 4 * S5_STATE), g3),
                  pl.BlockSpec((1, 4 * S5_STATE, S5_ROW), g3),
                  pl.BlockSpec((1, 2, 2 * S5_STATE), g3),
                  pl.BlockSpec((1, nb, 4 * S5_STATE), g3)],
        out_specs=(pl.BlockSpec((n, LANES), blk),
                   pl.BlockSpec((1, nb, 4 * S5_STATE), g3)),
        scratch_shapes=[pltpu.VMEM((rows, 2 * S5_STATE), F32)] * 6 + [
            pltpu.VMEM((S5_BLOCK_GROUPS, rows, S5_ROW), BF16),
            pltpu.VMEM((S5_BLOCK_GROUPS, rows, S5_ROW), BF16)],
        compiler_params=_params(("arbitrary", "arbitrary")),
        name="s5",
    )(u, perm, m, w_in, w_out, coef, h0)


def _dft_tables(seq):
    k = jnp.arange(seq, dtype=jnp.int32)
    na = seq // FNET_GROUP
    ang_a = (2.0 * math.pi / na) * ((jnp.arange(na, dtype=jnp.int32)[:, None] * k[None, :]) % na).astype(F32)
    ang_b = (2.0 * math.pi / seq) * ((jnp.arange(FNET_GROUP, dtype=jnp.int32)[:, None] * k[None, :]) % seq).astype(F32)
    ca, sa = jnp.cos(ang_a)[:, None, :], jnp.sin(ang_a)[:, None, :]
    cb, sb = jnp.cos(ang_b)[None, :, :], jnp.sin(ang_b)[None, :, :]
    cos_jk = (ca * cb - sa * sb).reshape(seq, seq)
    sin_jk = (sa * cb + ca * sb).reshape(seq, seq)
    cs = jnp.concatenate([cos_jk, -sin_jk], axis=1).astype(BF16)
    c = jnp.arange(D_FNET, dtype=jnp.int32)
    same = (c[:, None] // FNET_GROUP) == (c[None, :] // FNET_GROUP)
    angc = (2.0 * math.pi / FNET_GROUP) * (((c[:, None] % FNET_GROUP) * (c[None, :] % FNET_GROUP))
                                           % FNET_GROUP).astype(F32)
    scale = 1.0 / math.sqrt(seq * FNET_GROUP)
    bdc = jnp.where(same, jnp.cos(angc) * scale, 0.0).astype(BF16)
    bds = jnp.where(same, jnp.sin(angc) * scale, 0.0).astype(BF16)
    return cs, bdc, bds


def _fourier_kernel(seq, z_ref, cs_ref, bdc_ref, bds_ref, o_ref, zz_ref):
    @pl.when(pl.program_id(1) == 0)
    def _():
        z = z_ref[...]
        zz_ref[0:seq, :] = jnp.dot(z, bdc_ref[...], preferred_element_type=F32).astype(BF16)
        zz_ref[seq:, :] = jnp.dot(z, bds_ref[...], preferred_element_type=F32).astype(BF16)

    o_ref[...] = jnp.dot(cs_ref[...], zz_ref[...], preferred_element_type=F32).astype(BF16)


def _fourier(z, nb, seq, tl):
    cs, bdc, bds = _dft_tables(seq)
    nt = seq // tl
    return pl.pallas_call(
        functools.partial(_fourier_kernel, seq),
        out_shape=jax.ShapeDtypeStruct(z.shape, BF16),
        grid=(nb, nt),
        in_specs=[pl.BlockSpec((seq, D_FNET), lambda b, i: (b, 0)),
                  pl.BlockSpec((tl, 2 * seq), lambda b, i: (i, 0)),
                  pl.BlockSpec((D_FNET, D_FNET), lambda b, i: (0, 0)),
                  pl.BlockSpec((D_FNET, D_FNET), lambda b, i: (0, 0))],
        out_specs=pl.BlockSpec((tl, D_FNET), lambda b, i: (b * nt + i, 0)),
        scratch_shapes=[pltpu.VMEM((2 * seq, D_FNET), BF16)],
        compiler_params=_params(("arbitrary", "arbitrary")),
        name="fourier",
    )(z, cs, bdc, bds)


def _first_argmax_mask(v, iota, size):
    m = jnp.max(v, axis=0, keepdims=True)
    first = jnp.min(jnp.where(v == m, iota, size), axis=0, keepdims=True)
    return iota == first


def _route(logits_t, bias_col):
    tm = logits_t.shape[1]
    neg = -jnp.inf
    s = jax.nn.sigmoid(logits_t)
    biased = s + bias_col
    io8 = lax.broadcasted_iota(jnp.int32, (EXPERTS_PER_GROUP, tm), 0)
    gs_rows = []
    for g in range(N_EXPERT_GROUPS):
        blk = biased[g * EXPERTS_PER_GROUP:(g + 1) * EXPERTS_PER_GROUP, :]
        m1 = jnp.max(blk, axis=0, keepdims=True)
        rest = jnp.where(_first_argmax_mask(blk, io8, EXPERTS_PER_GROUP), neg, blk)
        gs_rows.append(m1 + jnp.max(rest, axis=0, keepdims=True))
    gs = jnp.concatenate(gs_rows, axis=0)
    iog = lax.broadcasted_iota(jnp.int32, (N_EXPERT_GROUPS, tm), 0)
    gsel = jnp.zeros((N_EXPERT_GROUPS, tm), F32)
    for _ in range(TOPK_GROUPS):
        sel = _first_argmax_mask(gs, iog, N_EXPERT_GROUPS)
        gsel = jnp.where(sel, 1.0, gsel)
        gs = jnp.where(sel, neg, gs)
    emask = jnp.concatenate(
        [jnp.broadcast_to(gsel[g:g + 1, :], (EXPERTS_PER_GROUP, tm)) for g in range(N_EXPERT_GROUPS)], axis=0)
    v = jnp.where(emask > 0.0, biased, neg)
    ioe = lax.broadcasted_iota(jnp.int32, (N_EXPERTS, tm), 0)
    idx_rows, s_rows = [], []
    for _ in range(TOP_K):
        sel = _first_argmax_mask(v, ioe, N_EXPERTS)
        idx_rows.append(jnp.sum(jnp.where(sel, ioe, 0), axis=0, keepdims=True))
        s_rows.append(jnp.sum(jnp.where(sel, s, 0.0), axis=0, keepdims=True))
        v = jnp.where(sel, neg, v)
    denom = s_rows[0]
    for r in s_rows[1:]:
        denom = denom + r
    pad = ROUTE_ROWS - TOP_K
    idx = jnp.concatenate(idx_rows + [jnp.zeros((pad, tm), jnp.int32)], axis=0)
    w = jnp.concatenate([r / denom * ROUTED_SCALE for r in s_rows] + [jnp.zeros((pad, tm), F32)], axis=0)
    return idx, w


def _merge_kernel(has_pos, *refs):
    if has_pos:
        (ys_ref, yf_ref, gt_ref, x_ref, pos_ref, mod_ref, n2_ref, wglu_ref, wps_ref, wpf_ref, wout_ref,
         wrt_ref, rb_ref, ws1_ref, ws3_ref, ws2_ref, xs_ref, h2_ref, ridx_ref, rw_ref) = refs
        x = x_ref[...] + pos_ref[...]
    else:
        (ys_ref, yf_ref, gt_ref, x_ref, mod_ref, n2_ref, wglu_ref, wps_ref, wpf_ref, wout_ref,
         wrt_ref, rb_ref, ws1_ref, ws3_ref, ws2_ref, xs_ref, h2_ref, ridx_ref, rw_ref) = refs
        x = x_ref[...]
    m = mod_ref[0]
    g = jax.nn.gelu(ys_ref[...].astype(F32))
    a = g * jax.nn.sigmoid(jnp.dot(g.astype(BF16), wglu_ref[...], preferred_element_type=F32))
    pa = jnp.dot(a.astype(BF16), wps_ref[...], preferred_element_type=F32)
    pb = jnp.dot(yf_ref[...], wpf_ref[...], preferred_element_type=F32)
    gt = gt_ref[...].astype(F32)
    merged = gt[:, :D_MODEL] * pa + gt[:, D_MODEL:] * pb
    x1 = x + m[2:3, :] * jnp.dot(merged.astype(BF16), wout_ref[...], preferred_element_type=F32)
    h2 = _rms(x1, n2_ref[...]) * (1.0 + m[4:5, :]) + m[3:4, :]
    hb = h2.astype(BF16)
    for j in range(ROW_SUB):
        h2_ref[pl.ds(j, h2.shape[0], stride=ROW_SUB), :] = h2[:, j * LANES:(j + 1) * LANES]
    wr = wrt_ref[...]
    wr_hi = wr.astype(BF16)
    wr_lo = (wr - wr_hi.astype(F32)).astype(BF16)
    h_lo = (h2 - hb.astype(F32)).astype(BF16)
    dn = (((1,), (1,)), ((), ()))
    logits_t = (lax.dot_general(wr_hi, hb, dn, preferred_element_type=F32)
                + lax.dot_general(wr_hi, h_lo, dn, preferred_element_type=F32)
                + lax.dot_general(wr_lo, hb, dn, preferred_element_type=F32))
    ridx_ref[...], rw_ref[...] = _route(logits_t, rb_ref[...])
    s1 = jnp.dot(hb, ws1_ref[...], preferred_element_type=F32)
    s3 = jnp.dot(hb, ws3_ref[...], preferred_element_type=F32)
    shared = jnp.dot((s1 * jax.nn.sigmoid(s1) * s3).astype(BF16), ws2_ref[...], preferred_element_type=F32)
    xs_ref[...] = x1 + m[5:6, :] * shared


def _merge(ys, yf, gt, x, pos, mod, mod_row, n2, weights, tm):
    n = x.shape[0]
    has_pos = pos is not None
    row = lambda i: (i, 0)
    const = lambda a: pl.BlockSpec(a.shape, lambda i: (0,) * a.ndim)
    in_specs = [pl.BlockSpec((tm, D_S5), row), pl.BlockSpec((tm, D_FNET), row),
                pl.BlockSpec((tm, 2 * D_MODEL), row), pl.BlockSpec((tm, D_MODEL), row)]
    args = [ys, yf, gt, x]
    if has_pos:
        nper = pos.shape[0] // tm
        in_specs.append(pl.BlockSpec((tm, D_MODEL), lambda i: (i % nper, 0)))
        args.append(pos)
    in_specs += [pl.BlockSpec((1, N_MOD, D_MODEL), lambda i: (mod_row(i, tm), 0, 0)), const(n2)]
    args += [mod, n2]
    in_specs += [const(w) for w in weights]
    args += list(weights)
    return pl.pallas_call(
        functools.partial(_merge_kernel, has_pos),
        out_shape=(jax.ShapeDtypeStruct((n, D_MODEL), F32),
                   jax.ShapeDtypeStruct((n * ROW_SUB, LANES), F32),
                   jax.ShapeDtypeStruct((ROUTE_ROWS, n), jnp.int32),
                   jax.ShapeDtypeStruct((ROUTE_ROWS, n), F32)),
        grid=(n // tm,),
        in_specs=in_specs,
        out_specs=(pl.BlockSpec((tm, D_MODEL), row), pl.BlockSpec((tm * ROW_SUB, LANES), row),
                   pl.BlockSpec((ROUTE_ROWS, tm), lambda i: (0, i)),
                   pl.BlockSpec((ROUTE_ROWS, tm), lambda i: (0, i))),
        compiler_params=_params(("arbitrary",)),
        name="merge",
    )(*args)


MOE_SUB = 4096
MOE_TM = 128
MOE_TMAX = MOE_SUB * TOP_K // MOE_TM + N_EXPERTS
MOE_SORT_SUBS = 2
MOE_PAD = 2
MOE_TS = MOE_TMAX + 2 * MOE_PAD
MOE_DUMMY = 256
MOE_RMW = 8
MOE_FT = 256
ROW_TILE = MOE_TM * ROW_SUB


def _moe_plan(ridx, rw):
    n = ridx.shape[1]
    nsub = n // MOE_SUB
    npair = n * TOP_K
    t = jnp.arange(n, dtype=jnp.int32)
    key = (((t // MOE_SUB) * N_EXPERTS)[None] + ridx[:TOP_K]) * MOE_SUB + (t % MOE_SUB)[None]
    group = MOE_SORT_SUBS * MOE_SUB
    parts = [lax.sort((key[:, g:g + group].reshape(-1), rw[:TOP_K, g:g + group].reshape(-1)), num_keys=1)
             for g in range(0, n, group)]
    skey = jnp.concatenate([pk for pk, _ in parts])
    sw = jnp.concatenate([pw for _, pw in parts])
    stok = jnp.concatenate([(skey % MOE_SUB) * ROW_SUB, jnp.zeros((MOE_TM,), jnp.int32)])
    sw_rows = jnp.concatenate([sw, jnp.zeros((MOE_TM,), F32)]).reshape((npair + MOE_TM) // LANES, 1, LANES)
    hits = ridx[:TOP_K].reshape(TOP_K, nsub, 1, MOE_SUB) == jnp.arange(N_EXPERTS, dtype=jnp.int32)[None, None, :, None]
    cnt = jnp.sum(hits.astype(jnp.int32), axis=(0, 3))
    poff = (jnp.cumsum(cnt.reshape(-1)) - cnt.reshape(-1)).reshape(nsub, N_EXPERTS)
    ntile = (cnt + MOE_TM - 1) // MOE_TM
    tcum = jnp.cumsum(ntile, axis=1)
    toff = tcum - ntile
    tstart = jnp.concatenate([toff, tcum[:, -1:]], axis=1).reshape(-1).astype(jnp.int32)
    j = jnp.arange(MOE_TS, dtype=jnp.int32) - MOE_PAD
    valid = (j[None] >= 0) & (j[None] < tcum[:, -1:])
    te = jnp.minimum(jnp.sum(j[None, :, None] >= tcum[:, None, :], axis=-1), N_EXPERTS - 1)
    pick = lambda a: jnp.take_along_axis(a, te, axis=1)
    first = (j[None] - pick(toff)) * MOE_TM
    p0 = jnp.where(valid, pick(poff) + first, 0).reshape(-1).astype(jnp.int32)
    nv = jnp.where(valid, jnp.minimum(pick(cnt) - first, MOE_TM), 0).reshape(-1).astype(jnp.int32)
    return tstart, p0, nv, stok, sw_rows


def _moe_kernel(ts_ref, p0_ref, nv_ref, tok_ref, sw_ref, src_ref, w1_ref, w3_ref, w2_ref, xs_ref, mod_ref,
                fg_ref, o_ref, y_ref, xt_ref, xb_ref, act_ref, ot_ref, w1b_ref, w3b_ref, w2b_ref, slot_ref):
    sub = pl.program_id(0)
    e = pl.program_id(1)
    base = sub * MOE_TS + MOE_PAD
    ec = jnp.minimum(e, N_EXPERTS - 1)
    first = ts_ref[sub * (N_EXPERTS + 1) + ec]
    last = jnp.where(e < N_EXPERTS, ts_ref[sub * (N_EXPERTS + 1) + ec + 1], first)

    def gather(p0):
        for mi in range(MOE_TM):
            tok = pl.multiple_of(tok_ref[p0 + mi], ROW_SUB)
            xt_ref[mi * ROW_SUB:(mi + 1) * ROW_SUB, :] = src_ref[pl.ds(tok, ROW_SUB), :]
        for j in range(ROW_SUB):
            xb_ref[:, j * LANES:(j + 1) * LANES] = xt_ref[pl.ds(j, MOE_TM, stride=ROW_SUB), :].astype(BF16)

    def scatter(p0, nv, masked):
        for u in range(0, MOE_TM, MOE_RMW):
            new = []
            for i in range(MOE_RMW):
                tok = tok_ref[p0 + u + i]
                if masked:
                    tok = jnp.where(u + i < nv, tok, MOE_SUB * ROW_SUB)
                tok = pl.multiple_of(tok, ROW_SUB)
                new.append((tok, y_ref[pl.ds(tok, ROW_SUB), :]
                            + ot_ref[(u + i) * ROW_SUB:(u + i + 1) * ROW_SUB, :]))
            for tok, v in new:
                y_ref[pl.ds(tok, ROW_SUB), :] = v

    @pl.when(e == 0)
    def _():
        y_ref[...] = jnp.zeros_like(y_ref)
        ot_ref[...] = jnp.zeros_like(ot_ref)
        act_ref[...] = jnp.zeros_like(act_ref)
        w2b_ref[...] = jnp.zeros_like(w2b_ref)
        slot_ref[0] = 0
        gather(p0_ref[base])

    @pl.when(last > first)
    def _():
        slot_ref[0] = 1 - slot_ref[0]
        w1b_ref[...] = w1_ref[0].astype(BF16)
        w3b_ref[...] = w3_ref[0].astype(BF16)
        w2b_ref[slot_ref[0]] = w2_ref[0].astype(BF16)

    slot = slot_ref[0]

    def down_proj(w2_slot):
        o = jnp.dot(act_ref[...], w2b_ref[w2_slot], preferred_element_type=F32)
        for j in range(ROW_SUB):
            ot_ref[pl.ds(j, MOE_TM, stride=ROW_SUB), :] = o[:, j * LANES:(j + 1) * LANES]

    def step(i, masked):
        cur = base + i
        scatter(p0_ref[cur - 2], nv_ref[cur - 2], masked)
        down_proj(jnp.where(i > first, slot, 1 - slot))
        p0 = p0_ref[cur]
        nv = nv_ref[cur]
        x = xb_ref[...]
        a = jnp.dot(x, w1b_ref[...], preferred_element_type=F32)
        b = jnp.dot(x, w3b_ref[...], preferred_element_type=F32)
        r0 = p0 // LANES
        c = p0 % LANES
        lane = lax.broadcasted_iota(jnp.int32, (1, LANES), 1)
        rows = lax.broadcasted_iota(jnp.int32, (LANES, LANES), 0)
        cols = lax.broadcasted_iota(jnp.int32, (LANES, LANES), 1)
        gparts = []
        for hh in range(MOE_TM // LANES):
            ga = pltpu.roll(sw_ref[r0 + hh], LANES - c, axis=1)
            gb = pltpu.roll(sw_ref[r0 + hh + 1], LANES - c, axis=1)
            g = jnp.where(lane + hh * LANES < nv, jnp.where(lane < LANES - c, ga, gb), 0.0)
            gparts.append(jnp.sum(jnp.where(rows == cols, jnp.broadcast_to(g, (LANES, LANES)), 0.0),
                                  axis=1, keepdims=True))
        gcol = jnp.concatenate(gparts, axis=0)
        act_ref[...] = (a * jax.nn.sigmoid(a) * b * gcol).astype(BF16)
        gather(p0_ref[cur + 1])

    def body(i, carry):
        step(i, True)
        return carry

    lax.fori_loop(first, last, body, 0)

    @pl.when(e == N_EXPERTS - 1)
    def _():
        scatter(p0_ref[base + last - 2], nv_ref[base + last - 2], True)
        down_proj(slot)
        scatter(p0_ref[base + last - 1], nv_ref[base + last - 1], True)

    @pl.when(e >= N_EXPERTS)
    def _():
        row0 = pl.multiple_of((e - N_EXPERTS) * (MOE_FT * ROW_SUB), ROW_SUB)
        y = jnp.concatenate([y_ref[pl.ds(row0 + j, MOE_FT, stride=ROW_SUB), :] for j in range(ROW_SUB)],
                            axis=1)
        x2 = xs_ref[...] + mod_ref[0][5:6, :] * y
        o_ref[...] = _rms(x2, fg_ref[...])


def _moe(h2_rows, plan, w1, w3, w2, xs, mod, mod_row, fg):
    tstart, p0, nv, stok, sw_rows = plan
    nsub = h2_rows.shape[0] // (MOE_SUB * ROW_SUB)
    per_sub = MOE_SUB // MOE_FT
    wmap = lambda s, e, ts, p0, nv: (jnp.minimum(e, N_EXPERTS - 1), 0, 0)
    sub2 = lambda s, e, ts, p0, nv: (s, 0)
    out_tile = lambda s, e: s * per_sub + jnp.clip(e - N_EXPERTS, 0, per_sub - 1)
    grid_spec = pltpu.PrefetchScalarGridSpec(
        num_scalar_prefetch=3,
        grid=(nsub, N_EXPERTS + per_sub),
        in_specs=[pl.BlockSpec(memory_space=pltpu.SMEM),
                  pl.BlockSpec(sw_rows.shape, lambda s, e, ts, p0, nv: (0, 0, 0)),
                  pl.BlockSpec((MOE_SUB * ROW_SUB, LANES), sub2, pipeline_mode=pl.Buffered(1)),
                  pl.BlockSpec((1, D_MODEL, D_EXPERT), wmap),
                  pl.BlockSpec((1, D_MODEL, D_EXPERT), wmap),
                  pl.BlockSpec((1, D_EXPERT, D_MODEL), wmap),
                  pl.BlockSpec((MOE_FT, D_MODEL), lambda s, e, ts, p0, nv: (out_tile(s, e), 0)),
                  pl.BlockSpec((1, N_MOD, D_MODEL),
                               lambda s, e, ts, p0, nv: (mod_row(out_tile(s, e), MOE_FT), 0, 0)),
                  pl.BlockSpec((1, D_MODEL), lambda s, e, ts, p0, nv: (0, 0))],
        out_specs=pl.BlockSpec((MOE_FT, D_MODEL), lambda s, e, ts, p0, nv: (out_tile(s, e), 0)),
        scratch_shapes=[pltpu.VMEM(((MOE_SUB + MOE_DUMMY) * ROW_SUB, LANES), F32),
                        pltpu.VMEM((ROW_TILE, LANES), F32), pltpu.VMEM((MOE_TM, D_MODEL), BF16),
                        pltpu.VMEM((MOE_TM, D_EXPERT), BF16), pltpu.VMEM((ROW_TILE, LANES), F32),
                        pltpu.VMEM((D_MODEL, D_EXPERT), BF16), pltpu.VMEM((D_MODEL, D_EXPERT), BF16),
                        pltpu.VMEM((2, D_EXPERT, D_MODEL), BF16), pltpu.SMEM((1,), jnp.int32)])
    return pl.pallas_call(
        _moe_kernel,
        grid_spec=grid_spec,
        out_shape=jax.ShapeDtypeStruct(xs.shape, F32),
        compiler_params=_params(("arbitrary", "arbitrary")),
        name="moe",
    )(tstart, p0, nv, stok, sw_rows, h2_rows, w1, w3, w2, xs, mod, fg)


def _grid_pos_embed(n_tokens):
    rows = n_tokens // GRID_W
    quarter = D_MODEL // 4
    omega = 1.0 / (10000.0 ** (jnp.arange(quarter, dtype=F32) / quarter))

    def emb(count):
        a = jnp.arange(count, dtype=F32)[:, None] * omega
        return jnp.concatenate([jnp.sin(a), jnp.cos(a)], axis=-1)

    by_row = jnp.repeat(emb(rows), GRID_W, axis=0)
    by_col = jnp.tile(emb(GRID_W), (rows, 1))
    return jnp.concatenate([by_row, by_col], axis=-1)


def _mixers(x3, pos, mod, first_row, h0, s5_ops, p):
    nb, seq, _ = x3.shape
    n = nb * seq
    nk = seq // S5_CHUNK
    x = x3.reshape(n, D_MODEL)
    per_seq_mod = first_row > 0

    def mod_row(i, tm):
        return first_row + (i * tm) // seq if per_seq_mod else 0

    us, uf, gt = _inproj(x, pos, mod, mod_row, p["norm1_g"], p["w_in"], 1024)
    ys, fin = _s5(us, s5_ops, h0, nb, nk)
    yf = _fourier(uf, nb, seq, min(seq, 512))
    xs, h2_rows, ridx, rw = _merge(ys, yf, gt, x, pos, mod, mod_row, p["norm2_g"], p["merge_w"], 512)
    return xs, h2_rows, ridx, rw, fin, mod_row


def _plan_of_stream(plan, s, n_tokens):
    tstart, p0, nv, stok, sw_rows = plan
    nsub = n_tokens // MOE_SUB
    npair = n_tokens * TOP_K
    off = s * npair
    tiles = slice(s * nsub * MOE_TS, (s + 1) * nsub * MOE_TS)
    nv_s = nv[tiles]
    p0_s = jnp.where(nv_s > 0, p0[tiles] - off, 0)
    return (tstart[s * nsub * (N_EXPERTS + 1):(s + 1) * nsub * (N_EXPERTS + 1)], p0_s, nv_s,
            stok[off:off + npair + MOE_TM], sw_rows[off // LANES:(off + npair + MOE_TM) // LANES])


def kernel(x_prompt, x_sample, state_s5_re, state_s5_im, c, c_ctx, w_ada, b_ada, norm1_g, norm2_g, w_in,
           lam_re, lam_im, log_dt, b_re, b_im, c_re, c_im, d_skip, w_glu, w_proj_s5, w_proj_f, w_out,
           w_router, router_bias, w1, w3, w2, ws1, ws3, ws2, final_norm_g):
    nb_ctx = x_prompt.shape[0]
    nb_lat, seq_lat, _ = x_sample.shape
    half = 2 * S5_STATE

    cond = jnp.concatenate([c_ctx[None], c, jnp.zeros((MOD_ROWS - 1 - nb_lat, D_MODEL), F32)], axis=0)
    mod = _adaln(cond, w_ada[0], b_ada[0]).reshape(MOD_ROWS, N_MOD, D_MODEL)

    s5_ops = _s5ops(lam_re[0], lam_im[0], log_dt[0], b_re[0], b_im[0], c_re[0], c_im[0], d_skip[0])
    p = dict(
        norm1_g=norm1_g[0][None], norm2_g=norm2_g[0][None], final_g=final_norm_g[None],
        w_in=w_in[0].astype(BF16), w1=w1[0], w3=w3[0], w2=w2[0],
        merge_w=(w_glu[0].astype(BF16), w_proj_s5[0].astype(BF16), w_proj_f[0].astype(BF16),
                 w_out[0].astype(BF16), w_router[0].T, router_bias[0][:, None],
                 ws1[0].astype(BF16), ws3[0].astype(BF16), ws2[0].astype(BF16)))

    def pack_state(sr, si):
        f = lambda a: a.astype(F32).transpose(2, 0, 1, 3).reshape(S5_GROUPS, a.shape[0], half)
        return jnp.concatenate([f(sr), f(si)], axis=-1)

    def unpack_state(fin, lo):
        nb = fin.shape[1]
        return fin[..., lo:lo + half].reshape(S5_GROUPS, nb, 2, S5_STATE).transpose(1, 2, 0, 3)[:, None]

    h0_ctx = jnp.zeros((S5_GROUPS, nb_ctx, 2 * half), F32)
    h0_lat = pack_state(state_s5_re[:, 0], state_s5_im[:, 0])
    streams = [(x_prompt, _mixers(x_prompt, None, mod, 0, h0_ctx, s5_ops, p)),
               (x_sample, _mixers(x_sample, _grid_pos_embed(seq_lat), mod, 1, h0_lat, s5_ops, p))]
    n_tokens = streams[0][1][0].shape[0]
    plan = _moe_plan(jnp.concatenate([m[2] for _, m in streams], axis=1),
                     jnp.concatenate([m[3] for _, m in streams], axis=1))
    outs = []
    for s, (x3, (xs, h2_rows, _, _, _, mod_row)) in enumerate(streams):
        out = _moe(h2_rows, _plan_of_stream(plan, s, n_tokens), p["w1"], p["w3"], p["w2"],
                   xs, mod, mod_row, p["final_g"])
        outs.append(out.reshape(x3.shape))
    fin = streams[0][1][4]
    return (outs[0], outs[1], unpack_state(fin, 0).astype(x_prompt.dtype),
            unpack_state(fin, half).astype(x_prompt.dtype))
```

```python
import functools
import math

import jax
import jax.numpy as jnp
from jax import lax
from jax.experimental import pallas as pl
from jax.experimental.pallas import tpu as pltpu

D_MODEL = 1024
GRID_W = 64
D_S5 = 768
S5_GROUP = 16
S5_GROUPS = 48
S5_STATE = 64
D_FNET = 256
FNET_GROUP = 64
N_EXPERTS = 64
TOP_K = 6
N_EXPERT_GROUPS = 8
EXPERTS_PER_GROUP = N_EXPERTS // N_EXPERT_GROUPS
TOPK_GROUPS = 4
D_EXPERT = 256
ROUTED_SCALE = 2.5
N_MOD = 6
EPS = 1e-6

S5_CHUNK = 16
S5_ROW = S5_CHUNK * S5_GROUP
MOD_ROWS = 8
ROUTE_ROWS = 8
LANES = 128
ROW_SUB = D_MODEL // LANES
VMEM_LIMIT = 56 * 1024 * 1024

BF16 = jnp.bfloat16
F32 = jnp.float32


def _params(sem, vmem=VMEM_LIMIT):
    return pltpu.CompilerParams(dimension_semantics=sem, vmem_limit_bytes=vmem)


def _rms(x, g):
    return x * lax.rsqrt(jnp.mean(x * x, axis=-1, keepdims=True) + EPS) * g


def _adaln_kernel(c_ref, w_ref, b_ref, o_ref):
    c = c_ref[...]
    s = c * jax.nn.sigmoid(c)
    w = w_ref[...]
    s_hi = s.astype(BF16)
    s_lo = (s - s_hi.astype(F32)).astype(BF16)
    w_hi = w.astype(BF16)
    w_lo = (w - w_hi.astype(F32)).astype(BF16)
    o_ref[...] = (jnp.dot(s_hi, w_hi, preferred_element_type=F32) + jnp.dot(s_lo, w_hi, preferred_element_type=F32)
                  + jnp.dot(s_hi, w_lo, preferred_element_type=F32) + b_ref[...])


def _adaln(cond, w_ada, b_ada):
    n_out = N_MOD * D_MODEL
    return pl.pallas_call(
        _adaln_kernel,
        out_shape=jax.ShapeDtypeStruct((MOD_ROWS, n_out), F32),
        grid=(N_MOD,),
        in_specs=[pl.BlockSpec((MOD_ROWS, D_MODEL), lambda i: (0, 0)),
                  pl.BlockSpec((D_MODEL, D_MODEL), lambda i: (0, i)),
                  pl.BlockSpec((1, D_MODEL), lambda i: (0, i))],
        out_specs=pl.BlockSpec((MOD_ROWS, D_MODEL), lambda i: (0, i)),
        compiler_params=_params(("arbitrary",)),
        name="adaln",
    )(cond, w_ada, b_ada.reshape(1, n_out))


def _inproj_kernel(has_pos, *refs):
    if has_pos:
        x_ref, pos_ref, mod_ref, g_ref, w_ref, us_ref, uf_ref, gt_ref = refs
        x = x_ref[...] + pos_ref[...]
    else:
        x_ref, mod_ref, g_ref, w_ref, us_ref, uf_ref, gt_ref = refs
        x = x_ref[...]
    m = mod_ref[0]
    h = _rms(x, g_ref[...]) * (1.0 + m[1:2, :]) + m[0:1, :]
    p = jnp.dot(h.astype(BF16), w_ref[...], preferred_element_type=F32)
    us_ref[...] = p[:, :D_S5]
    uf_ref[...] = p[:, D_S5:D_MODEL].astype(BF16)
    gt_ref[...] = jax.nn.sigmoid(p[:, D_MODEL:]).astype(BF16)


def _inproj(x, pos, mod, mod_row, norm_g, w_in_bf, tm):
    n = x.shape[0]
    has_pos = pos is not None
    row = lambda i: (i, 0)
    in_specs = [pl.BlockSpec((tm, D_MODEL), row)]
    args = [x]
    if has_pos:
        nper = pos.shape[0] // tm
        in_specs.append(pl.BlockSpec((tm, D_MODEL), lambda i: (i % nper, 0)))
        args.append(pos)
    in_specs += [pl.BlockSpec((1, N_MOD, D_MODEL), lambda i: (mod_row(i, tm), 0, 0)),
                 pl.BlockSpec((1, D_MODEL), lambda i: (0, 0)),
                 pl.BlockSpec(w_in_bf.shape, lambda i: (0, 0))]
    args += [mod, norm_g, w_in_bf]
    return pl.pallas_call(
        functools.partial(_inproj_kernel, has_pos),
        out_shape=(jax.ShapeDtypeStruct((n, D_S5), F32),
                   jax.ShapeDtypeStruct((n, D_FNET), BF16),
                   jax.ShapeDtypeStruct((n, 2 * D_MODEL), BF16)),
        grid=(n // tm,),
        in_specs=in_specs,
        out_specs=(pl.BlockSpec((tm, D_S5), row), pl.BlockSpec((tm, D_FNET), row),
                   pl.BlockSpec((tm, 2 * D_MODEL), row)),
        compiler_params=_params(("arbitrary",)),
        name="inproj",
    )(*args)


def _shift_lanes(x, k):
    if k == 0:
        return x
    z = jnp.zeros((x.shape[0], abs(k)), x.dtype)
    if k > 0:
        return jnp.concatenate([z, x[:, :x.shape[1] - k]], axis=1)
    return jnp.concatenate([x[:, -k:], z], axis=1)


def _s5ops_kernel(lam_re_ref, lam_im_ref, dt_ref, btr_ref, bti_ref, cr_ref, ci_ref, d_ref,
                  m_ref, wi_ref, wot_ref, coef_ref):
    hi = lax.Precision.HIGHEST
    lr = jnp.minimum(lam_re_ref[0], -1e-4)
    li = lam_im_ref[0]
    dt = jnp.exp(dt_ref[0])
    mag = jnp.exp(lr * dt)
    ar = mag * jnp.cos(li * dt)
    ai = mag * jnp.sin(li * dt)
    den = lr * lr + li * li
    nr = ar - 1.0
    qr = (nr * lr + ai * li) / den
    qi = (ai * lr - nr * li) / den
    npow = S5_CHUNK + 1
    nrow = (lax.broadcasted_iota(jnp.int32, (2 * npow, 1), 0) // 2).astype(F32)
    stack = lambda v: jnp.concatenate([v] * npow, axis=0)
    pmag = jnp.exp(nrow * stack(lr * dt))
    pang = nrow * stack(li * dt)
    pr_all = pmag * jnp.cos(pang)
    pi_all = pmag * jnp.sin(pang)
    pr = [pr_all[2 * n:2 * n + 2] for n in range(npow)]
    pi = [pi_all[2 * n:2 * n + 2] for n in range(npow)]
    bbr, bbi, car, cai = [], [], [], []
    for d in range(2):
        btr = btr_ref[0, d]
        bti = bti_ref[0, d]
        bbr.append(qr[d:d + 1] * btr - qi[d:d + 1] * bti)
        bbi.append(qr[d:d + 1] * bti + qi[d:d + 1] * btr)
        cr = cr_ref[0, d]
        ci = ci_ref[0, d]
        car.append([cr * pr[n][d:d + 1] - ci * pi[n][d:d + 1] for n in range(S5_CHUNK + 1)])
        cai.append([cr * pi[n][d:d + 1] + ci * pr[n][d:d + 1] for n in range(S5_CHUNK + 1)])

    def lag_kernels(d, order):
        a = jnp.concatenate([car[d][n] for n in order], axis=0)
        b = jnp.concatenate([cai[d][n] for n in order], axis=0)
        dn = (((1,), (1,)), ((), ()))
        return (lax.dot_general(bbr[d], a, dn, precision=hi, preferred_element_type=F32)
                - lax.dot_general(bbi[d], b, dn, precision=hi, preferred_element_type=F32))

    ktf = lag_kernels(0, range(S5_CHUNK))
    ktb = lag_kernels(1, range(S5_CHUNK - 1, -1, -1))
    row = lax.broadcasted_iota(jnp.int32, (S5_GROUP, S5_ROW), 0)
    lane = lax.broadcasted_iota(jnp.int32, (S5_GROUP, S5_ROW), 1)
    dcol = d_ref[0]
    for j in range(S5_CHUNK):
        rows = slice(j * S5_GROUP, (j + 1) * S5_GROUP)
        blk = _shift_lanes(ktf, S5_GROUP * j) + _shift_lanes(ktb, -S5_GROUP * (S5_CHUNK - 1 - j))
        blk = blk + jnp.where(lane == S5_GROUP * j + row, dcol, 0.0)
        m_ref[0, rows, :] = blk.astype(BF16)
        nf = S5_CHUNK - 1 - j
        wi = jnp.concatenate([pr[nf][0:1] * bbr[0] - pi[nf][0:1] * bbi[0],
                              pr[j][1:2] * bbr[1] - pi[j][1:2] * bbi[1],
                              pr[nf][0:1] * bbi[0] + pi[nf][0:1] * bbr[0],
                              pr[j][1:2] * bbi[1] + pi[j][1:2] * bbr[1]], axis=1)
        wi_ref[0, rows, :] = wi.astype(BF16)
        wot = jnp.concatenate([car[0][j + 1], car[1][S5_CHUNK - j],
                               -cai[0][j + 1], -cai[1][S5_CHUNK - j]], axis=1)
        wot_ref[0, rows, :] = wot.astype(BF16)
    coef_ref[0, 0:1, :] = jnp.concatenate([pr[S5_CHUNK][0:1], pr[S5_CHUNK][1:2]], axis=1)
    coef_ref[0, 1:2, :] = jnp.concatenate([pi[S5_CHUNK][0:1], pi[S5_CHUNK][1:2]], axis=1)


def _s5ops(lam_re, lam_im, log_dt, b_re, b_im, c_re, c_im, d_skip):
    g3 = lambda g: (g, 0, 0)
    g4 = lambda g: (g, 0, 0, 0)
    sw = lambda a: jnp.swapaxes(a.astype(F32), 0, 1)
    dt = jnp.broadcast_to(sw(log_dt)[..., None], (S5_GROUPS, 2, S5_STATE))
    args = (sw(lam_re), sw(lam_im), dt, sw(jnp.swapaxes(b_re, 2, 3)), sw(jnp.swapaxes(b_im, 2, 3)),
            sw(c_re), sw(c_im), d_skip.astype(F32).reshape(S5_GROUPS, S5_GROUP, 1))
    vec = pl.BlockSpec((1, 2, S5_STATE), g3)
    mat = pl.BlockSpec((1, 2, S5_GROUP, S5_STATE), g4)
    op = pl.BlockSpec((1, S5_ROW, S5_ROW), g3)
    return pl.pallas_call(
        _s5ops_kernel,
        out_shape=(jax.ShapeDtypeStruct((S5_GROUPS, S5_ROW, S5_ROW), BF16),) * 3
        + (jax.ShapeDtypeStruct((S5_GROUPS, 2, 2 * S5_STATE), F32),),
        grid=(S5_GROUPS,),
        in_specs=[vec, vec, vec, mat, mat, mat, mat, pl.BlockSpec((1, S5_GROUP, 1), g3)],
        out_specs=(op, op, op, pl.BlockSpec((1, 2, 2 * S5_STATE), g3)),
        compiler_params=_params(("arbitrary",)),
        name="s5ops",
    )(*args)


S5_BLOCK_GROUPS = LANES // S5_GROUP


S5_HALF_T = LANES // S5_GROUP


def _s5_perm():
    a = jnp.arange(S5_HALF_T * LANES, dtype=jnp.int32)
    dst = ((a % LANES) // S5_GROUP) * LANES + (a // LANES) * S5_GROUP + a % S5_GROUP
    return (dst[:, None] == a[None, :]).astype(BF16)


def _s5_kernel(nb, nk, u_ref, perm_ref, m_ref, wi_ref, wot_ref, coef_ref, h0_ref, y_ref, fin_ref,
               sr_ref, si_ref, efr_ref, ebr_ref, efi_ref, ebi_ref, uall_ref, yall_ref):
    gl = pl.program_id(1)
    rows = nb * nk
    half = 2 * S5_STATE

    @pl.when(gl == 0)
    def _():
        for hh in range(S5_CHUNK // S5_HALF_T):
            xcat = jnp.concatenate([u_ref[pl.ds(hh * S5_HALF_T + tt, rows, stride=S5_CHUNK), :].astype(BF16)
                                    for tt in range(S5_HALF_T)], axis=1)
            uh = jnp.dot(xcat, perm_ref[...], preferred_element_type=F32).astype(BF16)
            for g in range(S5_BLOCK_GROUPS):
                uall_ref[g, :, hh * LANES:(hh + 1) * LANES] = uh[:, g * LANES:(g + 1) * LANES]

    u = uall_ref[gl]
    s = jnp.dot(u, wi_ref[0], preferred_element_type=F32)
    sr_ref[...] = s[:, :half]
    si_ref[...] = s[:, half:]
    c_r = coef_ref[0, 0:1, :]
    c_i = coef_ref[0, 1:2, :]
    e_r = h0_ref[0, :, :half]
    e_i = h0_ref[0, :, half:]
    is_fwd = lax.broadcasted_iota(jnp.int32, (nb, half), 1) < S5_STATE
    for j in range(nk):
        rf = pl.ds(j, nb, stride=nk)
        rb = pl.ds(nk - 1 - j, nb, stride=nk)
        efr_ref[rf, :] = e_r
        ebr_ref[rb, :] = e_r
        efi_ref[rf, :] = e_i
        ebi_ref[rb, :] = e_i
        s_r = jnp.where(is_fwd, sr_ref[rf, :], sr_ref[rb, :])
        s_i = jnp.where(is_fwd, si_ref[rf, :], si_ref[rb, :])
        e_r, e_i = c_r * e_r - c_i * e_i + s_r, c_r * e_i + c_i * e_r + s_i
    fin_ref[0, :, :half] = e_r
    fin_ref[0, :, half:] = e_i
    y = jnp.dot(u, m_ref[0], preferred_element_type=F32)
    fwd_rows = lax.broadcasted_iota(jnp.int32, (rows, half), 1) < S5_STATE
    e = jnp.concatenate([jnp.where(fwd_rows, efr_ref[...], ebr_ref[...]),
                         jnp.where(fwd_rows, efi_ref[...], ebi_ref[...])], axis=1).astype(BF16)
    y = y + lax.dot_general(e, wot_ref[0], (((1,), (1,)), ((), ())), preferred_element_type=F32)
    yall_ref[gl] = y.astype(BF16)

    @pl.when(gl == S5_BLOCK_GROUPS - 1)
    def _():
        for hh in range(S5_CHUNK // S5_HALF_T):
            ycat = jnp.concatenate([yall_ref[g, :, hh * LANES:(hh + 1) * LANES]
                                    for g in range(S5_BLOCK_GROUPS)], axis=1)
            out = lax.dot_general(ycat, perm_ref[...], (((1,), (1,)), ((), ())), preferred_element_type=F32)
            for tt in range(S5_HALF_T):
                y_ref[pl.ds(hh * S5_HALF_T + tt, rows, stride=S5_CHUNK), :] = out[:, tt * LANES:(tt + 1) * LANES]


def _s5(u, ops, h0, nb, nk):
    m, w_in, w_out, coef = ops
    n = u.shape[0]
    rows = nb * nk
    perm = _s5_perm()
    g3 = lambda b, g: (b * S5_BLOCK_GROUPS + g, 0, 0)
    blk = lambda b, g: (0, b)
    return pl.pallas_call(
        functools.partial(_s5_kernel, nb, nk),
        out_shape=(jax.ShapeDtypeStruct((n, D_S5), F32),
                   jax.ShapeDtypeStruct((S5_GROUPS, nb, 4 * S5_STATE), F32)),
        grid=(S5_GROUPS // S5_BLOCK_GROUPS, S5_BLOCK_GROUPS),
        in_specs=[pl.BlockSpec((n, LANES), blk),
                  pl.BlockSpec(perm.shape, lambda b, g: (0, 0)),
                  pl.BlockSpec((1, S5_ROW, S5_ROW), g3),
                  pl.BlockSpec((1, S5_ROW, 4 * S5_STATE), g3),
                  pl.BlockSpec((1, 4 * S5_STATE, S5_ROW), g3),
                  pl.BlockSpec((1, 2, 2 * S5_STATE), g3),
                  pl.BlockSpec((1, nb, 4 * S5_STATE), g3)],
        out_specs=(pl.BlockSpec((n, LANES), blk),
                   pl.BlockSpec((1, nb, 4 * S5_STATE), g3)),
        scratch_shapes=[pltpu.VMEM((rows, 2 * S5_STATE), F32)] * 6 + [
            pltpu.VMEM((S5_BLOCK_GROUPS, rows, S5_ROW), BF16),
            pltpu.VMEM((S5_BLOCK_GROUPS, rows, S5_ROW), BF16)],
        compiler_params=_params(("arbitrary", "arbitrary")),
        name="s5",
    )(u, perm, m, w_in, w_out, coef, h0)


def _dft_tables(seq):
    k = jnp.arange(seq, dtype=jnp.int32)
    na = seq // FNET_GROUP
    ang_a = (2.0 * math.pi / na) * ((jnp.arange(na, dtype=jnp.int32)[:, None] * k[None, :]) % na).astype(F32)
    ang_b = (2.0 * math.pi / seq) * ((jnp.arange(FNET_GROUP, dtype=jnp.int32)[:, None] * k[None, :]) % seq).astype(F32)
    ca, sa = jnp.cos(ang_a)[:, None, :], jnp.sin(ang_a)[:, None, :]
    cb, sb = jnp.cos(ang_b)[None, :, :], jnp.sin(ang_b)[None, :, :]
    cos_jk = (ca * cb - sa * sb).reshape(seq, seq)
    sin_jk = (sa * cb + ca * sb).reshape(seq, seq)
    cs = jnp.concatenate([cos_jk, -sin_jk], axis=1).astype(BF16)
    c = jnp.arange(D_FNET, dtype=jnp.int32)
    same = (c[:, None] // FNET_GROUP) == (c[None, :] // FNET_GROUP)
    angc = (2.0 * math.pi / FNET_GROUP) * (((c[:, None] % FNET_GROUP) * (c[None, :] % FNET_GROUP))
                                           % FNET_GROUP).astype(F32)
    scale = 1.0 / math.sqrt(seq * FNET_GROUP)
    bdc = jnp.where(same, jnp.cos(angc) * scale, 0.0).astype(BF16)
    bds = jnp.where(same, jnp.sin(angc) * scale, 0.0).astype(BF16)
    return cs, bdc, bds


def _fourier_kernel(seq, z_ref, cs_ref, bdc_ref, bds_ref, o_ref, zz_ref):
    @pl.when(pl.program_id(1) == 0)
    def _():
        z = z_ref[...]
        zz_ref[0:seq, :] = jnp.dot(z, bdc_ref[...], preferred_element_type=F32).astype(BF16)
        zz_ref[seq:, :] = jnp.dot(z, bds_ref[...], preferred_element_type=F32).astype(BF16)

    o_ref[...] = jnp.dot(cs_ref[...], zz_ref[...], preferred_element_type=F32).astype(BF16)


def _fourier(z, nb, seq, tl):
    cs, bdc, bds = _dft_tables(seq)
    nt = seq // tl
    return pl.pallas_call(
        functools.partial(_fourier_kernel, seq),
        out_shape=jax.ShapeDtypeStruct(z.shape, BF16),
        grid=(nb, nt),
        in_specs=[pl.BlockSpec((seq, D_FNET), lambda b, i: (b, 0)),
                  pl.BlockSpec((tl, 2 * seq), lambda b, i: (i, 0)),
                  pl.BlockSpec((D_FNET, D_FNET), lambda b, i: (0, 0)),
                  pl.BlockSpec((D_FNET, D_FNET), lambda b, i: (0, 0))],
        out_specs=pl.BlockSpec((tl, D_FNET), lambda b, i: (b * nt + i, 0)),
        scratch_shapes=[pltpu.VMEM((2 * seq, D_FNET), BF16)],
        compiler_params=_params(("arbitrary", "arbitrary")),
        name="fourier",
    )(z, cs, bdc, bds)


def _first_argmax_mask(v, iota, size):
    m = jnp.max(v, axis=0, keepdims=True)
    first = jnp.min(jnp.where(v == m, iota, size), axis=0, keepdims=True)
    return iota == first


def _route(logits_t, bias_col):
    tm = logits_t.shape[1]
    neg = -jnp.inf
    s = jax.nn.sigmoid(logits_t)
    biased = s + bias_col
    io8 = lax.broadcasted_iota(jnp.int32, (EXPERTS_PER_GROUP, tm), 0)
    gs_rows = []
    for g in range(N_EXPERT_GROUPS):
        blk = biased[g * EXPERTS_PER_GROUP:(g + 1) * EXPERTS_PER_GROUP, :]
        m1 = jnp.max(blk, axis=0, keepdims=True)
        rest = jnp.where(_first_argmax_mask(blk, io8, EXPERTS_PER_GROUP), neg, blk)
        gs_rows.append(m1 + jnp.max(rest, axis=0, keepdims=True))
    gs = jnp.concatenate(gs_rows, axis=0)
    iog = lax.broadcasted_iota(jnp.int32, (N_EXPERT_GROUPS, tm), 0)
    gsel = jnp.zeros((N_EXPERT_GROUPS, tm), F32)
    for _ in range(TOPK_GROUPS):
        sel = _first_argmax_mask(gs, iog, N_EXPERT_GROUPS)
        gsel = jnp.where(sel, 1.0, gsel)
        gs = jnp.where(sel, neg, gs)
    emask = jnp.concatenate(
        [jnp.broadcast_to(gsel[g:g + 1, :], (EXPERTS_PER_GROUP, tm)) for g in range(N_EXPERT_GROUPS)], axis=0)
    v = jnp.where(emask > 0.0, biased, neg)
    ioe = lax.broadcasted_iota(jnp.int32, (N_EXPERTS, tm), 0)
    idx_rows, s_rows = [], []
    for _ in range(TOP_K):
        sel = _first_argmax_mask(v, ioe, N_EXPERTS)
        idx_rows.append(jnp.sum(jnp.where(sel, ioe, 0), axis=0, keepdims=True))
        s_rows.append(jnp.sum(jnp.where(sel, s, 0.0), axis=0, keepdims=True))
        v = jnp.where(sel, neg, v)
    denom = s_rows[0]
    for r in s_rows[1:]:
        denom = denom + r
    pad = ROUTE_ROWS - TOP_K
    idx = jnp.concatenate(idx_rows + [jnp.zeros((pad, tm), jnp.int32)], axis=0)
    w = jnp.concatenate([r / denom * ROUTED_SCALE for r in s_rows] + [jnp.zeros((pad, tm), F32)], axis=0)
    return idx, w


def _merge_kernel(has_pos, *refs):
    if has_pos:
        (ys_ref, yf_ref, gt_ref, x_ref, pos_ref, mod_ref, n2_ref, wglu_ref, wps_ref, wpf_ref, wout_ref,
         wrt_ref, rb_ref, ws1_ref, ws3_ref, ws2_ref, xs_ref, h2_ref, ridx_ref, rw_ref) = refs
        x = x_ref[...] + pos_ref[...]
    else:
        (ys_ref, yf_ref, gt_ref, x_ref, mod_ref, n2_ref, wglu_ref, wps_ref, wpf_ref, wout_ref,
         wrt_ref, rb_ref, ws1_ref, ws3_ref, ws2_ref, xs_ref, h2_ref, ridx_ref, rw_ref) = refs
        x = x_ref[...]
    m = mod_ref[0]
    g = jax.nn.gelu(ys_ref[...].astype(F32))
    a = g * jax.nn.sigmoid(jnp.dot(g.astype(BF16), wglu_ref[...], preferred_element_type=F32))
    pa = jnp.dot(a.astype(BF16), wps_ref[...], preferred_element_type=F32)
    pb = jnp.dot(yf_ref[...], wpf_ref[...], preferred_element_type=F32)
    gt = gt_ref[...].astype(F32)
    merged = gt[:, :D_MODEL] * pa + gt[:, D_MODEL:] * pb
    x1 = x + m[2:3, :] * jnp.dot(merged.astype(BF16), wout_ref[...], preferred_element_type=F32)
    h2 = _rms(x1, n2_ref[...]) * (1.0 + m[4:5, :]) + m[3:4, :]
    hb = h2.astype(BF16)
    for j in range(ROW_SUB):
        h2_ref[pl.ds(j, h2.shape[0], stride=ROW_SUB), :] = h2[:, j * LANES:(j + 1) * LANES]
    wr = wrt_ref[...]
    wr_hi = wr.astype(BF16)
    wr_lo = (wr - wr_hi.astype(F32)).astype(BF16)
    h_lo = (h2 - hb.astype(F32)).astype(BF16)
    dn = (((1,), (1,)), ((), ()))
    logits_t = (lax.dot_general(wr_hi, hb, dn, preferred_element_type=F32)
                + lax.dot_general(wr_hi, h_lo, dn, preferred_element_type=F32)
                + lax.dot_general(wr_lo, hb, dn, preferred_element_type=F32))
    ridx_ref[...], rw_ref[...] = _route(logits_t, rb_ref[...])
    s1 = jnp.dot(hb, ws1_ref[...], preferred_element_type=F32)
    s3 = jnp.dot(hb, ws3_ref[...], preferred_element_type=F32)
    shared = jnp.dot((s1 * jax.nn.sigmoid(s1) * s3).astype(BF16), ws2_ref[...], preferred_element_type=F32)
    xs_ref[...] = x1 + m[5:6, :] * shared


def _merge(ys, yf, gt, x, pos, mod, mod_row, n2, weights, tm):
    n = x.shape[0]
    has_pos = pos is not None
    row = lambda i: (i, 0)
    const = lambda a: pl.BlockSpec(a.shape, lambda i: (0,) * a.ndim)
    in_specs = [pl.BlockSpec((tm, D_S5), row), pl.BlockSpec((tm, D_FNET), row),
                pl.BlockSpec((tm, 2 * D_MODEL), row), pl.BlockSpec((tm, D_MODEL), row)]
    args = [ys, yf, gt, x]
    if has_pos:
        nper = pos.shape[0] // tm
        in_specs.append(pl.BlockSpec((tm, D_MODEL), lambda i: (i % nper, 0)))
        args.append(pos)
    in_specs += [pl.BlockSpec((1, N_MOD, D_MODEL), lambda i: (mod_row(i, tm), 0, 0)), const(n2)]
    args += [mod, n2]
    in_specs += [const(w) for w in weights]
    args += list(weights)
    return pl.pallas_call(
        functools.partial(_merge_kernel, has_pos),
        out_shape=(jax.ShapeDtypeStruct((n, D_MODEL), F32),
                   jax.ShapeDtypeStruct((n * ROW_SUB, LANES), F32),
                   jax.ShapeDtypeStruct((ROUTE_ROWS, n), jnp.int32),
                   jax.ShapeDtypeStruct((ROUTE_ROWS, n), F32)),
        grid=(n // tm,),
        in_specs=in_specs,
        out_specs=(pl.BlockSpec((tm, D_MODEL), row), pl.BlockSpec((tm * ROW_SUB, LANES), row),
                   pl.BlockSpec((ROUTE_ROWS, tm), lambda i: (0, i)),
                   pl.BlockSpec((ROUTE_ROWS, tm), lambda i: (0, i))),
        compiler_params=_params(("arbitrary",)),
        name="merge",
    )(*args)


MOE_SUB = 4096
MOE_TM = 128
MOE_TMAX = MOE_SUB * TOP_K // MOE_TM + N_EXPERTS
MOE_SORT_SUBS = 2
MOE_PAD = 2
MOE_TS = MOE_TMAX + 2 * MOE_PAD
MOE_DUMMY = 256
MOE_RMW = 8
MOE_FT = 256
ROW_TILE = MOE_TM * ROW_SUB


def _moe_plan(ridx, rw):
    n = ridx.shape[1]
    nsub = n // MOE_SUB
    npair = n * TOP_K
    t = jnp.arange(n, dtype=jnp.int32)
    key = (((t // MOE_SUB) * N_EXPERTS)[None] + ridx[:TOP_K]) * MOE_SUB + (t % MOE_SUB)[None]
    group = MOE_SORT_SUBS * MOE_SUB
    parts = [lax.sort((key[:, g:g + group].reshape(-1), rw[:TOP_K, g:g + group].reshape(-1)), num_keys=1)
             for g in range(0, n, group)]
    skey = jnp.concatenate([pk for pk, _ in parts])
    sw = jnp.concatenate([pw for _, pw in parts])
    stok = jnp.concatenate([(skey % MOE_SUB) * ROW_SUB, jnp.zeros((MOE_TM,), jnp.int32)])
    sw_rows = jnp.concatenate([sw, jnp.zeros((MOE_TM,), F32)]).reshape((npair + MOE_TM) // LANES, 1, LANES)
    hits = ridx[:TOP_K].reshape(TOP_K, nsub, 1, MOE_SUB) == jnp.arange(N_EXPERTS, dtype=jnp.int32)[None, None, :, None]
    cnt = jnp.sum(hits.astype(jnp.int32), axis=(0, 3))
    poff = (jnp.cumsum(cnt.reshape(-1)) - cnt.reshape(-1)).reshape(nsub, N_EXPERTS)
    ntile = (cnt + MOE_TM - 1) // MOE_TM
    tcum = jnp.cumsum(ntile, axis=1)
    toff = tcum - ntile
    tstart = jnp.concatenate([toff, tcum[:, -1:]], axis=1).reshape(-1).astype(jnp.int32)
    j = jnp.arange(MOE_TS, dtype=jnp.int32) - MOE_PAD
    valid = (j[None] >= 0) & (j[None] < tcum[:, -1:])
    te = jnp.minimum(jnp.sum(j[None, :, None] >= tcum[:, None, :], axis=-1), N_EXPERTS - 1)
    pick = lambda a: jnp.take_along_axis(a, te, axis=1)
    first = (j[None] - pick(toff)) * MOE_TM
    p0 = jnp.where(valid, pick(poff) + first, 0).reshape(-1).astype(jnp.int32)
    nv = jnp.where(valid, jnp.minimum(pick(cnt) - first, MOE_TM), 0).reshape(-1).astype(jnp.int32)
    return tstart, p0, nv, stok, sw_rows


def _moe_kernel(ts_ref, p0_ref, nv_ref, tok_ref, sw_ref, src_ref, w1_ref, w3_ref, w2_ref, xs_ref, mod_ref,
                fg_ref, o_ref, y_ref, xt_ref, xb_ref, act_ref, ot_ref, w1b_ref, w3b_ref, w2b_ref, slot_ref):
    sub = pl.program_id(0)
    e = pl.program_id(1)
    base = sub * MOE_TS + MOE_PAD
    ec = jnp.minimum(e, N_EXPERTS - 1)
    first = ts_ref[sub * (N_EXPERTS + 1) + ec]
    last = jnp.where(e < N_EXPERTS, ts_ref[sub * (N_EXPERTS + 1) + ec + 1], first)

    def gather(p0):
        for mi in range(MOE_TM):
            tok = pl.multiple_of(tok_ref[p0 + mi], ROW_SUB)
            xt_ref[mi * ROW_SUB:(mi + 1) * ROW_SUB, :] = src_ref[pl.ds(tok, ROW_SUB), :]
        for j in range(ROW_SUB):
            xb_ref[:, j * LANES:(j + 1) * LANES] = xt_ref[pl.ds(j, MOE_TM, stride=ROW_SUB), :].astype(BF16)

    def scatter(p0, nv, masked):
        for u in range(0, MOE_TM, MOE_RMW):
            new = []
            for i in range(MOE_RMW):
                tok = tok_ref[p0 + u + i]
                if masked:
                    tok = jnp.where(u + i < nv, tok, MOE_SUB * ROW_SUB)
                tok = pl.multiple_of(tok, ROW_SUB)
                new.append((tok, y_ref[pl.ds(tok, ROW_SUB), :]
                            + ot_ref[(u + i) * ROW_SUB:(u + i + 1) * ROW_SUB, :]))
            for tok, v in new:
                y_ref[pl.ds(tok, ROW_SUB), :] = v

    @pl.when(e == 0)
    def _():
        y_ref[...] = jnp.zeros_like(y_ref)
        ot_ref[...] = jnp.zeros_like(ot_ref)
        act_ref[...] = jnp.zeros_like(act_ref)
        w2b_ref[...] = jnp.zeros_like(w2b_ref)
        slot_ref[0] = 0
        gather(p0_ref[base])

    @pl.when(last > first)
    def _():
        slot_ref[0] = 1 - slot_ref[0]
        w1b_ref[...] = w1_ref[0].astype(BF16)
        w3b_ref[...] = w3_ref[0].astype(BF16)
        w2b_ref[slot_ref[0]] = w2_ref[0].astype(BF16)

    slot = slot_ref[0]

    def down_proj(w2_slot):
        o = jnp.dot(act_ref[...], w2b_ref[w2_slot], preferred_element_type=F32)
        for j in range(ROW_SUB):
            ot_ref[pl.ds(j, MOE_TM, stride=ROW_SUB), :] = o[:, j * LANES:(j + 1) * LANES]

    def step(i, masked):
        cur = base + i
        scatter(p0_ref[cur - 2], nv_ref[cur - 2], masked)
        down_proj(jnp.where(i > first, slot, 1 - slot))
        p0 = p0_ref[cur]
        nv = nv_ref[cur]
        x = xb_ref[...]
        a = jnp.dot(x, w1b_ref[...], preferred_element_type=F32)
        b = jnp.dot(x, w3b_ref[...], preferred_element_type=F32)
        r0 = p0 // LANES
        c = p0 % LANES
        lane = lax.broadcasted_iota(jnp.int32, (1, LANES), 1)
        rows = lax.broadcasted_iota(jnp.int32, (LANES, LANES), 0)
        cols = lax.broadcasted_iota(jnp.int32, (LANES, LANES), 1)
        gparts = []
        for hh in range(MOE_TM // LANES):
            ga = pltpu.roll(sw_ref[r0 + hh], LANES - c, axis=1)
            gb = pltpu.roll(sw_ref[r0 + hh + 1], LANES - c, axis=1)
            g = jnp.where(lane + hh * LANES < nv, jnp.where(lane < LANES - c, ga, gb), 0.0)
            gparts.append(jnp.sum(jnp.where(rows == cols, jnp.broadcast_to(g, (LANES, LANES)), 0.0),
                                  axis=1, keepdims=True))
        gcol = jnp.concatenate(gparts, axis=0)
        act_ref[...] = (a * jax.nn.sigmoid(a) * b * gcol).astype(BF16)
        gather(p0_ref[cur + 1])

    def body(i, carry):
        step(i, True)
        return carry

    lax.fori_loop(first, last, body, 0)

    @pl.when(e == N_EXPERTS - 1)
    def _():
        scatter(p0_ref[base + last - 2], nv_ref[base + last - 2], True)
        down_proj(slot)
        scatter(p0_ref[base + last - 1], nv_ref[base + last - 1], True)

    @pl.when(e >= N_EXPERTS)
    def _():
        row0 = pl.multiple_of((e - N_EXPERTS) * (MOE_FT * ROW_SUB), ROW_SUB)
        y = jnp.concatenate([y_ref[pl.ds(row0 + j, MOE_FT, stride=ROW_SUB), :] for j in range(ROW_SUB)],
                            axis=1)
        x2 = xs_ref[...] + mod_ref[0][5:6, :] * y
        o_ref[...] = _rms(x2, fg_ref[...])


def _moe(h2_rows, plan, w1, w3, w2, xs, mod, mod_row, fg):
    tstart, p0, nv, stok, sw_rows = plan
    nsub = h2_rows.shape[0] // (MOE_SUB * ROW_SUB)
    per_sub = MOE_SUB // MOE_FT
    wmap = lambda s, e, ts, p0, nv: (jnp.minimum(e, N_EXPERTS - 1), 0, 0)
    sub2 = lambda s, e, ts, p0, nv: (s, 0)
    out_tile = lambda s, e: s * per_sub + jnp.clip(e - N_EXPERTS, 0, per_sub - 1)
    grid_spec = pltpu.PrefetchScalarGridSpec(
        num_scalar_prefetch=3,
        grid=(nsub, N_EXPERTS + per_sub),
        in_specs=[pl.BlockSpec(memory_space=pltpu.SMEM),
                  pl.BlockSpec(sw_rows.shape, lambda s, e, ts, p0, nv: (0, 0, 0)),
                  pl.BlockSpec((MOE_SUB * ROW_SUB, LANES), sub2, pipeline_mode=pl.Buffered(1)),
                  pl.BlockSpec((1, D_MODEL, D_EXPERT), wmap),
                  pl.BlockSpec((1, D_MODEL, D_EXPERT), wmap),
                  pl.BlockSpec((1, D_EXPERT, D_MODEL), wmap),
                  pl.BlockSpec((MOE_FT, D_MODEL), lambda s, e, ts, p0, nv: (out_tile(s, e), 0)),
                  pl.BlockSpec((1, N_MOD, D_MODEL),
                               lambda s, e, ts, p0, nv: (mod_row(out_tile(s, e), MOE_FT), 0, 0)),
                  pl.BlockSpec((1, D_MODEL), lambda s, e, ts, p0, nv: (0, 0))],
        out_specs=pl.BlockSpec((MOE_FT, D_MODEL), lambda s, e, ts, p0, nv: (out_tile(s, e), 0)),
        scratch_shapes=[pltpu.VMEM(((MOE_SUB + MOE_DUMMY) * ROW_SUB, LANES), F32),
                        pltpu.VMEM((ROW_TILE, LANES), F32), pltpu.VMEM((MOE_TM, D_MODEL), BF16),
                        pltpu.VMEM((MOE_TM, D_EXPERT), BF16), pltpu.VMEM((ROW_TILE, LANES), F32),
                        pltpu.VMEM((D_MODEL, D_EXPERT), BF16), pltpu.VMEM((D_MODEL, D_EXPERT), BF16),
                        pltpu.VMEM((2, D_EXPERT, D_MODEL), BF16), pltpu.SMEM((1,), jnp.int32)])
    return pl.pallas_call(
        _moe_kernel,
        grid_spec=grid_spec,
        out_shape=jax.ShapeDtypeStruct(xs.shape, F32),
        compiler_params=_params(("arbitrary", "arbitrary")),
        name="moe",
    )(tstart, p0, nv, stok, sw_rows, h2_rows, w1, w3, w2, xs, mod, fg)


def _grid_pos_embed(n_tokens):
    rows = n_tokens // GRID_W
    quarter = D_MODEL // 4
    omega = 1.0 / (10000.0 ** (jnp.arange(quarter, dtype=F32) / quarter))

    def emb(count):
        a = jnp.arange(count, dtype=F32)[:, None] * omega
        return jnp.concatenate([jnp.sin(a), jnp.cos(a)], axis=-1)

    by_row = jnp.repeat(emb(rows), GRID_W, axis=0)
    by_col = jnp.tile(emb(GRID_W), (rows, 1))
    return jnp.concatenate([by_row, by_col], axis=-1)


def _mixers(x3, pos, mod, first_row, h0, s5_ops, p):
    nb, seq, _ = x3.shape
    n = nb * seq
    nk = seq // S5_CHUNK
    x = x3.reshape(n, D_MODEL)
    per_seq_mod = first_row > 0

    def mod_row(i, tm):
        return first_row + (i * tm) // seq if per_seq_mod else 0

    us, uf, gt = _inproj(x, pos, mod, mod_row, p["norm1_g"], p["w_in"], 1024)
    ys, fin = _s5(us, s5_ops, h0, nb, nk)
    yf = _fourier(uf, nb, seq, min(seq, 512))
    xs, h2_rows, ridx, rw = _merge(ys, yf, gt, x, pos, mod, mod_row, p["norm2_g"], p["merge_w"], 512)
    return xs, h2_rows, ridx, rw, fin, mod_row


def _plan_of_stream(plan, s, n_tokens):
    tstart, p0, nv, stok, sw_rows = plan
    nsub = n_tokens // MOE_SUB
    npair = n_tokens * TOP_K
    off = s * npair
    tiles = slice(s * nsub * MOE_TS, (s + 1) * nsub * MOE_TS)
    nv_s = nv[tiles]
    p0_s = jnp.where(nv_s > 0, p0[tiles] - off, 0)
    return (tstart[s * nsub * (N_EXPERTS + 1):(s + 1) * nsub * (N_EXPERTS + 1)], p0_s, nv_s,
            stok[off:off + npair + MOE_TM], sw_rows[off // LANES:(off + npair + MOE_TM) // LANES])


def kernel(x_prompt, x_sample, state_s5_re, state_s5_im, c, c_ctx, w_ada, b_ada, norm1_g, norm2_g, w_in,
           lam_re, lam_im, log_dt, b_re, b_im, c_re, c_im, d_skip, w_glu, w_proj_s5, w_proj_f, w_out,
           w_router, router_bias, w1, w3, w2, ws1, ws3, ws2, final_norm_g):
    nb_ctx = x_prompt.shape[0]
    nb_lat, seq_lat, _ = x_sample.shape
    half = 2 * S5_STATE

    cond = jnp.concatenate([c_ctx[None], c, jnp.zeros((MOD_ROWS - 1 - nb_lat, D_MODEL), F32)], axis=0)
    mod = _adaln(cond, w_ada[0], b_ada[0]).reshape(MOD_ROWS, N_MOD, D_MODEL)

    s5_ops = _s5ops(lam_re[0], lam_im[0], log_dt[0], b_re[0], b_im[0], c_re[0], c_im[0], d_skip[0])
    p = dict(
        norm1_g=norm1_g[0][None], norm2_g=norm2_g[0][None], final_g=final_norm_g[None],
        w_in=w_in[0].astype(BF16), w1=w1[0], w3=w3[0], w2=w2[0],
        merge_w=(w_glu[0].astype(BF16), w_proj_s5[0].astype(BF16), w_proj_f[0].astype(BF16),
                 w_out[0].astype(BF16), w_router[0].T, router_bias[0][:, None],
                 ws1[0].astype(BF16), ws3[0].astype(BF16), ws2[0].astype(BF16)))

    def pack_state(sr, si):
        f = lambda a: a.astype(F32).transpose(2, 0, 1, 3).reshape(S5_GROUPS, a.shape[0], half)
        return jnp.concatenate([f(sr), f(si)], axis=-1)

    def unpack_state(fin, lo):
        nb = fin.shape[1]
        return fin[..., lo:lo + half].reshape(S5_GROUPS, nb, 2, S5_STATE).transpose(1, 2, 0, 3)[:, None]

    h0_ctx = jnp.zeros((S5_GROUPS, nb_ctx, 2 * half), F32)
    h0_lat = pack_state(state_s5_re[:, 0], state_s5_im[:, 0])
    streams = [(x_prompt, _mixers(x_prompt, None, mod, 0, h0_ctx, s5_ops, p)),
               (x_sample, _mixers(x_sample, _grid_pos_embed(seq_lat), mod, 1, h0_lat, s5_ops, p))]
    n_tokens = streams[0][1][0].shape[0]
    assert all(m[0].shape[0] == n_tokens for _, m in streams), "the joint plan slices equal-sized streams"
    plan = _moe_plan(jnp.concatenate([m[2] for _, m in streams], axis=1),
                     jnp.concatenate([m[3] for _, m in streams], axis=1))
    outs = []
    for s, (x3, (xs, h2_rows, _, _, _, mod_row)) in enumerate(streams):
        out = _moe(h2_rows, _plan_of_stream(plan, s, n_tokens), p["w1"], p["w3"], p["w2"],
                   xs, mod, mod_row, p["final_g"])
        outs.append(out.reshape(x3.shape))
    fin = streams[0][1][4]
    return (outs[0], outs[1], unpack_state(fin, 0).astype(x_prompt.dtype),
            unpack_state(fin, half).astype(x_prompt.dtype))
```

```python
import functools
import math

import jax
import jax.numpy as jnp
from jax import lax
from jax.experimental import pallas as pl
from jax.experimental.pallas import tpu as pltpu

D_MODEL = 1024
GRID_W = 64
D_S5 = 768
S5_GROUP = 16
S5_GROUPS = 48
S5_STATE = 64
D_FNET = 256
FNET_GROUP = 64
N_EXPERTS = 64
TOP_K = 6
N_EXPERT_GROUPS = 8
EXPERTS_PER_GROUP = N_EXPERTS // N_EXPERT_GROUPS
TOPK_GROUPS = 4
D_EXPERT = 256
ROUTED_SCALE = 2.5
N_MOD = 6
EPS = 1e-6

S5_CHUNK = 16
S5_ROW = S5_CHUNK * S5_GROUP
MOD_ROWS = 8
ROUTE_ROWS = 8
LANES = 128
ROW_SUB = D_MODEL // LANES
VMEM_LIMIT = 56 * 1024 * 1024

BF16 = jnp.bfloat16
F32 = jnp.float32


def _params(sem, vmem=VMEM_LIMIT):
    return pltpu.CompilerParams(dimension_semantics=sem, vmem_limit_bytes=vmem)


def _rms(x, g):
    return x * lax.rsqrt(jnp.mean(x * x, axis=-1, keepdims=True) + EPS) * g


def _adaln_kernel(c_ref, w_ref, b_ref, o_ref):
    c = c_ref[...]
    s = c * jax.nn.sigmoid(c)
    w = w_ref[...]
    s_hi = s.astype(BF16)
    s_lo = (s - s_hi.astype(F32)).astype(BF16)
    w_hi = w.astype(BF16)
    w_lo = (w - w_hi.astype(F32)).astype(BF16)
    o_ref[...] = (jnp.dot(s_hi, w_hi, preferred_element_type=F32) + jnp.dot(s_lo, w_hi, preferred_element_type=F32)
                  + jnp.dot(s_hi, w_lo, preferred_element_type=F32) + b_ref[...])


def _adaln(cond, w_ada, b_ada):
    n_out = N_MOD * D_MODEL
    return pl.pallas_call(
        _adaln_kernel,
        out_shape=jax.ShapeDtypeStruct((MOD_ROWS, n_out), F32),
        grid=(N_MOD,),
        in_specs=[pl.BlockSpec((MOD_ROWS, D_MODEL), lambda i: (0, 0)),
                  pl.BlockSpec((D_MODEL, D_MODEL), lambda i: (0, i)),
                  pl.BlockSpec((1, D_MODEL), lambda i: (0, i))],
        out_specs=pl.BlockSpec((MOD_ROWS, D_MODEL), lambda i: (0, i)),
        compiler_params=_params(("arbitrary",)),
        name="adaln",
    )(cond, w_ada, b_ada.reshape(1, n_out))


def _inproj_kernel(has_pos, *refs):
    if has_pos:
        x_ref, pos_ref, mod_ref, g_ref, w_ref, us_ref, uf_ref, gt_ref = refs
        x = x_ref[...] + pos_ref[...]
    else:
        x_ref, mod_ref, g_ref, w_ref, us_ref, uf_ref, gt_ref = refs
        x = x_ref[...]
    m = mod_ref[0]
    h = _rms(x, g_ref[...]) * (1.0 + m[1:2, :]) + m[0:1, :]
    p = jnp.dot(h.astype(BF16), w_ref[...], preferred_element_type=F32)
    us_ref[...] = p[:, :D_S5]
    uf_ref[...] = p[:, D_S5:D_MODEL].astype(BF16)
    gt_ref[...] = jax.nn.sigmoid(p[:, D_MODEL:]).astype(BF16)


def _inproj(x, pos, mod, mod_row, norm_g, w_in_bf, tm):
    n = x.shape[0]
    has_pos = pos is not None
    row = lambda i: (i, 0)
    in_specs = [pl.BlockSpec((tm, D_MODEL), row)]
    args = [x]
    if has_pos:
        nper = pos.shape[0] // tm
        in_specs.append(pl.BlockSpec((tm, D_MODEL), lambda i: (i % nper, 0)))
        args.append(pos)
    in_specs += [pl.BlockSpec((1, N_MOD, D_MODEL), lambda i: (mod_row(i, tm), 0, 0)),
                 pl.BlockSpec((1, D_MODEL), lambda i: (0, 0)),
                 pl.BlockSpec(w_in_bf.shape, lambda i: (0, 0))]
    args += [mod, norm_g, w_in_bf]
    return pl.pallas_call(
        functools.partial(_inproj_kernel, has_pos),
        out_shape=(jax.ShapeDtypeStruct((n, D_S5), F32),
                   jax.ShapeDtypeStruct((n, D_FNET), BF16),
                   jax.ShapeDtypeStruct((n, 2 * D_MODEL), BF16)),
        grid=(n // tm,),
        in_specs=in_specs,
        out_specs=(pl.BlockSpec((tm, D_S5), row), pl.BlockSpec((tm, D_FNET), row),
                   pl.BlockSpec((tm, 2 * D_MODEL), row)),
        compiler_params=_params(("arbitrary",)),
        name="inproj",
    )(*args)


def _shift_lanes(x, k):
    if k == 0:
        return x
    z = jnp.zeros((x.shape[0], abs(k)), x.dtype)
    if k > 0:
        return jnp.concatenate([z, x[:, :x.shape[1] - k]], axis=1)
    return jnp.concatenate([x[:, -k:], z], axis=1)


def _s5ops_kernel(lam_re_ref, lam_im_ref, dt_ref, btr_ref, bti_ref, cr_ref, ci_ref, d_ref,
                  m_ref, wi_ref, wot_ref, coef_ref):
    hi = lax.Precision.HIGHEST
    lr = jnp.minimum(lam_re_ref[0], -1e-4)
    li = lam_im_ref[0]
    dt = jnp.exp(dt_ref[0])
    mag = jnp.exp(lr * dt)
    ar = mag * jnp.cos(li * dt)
    ai = mag * jnp.sin(li * dt)
    den = lr * lr + li * li
    nr = ar - 1.0
    qr = (nr * lr + ai * li) / den
    qi = (ai * lr - nr * li) / den
    npow = S5_CHUNK + 1
    nrow = (lax.broadcasted_iota(jnp.int32, (2 * npow, 1), 0) // 2).astype(F32)
    stack = lambda v: jnp.concatenate([v] * npow, axis=0)
    pmag = jnp.exp(nrow * stack(lr * dt))
    pang = nrow * stack(li * dt)
    pr_all = pmag * jnp.cos(pang)
    pi_all = pmag * jnp.sin(pang)
    pr = [pr_all[2 * n:2 * n + 2] for n in range(npow)]
    pi = [pi_all[2 * n:2 * n + 2] for n in range(npow)]
    bbr, bbi, car, cai = [], [], [], []
    for d in range(2):
        btr = btr_ref[0, d]
        bti = bti_ref[0, d]
        bbr.append(qr[d:d + 1] * btr - qi[d:d + 1] * bti)
        bbi.append(qr[d:d + 1] * bti + qi[d:d + 1] * btr)
        cr = cr_ref[0, d]
        ci = ci_ref[0, d]
        car.append([cr * pr[n][d:d + 1] - ci * pi[n][d:d + 1] for n in range(S5_CHUNK + 1)])
        cai.append([cr * pi[n][d:d + 1] + ci * pr[n][d:d + 1] for n in range(S5_CHUNK + 1)])

    def lag_kernels(d, order):
        a = jnp.concatenate([car[d][n] for n in order], axis=0)
        b = jnp.concatenate([cai[d][n] for n in order], axis=0)
        dn = (((1,), (1,)), ((), ()))
        return (lax.dot_general(bbr[d], a, dn, precision=hi, preferred_element_type=F32)
                - lax.dot_general(bbi[d], b, dn, precision=hi, preferred_element_type=F32))

    ktf = lag_kernels(0, range(S5_CHUNK))
    ktb = lag_kernels(1, range(S5_CHUNK - 1, -1, -1))
    row = lax.broadcasted_iota(jnp.int32, (S5_GROUP, S5_ROW), 0)
    lane = lax.broadcasted_iota(jnp.int32, (S5_GROUP, S5_ROW), 1)
    dcol = d_ref[0]
    for j in range(S5_CHUNK):
        rows = slice(j * S5_GROUP, (j + 1) * S5_GROUP)
        blk = _shift_lanes(ktf, S5_GROUP * j) + _shift_lanes(ktb, -S5_GROUP * (S5_CHUNK - 1 - j))
        blk = blk + jnp.where(lane == S5_GROUP * j + row, dcol, 0.0)
        m_ref[0, rows, :] = blk.astype(BF16)
        nf = S5_CHUNK - 1 - j
        wi = jnp.concatenate([pr[nf][0:1] * bbr[0] - pi[nf][0:1] * bbi[0],
                              pr[j][1:2] * bbr[1] - pi[j][1:2] * bbi[1],
                              pr[nf][0:1] * bbi[0] + pi[nf][0:1] * bbr[0],
                              pr[j][1:2] * bbi[1] + pi[j][1:2] * bbr[1]], axis=1)
        wi_ref[0, rows, :] = wi.astype(BF16)
        wot = jnp.concatenate([car[0][j + 1], car[1][S5_CHUNK - j],
                               -cai[0][j + 1], -cai[1][S5_CHUNK - j]], axis=1)
        wot_ref[0, rows, :] = wot.astype(BF16)
    coef_ref[0, 0:1, :] = jnp.concatenate([pr[S5_CHUNK][0:1], pr[S5_CHUNK][1:2]], axis=1)
    coef_ref[0, 1:2, :] = jnp.concatenate([pi[S5_CHUNK][0:1], pi[S5_CHUNK][1:2]], axis=1)


def _s5ops(lam_re, lam_im, log_dt, b_re, b_im, c_re, c_im, d_skip):
    g3 = lambda g: (g, 0, 0)
    g4 = lambda g: (g, 0, 0, 0)
    sw = lambda a: jnp.swapaxes(a.astype(F32), 0, 1)
    dt = jnp.broadcast_to(sw(log_dt)[..., None], (S5_GROUPS, 2, S5_STATE))
    args = (sw(lam_re), sw(lam_im), dt, sw(jnp.swapaxes(b_re, 2, 3)), sw(jnp.swapaxes(b_im, 2, 3)),
            sw(c_re), sw(c_im), d_skip.astype(F32).reshape(S5_GROUPS, S5_GROUP, 1))
    vec = pl.BlockSpec((1, 2, S5_STATE), g3)
    mat = pl.BlockSpec((1, 2, S5_GROUP, S5_STATE), g4)
    op = pl.BlockSpec((1, S5_ROW, S5_ROW), g3)
    return pl.pallas_call(
        _s5ops_kernel,
        out_shape=(jax.ShapeDtypeStruct((S5_GROUPS, S5_ROW, S5_ROW), BF16),) * 3
        + (jax.ShapeDtypeStruct((S5_GROUPS, 2, 2 * S5_STATE), F32),),
        grid=(S5_GROUPS,),
        in_specs=[vec, vec, vec, mat, mat, mat, mat, pl.BlockSpec((1, S5_GROUP, 1), g3)],
        out_specs=(op, op, op, pl.BlockSpec((1, 2, 2 * S5_STATE), g3)),
        compiler_params=_params(("arbitrary",)),
        name="s5ops",
    )(*args)


S5_BLOCK_GROUPS = LANES // S5_GROUP


S5_HALF_T = LANES // S5_GROUP


def _s5_perm():
    a = jnp.arange(S5_HALF_T * LANES, dtype=jnp.int32)
    dst = ((a % LANES) // S5_GROUP) * LANES + (a // LANES) * S5_GROUP + a % S5_GROUP
    return (dst[:, None] == a[None, :]).astype(BF16)


def _s5_kernel(nb, nk, u_ref, perm_ref, m_ref, wi_ref, wot_ref, coef_ref, h0_ref, y_ref, fin_ref,
               sr_ref, si_ref, efr_ref, ebr_ref, efi_ref, ebi_ref, uall_ref, yall_ref):
    gl = pl.program_id(1)
    rows = nb * nk
    half = 2 * S5_STATE

    @pl.when(gl == 0)
    def _():
        for hh in range(S5_CHUNK // S5_HALF_T):
            xcat = jnp.concatenate([u_ref[pl.ds(hh * S5_HALF_T + tt, rows, stride=S5_CHUNK), :].astype(BF16)
                                    for tt in range(S5_HALF_T)], axis=1)
            uh = jnp.dot(xcat, perm_ref[...], preferred_element_type=F32).astype(BF16)
            for g in range(S5_BLOCK_GROUPS):
                uall_ref[g, :, hh * LANES:(hh + 1) * LANES] = uh[:, g * LANES:(g + 1) * LANES]

    u = uall_ref[gl]
    s = jnp.dot(u, wi_ref[0], preferred_element_type=F32)
    sr_ref[...] = s[:, :half]
    si_ref[...] = s[:, half:]
    c_r = coef_ref[0, 0:1, :]
    c_i = coef_ref[0, 1:2, :]
    e_r = h0_ref[0, :, :half]
    e_i = h0_ref[0, :, half:]
    is_fwd = lax.broadcasted_iota(jnp.int32, (nb, half), 1) < S5_STATE
    for j in range(nk):
        rf = pl.ds(j, nb, stride=nk)
        rb = pl.ds(nk - 1 - j, nb, stride=nk)
        efr_ref[rf, :] = e_r
        ebr_ref[rb, :] = e_r
        efi_ref[rf, :] = e_i
        ebi_ref[rb, :] = e_i
        s_r = jnp.where(is_fwd, sr_ref[rf, :], sr_ref[rb, :])
        s_i = jnp.where(is_fwd, si_ref[rf, :], si_ref[rb, :])
        e_r, e_i = c_r * e_r - c_i * e_i + s_r, c_r * e_i + c_i * e_r + s_i
    fin_ref[0, :, :half] = e_r
    fin_ref[0, :, half:] = e_i
    y = jnp.dot(u, m_ref[0], preferred_element_type=F32)
    fwd_rows = lax.broadcasted_iota(jnp.int32, (rows, half), 1) < S5_STATE
    e = jnp.concatenate([jnp.where(fwd_rows, efr_ref[...], ebr_ref[...]),
                         jnp.where(fwd_rows, efi_ref[...], ebi_ref[...])], axis=1).astype(BF16)
    y = y + lax.dot_general(e, wot_ref[0], (((1,), (1,)), ((), ())), preferred_element_type=F32)
    yall_ref[gl] = y.astype(BF16)

    @pl.when(gl == S5_BLOCK_GROUPS - 1)
    def _():
        for hh in range(S5_CHUNK // S5_HALF_T):
            ycat = jnp.concatenate([yall_ref[g, :, hh * LANES:(hh + 1) * LANES]
                                    for g in range(S5_BLOCK_GROUPS)], axis=1)
            out = lax.dot_general(ycat, perm_ref[...], (((1,), (1,)), ((), ())), preferred_element_type=F32)
            for tt in range(S5_HALF_T):
                y_ref[pl.ds(hh * S5_HALF_T + tt, rows, stride=S5_CHUNK), :] = out[:, tt * LANES:(tt + 1) * LANES]


def _s5(u, ops, h0, nb, nk):
    m, w_in, w_out, coef = ops
    n = u.shape[0]
    rows = nb * nk
    perm = _s5_perm()
    g3 = lambda b, g: (b * S5_BLOCK_GROUPS + g, 0, 0)
    blk = lambda b, g: (0, b)
    return pl.pallas_call(
        functools.partial(_s5_kernel, nb, nk),
        out_shape=(jax.ShapeDtypeStruct((n, D_S5), F32),
                   jax.ShapeDtypeStruct((S5_GROUPS, nb, 4 * S5_STATE), F32)),
        grid=(S5_GROUPS // S5_BLOCK_GROUPS, S5_BLOCK_GROUPS),
        in_specs=[pl.BlockSpec((n, LANES), blk),
                  pl.BlockSpec(perm.shape, lambda b, g: (0, 0)),
                  pl.BlockSpec((1, S5_ROW, S5_ROW), g3),
                  pl.BlockSpec((1, S5_ROW, 4 * S5_STATE), g3),
                  pl.BlockSpec((1, 4 * S5_STATE, S5_ROW), g3),
                  pl.BlockSpec((1, 2, 2 * S5_STATE), g3),
                  pl.BlockSpec((1, nb, 4 * S5_STATE), g3)],
        out_specs=(pl.BlockSpec((n, LANES), blk),
                   pl.BlockSpec((1, nb, 4 * S5_STATE), g3)),
        scratch_shapes=[pltpu.VMEM((rows, 2 * S5_STATE), F32)] * 6 + [
            pltpu.VMEM((S5_BLOCK_GROUPS, rows, S5_ROW), BF16),
            pltpu.VMEM((S5_BLOCK_GROUPS, rows, S5_ROW), BF16)],
        compiler_params=_params(("arbitrary", "arbitrary")),
        name="s5",
    )(u, perm, m, w_in, w_out, coef, h0)


def _dft_tables(seq):
    k = jnp.arange(seq, dtype=jnp.int32)
    na = seq // FNET_GROUP
    ang_a = (2.0 * math.pi / na) * ((jnp.arange(na, dtype=jnp.int32)[:, None] * k[None, :]) % na).astype(F32)
    ang_b = (2.0 * math.pi / seq) * ((jnp.arange(FNET_GROUP, dtype=jnp.int32)[:, None] * k[None, :]) % seq).astype(F32)
    ca, sa = jnp.cos(ang_a)[:, None, :], jnp.sin(ang_a)[:, None, :]
    cb, sb = jnp.cos(ang_b)[None, :, :], jnp.sin(ang_b)[None, :, :]
    cos_jk = (ca * cb - sa * sb).reshape(seq, seq)
    sin_jk = (sa * cb + ca * sb).reshape(seq, seq)
    cs = jnp.concatenate([cos_jk, -sin_jk], axis=1).astype(BF16)
    c = jnp.arange(D_FNET, dtype=jnp.int32)
    same = (c[:, None] // FNET_GROUP) == (c[None, :] // FNET_GROUP)
    angc = (2.0 * math.pi / FNET_GROUP) * (((c[:, None] % FNET_GROUP) * (c[None, :] % FNET_GROUP))
                                           % FNET_GROUP).astype(F32)
    scale = 1.0 / math.sqrt(seq * FNET_GROUP)
    bdc = jnp.where(same, jnp.cos(angc) * scale, 0.0).astype(BF16)
    bds = jnp.where(same, jnp.sin(angc) * scale, 0.0).astype(BF16)
    return cs, bdc, bds


def _fourier_kernel(seq, z_ref, cs_ref, bdc_ref, bds_ref, o_ref, zz_ref):
    @pl.when(pl.program_id(1) == 0)
    def _():
        z = z_ref[...]
        zz_ref[0:seq, :] = jnp.dot(z, bdc_ref[...], preferred_element_type=F32).astype(BF16)
        zz_ref[seq:, :] = jnp.dot(z, bds_ref[...], preferred_element_type=F32).astype(BF16)

    o_ref[...] = jnp.dot(cs_ref[...], zz_ref[...], preferred_element_type=F32).astype(BF16)


def _fourier(z, nb, seq, tl):
    cs, bdc, bds = _dft_tables(seq)
    nt = seq // tl
    return pl.pallas_call(
        functools.partial(_fourier_kernel, seq),
        out_shape=jax.ShapeDtypeStruct(z.shape, BF16),
        grid=(nb, nt),
        in_specs=[pl.BlockSpec((seq, D_FNET), lambda b, i: (b, 0)),
                  pl.BlockSpec((tl, 2 * seq), lambda b, i: (i, 0)),
                  pl.BlockSpec((D_FNET, D_FNET), lambda b, i: (0, 0)),
                  pl.BlockSpec((D_FNET, D_FNET), lambda b, i: (0, 0))],
        out_specs=pl.BlockSpec((tl, D_FNET), lambda b, i: (b * nt + i, 0)),
        scratch_shapes=[pltpu.VMEM((2 * seq, D_FNET), BF16)],
        compiler_params=_params(("arbitrary", "arbitrary")),
        name="fourier",
    )(z, cs, bdc, bds)


def _first_argmax_mask(v, iota, size):
    m = jnp.max(v, axis=0, keepdims=True)
    first = jnp.min(jnp.where(v == m, iota, size), axis=0, keepdims=True)
    return iota == first


def _route(logits_t, bias_col):
    tm = logits_t.shape[1]
    neg = -jnp.inf
    s = jax.nn.sigmoid(logits_t)
    biased = s + bias_col
    io8 = lax.broadcasted_iota(jnp.int32, (EXPERTS_PER_GROUP, tm), 0)
    gs_rows = []
    for g in range(N_EXPERT_GROUPS):
        blk = biased[g * EXPERTS_PER_GROUP:(g + 1) * EXPERTS_PER_GROUP, :]
        m1 = jnp.max(blk, axis=0, keepdims=True)
        rest = jnp.where(_first_argmax_mask(blk, io8, EXPERTS_PER_GROUP), neg, blk)
        gs_rows.append(m1 + jnp.max(rest, axis=0, keepdims=True))
    gs = jnp.concatenate(gs_rows, axis=0)
    iog = lax.broadcasted_iota(jnp.int32, (N_EXPERT_GROUPS, tm), 0)
    gsel = jnp.zeros((N_EXPERT_GROUPS, tm), F32)
    for _ in range(TOPK_GROUPS):
        sel = _first_argmax_mask(gs, iog, N_EXPERT_GROUPS)
        gsel = jnp.where(sel, 1.0, gsel)
        gs = jnp.where(sel, neg, gs)
    emask = jnp.concatenate(
        [jnp.broadcast_to(gsel[g:g + 1, :], (EXPERTS_PER_GROUP, tm)) for g in range(N_EXPERT_GROUPS)], axis=0)
    v = jnp.where(emask > 0.0, biased, neg)
    ioe = lax.broadcasted_iota(jnp.int32, (N_EXPERTS, tm), 0)
    idx_rows, s_rows = [], []
    for _ in range(TOP_K):
        sel = _first_argmax_mask(v, ioe, N_EXPERTS)
        idx_rows.append(jnp.sum(jnp.where(sel, ioe, 0), axis=0, keepdims=True))
        s_rows.append(jnp.sum(jnp.where(sel, s, 0.0), axis=0, keepdims=True))
        v = jnp.where(sel, neg, v)
    denom = s_rows[0]
    for r in s_rows[1:]:
        denom = denom + r
    pad = ROUTE_ROWS - TOP_K
    idx = jnp.concatenate(idx_rows + [jnp.zeros((pad, tm), jnp.int32)], axis=0)
    w = jnp.concatenate([r / denom * ROUTED_SCALE for r in s_rows] + [jnp.zeros((pad, tm), F32)], axis=0)
    return idx, w


def _merge_kernel(has_pos, *refs):
    if has_pos:
        (ys_ref, yf_ref, gt_ref, x_ref, pos_ref, mod_ref, n2_ref, wglu_ref, wps_ref, wpf_ref, wout_ref,
         wrt_ref, rb_ref, ws1_ref, ws3_ref, ws2_ref, xs_ref, h2_ref, ridx_ref, rw_ref) = refs
        x = x_ref[...] + pos_ref[...]
    else:
        (ys_ref, yf_ref, gt_ref, x_ref, mod_ref, n2_ref, wglu_ref, wps_ref, wpf_ref, wout_ref,
         wrt_ref, rb_ref, ws1_ref, ws3_ref, ws2_ref, xs_ref, h2_ref, ridx_ref, rw_ref) = refs
        x = x_ref[...]
    m = mod_ref[0]
    g = jax.nn.gelu(ys_ref[...].astype(F32))
    a = g * jax.nn.sigmoid(jnp.dot(g.astype(BF16), wglu_ref[...], preferred_element_type=F32))
    pa = jnp.dot(a.astype(BF16), wps_ref[...], preferred_element_type=F32)
    pb = jnp.dot(yf_ref[...], wpf_ref[...], preferred_element_type=F32)
    gt = gt_ref[...].astype(F32)
    merged = gt[:, :D_MODEL] * pa + gt[:, D_MODEL:] * pb
    x1 = x + m[2:3, :] * jnp.dot(merged.astype(BF16), wout_ref[...], preferred_element_type=F32)
    h2 = _rms(x1, n2_ref[...]) * (1.0 + m[4:5, :]) + m[3:4, :]
    hb = h2.astype(BF16)
    for j in range(ROW_SUB):
        h2_ref[pl.ds(j, h2.shape[0], stride=ROW_SUB), :] = h2[:, j * LANES:(j + 1) * LANES]
    wr = wrt_ref[...]
    wr_hi = wr.astype(BF16)
    wr_lo = (wr - wr_hi.astype(F32)).astype(BF16)
    h_lo = (h2 - hb.astype(F32)).astype(BF16)
    dn = (((1,), (1,)), ((), ()))
    logits_t = (lax.dot_general(wr_hi, hb, dn, preferred_element_type=F32)
                + lax.dot_general(wr_hi, h_lo, dn, preferred_element_type=F32)
                + lax.dot_general(wr_lo, hb, dn, preferred_element_type=F32))
    ridx_ref[...], rw_ref[...] = _route(logits_t, rb_ref[...])
    s1 = jnp.dot(hb, ws1_ref[...], preferred_element_type=F32)
    s3 = jnp.dot(hb, ws3_ref[...], preferred_element_type=F32)
    shared = jnp.dot((s1 * jax.nn.sigmoid(s1) * s3).astype(BF16), ws2_ref[...], preferred_element_type=F32)
    xs_ref[...] = x1 + m[5:6, :] * shared


def _merge(ys, yf, gt, x, pos, mod, mod_row, n2, weights, tm):
    n = x.shape[0]
    has_pos = pos is not None
    row = lambda i: (i, 0)
    const = lambda a: pl.BlockSpec(a.shape, lambda i: (0,) * a.ndim)
    in_specs = [pl.BlockSpec((tm, D_S5), row), pl.BlockSpec((tm, D_FNET), row),
                pl.BlockSpec((tm, 2 * D_MODEL), row), pl.BlockSpec((tm, D_MODEL), row)]
    args = [ys, yf, gt, x]
    if has_pos:
        nper = pos.shape[0] // tm
        in_specs.append(pl.BlockSpec((tm, D_MODEL), lambda i: (i % nper, 0)))
        args.append(pos)
    in_specs += [pl.BlockSpec((1, N_MOD, D_MODEL), lambda i: (mod_row(i, tm), 0, 0)), const(n2)]
    args += [mod, n2]
    in_specs += [const(w) for w in weights]
    args += list(weights)
    return pl.pallas_call(
        functools.partial(_merge_kernel, has_pos),
        out_shape=(jax.ShapeDtypeStruct((n, D_MODEL), F32),
                   jax.ShapeDtypeStruct((n * ROW_SUB, LANES), F32),
                   jax.ShapeDtypeStruct((ROUTE_ROWS, n), jnp.int32),
                   jax.ShapeDtypeStruct((ROUTE_ROWS, n), F32)),
        grid=(n // tm,),
        in_specs=in_specs,
        out_specs=(pl.BlockSpec((tm, D_MODEL), row), pl.BlockSpec((tm * ROW_SUB, LANES), row),
                   pl.BlockSpec((ROUTE_ROWS, tm), lambda i: (0, i)),
                   pl.BlockSpec((ROUTE_ROWS, tm), lambda i: (0, i))),
        compiler_params=_params(("arbitrary",)),
        name="merge",
    )(*args)


MOE_SUB = 4096
MOE_TM = 128
MOE_TMAX = MOE_SUB * TOP_K // MOE_TM + N_EXPERTS
MOE_SORT_SUBS = 2
MOE_PAD = 2
MOE_TS = MOE_TMAX + 2 * MOE_PAD
MOE_DUMMY = 256
MOE_RMW = 8
MOE_FT = 512
ROW_TILE = MOE_TM * ROW_SUB


def _moe_plan(ridx, rw):
    n = ridx.shape[1]
    nsub = n // MOE_SUB
    npair = n * TOP_K
    t = jnp.arange(n, dtype=jnp.int32)
    key = (((t // MOE_SUB) * N_EXPERTS)[None] + ridx[:TOP_K]) * MOE_SUB + (t % MOE_SUB)[None]
    group = MOE_SORT_SUBS * MOE_SUB
    parts = [lax.sort((key[:, g:g + group].reshape(-1), rw[:TOP_K, g:g + group].reshape(-1)), num_keys=1)
             for g in range(0, n, group)]
    skey = jnp.concatenate([pk for pk, _ in parts])
    sw = jnp.concatenate([pw for _, pw in parts])
    stok = jnp.concatenate([(skey % MOE_SUB) * ROW_SUB, jnp.zeros((MOE_TM,), jnp.int32)])
    sw_rows = jnp.concatenate([sw, jnp.zeros((MOE_TM,), F32)]).reshape((npair + MOE_TM) // LANES, 1, LANES)
    hits = ridx[:TOP_K].reshape(TOP_K, nsub, 1, MOE_SUB) == jnp.arange(N_EXPERTS, dtype=jnp.int32)[None, None, :, None]
    cnt = jnp.sum(hits.astype(jnp.int32), axis=(0, 3))
    poff = (jnp.cumsum(cnt.reshape(-1)) - cnt.reshape(-1)).reshape(nsub, N_EXPERTS)
    ntile = (cnt + MOE_TM - 1) // MOE_TM
    tcum = jnp.cumsum(ntile, axis=1)
    toff = tcum - ntile
    tstart = jnp.concatenate([toff, tcum[:, -1:]], axis=1).reshape(-1).astype(jnp.int32)
    j = jnp.arange(MOE_TS, dtype=jnp.int32) - MOE_PAD
    valid = (j[None] >= 0) & (j[None] < tcum[:, -1:])
    te = jnp.minimum(jnp.sum(j[None, :, None] >= tcum[:, None, :], axis=-1), N_EXPERTS - 1)
    pick = lambda a: jnp.take_along_axis(a, te, axis=1)
    first = (j[None] - pick(toff)) * MOE_TM
    p0 = jnp.where(valid, pick(poff) + first, 0).reshape(-1).astype(jnp.int32)
    nv = jnp.where(valid, jnp.minimum(pick(cnt) - first, MOE_TM), 0).reshape(-1).astype(jnp.int32)
    return tstart, p0, nv, stok, sw_rows


def _moe_kernel(ts_ref, p0_ref, nv_ref, tok_ref, sw_ref, src_ref, w1_ref, w3_ref, w2_ref, xs_ref, mod_ref,
                fg_ref, o_ref, y_ref, xt_ref, xb_ref, act_ref, ot_ref, w1b_ref, w3b_ref, w2b_ref, slot_ref):
    sub = pl.program_id(0)
    e = pl.program_id(1)
    base = sub * MOE_TS + MOE_PAD
    ec = jnp.minimum(e, N_EXPERTS - 1)
    first = ts_ref[sub * (N_EXPERTS + 1) + ec]
    last = jnp.where(e < N_EXPERTS, ts_ref[sub * (N_EXPERTS + 1) + ec + 1], first)

    def gather(p0):
        for mi in range(MOE_TM):
            tok = pl.multiple_of(tok_ref[p0 + mi], ROW_SUB)
            xt_ref[mi * ROW_SUB:(mi + 1) * ROW_SUB, :] = src_ref[pl.ds(tok, ROW_SUB), :]
        for j in range(ROW_SUB):
            xb_ref[:, j * LANES:(j + 1) * LANES] = xt_ref[pl.ds(j, MOE_TM, stride=ROW_SUB), :].astype(BF16)

    def scatter(p0, nv, masked):
        for u in range(0, MOE_TM, MOE_RMW):
            new = []
            for i in range(MOE_RMW):
                tok = tok_ref[p0 + u + i]
                if masked:
                    tok = jnp.where(u + i < nv, tok, MOE_SUB * ROW_SUB)
                tok = pl.multiple_of(tok, ROW_SUB)
                new.append((tok, y_ref[pl.ds(tok, ROW_SUB), :]
                            + ot_ref[(u + i) * ROW_SUB:(u + i + 1) * ROW_SUB, :]))
            for tok, v in new:
                y_ref[pl.ds(tok, ROW_SUB), :] = v

    @pl.when(e == 0)
    def _():
        y_ref[...] = jnp.zeros_like(y_ref)
        ot_ref[...] = jnp.zeros_like(ot_ref)
        act_ref[...] = jnp.zeros_like(act_ref)
        w2b_ref[...] = jnp.zeros_like(w2b_ref)
        slot_ref[0] = 0
        gather(p0_ref[base])

    @pl.when(last > first)
    def _():
        slot_ref[0] = 1 - slot_ref[0]
        w1b_ref[...] = w1_ref[0].astype(BF16)
        w3b_ref[...] = w3_ref[0].astype(BF16)
        w2b_ref[slot_ref[0]] = w2_ref[0].astype(BF16)

    slot = slot_ref[0]

    def down_proj(w2_slot):
        o = jnp.dot(act_ref[...], w2b_ref[w2_slot], preferred_element_type=F32)
        for j in range(ROW_SUB):
            ot_ref[pl.ds(j, MOE_TM, stride=ROW_SUB), :] = o[:, j * LANES:(j + 1) * LANES]

    def step(i, masked):
        cur = base + i
        scatter(p0_ref[cur - 2], nv_ref[cur - 2], masked)
        down_proj(jnp.where(i > first, slot, 1 - slot))
        p0 = p0_ref[cur]
        nv = nv_ref[cur]
        x = xb_ref[...]
        a = jnp.dot(x, w1b_ref[...], preferred_element_type=F32)
        b = jnp.dot(x, w3b_ref[...], preferred_element_type=F32)
        r0 = p0 // LANES
        c = p0 % LANES
        lane = lax.broadcasted_iota(jnp.int32, (1, LANES), 1)
        rows = lax.broadcasted_iota(jnp.int32, (LANES, LANES), 0)
        cols = lax.broadcasted_iota(jnp.int32, (LANES, LANES), 1)
        gparts = []
        for hh in range(MOE_TM // LANES):
            ga = pltpu.roll(sw_ref[r0 + hh], LANES - c, axis=1)
            gb = pltpu.roll(sw_ref[r0 + hh + 1], LANES - c, axis=1)
            g = jnp.where(lane + hh * LANES < nv, jnp.where(lane < LANES - c, ga, gb), 0.0)
            gparts.append(jnp.sum(jnp.where(rows == cols, jnp.broadcast_to(g, (LANES, LANES)), 0.0),
                                  axis=1, keepdims=True))
        gcol = jnp.concatenate(gparts, axis=0)
        act_ref[...] = (a * jax.nn.sigmoid(a) * b * gcol).astype(BF16)
        gather(p0_ref[cur + 1])

    def body(i, carry):
        step(i, True)
        return carry

    lax.fori_loop(first, last, body, 0)

    @pl.when(e == N_EXPERTS - 1)
    def _():
        scatter(p0_ref[base + last - 2], nv_ref[base + last - 2], True)
        down_proj(slot)
        scatter(p0_ref[base + last - 1], nv_ref[base + last - 1], True)

    @pl.when(e >= N_EXPERTS)
    def _():
        row0 = pl.multiple_of((e - N_EXPERTS) * (MOE_FT * ROW_SUB), ROW_SUB)
        y = jnp.concatenate([y_ref[pl.ds(row0 + j, MOE_FT, stride=ROW_SUB), :] for j in range(ROW_SUB)],
                            axis=1)
        x2 = xs_ref[...] + mod_ref[0][5:6, :] * y
        o_ref[...] = _rms(x2, fg_ref[...])


def _moe(h2_rows, plan, w1, w3, w2, xs, mod, mod_row, fg):
    tstart, p0, nv, stok, sw_rows = plan
    nsub = h2_rows.shape[0] // (MOE_SUB * ROW_SUB)
    per_sub = MOE_SUB // MOE_FT
    wmap = lambda s, e, ts, p0, nv: (jnp.minimum(e, N_EXPERTS - 1), 0, 0)
    sub2 = lambda s, e, ts, p0, nv: (s, 0)
    out_tile = lambda s, e: s * per_sub + jnp.clip(e - N_EXPERTS, 0, per_sub - 1)
    grid_spec = pltpu.PrefetchScalarGridSpec(
        num_scalar_prefetch=3,
        grid=(nsub, N_EXPERTS + per_sub),
        in_specs=[pl.BlockSpec(memory_space=pltpu.SMEM),
                  pl.BlockSpec(sw_rows.shape, lambda s, e, ts, p0, nv: (0, 0, 0)),
                  pl.BlockSpec((MOE_SUB * ROW_SUB, LANES), sub2, pipeline_mode=pl.Buffered(1)),
                  pl.BlockSpec((1, D_MODEL, D_EXPERT), wmap),
                  pl.BlockSpec((1, D_MODEL, D_EXPERT), wmap),
                  pl.BlockSpec((1, D_EXPERT, D_MODEL), wmap),
                  pl.BlockSpec((MOE_FT, D_MODEL), lambda s, e, ts, p0, nv: (out_tile(s, e), 0)),
                  pl.BlockSpec((1, N_MOD, D_MODEL),
                               lambda s, e, ts, p0, nv: (mod_row(out_tile(s, e), MOE_FT), 0, 0)),
                  pl.BlockSpec((1, D_MODEL), lambda s, e, ts, p0, nv: (0, 0))],
        out_specs=pl.BlockSpec((MOE_FT, D_MODEL), lambda s, e, ts, p0, nv: (out_tile(s, e), 0)),
        scratch_shapes=[pltpu.VMEM(((MOE_SUB + MOE_DUMMY) * ROW_SUB, LANES), F32),
                        pltpu.VMEM((ROW_TILE, LANES), F32), pltpu.VMEM((MOE_TM, D_MODEL), BF16),
                        pltpu.VMEM((MOE_TM, D_EXPERT), BF16), pltpu.VMEM((ROW_TILE, LANES), F32),
                        pltpu.VMEM((D_MODEL, D_EXPERT), BF16), pltpu.VMEM((D_MODEL, D_EXPERT), BF16),
                        pltpu.VMEM((2, D_EXPERT, D_MODEL), BF16), pltpu.SMEM((1,), jnp.int32)])
    return pl.pallas_call(
        _moe_kernel,
        grid_spec=grid_spec,
        out_shape=jax.ShapeDtypeStruct(xs.shape, F32),
        compiler_params=_params(("arbitrary", "arbitrary")),
        name="moe",
    )(tstart, p0, nv, stok, sw_rows, h2_rows, w1, w3, w2, xs, mod, fg)


def _grid_pos_embed(n_tokens):
    rows = n_tokens // GRID_W
    quarter = D_MODEL // 4
    omega = 1.0 / (10000.0 ** (jnp.arange(quarter, dtype=F32) / quarter))

    def emb(count):
        a = jnp.arange(count, dtype=F32)[:, None] * omega
        return jnp.concatenate([jnp.sin(a), jnp.cos(a)], axis=-1)

    by_row = jnp.repeat(emb(rows), GRID_W, axis=0)
    by_col = jnp.tile(emb(GRID_W), (rows, 1))
    return jnp.concatenate([by_row, by_col], axis=-1)


def _mixers(x3, pos, mod, first_row, h0, s5_ops, p):
    nb, seq, _ = x3.shape
    n = nb * seq
    nk = seq // S5_CHUNK
    x = x3.reshape(n, D_MODEL)
    per_seq_mod = first_row > 0

    def mod_row(i, tm):
        return first_row + (i * tm) // seq if per_seq_mod else 0

    us, uf, gt = _inproj(x, pos, mod, mod_row, p["norm1_g"], p["w_in"], 1024)
    ys, fin = _s5(us, s5_ops, h0, nb, nk)
    yf = _fourier(uf, nb, seq, min(seq, 512))
    xs, h2_rows, ridx, rw = _merge(ys, yf, gt, x, pos, mod, mod_row, p["norm2_g"], p["merge_w"], 512)
    return xs, h2_rows, ridx, rw, fin, mod_row


def _plan_of_stream(plan, s, n_tokens):
    tstart, p0, nv, stok, sw_rows = plan
    nsub = n_tokens // MOE_SUB
    npair = n_tokens * TOP_K
    off = s * npair
    tiles = slice(s * nsub * MOE_TS, (s + 1) * nsub * MOE_TS)
    nv_s = nv[tiles]
    p0_s = jnp.where(nv_s > 0, p0[tiles] - off, 0)
    return (tstart[s * nsub * (N_EXPERTS + 1):(s + 1) * nsub * (N_EXPERTS + 1)], p0_s, nv_s,
            stok[off:off + npair + MOE_TM], sw_rows[off // LANES:(off + npair + MOE_TM) // LANES])


def kernel(x_prompt, x_sample, state_s5_re, state_s5_im, c, c_ctx, w_ada, b_ada, norm1_g, norm2_g, w_in,
           lam_re, lam_im, log_dt, b_re, b_im, c_re, c_im, d_skip, w_glu, w_proj_s5, w_proj_f, w_out,
           w_router, router_bias, w1, w3, w2, ws1, ws3, ws2, final_norm_g):
    nb_ctx = x_prompt.shape[0]
    nb_lat, seq_lat, _ = x_sample.shape
    half = 2 * S5_STATE

    cond = jnp.concatenate([c_ctx[None], c, jnp.zeros((MOD_ROWS - 1 - nb_lat, D_MODEL), F32)], axis=0)
    mod = _adaln(cond, w_ada[0], b_ada[0]).reshape(MOD_ROWS, N_MOD, D_MODEL)

    s5_ops = _s5ops(lam_re[0], lam_im[0], log_dt[0], b_re[0], b_im[0], c_re[0], c_im[0], d_skip[0])
    p = dict(
        norm1_g=norm1_g[0][None], norm2_g=norm2_g[0][None], final_g=final_norm_g[None],
        w_in=w_in[0].astype(BF16), w1=w1[0], w3=w3[0], w2=w2[0],
        merge_w=(w_glu[0].astype(BF16), w_proj_s5[0].astype(BF16), w_proj_f[0].astype(BF16),
                 w_out[0].astype(BF16), w_router[0].T, router_bias[0][:, None],
                 ws1[0].astype(BF16), ws3[0].astype(BF16), ws2[0].astype(BF16)))

    def pack_state(sr, si):
        f = lambda a: a.astype(F32).transpose(2, 0, 1, 3).reshape(S5_GROUPS, a.shape[0], half)
        return jnp.concatenate([f(sr), f(si)], axis=-1)

    def unpack_state(fin, lo):
        nb = fin.shape[1]
        return fin[..., lo:lo + half].reshape(S5_GROUPS, nb, 2, S5_STATE).transpose(1, 2, 0, 3)[:, None]

    h0_ctx = jnp.zeros((S5_GROUPS, nb_ctx, 2 * half), F32)
    h0_lat = pack_state(state_s5_re[:, 0], state_s5_im[:, 0])
    streams = [(x_prompt, _mixers(x_prompt, None, mod, 0, h0_ctx, s5_ops, p)),
               (x_sample, _mixers(x_sample, _grid_pos_embed(seq_lat), mod, 1, h0_lat, s5_ops, p))]
    n_tokens = streams[0][1][0].shape[0]
    assert all(m[0].shape[0] == n_tokens for _, m in streams), "the joint plan slices equal-sized streams"
    plan = _moe_plan(jnp.concatenate([m[2] for _, m in streams], axis=1),
                     jnp.concatenate([m[3] for _, m in streams], axis=1))
    outs = []
    for s, (x3, (xs, h2_rows, _, _, _, mod_row)) in enumerate(streams):
        out = _moe(h2_rows, _plan_of_stream(plan, s, n_tokens), p["w1"], p["w3"], p["w2"],
                   xs, mod, mod_row, p["final_g"])
        outs.append(out.reshape(x3.shape))
    fin = streams[0][1][4]
    return (outs[0], outs[1], unpack_state(fin, 0).astype(x_prompt.dtype),
            unpack_state(fin, half).astype(x_prompt.dtype))
```

```python
import functools
import math

import jax
import jax.numpy as jnp
from jax import lax
from jax.experimental import pallas as pl
from jax.experimental.pallas import tpu as pltpu

D_MODEL = 1024
GRID_W = 64
D_S5 = 768
S5_GROUP = 16
S5_GROUPS = 48
S5_STATE = 64
D_FNET = 256
FNET_GROUP = 64
N_EXPERTS = 64
TOP_K = 6
N_EXPERT_GROUPS = 8
EXPERTS_PER_GROUP = N_EXPERTS // N_EXPERT_GROUPS
TOPK_GROUPS = 4
D_EXPERT = 256
ROUTED_SCALE = 2.5
N_MOD = 6
EPS = 1e-6

S5_CHUNK = 16
S5_ROW = S5_CHUNK * S5_GROUP
MOD_ROWS = 8
ROUTE_ROWS = 8
LANES = 128
ROW_SUB = D_MODEL // LANES
VMEM_LIMIT = 56 * 1024 * 1024

BF16 = jnp.bfloat16
F32 = jnp.float32


def _params(sem, vmem=VMEM_LIMIT):
    return pltpu.CompilerParams(dimension_semantics=sem, vmem_limit_bytes=vmem)


def _rms(x, g):
    return x * lax.rsqrt(jnp.mean(x * x, axis=-1, keepdims=True) + EPS) * g


def _adaln_kernel(c_ref, w_ref, b_ref, o_ref):
    c = c_ref[...]
    s = c * jax.nn.sigmoid(c)
    w = w_ref[...]
    s_hi = s.astype(BF16)
    s_lo = (s - s_hi.astype(F32)).astype(BF16)
    w_hi = w.astype(BF16)
    w_lo = (w - w_hi.astype(F32)).astype(BF16)
    o_ref[...] = (jnp.dot(s_hi, w_hi, preferred_element_type=F32) + jnp.dot(s_lo, w_hi, preferred_element_type=F32)
                  + jnp.dot(s_hi, w_lo, preferred_element_type=F32) + b_ref[...])


def _adaln(cond, w_ada, b_ada):
    n_out = N_MOD * D_MODEL
    return pl.pallas_call(
        _adaln_kernel,
        out_shape=jax.ShapeDtypeStruct((MOD_ROWS, n_out), F32),
        grid=(N_MOD,),
        in_specs=[pl.BlockSpec((MOD_ROWS, D_MODEL), lambda i: (0, 0)),
                  pl.BlockSpec((D_MODEL, D_MODEL), lambda i: (0, i)),
                  pl.BlockSpec((1, D_MODEL), lambda i: (0, i))],
        out_specs=pl.BlockSpec((MOD_ROWS, D_MODEL), lambda i: (0, i)),
        compiler_params=_params(("arbitrary",)),
        name="adaln",
    )(cond, w_ada, b_ada.reshape(1, n_out))


def _inproj_kernel(has_pos, *refs):
    if has_pos:
        x_ref, pos_ref, mod_ref, g_ref, w_ref, us_ref, uf_ref, gt_ref = refs
        x = x_ref[...] + pos_ref[...]
    else:
        x_ref, mod_ref, g_ref, w_ref, us_ref, uf_ref, gt_ref = refs
        x = x_ref[...]
    m = mod_ref[0]
    h = _rms(x, g_ref[...]) * (1.0 + m[1:2, :]) + m[0:1, :]
    p = jnp.dot(h.astype(BF16), w_ref[...], preferred_element_type=F32)
    us_ref[...] = p[:, :D_S5]
    uf_ref[...] = p[:, D_S5:D_MODEL].astype(BF16)
    gt_ref[...] = jax.nn.sigmoid(p[:, D_MODEL:]).astype(BF16)


def _inproj(x, pos, mod, mod_row, norm_g, w_in_bf, tm):
    n = x.shape[0]
    has_pos = pos is not None
    row = lambda i: (i, 0)
    in_specs = [pl.BlockSpec((tm, D_MODEL), row)]
    args = [x]
    if has_pos:
        nper = pos.shape[0] // tm
        in_specs.append(pl.BlockSpec((tm, D_MODEL), lambda i: (i % nper, 0)))
        args.append(pos)
    in_specs += [pl.BlockSpec((1, N_MOD, D_MODEL), lambda i: (mod_row(i, tm), 0, 0)),
                 pl.BlockSpec((1, D_MODEL), lambda i: (0, 0)),
                 pl.BlockSpec(w_in_bf.shape, lambda i: (0, 0))]
    args += [mod, norm_g, w_in_bf]
    return pl.pallas_call(
        functools.partial(_inproj_kernel, has_pos),
        out_shape=(jax.ShapeDtypeStruct((n, D_S5), F32),
                   jax.ShapeDtypeStruct((n, D_FNET), BF16),
                   jax.ShapeDtypeStruct((n, 2 * D_MODEL), BF16)),
        grid=(n // tm,),
        in_specs=in_specs,
        out_specs=(pl.BlockSpec((tm, D_S5), row), pl.BlockSpec((tm, D_FNET), row),
                   pl.BlockSpec((tm, 2 * D_MODEL), row)),
        compiler_params=_params(("arbitrary",)),
        name="inproj",
    )(*args)


def _shift_lanes(x, k):
    if k == 0:
        return x
    z = jnp.zeros((x.shape[0], abs(k)), x.dtype)
    if k > 0:
        return jnp.concatenate([z, x[:, :x.shape[1] - k]], axis=1)
    return jnp.concatenate([x[:, -k:], z], axis=1)


def _s5ops_kernel(lam_re_ref, lam_im_ref, dt_ref, btr_ref, bti_ref, cr_ref, ci_ref, d_ref,
                  m_ref, wi_ref, wot_ref, coef_ref):
    hi = lax.Precision.HIGHEST
    lr = jnp.minimum(lam_re_ref[0], -1e-4)
    li = lam_im_ref[0]
    dt = jnp.exp(dt_ref[0])
    mag = jnp.exp(lr * dt)
    ar = mag * jnp.cos(li * dt)
    ai = mag * jnp.sin(li * dt)
    den = lr * lr + li * li
    nr = ar - 1.0
    qr = (nr * lr + ai * li) / den
    qi = (ai * lr - nr * li) / den
    npow = S5_CHUNK + 1
    nrow = (lax.broadcasted_iota(jnp.int32, (2 * npow, 1), 0) // 2).astype(F32)
    stack = lambda v: jnp.concatenate([v] * npow, axis=0)
    pmag = jnp.exp(nrow * stack(lr * dt))
    pang = nrow * stack(li * dt)
    pr_all = pmag * jnp.cos(pang)
    pi_all = pmag * jnp.sin(pang)
    pr = [pr_all[2 * n:2 * n + 2] for n in range(npow)]
    pi = [pi_all[2 * n:2 * n + 2] for n in range(npow)]
    bbr, bbi, car, cai = [], [], [], []
    for d in range(2):
        btr = btr_ref[0, d]
        bti = bti_ref[0, d]
        bbr.append(qr[d:d + 1] * btr - qi[d:d + 1] * bti)
        bbi.append(qr[d:d + 1] * bti + qi[d:d + 1] * btr)
        cr = cr_ref[0, d]
        ci = ci_ref[0, d]
        car.append([cr * pr[n][d:d + 1] - ci * pi[n][d:d + 1] for n in range(S5_CHUNK + 1)])
        cai.append([cr * pi[n][d:d + 1] + ci * pr[n][d:d + 1] for n in range(S5_CHUNK + 1)])

    def lag_kernels(d, order):
        a = jnp.concatenate([car[d][n] for n in order], axis=0)
        b = jnp.concatenate([cai[d][n] for n in order], axis=0)
        dn = (((1,), (1,)), ((), ()))
        return (lax.dot_general(bbr[d], a, dn, precision=hi, preferred_element_type=F32)
                - lax.dot_general(bbi[d], b, dn, precision=hi, preferred_element_type=F32))

    ktf = lag_kernels(0, range(S5_CHUNK))
    ktb = lag_kernels(1, range(S5_CHUNK - 1, -1, -1))
    row = lax.broadcasted_iota(jnp.int32, (S5_GROUP, S5_ROW), 0)
    lane = lax.broadcasted_iota(jnp.int32, (S5_GROUP, S5_ROW), 1)
    dcol = d_ref[0]
    for j in range(S5_CHUNK):
        rows = slice(j * S5_GROUP, (j + 1) * S5_GROUP)
        blk = _shift_lanes(ktf, S5_GROUP * j) + _shift_lanes(ktb, -S5_GROUP * (S5_CHUNK - 1 - j))
        blk = blk + jnp.where(lane == S5_GROUP * j + row, dcol, 0.0)
        m_ref[0, rows, :] = blk.astype(BF16)
        nf = S5_CHUNK - 1 - j
        wi = jnp.concatenate([pr[nf][0:1] * bbr[0] - pi[nf][0:1] * bbi[0],
                              pr[j][1:2] * bbr[1] - pi[j][1:2] * bbi[1],
                              pr[nf][0:1] * bbi[0] + pi[nf][0:1] * bbr[0],
                              pr[j][1:2] * bbi[1] + pi[j][1:2] * bbr[1]], axis=1)
        wi_ref[0, rows, :] = wi.astype(BF16)
        wot = jnp.concatenate([car[0][j + 1], car[1][S5_CHUNK - j],
                               -cai[0][j + 1], -cai[1][S5_CHUNK - j]], axis=1)
        wot_ref[0, rows, :] = wot.astype(BF16)
    coef_ref[0, 0:1, :] = jnp.concatenate([pr[S5_CHUNK][0:1], pr[S5_CHUNK][1:2]], axis=1)
    coef_ref[0, 1:2, :] = jnp.concatenate([pi[S5_CHUNK][0:1], pi[S5_CHUNK][1:2]], axis=1)


def _s5ops(lam_re, lam_im, log_dt, b_re, b_im, c_re, c_im, d_skip):
    g3 = lambda g: (g, 0, 0)
    g4 = lambda g: (g, 0, 0, 0)
    sw = lambda a: jnp.swapaxes(a.astype(F32), 0, 1)
    dt = jnp.broadcast_to(sw(log_dt)[..., None], (S5_GROUPS, 2, S5_STATE))
    args = (sw(lam_re), sw(lam_im), dt, sw(jnp.swapaxes(b_re, 2, 3)), sw(jnp.swapaxes(b_im, 2, 3)),
            sw(c_re), sw(c_im), d_skip.astype(F32).reshape(S5_GROUPS, S5_GROUP, 1))
    vec = pl.BlockSpec((1, 2, S5_STATE), g3)
    mat = pl.BlockSpec((1, 2, S5_GROUP, S5_STATE), g4)
    op = pl.BlockSpec((1, S5_ROW, S5_ROW), g3)
    return pl.pallas_call(
        _s5ops_kernel,
        out_shape=(jax.ShapeDtypeStruct((S5_GROUPS, S5_ROW, S5_ROW), BF16),) * 3
        + (jax.ShapeDtypeStruct((S5_GROUPS, 2, 2 * S5_STATE), F32),),
        grid=(S5_GROUPS,),
        in_specs=[vec, vec, vec, mat, mat, mat, mat, pl.BlockSpec((1, S5_GROUP, 1), g3)],
        out_specs=(op, op, op, pl.BlockSpec((1, 2, 2 * S5_STATE), g3)),
        compiler_params=_params(("arbitrary",)),
        name="s5ops",
    )(*args)


S5_BLOCK_GROUPS = LANES // S5_GROUP


S5_HALF_T = LANES // S5_GROUP


def _s5_perm():
    a = jnp.arange(S5_HALF_T * LANES, dtype=jnp.int32)
    dst = ((a % LANES) // S5_GROUP) * LANES + (a // LANES) * S5_GROUP + a % S5_GROUP
    return (dst[:, None] == a[None, :]).astype(BF16)


def _s5_kernel(nb, nk, u_ref, perm_ref, m_ref, wi_ref, wot_ref, coef_ref, h0_ref, y_ref, fin_ref,
               sr_ref, si_ref, efr_ref, ebr_ref, efi_ref, ebi_ref, uall_ref, yall_ref):
    gl = pl.program_id(1)
    rows = nb * nk
    half = 2 * S5_STATE

    @pl.when(gl == 0)
    def _():
        for hh in range(S5_CHUNK // S5_HALF_T):
            xcat = jnp.concatenate([u_ref[pl.ds(hh * S5_HALF_T + tt, rows, stride=S5_CHUNK), :].astype(BF16)
                                    for tt in range(S5_HALF_T)], axis=1)
            uh = jnp.dot(xcat, perm_ref[...], preferred_element_type=F32).astype(BF16)
            for g in range(S5_BLOCK_GROUPS):
                uall_ref[g, :, hh * LANES:(hh + 1) * LANES] = uh[:, g * LANES:(g + 1) * LANES]

    u = uall_ref[gl]
    s = jnp.dot(u, wi_ref[0], preferred_element_type=F32)
    sr_ref[...] = s[:, :half]
    si_ref[...] = s[:, half:]
    c_r = coef_ref[0, 0:1, :]
    c_i = coef_ref[0, 1:2, :]
    e_r = h0_ref[0, :, :half]
    e_i = h0_ref[0, :, half:]
    is_fwd = lax.broadcasted_iota(jnp.int32, (nb, half), 1) < S5_STATE
    for j in range(nk):
        rf = pl.ds(j, nb, stride=nk)
        rb = pl.ds(nk - 1 - j, nb, stride=nk)
        efr_ref[rf, :] = e_r
        ebr_ref[rb, :] = e_r
        efi_ref[rf, :] = e_i
        ebi_ref[rb, :] = e_i
        s_r = jnp.where(is_fwd, sr_ref[rf, :], sr_ref[rb, :])
        s_i = jnp.where(is_fwd, si_ref[rf, :], si_ref[rb, :])
        e_r, e_i = c_r * e_r - c_i * e_i + s_r, c_r * e_i + c_i * e_r + s_i
    fin_ref[0, :, :half] = e_r
    fin_ref[0, :, half:] = e_i
    y = jnp.dot(u, m_ref[0], preferred_element_type=F32)
    fwd_rows = lax.broadcasted_iota(jnp.int32, (rows, half), 1) < S5_STATE
    e = jnp.concatenate([jnp.where(fwd_rows, efr_ref[...], ebr_ref[...]),
                         jnp.where(fwd_rows, efi_ref[...], ebi_ref[...])], axis=1).astype(BF16)
    y = y + lax.dot_general(e, wot_ref[0], (((1,), (1,)), ((), ())), preferred_element_type=F32)
    yall_ref[gl] = y.astype(BF16)

    @pl.when(gl == S5_BLOCK_GROUPS - 1)
    def _():
        for hh in range(S5_CHUNK // S5_HALF_T):
            ycat = jnp.concatenate([yall_ref[g, :, hh * LANES:(hh + 1) * LANES]
                                    for g in range(S5_BLOCK_GROUPS)], axis=1)
            out = lax.dot_general(ycat, perm_ref[...], (((1,), (1,)), ((), ())), preferred_element_type=F32)
            for tt in range(S5_HALF_T):
                y_ref[pl.ds(hh * S5_HALF_T + tt, rows, stride=S5_CHUNK), :] = out[:, tt * LANES:(tt + 1) * LANES]


def _s5(u, ops, h0, nb, nk):
    m, w_in, w_out, coef = ops
    n = u.shape[0]
    rows = nb * nk
    perm = _s5_perm()
    g3 = lambda b, g: (b * S5_BLOCK_GROUPS + g, 0, 0)
    blk = lambda b, g: (0, b)
    return pl.pallas_call(
        functools.partial(_s5_kernel, nb, nk),
        out_shape=(jax.ShapeDtypeStruct((n, D_S5), F32),
                   jax.ShapeDtypeStruct((S5_GROUPS, nb, 4 * S5_STATE), F32)),
        grid=(S5_GROUPS // S5_BLOCK_GROUPS, S5_BLOCK_GROUPS),
        in_specs=[pl.BlockSpec((n, LANES), blk),
                  pl.BlockSpec(perm.shape, lambda b, g: (0, 0)),
                  pl.BlockSpec((1, S5_ROW, S5_ROW), g3),
                  pl.BlockSpec((1, S5_ROW, 4 * S5_STATE), g3),
                  pl.BlockSpec((1, 4 * S5_STATE, S5_ROW), g3),
                  pl.BlockSpec((1, 2, 2 * S5_STATE), g3),
                  pl.BlockSpec((1, nb, 4 * S5_STATE), g3)],
        out_specs=(pl.BlockSpec((n, LANES), blk),
                   pl.BlockSpec((1, nb, 4 * S5_STATE), g3)),
        scratch_shapes=[pltpu.VMEM((rows, 2 * S5_STATE), F32)] * 6 + [
            pltpu.VMEM((S5_BLOCK_GROUPS, rows, S5_ROW), BF16),
            pltpu.VMEM((S5_BLOCK_GROUPS, rows, S5_ROW), BF16)],
        compiler_params=_params(("arbitrary", "arbitrary")),
        name="s5",
    )(u, perm, m, w_in, w_out, coef, h0)


def _dft_tables(seq):
    k = jnp.arange(seq, dtype=jnp.int32)
    na = seq // FNET_GROUP
    ang_a = (2.0 * math.pi / na) * ((jnp.arange(na, dtype=jnp.int32)[:, None] * k[None, :]) % na).astype(F32)
    ang_b = (2.0 * math.pi / seq) * ((jnp.arange(FNET_GROUP, dtype=jnp.int32)[:, None] * k[None, :]) % seq).astype(F32)
    ca, sa = jnp.cos(ang_a)[:, None, :], jnp.sin(ang_a)[:, None, :]
    cb, sb = jnp.cos(ang_b)[None, :, :], jnp.sin(ang_b)[None, :, :]
    cos_jk = (ca * cb - sa * sb).reshape(seq, seq)
    sin_jk = (sa * cb + ca * sb).reshape(seq, seq)
    cs = jnp.concatenate([cos_jk, -sin_jk], axis=1).astype(BF16)
    c = jnp.arange(D_FNET, dtype=jnp.int32)
    same = (c[:, None] // FNET_GROUP) == (c[None, :] // FNET_GROUP)
    angc = (2.0 * math.pi / FNET_GROUP) * (((c[:, None] % FNET_GROUP) * (c[None, :] % FNET_GROUP))
                                           % FNET_GROUP).astype(F32)
    scale = 1.0 / math.sqrt(seq * FNET_GROUP)
    bdc = jnp.where(same, jnp.cos(angc) * scale, 0.0).astype(BF16)
    bds = jnp.where(same, jnp.sin(angc) * scale, 0.0).astype(BF16)
    return cs, bdc, bds


def _fourier_kernel(seq, z_ref, cs_ref, bdc_ref, bds_ref, o_ref, zz_ref):
    @pl.when(pl.program_id(1) == 0)
    def _():
        z = z_ref[...]
        zz_ref[0:seq, :] = jnp.dot(z, bdc_ref[...], preferred_element_type=F32).astype(BF16)
        zz_ref[seq:, :] = jnp.dot(z, bds_ref[...], preferred_element_type=F32).astype(BF16)

    o_ref[...] = jnp.dot(cs_ref[...], zz_ref[...], preferred_element_type=F32).astype(BF16)


def _fourier(z, nb, seq, tl):
    cs, bdc, bds = _dft_tables(seq)
    nt = seq // tl
    return pl.pallas_call(
        functools.partial(_fourier_kernel, seq),
        out_shape=jax.ShapeDtypeStruct(z.shape, BF16),
        grid=(nb, nt),
        in_specs=[pl.BlockSpec((seq, D_FNET), lambda b, i: (b, 0)),
                  pl.BlockSpec((tl, 2 * seq), lambda b, i: (i, 0)),
                  pl.BlockSpec((D_FNET, D_FNET), lambda b, i: (0, 0)),
                  pl.BlockSpec((D_FNET, D_FNET), lambda b, i: (0, 0))],
        out_specs=pl.BlockSpec((tl, D_FNET), lambda b, i: (b * nt + i, 0)),
        scratch_shapes=[pltpu.VMEM((2 * seq, D_FNET), BF16)],
        compiler_params=_params(("arbitrary", "arbitrary")),
        name="fourier",
    )(z, cs, bdc, bds)


def _first_argmax_mask(v, iota, size):
    m = jnp.max(v, axis=0, keepdims=True)
    first = jnp.min(jnp.where(v == m, iota, size), axis=0, keepdims=True)
    return iota == first


def _route(logits_t, bias_col):
    tm = logits_t.shape[1]
    neg = -jnp.inf
    s = jax.nn.sigmoid(logits_t)
    biased = s + bias_col
    io8 = lax.broadcasted_iota(jnp.int32, (EXPERTS_PER_GROUP, tm), 0)
    gs_rows = []
    for g in range(N_EXPERT_GROUPS):
        blk = biased[g * EXPERTS_PER_GROUP:(g + 1) * EXPERTS_PER_GROUP, :]
        m1 = jnp.max(blk, axis=0, keepdims=True)
        rest = jnp.where(_first_argmax_mask(blk, io8, EXPERTS_PER_GROUP), neg, blk)
        gs_rows.append(m1 + jnp.max(rest, axis=0, keepdims=True))
    gs = jnp.concatenate(gs_rows, axis=0)
    iog = lax.broadcasted_iota(jnp.int32, (N_EXPERT_GROUPS, tm), 0)
    gsel = jnp.zeros((N_EXPERT_GROUPS, tm), F32)
    for _ in range(TOPK_GROUPS):
        sel = _first_argmax_mask(gs, iog, N_EXPERT_GROUPS)
        gsel = jnp.where(sel, 1.0, gsel)
        gs = jnp.where(sel, neg, gs)
    emask = jnp.concatenate(
        [jnp.broadcast_to(gsel[g:g + 1, :], (EXPERTS_PER_GROUP, tm)) for g in range(N_EXPERT_GROUPS)], axis=0)
    v = jnp.where(emask > 0.0, biased, neg)
    ioe = lax.broadcasted_iota(jnp.int32, (N_EXPERTS, tm), 0)
    idx_rows, s_rows = [], []
    for _ in range(TOP_K):
        sel = _first_argmax_mask(v, ioe, N_EXPERTS)
        idx_rows.append(jnp.sum(jnp.where(sel, ioe, 0), axis=0, keepdims=True))
        s_rows.append(jnp.sum(jnp.where(sel, s, 0.0), axis=0, keepdims=True))
        v = jnp.where(sel, neg, v)
    denom = s_rows[0]
    for r in s_rows[1:]:
        denom = denom + r
    pad = ROUTE_ROWS - TOP_K
    idx = jnp.concatenate(idx_rows + [jnp.zeros((pad, tm), jnp.int32)], axis=0)
    w = jnp.concatenate([r / denom * ROUTED_SCALE for r in s_rows] + [jnp.zeros((pad, tm), F32)], axis=0)
    return idx, w


def _merge_kernel(has_pos, *refs):
    if has_pos:
        (ys_ref, yf_ref, gt_ref, x_ref, pos_ref, mod_ref, n2_ref, wglu_ref, wps_ref, wpf_ref, wout_ref,
         wrt_ref, rb_ref, ws1_ref, ws3_ref, ws2_ref, xs_ref, h2_ref, ridx_ref, rw_ref) = refs
        x = x_ref[...] + pos_ref[...]
    else:
        (ys_ref, yf_ref, gt_ref, x_ref, mod_ref, n2_ref, wglu_ref, wps_ref, wpf_ref, wout_ref,
         wrt_ref, rb_ref, ws1_ref, ws3_ref, ws2_ref, xs_ref, h2_ref, ridx_ref, rw_ref) = refs
        x = x_ref[...]
    m = mod_ref[0]
    g = jax.nn.gelu(ys_ref[...].astype(F32))
    a = g * jax.nn.sigmoid(jnp.dot(g.astype(BF16), wglu_ref[...], preferred_element_type=F32))
    pa = jnp.dot(a.astype(BF16), wps_ref[...], preferred_element_type=F32)
    pb = jnp.dot(yf_ref[...], wpf_ref[...], preferred_element_type=F32)
    gt = gt_ref[...].astype(F32)
    merged = gt[:, :D_MODEL] * pa + gt[:, D_MODEL:] * pb
    x1 = x + m[2:3, :] * jnp.dot(merged.astype(BF16), wout_ref[...], preferred_element_type=F32)
    h2 = _rms(x1, n2_ref[...]) * (1.0 + m[4:5, :]) + m[3:4, :]
    hb = h2.astype(BF16)
    for j in range(ROW_SUB):
        h2_ref[pl.ds(j, h2.shape[0], stride=ROW_SUB), :] = h2[:, j * LANES:(j + 1) * LANES]
    wr = wrt_ref[...]
    wr_hi = wr.astype(BF16)
    wr_lo = (wr - wr_hi.astype(F32)).astype(BF16)
    h_lo = (h2 - hb.astype(F32)).astype(BF16)
    dn = (((1,), (1,)), ((), ()))
    logits_t = (lax.dot_general(wr_hi, hb, dn, preferred_element_type=F32)
                + lax.dot_general(wr_hi, h_lo, dn, preferred_element_type=F32)
                + lax.dot_general(wr_lo, hb, dn, preferred_element_type=F32))
    ridx_ref[...], rw_ref[...] = _route(logits_t, rb_ref[...])
    s1 = jnp.dot(hb, ws1_ref[...], preferred_element_type=F32)
    s3 = jnp.dot(hb, ws3_ref[...], preferred_element_type=F32)
    shared = jnp.dot((s1 * jax.nn.sigmoid(s1) * s3).astype(BF16), ws2_ref[...], preferred_element_type=F32)
    xs_ref[...] = x1 + m[5:6, :] * shared


def _merge(ys, yf, gt, x, pos, mod, mod_row, n2, weights, tm):
    n = x.shape[0]
    has_pos = pos is not None
    row = lambda i: (i, 0)
    const = lambda a: pl.BlockSpec(a.shape, lambda i: (0,) * a.ndim)
    in_specs = [pl.BlockSpec((tm, D_S5), row), pl.BlockSpec((tm, D_FNET), row),
                pl.BlockSpec((tm, 2 * D_MODEL), row), pl.BlockSpec((tm, D_MODEL), row)]
    args = [ys, yf, gt, x]
    if has_pos:
        nper = pos.shape[0] // tm
        in_specs.append(pl.BlockSpec((tm, D_MODEL), lambda i: (i % nper, 0)))
        args.append(pos)
    in_specs += [pl.BlockSpec((1, N_MOD, D_MODEL), lambda i: (mod_row(i, tm), 0, 0)), const(n2)]
    args += [mod, n2]
    in_specs += [const(w) for w in weights]
    args += list(weights)
    return pl.pallas_call(
        functools.partial(_merge_kernel, has_pos),
        out_shape=(jax.ShapeDtypeStruct((n, D_MODEL), F32),
                   jax.ShapeDtypeStruct((n * ROW_SUB, LANES), F32),
                   jax.ShapeDtypeStruct((ROUTE_ROWS, n), jnp.int32),
                   jax.ShapeDtypeStruct((ROUTE_ROWS, n), F32)),
        grid=(n // tm,),
        in_specs=in_specs,
        out_specs=(pl.BlockSpec((tm, D_MODEL), row), pl.BlockSpec((tm * ROW_SUB, LANES), row),
                   pl.BlockSpec((ROUTE_ROWS, tm), lambda i: (0, i)),
                   pl.BlockSpec((ROUTE_ROWS, tm), lambda i: (0, i))),
        compiler_params=_params(("arbitrary",)),
        name="merge",
    )(*args)


MOE_SUB = 4096
MOE_TM = 128
MOE_TMAX = MOE_SUB * TOP_K // MOE_TM + N_EXPERTS
MOE_SORT_SUBS = 2
MOE_PAD = 2
MOE_TS = MOE_TMAX + 2 * MOE_PAD
MOE_DUMMY = 256
MOE_RMW = 8
MOE_FT = 256
MOE_EPS = 2
MOE_ESTEPS = N_EXPERTS // MOE_EPS
ROW_TILE = MOE_TM * ROW_SUB


def _moe_plan(ridx, rw):
    n = ridx.shape[1]
    nsub = n // MOE_SUB
    npair = n * TOP_K
    t = jnp.arange(n, dtype=jnp.int32)
    key = (((t // MOE_SUB) * N_EXPERTS)[None] + ridx[:TOP_K]) * MOE_SUB + (t % MOE_SUB)[None]
    group = MOE_SORT_SUBS * MOE_SUB
    parts = [lax.sort((key[:, g:g + group].reshape(-1), rw[:TOP_K, g:g + group].reshape(-1)), num_keys=1)
             for g in range(0, n, group)]
    skey = jnp.concatenate([pk for pk, _ in parts])
    sw = jnp.concatenate([pw for _, pw in parts])
    stok = jnp.concatenate([(skey % MOE_SUB) * ROW_SUB, jnp.zeros((MOE_TM,), jnp.int32)])
    sw_rows = jnp.concatenate([sw, jnp.zeros((MOE_TM,), F32)]).reshape((npair + MOE_TM) // LANES, 1, LANES)
    hits = ridx[:TOP_K].reshape(TOP_K, nsub, 1, MOE_SUB) == jnp.arange(N_EXPERTS, dtype=jnp.int32)[None, None, :, None]
    cnt = jnp.sum(hits.astype(jnp.int32), axis=(0, 3))
    poff = (jnp.cumsum(cnt.reshape(-1)) - cnt.reshape(-1)).reshape(nsub, N_EXPERTS)
    ntile = (cnt + MOE_TM - 1) // MOE_TM
    tcum = jnp.cumsum(ntile, axis=1)
    toff = tcum - ntile
    tstart = jnp.concatenate([toff, tcum[:, -1:]], axis=1).reshape(-1).astype(jnp.int32)
    j = jnp.arange(MOE_TS, dtype=jnp.int32) - MOE_PAD
    valid = (j[None] >= 0) & (j[None] < tcum[:, -1:])
    te = jnp.minimum(jnp.sum(j[None, :, None] >= tcum[:, None, :], axis=-1), N_EXPERTS - 1)
    pick = lambda a: jnp.take_along_axis(a, te, axis=1)
    first = (j[None] - pick(toff)) * MOE_TM
    p0 = jnp.where(valid, pick(poff) + first, 0).reshape(-1).astype(jnp.int32)
    nv = jnp.where(valid, jnp.minimum(pick(cnt) - first, MOE_TM), 0).reshape(-1).astype(jnp.int32)
    return tstart, p0, nv, stok, sw_rows


def _moe_kernel(ts_ref, p0_ref, nv_ref, tok_ref, sw_ref, src_ref, w1_ref, w3_ref, w2_ref, xs_ref, mod_ref,
                fg_ref, o_ref, y_ref, xt_ref, xb_ref, act_ref, ot_ref, w1b_ref, w3b_ref, w2b_ref, slot_ref):
    sub = pl.program_id(0)
    e = pl.program_id(1)
    base = sub * MOE_TS + MOE_PAD

    def gather(p0):
        for mi in range(MOE_TM):
            tok = pl.multiple_of(tok_ref[p0 + mi], ROW_SUB)
            xt_ref[mi * ROW_SUB:(mi + 1) * ROW_SUB, :] = src_ref[pl.ds(tok, ROW_SUB), :]
        for j in range(ROW_SUB):
            xb_ref[:, j * LANES:(j + 1) * LANES] = xt_ref[pl.ds(j, MOE_TM, stride=ROW_SUB), :].astype(BF16)

    def scatter(p0, nv, masked):
        for u in range(0, MOE_TM, MOE_RMW):
            new = []
            for i in range(MOE_RMW):
                tok = tok_ref[p0 + u + i]
                if masked:
                    tok = jnp.where(u + i < nv, tok, MOE_SUB * ROW_SUB)
                tok = pl.multiple_of(tok, ROW_SUB)
                new.append((tok, y_ref[pl.ds(tok, ROW_SUB), :]
                            + ot_ref[(u + i) * ROW_SUB:(u + i + 1) * ROW_SUB, :]))
            for tok, v in new:
                y_ref[pl.ds(tok, ROW_SUB), :] = v

    @pl.when(e == 0)
    def _():
        y_ref[...] = jnp.zeros_like(y_ref)
        ot_ref[...] = jnp.zeros_like(ot_ref)
        act_ref[...] = jnp.zeros_like(act_ref)
        w2b_ref[...] = jnp.zeros_like(w2b_ref)
        slot_ref[0] = 0
        gather(p0_ref[base])

    def down_proj(w2_slot):
        o = jnp.dot(act_ref[...], w2b_ref[w2_slot], preferred_element_type=F32)
        for j in range(ROW_SUB):
            ot_ref[pl.ds(j, MOE_TM, stride=ROW_SUB), :] = o[:, j * LANES:(j + 1) * LANES]

    def step(i, first, slot):
        cur = base + i
        scatter(p0_ref[cur - 2], nv_ref[cur - 2], True)
        down_proj(jnp.where(i > first, slot, 1 - slot))
        p0 = p0_ref[cur]
        nv = nv_ref[cur]
        x = xb_ref[...]
        a = jnp.dot(x, w1b_ref[...], preferred_element_type=F32)
        b = jnp.dot(x, w3b_ref[...], preferred_element_type=F32)
        r0 = p0 // LANES
        c = p0 % LANES
        lane = lax.broadcasted_iota(jnp.int32, (1, LANES), 1)
        rows = lax.broadcasted_iota(jnp.int32, (LANES, LANES), 0)
        cols = lax.broadcasted_iota(jnp.int32, (LANES, LANES), 1)
        gparts = []
        for hh in range(MOE_TM // LANES):
            ga = pltpu.roll(sw_ref[r0 + hh], LANES - c, axis=1)
            gb = pltpu.roll(sw_ref[r0 + hh + 1], LANES - c, axis=1)
            g = jnp.where(lane + hh * LANES < nv, jnp.where(lane < LANES - c, ga, gb), 0.0)
            gparts.append(jnp.sum(jnp.where(rows == cols, jnp.broadcast_to(g, (LANES, LANES)), 0.0),
                                  axis=1, keepdims=True))
        gcol = jnp.concatenate(gparts, axis=0)
        act_ref[...] = (a * jax.nn.sigmoid(a) * b * gcol).astype(BF16)
        gather(p0_ref[cur + 1])

    def run_expert(ee):
        ex = jnp.minimum(e, MOE_ESTEPS - 1) * MOE_EPS + ee
        first = ts_ref[sub * (N_EXPERTS + 1) + ex]
        last = jnp.where(e < MOE_ESTEPS, ts_ref[sub * (N_EXPERTS + 1) + ex + 1], first)

        @pl.when(last > first)
        def _():
            slot_ref[0] = 1 - slot_ref[0]
            w1b_ref[...] = w1_ref[ee].astype(BF16)
            w3b_ref[...] = w3_ref[ee].astype(BF16)
            w2b_ref[slot_ref[0]] = w2_ref[ee].astype(BF16)

        slot = slot_ref[0]

        def body(i, carry):
            step(i, first, slot)
            return carry

        lax.fori_loop(first, last, body, 0)

        if ee == MOE_EPS - 1:
            @pl.when(e == MOE_ESTEPS - 1)
            def _():
                scatter(p0_ref[base + last - 2], nv_ref[base + last - 2], True)
                down_proj(slot)
                scatter(p0_ref[base + last - 1], nv_ref[base + last - 1], True)

    for ee in range(MOE_EPS):
        run_expert(ee)

    @pl.when(e >= MOE_ESTEPS)
    def _():
        row0 = pl.multiple_of((e - MOE_ESTEPS) * (MOE_FT * ROW_SUB), ROW_SUB)
        y = jnp.concatenate([y_ref[pl.ds(row0 + j, MOE_FT, stride=ROW_SUB), :] for j in range(ROW_SUB)],
                            axis=1)
        x2 = xs_ref[...] + mod_ref[0][5:6, :] * y
        o_ref[...] = _rms(x2, fg_ref[...])


def _moe(h2_rows, plan, w1, w3, w2, xs, mod, mod_row, fg):
    tstart, p0, nv, stok, sw_rows = plan
    nsub = h2_rows.shape[0] // (MOE_SUB * ROW_SUB)
    per_sub = MOE_SUB // MOE_FT
    wmap = lambda s, e, ts, p0, nv: (jnp.minimum(e, MOE_ESTEPS - 1), 0, 0)
    sub2 = lambda s, e, ts, p0, nv: (s, 0)
    out_tile = lambda s, e: s * per_sub + jnp.clip(e - MOE_ESTEPS, 0, per_sub - 1)
    grid_spec = pltpu.PrefetchScalarGridSpec(
        num_scalar_prefetch=3,
        grid=(nsub, MOE_ESTEPS + per_sub),
        in_specs=[pl.BlockSpec(memory_space=pltpu.SMEM),
                  pl.BlockSpec(sw_rows.shape, lambda s, e, ts, p0, nv: (0, 0, 0)),
                  pl.BlockSpec((MOE_SUB * ROW_SUB, LANES), sub2, pipeline_mode=pl.Buffered(1)),
                  pl.BlockSpec((MOE_EPS, D_MODEL, D_EXPERT), wmap),
                  pl.BlockSpec((MOE_EPS, D_MODEL, D_EXPERT), wmap),
                  pl.BlockSpec((MOE_EPS, D_EXPERT, D_MODEL), wmap),
                  pl.BlockSpec((MOE_FT, D_MODEL), lambda s, e, ts, p0, nv: (out_tile(s, e), 0)),
                  pl.BlockSpec((1, N_MOD, D_MODEL),
                               lambda s, e, ts, p0, nv: (mod_row(out_tile(s, e), MOE_FT), 0, 0)),
                  pl.BlockSpec((1, D_MODEL), lambda s, e, ts, p0, nv: (0, 0))],
        out_specs=pl.BlockSpec((MOE_FT, D_MODEL), lambda s, e, ts, p0, nv: (out_tile(s, e), 0)),
        scratch_shapes=[pltpu.VMEM(((MOE_SUB + MOE_DUMMY) * ROW_SUB, LANES), F32),
                        pltpu.VMEM((ROW_TILE, LANES), F32), pltpu.VMEM((MOE_TM, D_MODEL), BF16),
                        pltpu.VMEM((MOE_TM, D_EXPERT), BF16), pltpu.VMEM((ROW_TILE, LANES), F32),
                        pltpu.VMEM((D_MODEL, D_EXPERT), BF16), pltpu.VMEM((D_MODEL, D_EXPERT), BF16),
                        pltpu.VMEM((2, D_EXPERT, D_MODEL), BF16), pltpu.SMEM((1,), jnp.int32)])
    return pl.pallas_call(
        _moe_kernel,
        grid_spec=grid_spec,
        out_shape=jax.ShapeDtypeStruct(xs.shape, F32),
        compiler_params=_params(("arbitrary", "arbitrary")),
        name="moe",
    )(tstart, p0, nv, stok, sw_rows, h2_rows, w1, w3, w2, xs, mod, fg)


def _grid_pos_embed(n_tokens):
    rows = n_tokens // GRID_W
    quarter = D_MODEL // 4
    omega = 1.0 / (10000.0 ** (jnp.arange(quarter, dtype=F32) / quarter))

    def emb(count):
        a = jnp.arange(count, dtype=F32)[:, None] * omega
        return jnp.concatenate([jnp.sin(a), jnp.cos(a)], axis=-1)

    by_row = jnp.repeat(emb(rows), GRID_W, axis=0)
    by_col = jnp.tile(emb(GRID_W), (rows, 1))
    return jnp.concatenate([by_row, by_col], axis=-1)


def _mixers(x3, pos, mod, first_row, h0, s5_ops, p):
    nb, seq, _ = x3.shape
    n = nb * seq
    nk = seq // S5_CHUNK
    x = x3.reshape(n, D_MODEL)
    per_seq_mod = first_row > 0

    def mod_row(i, tm):
        return first_row + (i * tm) // seq if per_seq_mod else 0

    us, uf, gt = _inproj(x, pos, mod, mod_row, p["norm1_g"], p["w_in"], 1024)
    ys, fin = _s5(us, s5_ops, h0, nb, nk)
    yf = _fourier(uf, nb, seq, min(seq, 512))
    xs, h2_rows, ridx, rw = _merge(ys, yf, gt, x, pos, mod, mod_row, p["norm2_g"], p["merge_w"], 512)
    return xs, h2_rows, ridx, rw, fin, mod_row


def _plan_of_stream(plan, s, n_tokens):
    tstart, p0, nv, stok, sw_rows = plan
    nsub = n_tokens // MOE_SUB
    npair = n_tokens * TOP_K
    off = s * npair
    tiles = slice(s * nsub * MOE_TS, (s + 1) * nsub * MOE_TS)
    nv_s = nv[tiles]
    p0_s = jnp.where(nv_s > 0, p0[tiles] - off, 0)
    return (tstart[s * nsub * (N_EXPERTS + 1):(s + 1) * nsub * (N_EXPERTS + 1)], p0_s, nv_s,
            stok[off:off + npair + MOE_TM], sw_rows[off // LANES:(off + npair + MOE_TM) // LANES])


def kernel(x_prompt, x_sample, state_s5_re, state_s5_im, c, c_ctx, w_ada, b_ada, norm1_g, norm2_g, w_in,
           lam_re, lam_im, log_dt, b_re, b_im, c_re, c_im, d_skip, w_glu, w_proj_s5, w_proj_f, w_out,
           w_router, router_bias, w1, w3, w2, ws1, ws3, ws2, final_norm_g):
    nb_ctx = x_prompt.shape[0]
    nb_lat, seq_lat, _ = x_sample.shape
    half = 2 * S5_STATE

    cond = jnp.concatenate([c_ctx[None], c, jnp.zeros((MOD_ROWS - 1 - nb_lat, D_MODEL), F32)], axis=0)
    mod = _adaln(cond, w_ada[0], b_ada[0]).reshape(MOD_ROWS, N_MOD, D_MODEL)

    s5_ops = _s5ops(lam_re[0], lam_im[0], log_dt[0], b_re[0], b_im[0], c_re[0], c_im[0], d_skip[0])
    p = dict(
        norm1_g=norm1_g[0][None], norm2_g=norm2_g[0][None], final_g=final_norm_g[None],
        w_in=w_in[0].astype(BF16), w1=w1[0], w3=w3[0], w2=w2[0],
        merge_w=(w_glu[0].astype(BF16), w_proj_s5[0].astype(BF16), w_proj_f[0].astype(BF16),
                 w_out[0].astype(BF16), w_router[0].T, router_bias[0][:, None],
                 ws1[0].astype(BF16), ws3[0].astype(BF16), ws2[0].astype(BF16)))

    def pack_state(sr, si):
        f = lambda a: a.astype(F32).transpose(2, 0, 1, 3).reshape(S5_GROUPS, a.shape[0], half)
        return jnp.concatenate([f(sr), f(si)], axis=-1)

    def unpack_state(fin, lo):
        nb = fin.shape[1]
        return fin[..., lo:lo + half].reshape(S5_GROUPS, nb, 2, S5_STATE).transpose(1, 2, 0, 3)[:, None]

    h0_ctx = jnp.zeros((S5_GROUPS, nb_ctx, 2 * half), F32)
    h0_lat = pack_state(state_s5_re[:, 0], state_s5_im[:, 0])
    streams = [(x_prompt, _mixers(x_prompt, None, mod, 0, h0_ctx, s5_ops, p)),
               (x_sample, _mixers(x_sample, _grid_pos_embed(seq_lat), mod, 1, h0_lat, s5_ops, p))]
    n_tokens = streams[0][1][0].shape[0]
    assert all(m[0].shape[0] == n_tokens for _, m in streams), "the joint plan slices equal-sized streams"
    plan = _moe_plan(jnp.concatenate([m[2] for _, m in streams], axis=1),
                     jnp.concatenate([m[3] for _, m in streams], axis=1))
    outs = []
    for s, (x3, (xs, h2_rows, _, _, _, mod_row)) in enumerate(streams):
        out = _moe(h2_rows, _plan_of_stream(plan, s, n_tokens), p["w1"], p["w3"], p["w2"],
                   xs, mod, mod_row, p["final_g"])
        outs.append(out.reshape(x3.shape))
    fin = streams[0][1][4]
    return (outs[0], outs[1], unpack_state(fin, 0).astype(x_prompt.dtype),
            unpack_state(fin, half).astype(x_prompt.dtype))
```

```python
import functools
import math

import jax
import jax.numpy as jnp
from jax import lax
from jax.experimental import pallas as pl
from jax.experimental.pallas import tpu as pltpu

D_MODEL = 1024
GRID_W = 64
D_S5 = 768
S5_GROUP = 16
S5_GROUPS = 48
S5_STATE = 64
D_FNET = 256
FNET_GROUP = 64
N_EXPERTS = 64
TOP_K = 6
N_EXPERT_GROUPS = 8
EXPERTS_PER_GROUP = N_EXPERTS // N_EXPERT_GROUPS
TOPK_GROUPS = 4
D_EXPERT = 256
ROUTED_SCALE = 2.5
N_MOD = 6
EPS = 1e-6

S5_CHUNK = 16
S5_ROW = S5_CHUNK * S5_GROUP
MOD_ROWS = 8
ROUTE_ROWS = 8
LANES = 128
ROW_SUB = D_MODEL // LANES
VMEM_LIMIT = 60 * 1024 * 1024

BF16 = jnp.bfloat16
F32 = jnp.float32


def _params(sem, vmem=VMEM_LIMIT):
    return pltpu.CompilerParams(dimension_semantics=sem, vmem_limit_bytes=vmem)


def _rms(x, g):
    return x * lax.rsqrt(jnp.mean(x * x, axis=-1, keepdims=True) + EPS) * g


def _adaln_kernel(c_ref, w_ref, b_ref, o_ref):
    c = c_ref[...]
    s = c * jax.nn.sigmoid(c)
    w = w_ref[...]
    s_hi = s.astype(BF16)
    s_lo = (s - s_hi.astype(F32)).astype(BF16)
    w_hi = w.astype(BF16)
    w_lo = (w - w_hi.astype(F32)).astype(BF16)
    o_ref[...] = (jnp.dot(s_hi, w_hi, preferred_element_type=F32) + jnp.dot(s_lo, w_hi, preferred_element_type=F32)
                  + jnp.dot(s_hi, w_lo, preferred_element_type=F32) + b_ref[...])


def _adaln(cond, w_ada, b_ada):
    n_out = N_MOD * D_MODEL
    return pl.pallas_call(
        _adaln_kernel,
        out_shape=jax.ShapeDtypeStruct((MOD_ROWS, n_out), F32),
        grid=(N_MOD,),
        in_specs=[pl.BlockSpec((MOD_ROWS, D_MODEL), lambda i: (0, 0)),
                  pl.BlockSpec((D_MODEL, D_MODEL), lambda i: (0, i)),
                  pl.BlockSpec((1, D_MODEL), lambda i: (0, i))],
        out_specs=pl.BlockSpec((MOD_ROWS, D_MODEL), lambda i: (0, i)),
        compiler_params=_params(("arbitrary",)),
        name="adaln",
    )(cond, w_ada, b_ada.reshape(1, n_out))


def _inproj_kernel(has_pos, *refs):
    if has_pos:
        x_ref, pos_ref, mod_ref, g_ref, w_ref, us_ref, uf_ref, gt_ref = refs
        x = x_ref[...] + pos_ref[...]
    else:
        x_ref, mod_ref, g_ref, w_ref, us_ref, uf_ref, gt_ref = refs
        x = x_ref[...]
    m = mod_ref[0]
    h = _rms(x, g_ref[...]) * (1.0 + m[1:2, :]) + m[0:1, :]
    p = jnp.dot(h.astype(BF16), w_ref[...], preferred_element_type=F32)
    us_ref[...] = p[:, :D_S5]
    uf_ref[...] = p[:, D_S5:D_MODEL].astype(BF16)
    gt_ref[...] = jax.nn.sigmoid(p[:, D_MODEL:]).astype(BF16)


def _inproj(x, pos, mod, mod_row, norm_g, w_in_bf, tm):
    n = x.shape[0]
    has_pos = pos is not None
    row = lambda i: (i, 0)
    in_specs = [pl.BlockSpec((tm, D_MODEL), row)]
    args = [x]
    if has_pos:
        nper = pos.shape[0] // tm
        in_specs.append(pl.BlockSpec((tm, D_MODEL), lambda i: (i % nper, 0)))
        args.append(pos)
    in_specs += [pl.BlockSpec((1, N_MOD, D_MODEL), lambda i: (mod_row(i, tm), 0, 0)),
                 pl.BlockSpec((1, D_MODEL), lambda i: (0, 0)),
                 pl.BlockSpec(w_in_bf.shape, lambda i: (0, 0))]
    args += [mod, norm_g, w_in_bf]
    return pl.pallas_call(
        functools.partial(_inproj_kernel, has_pos),
        out_shape=(jax.ShapeDtypeStruct((n, D_S5), F32),
                   jax.ShapeDtypeStruct((n, D_FNET), BF16),
                   jax.ShapeDtypeStruct((n, 2 * D_MODEL), BF16)),
        grid=(n // tm,),
        in_specs=in_specs,
        out_specs=(pl.BlockSpec((tm, D_S5), row), pl.BlockSpec((tm, D_FNET), row),
                   pl.BlockSpec((tm, 2 * D_MODEL), row)),
        compiler_params=_params(("arbitrary",)),
        name="inproj",
    )(*args)


def _shift_lanes(x, k):
    if k == 0:
        return x
    z = jnp.zeros((x.shape[0], abs(k)), x.dtype)
    if k > 0:
        return jnp.concatenate([z, x[:, :x.shape[1] - k]], axis=1)
    return jnp.concatenate([x[:, -k:], z], axis=1)


def _s5ops_kernel(lam_re_ref, lam_im_ref, dt_ref, btr_ref, bti_ref, cr_ref, ci_ref, d_ref,
                  m_ref, wi_ref, wot_ref, coef_ref):
    hi = lax.Precision.HIGHEST
    lr = jnp.minimum(lam_re_ref[0], -1e-4)
    li = lam_im_ref[0]
    dt = jnp.exp(dt_ref[0])
    mag = jnp.exp(lr * dt)
    ar = mag * jnp.cos(li * dt)
    ai = mag * jnp.sin(li * dt)
    den = lr * lr + li * li
    nr = ar - 1.0
    qr = (nr * lr + ai * li) / den
    qi = (ai * lr - nr * li) / den
    npow = S5_CHUNK + 1
    nrow = (lax.broadcasted_iota(jnp.int32, (2 * npow, 1), 0) // 2).astype(F32)
    stack = lambda v: jnp.concatenate([v] * npow, axis=0)
    pmag = jnp.exp(nrow * stack(lr * dt))
    pang = nrow * stack(li * dt)
    pr_all = pmag * jnp.cos(pang)
    pi_all = pmag * jnp.sin(pang)
    pr = [pr_all[2 * n:2 * n + 2] for n in range(npow)]
    pi = [pi_all[2 * n:2 * n + 2] for n in range(npow)]
    bbr, bbi, car, cai = [], [], [], []
    for d in range(2):
        btr = btr_ref[0, d]
        bti = bti_ref[0, d]
        bbr.append(qr[d:d + 1] * btr - qi[d:d + 1] * bti)
        bbi.append(qr[d:d + 1] * bti + qi[d:d + 1] * btr)
        cr = cr_ref[0, d]
        ci = ci_ref[0, d]
        car.append([cr * pr[n][d:d + 1] - ci * pi[n][d:d + 1] for n in range(S5_CHUNK + 1)])
        cai.append([cr * pi[n][d:d + 1] + ci * pr[n][d:d + 1] for n in range(S5_CHUNK + 1)])

    def lag_kernels(d, order):
        a = jnp.concatenate([car[d][n] for n in order], axis=0)
        b = jnp.concatenate([cai[d][n] for n in order], axis=0)
        dn = (((1,), (1,)), ((), ()))
        return (lax.dot_general(bbr[d], a, dn, precision=hi, preferred_element_type=F32)
                - lax.dot_general(bbi[d], b, dn, precision=hi, preferred_element_type=F32))

    ktf = lag_kernels(0, range(S5_CHUNK))
    ktb = lag_kernels(1, range(S5_CHUNK - 1, -1, -1))
    row = lax.broadcasted_iota(jnp.int32, (S5_GROUP, S5_ROW), 0)
    lane = lax.broadcasted_iota(jnp.int32, (S5_GROUP, S5_ROW), 1)
    dcol = d_ref[0]
    for j in range(S5_CHUNK):
        rows = slice(j * S5_GROUP, (j + 1) * S5_GROUP)
        blk = _shift_lanes(ktf, S5_GROUP * j) + _shift_lanes(ktb, -S5_GROUP * (S5_CHUNK - 1 - j))
        blk = blk + jnp.where(lane == S5_GROUP * j + row, dcol, 0.0)
        m_ref[0, rows, :] = blk.astype(BF16)
        nf = S5_CHUNK - 1 - j
        wi = jnp.concatenate([pr[nf][0:1] * bbr[0] - pi[nf][0:1] * bbi[0],
                              pr[j][1:2] * bbr[1] - pi[j][1:2] * bbi[1],
                              pr[nf][0:1] * bbi[0] + pi[nf][0:1] * bbr[0],
                              pr[j][1:2] * bbi[1] + pi[j][1:2] * bbr[1]], axis=1)
        wi_ref[0, rows, :] = wi.astype(BF16)
        wot = jnp.concatenate([car[0][j + 1], car[1][S5_CHUNK - j],
                               -cai[0][j + 1], -cai[1][S5_CHUNK - j]], axis=1)
        wot_ref[0, rows, :] = wot.astype(BF16)
    coef_ref[0, 0:1, :] = jnp.concatenate([pr[S5_CHUNK][0:1], pr[S5_CHUNK][1:2]], axis=1)
    coef_ref[0, 1:2, :] = jnp.concatenate([pi[S5_CHUNK][0:1], pi[S5_CHUNK][1:2]], axis=1)


def _s5ops(lam_re, lam_im, log_dt, b_re, b_im, c_re, c_im, d_skip):
    g3 = lambda g: (g, 0, 0)
    g4 = lambda g: (g, 0, 0, 0)
    sw = lambda a: jnp.swapaxes(a.astype(F32), 0, 1)
    dt = jnp.broadcast_to(sw(log_dt)[..., None], (S5_GROUPS, 2, S5_STATE))
    args = (sw(lam_re), sw(lam_im), dt, sw(jnp.swapaxes(b_re, 2, 3)), sw(jnp.swapaxes(b_im, 2, 3)),
            sw(c_re), sw(c_im), d_skip.astype(F32).reshape(S5_GROUPS, S5_GROUP, 1))
    vec = pl.BlockSpec((1, 2, S5_STATE), g3)
    mat = pl.BlockSpec((1, 2, S5_GROUP, S5_STATE), g4)
    op = pl.BlockSpec((1, S5_ROW, S5_ROW), g3)
    return pl.pallas_call(
        _s5ops_kernel,
        out_shape=(jax.ShapeDtypeStruct((S5_GROUPS, S5_ROW, S5_ROW), BF16),) * 3
        + (jax.ShapeDtypeStruct((S5_GROUPS, 2, 2 * S5_STATE), F32),),
        grid=(S5_GROUPS,),
        in_specs=[vec, vec, vec, mat, mat, mat, mat, pl.BlockSpec((1, S5_GROUP, 1), g3)],
        out_specs=(op, op, op, pl.BlockSpec((1, 2, 2 * S5_STATE), g3)),
        compiler_params=_params(("arbitrary",)),
        name="s5ops",
    )(*args)


S5_BLOCK_GROUPS = LANES // S5_GROUP


S5_HALF_T = LANES // S5_GROUP


def _s5_perm():
    a = jnp.arange(S5_HALF_T * LANES, dtype=jnp.int32)
    dst = ((a % LANES) // S5_GROUP) * LANES + (a // LANES) * S5_GROUP + a % S5_GROUP
    return (dst[:, None] == a[None, :]).astype(BF16)


def _s5_kernel(nb, nk, u_ref, perm_ref, m_ref, wi_ref, wot_ref, coef_ref, h0_ref, y_ref, fin_ref,
               sr_ref, si_ref, efr_ref, ebr_ref, efi_ref, ebi_ref, uall_ref, yall_ref):
    gl = pl.program_id(1)
    rows = nb * nk
    half = 2 * S5_STATE

    @pl.when(gl == 0)
    def _():
        for hh in range(S5_CHUNK // S5_HALF_T):
            xcat = jnp.concatenate([u_ref[pl.ds(hh * S5_HALF_T + tt, rows, stride=S5_CHUNK), :].astype(BF16)
                                    for tt in range(S5_HALF_T)], axis=1)
            uh = jnp.dot(xcat, perm_ref[...], preferred_element_type=F32).astype(BF16)
            for g in range(S5_BLOCK_GROUPS):
                uall_ref[g, :, hh * LANES:(hh + 1) * LANES] = uh[:, g * LANES:(g + 1) * LANES]

    u = uall_ref[gl]
    s = jnp.dot(u, wi_ref[0], preferred_element_type=F32)
    sr_ref[...] = s[:, :half]
    si_ref[...] = s[:, half:]
    c_r = coef_ref[0, 0:1, :]
    c_i = coef_ref[0, 1:2, :]
    e_r = h0_ref[0, :, :half]
    e_i = h0_ref[0, :, half:]
    is_fwd = lax.broadcasted_iota(jnp.int32, (nb, half), 1) < S5_STATE
    for j in range(nk):
        rf = pl.ds(j, nb, stride=nk)
        rb = pl.ds(nk - 1 - j, nb, stride=nk)
        efr_ref[rf, :] = e_r
        ebr_ref[rb, :] = e_r
        efi_ref[rf, :] = e_i
        ebi_ref[rb, :] = e_i
        s_r = jnp.where(is_fwd, sr_ref[rf, :], sr_ref[rb, :])
        s_i = jnp.where(is_fwd, si_ref[rf, :], si_ref[rb, :])
        e_r, e_i = c_r * e_r - c_i * e_i + s_r, c_r * e_i + c_i * e_r + s_i
    fin_ref[0, :, :half] = e_r
    fin_ref[0, :, half:] = e_i
    y = jnp.dot(u, m_ref[0], preferred_element_type=F32)
    fwd_rows = lax.broadcasted_iota(jnp.int32, (rows, half), 1) < S5_STATE
    e = jnp.concatenate([jnp.where(fwd_rows, efr_ref[...], ebr_ref[...]),
                         jnp.where(fwd_rows, efi_ref[...], ebi_ref[...])], axis=1).astype(BF16)
    y = y + lax.dot_general(e, wot_ref[0], (((1,), (1,)), ((), ())), preferred_element_type=F32)
    yall_ref[gl] = y.astype(BF16)

    @pl.when(gl == S5_BLOCK_GROUPS - 1)
    def _():
        for hh in range(S5_CHUNK // S5_HALF_T):
            ycat = jnp.concatenate([yall_ref[g, :, hh * LANES:(hh + 1) * LANES]
                                    for g in range(S5_BLOCK_GROUPS)], axis=1)
            out = lax.dot_general(ycat, perm_ref[...], (((1,), (1,)), ((), ())), preferred_element_type=F32)
            for tt in range(S5_HALF_T):
                y_ref[pl.ds(hh * S5_HALF_T + tt, rows, stride=S5_CHUNK), :] = out[:, tt * LANES:(tt + 1) * LANES]


def _s5(u, ops, h0, nb, nk):
    m, w_in, w_out, coef = ops
    n = u.shape[0]
    rows = nb * nk
    perm = _s5_perm()
    g3 = lambda b, g: (b * S5_BLOCK_GROUPS + g, 0, 0)
    blk = lambda b, g: (0, b)
    return pl.pallas_call(
        functools.partial(_s5_kernel, nb, nk),
        out_shape=(jax.ShapeDtypeStruct((n, D_S5), F32),
                   jax.ShapeDtypeStruct((S5_GROUPS, nb, 4 * S5_STATE), F32)),
        grid=(S5_GROUPS // S5_BLOCK_GROUPS, S5_BLOCK_GROUPS),
        in_specs=[pl.BlockSpec((n, LANES), blk),
                  pl.BlockSpec(perm.shape, lambda b, g: (0, 0)),
                  pl.BlockSpec((1, S5_ROW, S5_ROW), g3),
                  pl.BlockSpec((1, S5_ROW, 4 * S5_STATE), g3),
                  pl.BlockSpec((1, 4 * S5_STATE, S5_ROW), g3),
                  pl.BlockSpec((1, 2, 2 * S5_STATE), g3),
                  pl.BlockSpec((1, nb, 4 * S5_STATE), g3)],
        out_specs=(pl.BlockSpec((n, LANES), blk),
                   pl.BlockSpec((1, nb, 4 * S5_STATE), g3)),
        scratch_shapes=[pltpu.VMEM((rows, 2 * S5_STATE), F32)] * 6 + [
            pltpu.VMEM((S5_BLOCK_GROUPS, rows, S5_ROW), BF16),
            pltpu.VMEM((S5_BLOCK_GROUPS, rows, S5_ROW), BF16)],
        compiler_params=_params(("arbitrary", "arbitrary")),
        name="s5",
    )(u, perm, m, w_in, w_out, coef, h0)


def _dft_tables(seq):
    k = jnp.arange(seq, dtype=jnp.int32)
    na = seq // FNET_GROUP
    ang_a = (2.0 * math.pi / na) * ((jnp.arange(na, dtype=jnp.int32)[:, None] * k[None, :]) % na).astype(F32)
    ang_b = (2.0 * math.pi / seq) * ((jnp.arange(FNET_GROUP, dtype=jnp.int32)[:, None] * k[None, :]) % seq).astype(F32)
    ca, sa = jnp.cos(ang_a)[:, None, :], jnp.sin(ang_a)[:, None, :]
    cb, sb = jnp.cos(ang_b)[None, :, :], jnp.sin(ang_b)[None, :, :]
    cos_jk = (ca * cb - sa * sb).reshape(seq, seq)
    sin_jk = (sa * cb + ca * sb).reshape(seq, seq)
    cs = jnp.concatenate([cos_jk, -sin_jk], axis=1).astype(BF16)
    c = jnp.arange(D_FNET, dtype=jnp.int32)
    same = (c[:, None] // FNET_GROUP) == (c[None, :] // FNET_GROUP)
    angc = (2.0 * math.pi / FNET_GROUP) * (((c[:, None] % FNET_GROUP) * (c[None, :] % FNET_GROUP))
                                           % FNET_GROUP).astype(F32)
    scale = 1.0 / math.sqrt(seq * FNET_GROUP)
    bdc = jnp.where(same, jnp.cos(angc) * scale, 0.0).astype(BF16)
    bds = jnp.where(same, jnp.sin(angc) * scale, 0.0).astype(BF16)
    return cs, bdc, bds


def _fourier_kernel(seq, z_ref, cs_ref, bdc_ref, bds_ref, o_ref, zz_ref):
    @pl.when(pl.program_id(1) == 0)
    def _():
        z = z_ref[...]
        zz_ref[0:seq, :] = jnp.dot(z, bdc_ref[...], preferred_element_type=F32).astype(BF16)
        zz_ref[seq:, :] = jnp.dot(z, bds_ref[...], preferred_element_type=F32).astype(BF16)

    o_ref[...] = jnp.dot(cs_ref[...], zz_ref[...], preferred_element_type=F32).astype(BF16)


def _fourier(z, nb, seq, tl):
    cs, bdc, bds = _dft_tables(seq)
    nt = seq // tl
    return pl.pallas_call(
        functools.partial(_fourier_kernel, seq),
        out_shape=jax.ShapeDtypeStruct(z.shape, BF16),
        grid=(nb, nt),
        in_specs=[pl.BlockSpec((seq, D_FNET), lambda b, i: (b, 0)),
                  pl.BlockSpec((tl, 2 * seq), lambda b, i: (i, 0)),
                  pl.BlockSpec((D_FNET, D_FNET), lambda b, i: (0, 0)),
                  pl.BlockSpec((D_FNET, D_FNET), lambda b, i: (0, 0))],
        out_specs=pl.BlockSpec((tl, D_FNET), lambda b, i: (b * nt + i, 0)),
        scratch_shapes=[pltpu.VMEM((2 * seq, D_FNET), BF16)],
        compiler_params=_params(("arbitrary", "arbitrary")),
        name="fourier",
    )(z, cs, bdc, bds)


def _first_argmax_mask(v, iota, size):
    m = jnp.max(v, axis=0, keepdims=True)
    first = jnp.min(jnp.where(v == m, iota, size), axis=0, keepdims=True)
    return iota == first


def _route(logits_t, bias_col):
    tm = logits_t.shape[1]
    neg = -jnp.inf
    s = jax.nn.sigmoid(logits_t)
    biased = s + bias_col
    io8 = lax.broadcasted_iota(jnp.int32, (EXPERTS_PER_GROUP, tm), 0)
    gs_rows = []
    for g in range(N_EXPERT_GROUPS):
        blk = biased[g * EXPERTS_PER_GROUP:(g + 1) * EXPERTS_PER_GROUP, :]
        m1 = jnp.max(blk, axis=0, keepdims=True)
        rest = jnp.where(_first_argmax_mask(blk, io8, EXPERTS_PER_GROUP), neg, blk)
        gs_rows.append(m1 + jnp.max(rest, axis=0, keepdims=True))
    gs = jnp.concatenate(gs_rows, axis=0)
    iog = lax.broadcasted_iota(jnp.int32, (N_EXPERT_GROUPS, tm), 0)
    gsel = jnp.zeros((N_EXPERT_GROUPS, tm), F32)
    for _ in range(TOPK_GROUPS):
        sel = _first_argmax_mask(gs, iog, N_EXPERT_GROUPS)
        gsel = jnp.where(sel, 1.0, gsel)
        gs = jnp.where(sel, neg, gs)
    emask = jnp.concatenate(
        [jnp.broadcast_to(gsel[g:g + 1, :], (EXPERTS_PER_GROUP, tm)) for g in range(N_EXPERT_GROUPS)], axis=0)
    v = jnp.where(emask > 0.0, biased, neg)
    ioe = lax.broadcasted_iota(jnp.int32, (N_EXPERTS, tm), 0)
    idx_rows, s_rows = [], []
    for _ in range(TOP_K):
        sel = _first_argmax_mask(v, ioe, N_EXPERTS)
        idx_rows.append(jnp.sum(jnp.where(sel, ioe, 0), axis=0, keepdims=True))
        s_rows.append(jnp.sum(jnp.where(sel, s, 0.0), axis=0, keepdims=True))
        v = jnp.where(sel, neg, v)
    denom = s_rows[0]
    for r in s_rows[1:]:
        denom = denom + r
    pad = ROUTE_ROWS - TOP_K
    idx = jnp.concatenate(idx_rows + [jnp.zeros((pad, tm), jnp.int32)], axis=0)
    w = jnp.concatenate([r / denom * ROUTED_SCALE for r in s_rows] + [jnp.zeros((pad, tm), F32)], axis=0)
    return idx, w


def _merge_kernel(has_pos, *refs):
    if has_pos:
        (ys_ref, yf_ref, gt_ref, x_ref, pos_ref, mod_ref, n2_ref, wglu_ref, wps_ref, wpf_ref, wout_ref,
         wrt_ref, rb_ref, ws1_ref, ws3_ref, ws2_ref, xs_ref, h2_ref, ridx_ref, rw_ref) = refs
        x = x_ref[...] + pos_ref[...]
    else:
        (ys_ref, yf_ref, gt_ref, x_ref, mod_ref, n2_ref, wglu_ref, wps_ref, wpf_ref, wout_ref,
         wrt_ref, rb_ref, ws1_ref, ws3_ref, ws2_ref, xs_ref, h2_ref, ridx_ref, rw_ref) = refs
        x = x_ref[...]
    m = mod_ref[0]
    g = jax.nn.gelu(ys_ref[...].astype(F32))
    a = g * jax.nn.sigmoid(jnp.dot(g.astype(BF16), wglu_ref[...], preferred_element_type=F32))
    pa = jnp.dot(a.astype(BF16), wps_ref[...], preferred_element_type=F32)
    pb = jnp.dot(yf_ref[...], wpf_ref[...], preferred_element_type=F32)
    gt = gt_ref[...].astype(F32)
    merged = gt[:, :D_MODEL] * pa + gt[:, D_MODEL:] * pb
    x1 = x + m[2:3, :] * jnp.dot(merged.astype(BF16), wout_ref[...], preferred_element_type=F32)
    h2 = _rms(x1, n2_ref[...]) * (1.0 + m[4:5, :]) + m[3:4, :]
    hb = h2.astype(BF16)
    for j in range(ROW_SUB):
        h2_ref[pl.ds(j, h2.shape[0], stride=ROW_SUB), :] = h2[:, j * LANES:(j + 1) * LANES]
    wr = wrt_ref[...]
    wr_hi = wr.astype(BF16)
    wr_lo = (wr - wr_hi.astype(F32)).astype(BF16)
    h_lo = (h2 - hb.astype(F32)).astype(BF16)
    dn = (((1,), (1,)), ((), ()))
    logits_t = (lax.dot_general(wr_hi, hb, dn, preferred_element_type=F32)
                + lax.dot_general(wr_hi, h_lo, dn, preferred_element_type=F32)
                + lax.dot_general(wr_lo, hb, dn, preferred_element_type=F32))
    ridx_ref[...], rw_ref[...] = _route(logits_t, rb_ref[...])
    s1 = jnp.dot(hb, ws1_ref[...], preferred_element_type=F32)
    s3 = jnp.dot(hb, ws3_ref[...], preferred_element_type=F32)
    shared = jnp.dot((s1 * jax.nn.sigmoid(s1) * s3).astype(BF16), ws2_ref[...], preferred_element_type=F32)
    xs_ref[...] = x1 + m[5:6, :] * shared


def _merge(ys, yf, gt, x, pos, mod, mod_row, n2, weights, tm):
    n = x.shape[0]
    has_pos = pos is not None
    row = lambda i: (i, 0)
    const = lambda a: pl.BlockSpec(a.shape, lambda i: (0,) * a.ndim)
    in_specs = [pl.BlockSpec((tm, D_S5), row), pl.BlockSpec((tm, D_FNET), row),
                pl.BlockSpec((tm, 2 * D_MODEL), row), pl.BlockSpec((tm, D_MODEL), row)]
    args = [ys, yf, gt, x]
    if has_pos:
        nper = pos.shape[0] // tm
        in_specs.append(pl.BlockSpec((tm, D_MODEL), lambda i: (i % nper, 0)))
        args.append(pos)
    in_specs += [pl.BlockSpec((1, N_MOD, D_MODEL), lambda i: (mod_row(i, tm), 0, 0)), const(n2)]
    args += [mod, n2]
    in_specs += [const(w) for w in weights]
    args += list(weights)
    return pl.pallas_call(
        functools.partial(_merge_kernel, has_pos),
        out_shape=(jax.ShapeDtypeStruct((n, D_MODEL), F32),
                   jax.ShapeDtypeStruct((n * ROW_SUB, LANES), F32),
                   jax.ShapeDtypeStruct((ROUTE_ROWS, n), jnp.int32),
                   jax.ShapeDtypeStruct((ROUTE_ROWS, n), F32)),
        grid=(n // tm,),
        in_specs=in_specs,
        out_specs=(pl.BlockSpec((tm, D_MODEL), row), pl.BlockSpec((tm * ROW_SUB, LANES), row),
                   pl.BlockSpec((ROUTE_ROWS, tm), lambda i: (0, i)),
                   pl.BlockSpec((ROUTE_ROWS, tm), lambda i: (0, i))),
        compiler_params=_params(("arbitrary",)),
        name="merge",
    )(*args)


MOE_SUB = 4096
MOE_TM = 128
MOE_TMAX = MOE_SUB * TOP_K // MOE_TM + N_EXPERTS
MOE_SORT_SUBS = 2
MOE_PAD = 2
MOE_TS = MOE_TMAX + 2 * MOE_PAD
MOE_DUMMY = 256
MOE_RMW = 8
MOE_FT = 512
MOE_EPS = 2
MOE_ESTEPS = N_EXPERTS // MOE_EPS
ROW_TILE = MOE_TM * ROW_SUB


def _moe_plan(ridx, rw):
    n = ridx.shape[1]
    nsub = n // MOE_SUB
    npair = n * TOP_K
    t = jnp.arange(n, dtype=jnp.int32)
    key = (((t // MOE_SUB) * N_EXPERTS)[None] + ridx[:TOP_K]) * MOE_SUB + (t % MOE_SUB)[None]
    group = MOE_SORT_SUBS * MOE_SUB
    parts = [lax.sort((key[:, g:g + group].reshape(-1), rw[:TOP_K, g:g + group].reshape(-1)), num_keys=1)
             for g in range(0, n, group)]
    skey = jnp.concatenate([pk for pk, _ in parts])
    sw = jnp.concatenate([pw for _, pw in parts])
    stok = jnp.concatenate([(skey % MOE_SUB) * ROW_SUB, jnp.zeros((MOE_TM,), jnp.int32)])
    sw_rows = jnp.concatenate([sw, jnp.zeros((MOE_TM,), F32)]).reshape((npair + MOE_TM) // LANES, 1, LANES)
    hits = ridx[:TOP_K].reshape(TOP_K, nsub, 1, MOE_SUB) == jnp.arange(N_EXPERTS, dtype=jnp.int32)[None, None, :, None]
    cnt = jnp.sum(hits.astype(jnp.int32), axis=(0, 3))
    poff = (jnp.cumsum(cnt.reshape(-1)) - cnt.reshape(-1)).reshape(nsub, N_EXPERTS)
    ntile = (cnt + MOE_TM - 1) // MOE_TM
    tcum = jnp.cumsum(ntile, axis=1)
    toff = tcum - ntile
    tstart = jnp.concatenate([toff, tcum[:, -1:]], axis=1).reshape(-1).astype(jnp.int32)
    j = jnp.arange(MOE_TS, dtype=jnp.int32) - MOE_PAD
    valid = (j[None] >= 0) & (j[None] < tcum[:, -1:])
    te = jnp.minimum(jnp.sum(j[None, :, None] >= tcum[:, None, :], axis=-1), N_EXPERTS - 1)
    pick = lambda a: jnp.take_along_axis(a, te, axis=1)
    first = (j[None] - pick(toff)) * MOE_TM
    p0 = jnp.where(valid, pick(poff) + first, 0).reshape(-1).astype(jnp.int32)
    nv = jnp.where(valid, jnp.minimum(pick(cnt) - first, MOE_TM), 0).reshape(-1).astype(jnp.int32)
    return tstart, p0, nv, stok, sw_rows


def _moe_kernel(ts_ref, p0_ref, nv_ref, tok_ref, sw_ref, src_ref, w1_ref, w3_ref, w2_ref, xs_ref, mod_ref,
                fg_ref, o_ref, y_ref, xt_ref, xb_ref, act_ref, ot_ref, w1b_ref, w3b_ref, w2b_ref, slot_ref):
    sub = pl.program_id(0)
    e = pl.program_id(1)
    base = sub * MOE_TS + MOE_PAD

    def gather(p0):
        for mi in range(MOE_TM):
            tok = pl.multiple_of(tok_ref[p0 + mi], ROW_SUB)
            xt_ref[mi * ROW_SUB:(mi + 1) * ROW_SUB, :] = src_ref[pl.ds(tok, ROW_SUB), :]
        for j in range(ROW_SUB):
            xb_ref[:, j * LANES:(j + 1) * LANES] = xt_ref[pl.ds(j, MOE_TM, stride=ROW_SUB), :].astype(BF16)

    def scatter(p0, nv, masked):
        for u in range(0, MOE_TM, MOE_RMW):
            new = []
            for i in range(MOE_RMW):
                tok = tok_ref[p0 + u + i]
                if masked:
                    tok = jnp.where(u + i < nv, tok, MOE_SUB * ROW_SUB)
                tok = pl.multiple_of(tok, ROW_SUB)
                new.append((tok, y_ref[pl.ds(tok, ROW_SUB), :]
                            + ot_ref[(u + i) * ROW_SUB:(u + i + 1) * ROW_SUB, :]))
            for tok, v in new:
                y_ref[pl.ds(tok, ROW_SUB), :] = v

    @pl.when(e == 0)
    def _():
        y_ref[...] = jnp.zeros_like(y_ref)
        ot_ref[...] = jnp.zeros_like(ot_ref)
        act_ref[...] = jnp.zeros_like(act_ref)
        w2b_ref[...] = jnp.zeros_like(w2b_ref)
        slot_ref[0] = 0
        gather(p0_ref[base])

    def down_proj(w2_slot):
        o = jnp.dot(act_ref[...], w2b_ref[w2_slot], preferred_element_type=F32)
        for j in range(ROW_SUB):
            ot_ref[pl.ds(j, MOE_TM, stride=ROW_SUB), :] = o[:, j * LANES:(j + 1) * LANES]

    def step(i, first, slot):
        cur = base + i
        scatter(p0_ref[cur - 2], nv_ref[cur - 2], True)
        down_proj(jnp.where(i > first, slot, 1 - slot))
        p0 = p0_ref[cur]
        nv = nv_ref[cur]
        x = xb_ref[...]
        a = jnp.dot(x, w1b_ref[...], preferred_element_type=F32)
        b = jnp.dot(x, w3b_ref[...], preferred_element_type=F32)
        r0 = p0 // LANES
        c = p0 % LANES
        lane = lax.broadcasted_iota(jnp.int32, (1, LANES), 1)
        rows = lax.broadcasted_iota(jnp.int32, (LANES, LANES), 0)
        cols = lax.broadcasted_iota(jnp.int32, (LANES, LANES), 1)
        gparts = []
        for hh in range(MOE_TM // LANES):
            ga = pltpu.roll(sw_ref[r0 + hh], LANES - c, axis=1)
            gb = pltpu.roll(sw_ref[r0 + hh + 1], LANES - c, axis=1)
            g = jnp.where(lane + hh * LANES < nv, jnp.where(lane < LANES - c, ga, gb), 0.0)
            gparts.append(jnp.sum(jnp.where(rows == cols, jnp.broadcast_to(g, (LANES, LANES)), 0.0),
                                  axis=1, keepdims=True))
        gcol = jnp.concatenate(gparts, axis=0)
        act_ref[...] = (a * jax.nn.sigmoid(a) * b * gcol).astype(BF16)
        gather(p0_ref[cur + 1])

    def run_expert(ee):
        ex = jnp.minimum(e, MOE_ESTEPS - 1) * MOE_EPS + ee
        first = ts_ref[sub * (N_EXPERTS + 1) + ex]
        last = jnp.where(e < MOE_ESTEPS, ts_ref[sub * (N_EXPERTS + 1) + ex + 1], first)

        @pl.when(last > first)
        def _():
            slot_ref[0] = 1 - slot_ref[0]
            w1b_ref[...] = w1_ref[ee].astype(BF16)
            w3b_ref[...] = w3_ref[ee].astype(BF16)
            w2b_ref[slot_ref[0]] = w2_ref[ee].astype(BF16)

        slot = slot_ref[0]

        def body(i, carry):
            step(i, first, slot)
            return carry

        lax.fori_loop(first, last, body, 0)

        if ee == MOE_EPS - 1:
            @pl.when(e == MOE_ESTEPS - 1)
            def _():
                scatter(p0_ref[base + last - 2], nv_ref[base + last - 2], True)
                down_proj(slot)
                scatter(p0_ref[base + last - 1], nv_ref[base + last - 1], True)

    for ee in range(MOE_EPS):
        run_expert(ee)

    @pl.when(e >= MOE_ESTEPS)
    def _():
        row0 = pl.multiple_of((e - MOE_ESTEPS) * (MOE_FT * ROW_SUB), ROW_SUB)
        y = jnp.concatenate([y_ref[pl.ds(row0 + j, MOE_FT, stride=ROW_SUB), :] for j in range(ROW_SUB)],
                            axis=1)
        x2 = xs_ref[...] + mod_ref[0][5:6, :] * y
        o_ref[...] = _rms(x2, fg_ref[...])


def _moe(h2_rows, plan, w1, w3, w2, xs, mod, mod_row, fg):
    tstart, p0, nv, stok, sw_rows = plan
    nsub = h2_rows.shape[0] // (MOE_SUB * ROW_SUB)
    per_sub = MOE_SUB // MOE_FT
    wmap = lambda s, e, ts, p0, nv: (jnp.minimum(e, MOE_ESTEPS - 1), 0, 0)
    sub2 = lambda s, e, ts, p0, nv: (s, 0)
    out_tile = lambda s, e: s * per_sub + jnp.clip(e - MOE_ESTEPS, 0, per_sub - 1)
    grid_spec = pltpu.PrefetchScalarGridSpec(
        num_scalar_prefetch=3,
        grid=(nsub, MOE_ESTEPS + per_sub),
        in_specs=[pl.BlockSpec(memory_space=pltpu.SMEM),
                  pl.BlockSpec(sw_rows.shape, lambda s, e, ts, p0, nv: (0, 0, 0)),
                  pl.BlockSpec((MOE_SUB * ROW_SUB, LANES), sub2, pipeline_mode=pl.Buffered(1)),
                  pl.BlockSpec((MOE_EPS, D_MODEL, D_EXPERT), wmap),
                  pl.BlockSpec((MOE_EPS, D_MODEL, D_EXPERT), wmap),
                  pl.BlockSpec((MOE_EPS, D_EXPERT, D_MODEL), wmap),
                  pl.BlockSpec((MOE_FT, D_MODEL), lambda s, e, ts, p0, nv: (out_tile(s, e), 0)),
                  pl.BlockSpec((1, N_MOD, D_MODEL),
                               lambda s, e, ts, p0, nv: (mod_row(out_tile(s, e), MOE_FT), 0, 0)),
                  pl.BlockSpec((1, D_MODEL), lambda s, e, ts, p0, nv: (0, 0))],
        out_specs=pl.BlockSpec((MOE_FT, D_MODEL), lambda s, e, ts, p0, nv: (out_tile(s, e), 0)),
        scratch_shapes=[pltpu.VMEM(((MOE_SUB + MOE_DUMMY) * ROW_SUB, LANES), F32),
                        pltpu.VMEM((ROW_TILE, LANES), F32), pltpu.VMEM((MOE_TM, D_MODEL), BF16),
                        pltpu.VMEM((MOE_TM, D_EXPERT), BF16), pltpu.VMEM((ROW_TILE, LANES), F32),
                        pltpu.VMEM((D_MODEL, D_EXPERT), BF16), pltpu.VMEM((D_MODEL, D_EXPERT), BF16),
                        pltpu.VMEM((2, D_EXPERT, D_MODEL), BF16), pltpu.SMEM((1,), jnp.int32)])
    return pl.pallas_call(
        _moe_kernel,
        grid_spec=grid_spec,
        out_shape=jax.ShapeDtypeStruct(xs.shape, F32),
        compiler_params=_params(("arbitrary", "arbitrary")),
        name="moe",
    )(tstart, p0, nv, stok, sw_rows, h2_rows, w1, w3, w2, xs, mod, fg)


def _grid_pos_embed(n_tokens):
    rows = n_tokens // GRID_W
    quarter = D_MODEL // 4
    omega = 1.0 / (10000.0 ** (jnp.arange(quarter, dtype=F32) / quarter))

    def emb(count):
        a = jnp.arange(count, dtype=F32)[:, None] * omega
        return jnp.concatenate([jnp.sin(a), jnp.cos(a)], axis=-1)

    by_row = jnp.repeat(emb(rows), GRID_W, axis=0)
    by_col = jnp.tile(emb(GRID_W), (rows, 1))
    return jnp.concatenate([by_row, by_col], axis=-1)


def _mixers(x3, pos, mod, first_row, h0, s5_ops, p):
    nb, seq, _ = x3.shape
    n = nb * seq
    nk = seq // S5_CHUNK
    x = x3.reshape(n, D_MODEL)
    per_seq_mod = first_row > 0

    def mod_row(i, tm):
        return first_row + (i * tm) // seq if per_seq_mod else 0

    us, uf, gt = _inproj(x, pos, mod, mod_row, p["norm1_g"], p["w_in"], 1024)
    ys, fin = _s5(us, s5_ops, h0, nb, nk)
    yf = _fourier(uf, nb, seq, min(seq, 512))
    xs, h2_rows, ridx, rw = _merge(ys, yf, gt, x, pos, mod, mod_row, p["norm2_g"], p["merge_w"], 512)
    return xs, h2_rows, ridx, rw, fin, mod_row


def _plan_of_stream(plan, s, n_tokens):
    tstart, p0, nv, stok, sw_rows = plan
    nsub = n_tokens // MOE_SUB
    npair = n_tokens * TOP_K
    off = s * npair
    tiles = slice(s * nsub * MOE_TS, (s + 1) * nsub * MOE_TS)
    nv_s = nv[tiles]
    p0_s = jnp.where(nv_s > 0, p0[tiles] - off, 0)
    return (tstart[s * nsub * (N_EXPERTS + 1):(s + 1) * nsub * (N_EXPERTS + 1)], p0_s, nv_s,
            stok[off:off + npair + MOE_TM], sw_rows[off // LANES:(off + npair + MOE_TM) // LANES])


def kernel(x_prompt, x_sample, state_s5_re, state_s5_im, c, c_ctx, w_ada, b_ada, norm1_g, norm2_g, w_in,
           lam_re, lam_im, log_dt, b_re, b_im, c_re, c_im, d_skip, w_glu, w_proj_s5, w_proj_f, w_out,
           w_router, router_bias, w1, w3, w2, ws1, ws3, ws2, final_norm_g):
    nb_ctx = x_prompt.shape[0]
    nb_lat, seq_lat, _ = x_sample.shape
    half = 2 * S5_STATE

    cond = jnp.concatenate([c_ctx[None], c, jnp.zeros((MOD_ROWS - 1 - nb_lat, D_MODEL), F32)], axis=0)
    mod = _adaln(cond, w_ada[0], b_ada[0]).reshape(MOD_ROWS, N_MOD, D_MODEL)

    s5_ops = _s5ops(lam_re[0], lam_im[0], log_dt[0], b_re[0], b_im[0], c_re[0], c_im[0], d_skip[0])
    p = dict(
        norm1_g=norm1_g[0][None], norm2_g=norm2_g[0][None], final_g=final_norm_g[None],
        w_in=w_in[0].astype(BF16), w1=w1[0], w3=w3[0], w2=w2[0],
        merge_w=(w_glu[0].astype(BF16), w_proj_s5[0].astype(BF16), w_proj_f[0].astype(BF16),
                 w_out[0].astype(BF16), w_router[0].T, router_bias[0][:, None],
                 ws1[0].astype(BF16), ws3[0].astype(BF16), ws2[0].astype(BF16)))

    def pack_state(sr, si):
        f = lambda a: a.astype(F32).transpose(2, 0, 1, 3).reshape(S5_GROUPS, a.shape[0], half)
        return jnp.concatenate([f(sr), f(si)], axis=-1)

    def unpack_state(fin, lo):
        nb = fin.shape[1]
        return fin[..., lo:lo + half].reshape(S5_GROUPS, nb, 2, S5_STATE).transpose(1, 2, 0, 3)[:, None]

    h0_ctx = jnp.zeros((S5_GROUPS, nb_ctx, 2 * half), F32)
    h0_lat = pack_state(state_s5_re[:, 0], state_s5_im[:, 0])
    streams = [(x_prompt, _mixers(x_prompt, None, mod, 0, h0_ctx, s5_ops, p)),
               (x_sample, _mixers(x_sample, _grid_pos_embed(seq_lat), mod, 1, h0_lat, s5_ops, p))]
    n_tokens = streams[0][1][0].shape[0]
    assert all(m[0].shape[0] == n_tokens for _, m in streams), "the joint plan slices equal-sized streams"
    plan = _moe_plan(jnp.concatenate([m[2] for _, m in streams], axis=1),
                     jnp.concatenate([m[3] for _, m in streams], axis=1))
    outs = []
    for s, (x3, (xs, h2_rows, _, _, _, mod_row)) in enumerate(streams):
        out = _moe(h2_rows, _plan_of_stream(plan, s, n_tokens), p["w1"], p["w3"], p["w2"],
                   xs, mod, mod_row, p["final_g"])
        outs.append(out.reshape(x3.shape))
    fin = streams[0][1][4]
    return (outs[0], outs[1], unpack_state(fin, 0).astype(x_prompt.dtype),
            unpack_state(fin, half).astype(x_prompt.dtype))
```
